```python
import functools
import jax
import jax.numpy as jnp
from jax import lax
import numpy as np

D_MODEL = 1024
BATCH = 16
SEQ = 256
DEPTH = 2
DEC_BATCH = 4
DEC_SEQ = 2048
PAST_LEN = 256

GRID_W = 64
C_CONV = 256
CONV_W = 31
C_FNO = 256
FNO_GROUPS = 4
FNO_GW = C_FNO // FNO_GROUPS
N_HEADS = 8
HEAD_DIM = 64
C_RWKV = N_HEADS * HEAD_DIM
G_RANK = 128
W_RANK = 64
A_RANK = 64
N_DIR = 2
N_BRANCH = 3
D_FF = 2816
N_EXPERTS = 8
TOP_K = 2
D_FF_E = 1408
N_DENSE = (DEPTH + 1) // 2
N_MOE = DEPTH // 2
RMS_EPS = 1e-6
LN_EPS = 1e-5
GN_EPS = 64e-5

OFF_CONV = 0
OFF_FNO = OFF_CONV + 2 * C_CONV
OFF_RWKV = OFF_FNO + C_FNO
N_SHIFT = 3 * C_RWKV + G_RANK + N_DIR * (W_RANK + A_RANK)
OFF_GATE = OFF_RWKV + N_SHIFT
D_IN = OFF_GATE + N_BRANCH * D_MODEL

kernel_name = "hybrid_conv_fnet_rwkv7_diffusion_step"


def rmsnorm(x, g):
    xf = x.astype(jnp.float32)
    y = xf * lax.rsqrt(jnp.mean(xf * xf, axis=-1, keepdims=True) + RMS_EPS)
    return (y * g.astype(jnp.float32)).astype(x.dtype)


def layernorm(x, g, b, eps):
    xf = x.astype(jnp.float32)
    mu = jnp.mean(xf, axis=-1, keepdims=True)
    var = jnp.mean(jnp.square(xf - mu), axis=-1, keepdims=True)
    y = (xf - mu) * lax.rsqrt(var + eps) * g.astype(jnp.float32) + b.astype(jnp.float32)
    return y.astype(x.dtype)


def conv_branch(u, n_rows, dw_w, dw_b, ln_g, ln_b, w_out):
    bsz, t_len, _ = u.shape
    h = u[..., :C_CONV] * jax.nn.sigmoid(u[..., C_CONV:])
    if n_rows is not None:
        h = h.reshape(bsz * n_rows, GRID_W, C_CONV)
    h = lax.conv_general_dilated(
        h, dw_w[:, None, :].astype(h.dtype), window_strides=(1,),
        padding=[(CONV_W // 2, CONV_W // 2)],
        dimension_numbers=("NWC", "WIO", "NWC"), feature_group_count=C_CONV)
    h = h.reshape(bsz, t_len, C_CONV) + dw_b
    h = jax.nn.silu(layernorm(h, ln_g, ln_b, LN_EPS))
    return h @ w_out


def fourier_branch(u, w_out):
    bsz, t_len, _ = u.shape
    h = u.astype(jnp.float32).reshape(bsz, t_len, FNO_GROUPS, FNO_GW)
    h = jnp.fft.fft2(h, axes=(1, 3), norm="ortho").real
    return h.reshape(bsz, t_len, C_FNO).astype(u.dtype) @ w_out


def wkv_scan(r, w, k, v, a_, b_, s0, reverse):
    def step(s, inp):
        r_t, w_t, k_t, v_t, a_t, b_t = inp
        sa = jnp.einsum("bhvk,bhk->bhv", s, a_t)
        s = (s * w_t[:, :, None, :] + sa[..., None] * b_t[:, :, None, :]
             + v_t[..., None] * k_t[:, :, None, :])
        return s, jnp.einsum("bhvk,bhk->bhv", s, r_t)
    xs = tuple(jnp.moveaxis(t, 1, 0) for t in (r, w, k, v, a_, b_))
    s_fin, ys = lax.scan(step, s0, xs, reverse=reverse)
    return jnp.moveaxis(ys, 0, 1), s_fin


def rwkv_branch(z, s0, mu, g_up, dec_w0, dec_up, iclr_a0, iclr_up, k_k, k_a, r_k,
                gn_g, gn_b, w_out):
    f32 = jnp.float32
    bsz, t_len, _ = z.shape
    zp = jnp.pad(z, ((0, 0), (1, 1), (0, 0)))
    z = z + mu * (0.5 * (zp[:, :-2] + zp[:, 2:]) - z)
    zf = z.astype(f32)
    heads = lambda t: t.reshape(t.shape[:-1] + (N_HEADS, HEAD_DIM))
    r = zf[..., 0:C_RWKV]
    k = zf[..., C_RWKV:2 * C_RWKV]
    v = zf[..., 2 * C_RWKV:3 * C_RWKV]
    o = 3 * C_RWKV
    g = jax.nn.sigmoid(zf[..., o:o + G_RANK]) @ g_up.astype(f32)
    o += G_RANK
    wd = zf[..., o:o + N_DIR * W_RANK].reshape(bsz, t_len, N_DIR, W_RANK)
    o += N_DIR * W_RANK
    ad = zf[..., o:o + N_DIR * A_RANK].reshape(bsz, t_len, N_DIR, A_RANK)
    w_log = -jax.nn.softplus(-(dec_w0 + jnp.einsum("btdr,drc->btdc", jnp.tanh(wd), dec_up))) - 0.5
    decay = jnp.exp(-jnp.exp(w_log))
    a = jax.nn.sigmoid(iclr_a0 + jnp.einsum("btdr,drc->btdc", ad, iclr_up))
    kk = heads(k * k_k)
    kk = kk / jnp.maximum(jnp.linalg.norm(kk, axis=-1, keepdims=True), 1e-12)
    rh, vh = heads(r), heads(v)
    ys, finals = [], []
    for d in range(N_DIR):
        a_d = a[:, :, d]
        k_d = heads(k * (1 + (a_d - 1) * k_a))
        y_d, s_d = wkv_scan(rh, heads(decay[:, :, d]), k_d, vh, -kk, kk * heads(a_d),
                            s0[:, d].astype(f32), reverse=(d == 1))
        ys.append(y_d)
        finals.append(s_d)
    y = layernorm(ys[0] + ys[1], gn_g.reshape(N_HEADS, HEAD_DIM),
                  gn_b.reshape(N_HEADS, HEAD_DIM), GN_EPS)
    y = y + jnp.sum(rh * heads(k) * r_k, axis=-1, keepdims=True) * vh
    y = y.reshape(bsz, t_len, C_RWKV) * g
    return y.astype(z.dtype) @ w_out, jnp.stack(finals, axis=1)


def swiglu(h, w_gate, w_up, w_down):
    return (jax.nn.silu(h @ w_gate) * (h @ w_up)) @ w_down


def moe_swiglu(h, w_router, b_router, w_gate, w_up, w_down):
    bsz, t_len, dm = h.shape
    t = h.reshape(bsz * t_len, dm)
    logits = (t @ w_router).astype(jnp.float32) + b_router.astype(jnp.float32)
    top_v, top_i = lax.top_k(logits, TOP_K)
    top_p = jax.nn.softmax(top_v, axis=-1)
    comb = jnp.sum(jax.nn.one_hot(top_i, N_EXPERTS, dtype=jnp.float32) * top_p[..., None], axis=1)
    y = jnp.zeros_like(t)
    for e in range(N_EXPERTS):
        y = y + comb[:, e:e + 1].astype(t.dtype) * swiglu(t, w_gate[e], w_up[e], w_down[e])
    return y.reshape(bsz, t_len, dm)


def trunk_layer(x, mod, s0, n_rows, lp, ffn):
    bsz, t_len, _ = x.shape
    sh1, sc1, g1, sh2, sc2, g2 = jnp.split(mod, 6, axis=-1)
    h = rmsnorm(x, lp["norm1"]) * (1 + sc1) + sh1
    u = h @ lp["w_in"]
    y_conv = conv_branch(u[..., OFF_CONV:OFF_FNO], n_rows, lp["dw_w"], lp["dw_b"],
                         lp["conv_ln_g"], lp["conv_ln_b"], lp["w_conv_out"])
    y_fno = fourier_branch(u[..., OFF_FNO:OFF_RWKV], lp["w_fno_out"])
    y_rwkv, s_fin = rwkv_branch(u[..., OFF_RWKV:OFF_GATE], s0, lp["shift_mu"], lp["g_up"],
                                lp["dec_w0"], lp["dec_up"], lp["iclr_a0"], lp["iclr_up"],
                                lp["k_k"], lp["k_a"], lp["r_k"], lp["gn_g"], lp["gn_b"],
                                lp["w_rwkv_out"])
    gates = jax.nn.sigmoid(u[..., OFF_GATE:].reshape(bsz, t_len, N_BRANCH, D_MODEL))
    merged = gates[:, :, 0] * y_conv + gates[:, :, 1] * y_fno + gates[:, :, 2] * y_rwkv
    x = x + g1 * (merged @ lp["w_o"])
    h = rmsnorm(x, lp["norm2"]) * (1 + sc2) + sh2
    x = x + g2 * ffn(h)
    return x, s_fin


def setup_inputs(seed: int = 0) -> dict:
    key = jax.random.key(seed)
    ks = jax.random.split(key, 40)
    L = DEPTH
    nrm = lambda i, shape, s: jax.random.normal(ks[i], shape, jnp.float32) * s
    gain = lambda i, shape: 1.0 + nrm(i, shape, 0.05)
    return {
        "x_prompt": nrm(0, (BATCH, SEQ, D_MODEL), 1.0),
        "x_sample": nrm(1, (DEC_BATCH, DEC_SEQ, D_MODEL), 1.0),
        "state_rwkv": nrm(2, (DEC_BATCH, L, N_DIR, N_HEADS, HEAD_DIM, HEAD_DIM), 0.5),
        "c": nrm(3, (DEC_BATCH, D_MODEL), 1.0),
        "c_ctx": nrm(4, (D_MODEL,), 1.0),
        "norm1": gain(5, (L, D_MODEL)),
        "norm2": gain(6, (L, D_MODEL)),
        "w_ada": nrm(7, (L, D_MODEL, 6 * D_MODEL), 0.5 * D_MODEL ** -0.5),
        "b_ada": nrm(8, (L, 6 * D_MODEL), 0.02),
        "w_in": nrm(9, (L, D_MODEL, D_IN), D_MODEL ** -0.5),
        "dw_w": nrm(10, (L, CONV_W, C_CONV), CONV_W ** -0.5),
        "dw_b": nrm(11, (L, C_CONV), 0.02),
        "conv_ln_g": gain(12, (L, C_CONV)),
        "conv_ln_b": nrm(13, (L, C_CONV), 0.02),
        "w_conv_out": nrm(14, (L, C_CONV, D_MODEL), C_CONV ** -0.5),
        "w_fno_out": nrm(15, (L, C_FNO, D_MODEL), C_FNO ** -0.5),
        "shift_mu": jax.random.uniform(ks[16], (L, N_SHIFT), jnp.float32),
        "g_up": nrm(17, (L, G_RANK, C_RWKV), G_RANK ** -0.5),
        "dec_w0": jax.random.uniform(ks[18], (L, N_DIR, C_RWKV), jnp.float32, -1.5, 1.5),
        "dec_up": nrm(19, (L, N_DIR, W_RANK, C_RWKV), 0.1),
        "iclr_a0": nrm(20, (L, N_DIR, C_RWKV), 0.5),
        "iclr_up": nrm(21, (L, N_DIR, A_RANK, C_RWKV), A_RANK ** -0.5),
        "k_k": 1.0 + nrm(22, (L, C_RWKV), 0.1),
        "k_a": 1.0 + nrm(23, (L, C_RWKV), 0.1),
        "r_k": nrm(24, (L, N_HEADS, HEAD_DIM), 0.1),
        "gn_g": gain(25, (L, C_RWKV)),
        "gn_b": nrm(26, (L, C_RWKV), 0.02),
        "w_rwkv_out": nrm(27, (L, C_RWKV, D_MODEL), C_RWKV ** -0.5),
        "w_o": nrm(28, (L, D_MODEL, D_MODEL), D_MODEL ** -0.5),
        "ffn_w_gate": nrm(29, (N_DENSE, D_MODEL, D_FF), D_MODEL ** -0.5),
        "ffn_w_up": nrm(30, (N_DENSE, D_MODEL, D_FF), D_MODEL ** -0.5),
        "ffn_w_down": nrm(31, (N_DENSE, D_FF, D_MODEL), D_FF ** -0.5),
        "w_router": nrm(32, (N_MOE, D_MODEL, N_EXPERTS), D_MODEL ** -0.5),
        "b_router": nrm(33, (N_MOE, N_EXPERTS), 0.01),
        "moe_w_gate": nrm(34, (N_MOE, N_EXPERTS, D_MODEL, D_FF_E), D_MODEL ** -0.5),
        "moe_w_up": nrm(35, (N_MOE, N_EXPERTS, D_MODEL, D_FF_E), D_MODEL ** -0.5),
        "moe_w_down": nrm(36, (N_MOE, N_EXPERTS, D_FF_E, D_MODEL), D_FF_E ** -0.5),
        "final_norm": gain(37, (D_MODEL,)),
    }


def reference(x_prompt, x_sample, state_rwkv, c, c_ctx, norm1, norm2, w_ada, b_ada, w_in,
              dw_w, dw_b, conv_ln_g, conv_ln_b, w_conv_out, w_fno_out, shift_mu, g_up,
              dec_w0, dec_up, iclr_a0, iclr_up, k_k, k_a, r_k, gn_g, gn_b, w_rwkv_out, w_o,
              ffn_w_gate, ffn_w_up, ffn_w_down, w_router, b_router,
              moe_w_gate, moe_w_up, moe_w_down, final_norm):
    n_rows = x_sample.shape[1] // GRID_W
    s_zero = jnp.zeros((x_prompt.shape[0], N_DIR, N_HEADS, HEAD_DIM, HEAD_DIM), jnp.float32)
    xp, xs = x_prompt, x_sample
    ctx_states = []
    for l in range(DEPTH):
        lp = {
            "norm1": norm1[l], "norm2": norm2[l], "w_in": w_in[l],
            "dw_w": dw_w[l], "dw_b": dw_b[l], "conv_ln_g": conv_ln_g[l],
            "conv_ln_b": conv_ln_b[l], "w_conv_out": w_conv_out[l], "w_fno_out": w_fno_out[l],
            "shift_mu": shift_mu[l], "g_up": g_up[l], "dec_w0": dec_w0[l], "dec_up": dec_up[l],
            "iclr_a0": iclr_a0[l], "iclr_up": iclr_up[l], "k_k": k_k[l], "k_a": k_a[l],
            "r_k": r_k[l], "gn_g": gn_g[l], "gn_b": gn_b[l], "w_rwkv_out": w_rwkv_out[l],
            "w_o": w_o[l],
        }
        i = l // 2
        if l % 2 == 0:
            ffn = functools.partial(swiglu, w_gate=ffn_w_gate[i], w_up=ffn_w_up[i],
                                    w_down=ffn_w_down[i])
        else:
            ffn = functools.partial(moe_swiglu, w_router=w_router[i], b_router=b_router[i],
                                    w_gate=moe_w_gate[i], w_up=moe_w_up[i],
                                    w_down=moe_w_down[i])
        mod_p = (jax.nn.silu(c_ctx) @ w_ada[l] + b_ada[l])[None, None, :]
        mod_s = (jax.nn.silu(c) @ w_ada[l] + b_ada[l])[:, None, :]
        xp, s_ctx = trunk_layer(xp, mod_p, s_zero, None, lp, ffn)
        xs, _ = trunk_layer(xs, mod_s, state_rwkv[:, l], n_rows, lp, ffn)
        ctx_states.append(s_ctx)
    y_prompt = rmsnorm(xp, final_norm)
    y_sample = rmsnorm(xs, final_norm)
    new_state_rwkv = jnp.stack(ctx_states, axis=1).astype(x_prompt.dtype)
    return (y_prompt, y_sample, new_state_rwkv)
```

```python
import functools

import numpy as np
import jax
import jax.numpy as jnp
from jax import lax
from jax.experimental import pallas as pl
from jax.experimental.pallas import tpu as pltpu

F32 = jnp.float32
BF16 = jnp.bfloat16

D_MODEL = 1024
GRID_W = 64
C_CONV = 256
CONV_W = 31
C_FNO = 256
FNO_GW = 64
N_HEADS = 8
HEAD_DIM = 64
C_RWKV = N_HEADS * HEAD_DIM
G_RANK = 128
W_RANK = 64
A_RANK = 64
D_FF = 2816
N_EXPERTS = 8
D_FF_E = 1408
RMS_EPS = 1e-6
LN_EPS = 1e-5
GN_EPS = 64e-5

OFF_FNO = 2 * C_CONV
OFF_RWKV = OFF_FNO + C_FNO
N_SHIFT = 3 * C_RWKV + G_RANK + 2 * (W_RANK + A_RANK)
OFF_GATE = OFF_RWKV + N_SHIFT
D_IN = OFF_GATE + 3 * D_MODEL

ROW_TILE = 256
FFN_TILE = 512
CHUNK = 64
HEADS_PER_STEP = 4
CTX_ROW = 4
LANES = 128
VMEM_LIMIT = 56 * 1024 * 1024


def _cparams(sem):
    return pltpu.CompilerParams(dimension_semantics=sem, vmem_limit_bytes=VMEM_LIMIT)


def _sigmoid(x):
    return 1.0 / (1.0 + jnp.exp(-x))


def _silu(x):
    return x * _sigmoid(x)


def _dot(a, b):
    return jnp.dot(a, b, preferred_element_type=F32)


def _split(x):
    hi = x.astype(BF16)
    lo = (x - hi.astype(F32)).astype(BF16)
    return hi, lo


def _dot_x3(a, b):
    ah, al = _split(a)
    bh, bl = _split(b)
    return _dot(ah, bh) + (_dot(ah, bl) + _dot(al, bh))


def _dot_x3_nt(a, b):
    dn = (((1,), (1,)), ((), ()))
    ah, al = _split(a)
    bh, bl = _split(b)
    f = lambda p, q: lax.dot_general(p, q, dn, preferred_element_type=F32)
    return f(ah, bh) + (f(ah, bl) + f(al, bh))


def _seg_sum(x, bd):
    hi, lo = _split(x)
    return _dot(hi, bd) + _dot(lo, bd)


def _mod_row(i, tile, n_ctx_rows, seq_s):
    n_ctx_tiles = n_ctx_rows // tile
    return jnp.where(i < n_ctx_tiles, CTX_ROW, (i - n_ctx_tiles) // (seq_s // tile))


def _ada_kernel(c_ref, w_ref, b_ref, o_ref):
    s = _silu(c_ref[...])
    o_ref[0] = _dot(s.astype(BF16), w_ref[0].astype(BF16)) + b_ref[0]


def _ada(cond, w_ada, b_ada):
    n_l = w_ada.shape[0]
    tn = 1536
    return pl.pallas_call(
        _ada_kernel,
        grid=(n_l, 6 * D_MODEL // tn),
        in_specs=[
            pl.BlockSpec((8, D_MODEL), lambda l, j: (0, 0)),
            pl.BlockSpec((1, D_MODEL, tn), lambda l, j: (l, 0, j)),
            pl.BlockSpec((1, 1, tn), lambda l, j: (l, 0, j)),
        ],
        out_specs=pl.BlockSpec((1, 8, tn), lambda l, j: (l, 0, j)),
        out_shape=jax.ShapeDtypeStruct((n_l, 8, 6 * D_MODEL), F32),
        compiler_params=_cparams(("parallel", "parallel")),
        name="ada",
    )(cond, w_ada, b_ada.reshape(n_l, 1, 6 * D_MODEL))


def _inproj_kernel(x_ref, mod_ref, g_ref, w_ref, oc_ref, of_ref, or_ref, og_ref, *, n_ctx, seq_s):
    row = _mod_row(pl.program_id(0), ROW_TILE, n_ctx, seq_s)
    m = mod_ref[pl.ds(row, 1), :]
    sh = m[:, 0:D_MODEL]
    sc = m[:, D_MODEL:2 * D_MODEL]
    x = x_ref[...]
    y = x * lax.rsqrt(jnp.mean(x * x, axis=-1, keepdims=True) + RMS_EPS) * g_ref[...]
    h = (y * (1.0 + sc) + sh).astype(BF16)
    oc_ref[...] = _dot(h, w_ref[:, 0:OFF_FNO])
    of_ref[...] = _dot(h, w_ref[:, OFF_FNO:OFF_RWKV])
    or_ref[...] = _dot(h, w_ref[:, OFF_RWKV:OFF_GATE])
    og_ref[...] = _dot(h, w_ref[:, OFF_GATE:D_IN])


def _inproj(x, mod, gain, w_in, n_ctx, seq_s):
    n = x.shape[0]
    widths = (OFF_FNO, C_FNO, N_SHIFT, 3 * D_MODEL)
    return pl.pallas_call(
        functools.partial(_inproj_kernel, n_ctx=n_ctx, seq_s=seq_s),
        grid=(n // ROW_TILE,),
        in_specs=[
            pl.BlockSpec((ROW_TILE, D_MODEL), lambda i: (i, 0)),
            pl.BlockSpec((8, 6 * D_MODEL), lambda i: (0, 0)),
            pl.BlockSpec((1, D_MODEL), lambda i: (0, 0)),
            pl.BlockSpec((D_MODEL, D_IN), lambda i: (0, 0)),
        ],
        out_specs=[pl.BlockSpec((ROW_TILE, w), lambda i: (i, 0)) for w in widths],
        out_shape=[jax.ShapeDtypeStruct((n, w), F32) for w in widths],
        compiler_params=_cparams(("parallel",)),
        name="inproj",
    )(x, mod, gain.reshape(1, D_MODEL), w_in)


_CONV_HALO = 16
_CONV_ROWS = 64


def _conv_kernel(u_ref, w_ref, b_ref, g_ref, be_ref, o_ref, pad_ref, *, n_ctx):
    is_ctx = pl.program_id(0) < n_ctx // ROW_TILE
    zeros = jnp.zeros((_CONV_HALO, C_CONV), F32)
    n_parts = ROW_TILE // _CONV_ROWS

    def glu(lo, hi):
        return u_ref[lo:hi, 0:C_CONV] * _sigmoid(u_ref[lo:hi, C_CONV:2 * C_CONV])

    def finish(starts):
        for p in range(n_parts):
            acc = jnp.zeros((_CONV_ROWS, C_CONV), F32)
            for j in range(CONV_W):
                acc = acc + w_ref[j:j + 1, :] * pad_ref[starts[p] + j:starts[p] + j + _CONV_ROWS, :]
            acc = acc + b_ref[...]
            mu = jnp.mean(acc, axis=-1, keepdims=True)
            d = acc - mu
            var = jnp.mean(d * d, axis=-1, keepdims=True)
            y = d * lax.rsqrt(var + LN_EPS) * g_ref[...] + be_ref[...]
            o_ref[p * _CONV_ROWS:(p + 1) * _CONV_ROWS, :] = _silu(y).astype(o_ref.dtype)

    shift = _CONV_HALO - CONV_W // 2

    @pl.when(is_ctx)
    def _():
        pad_ref[0:_CONV_HALO, :] = zeros
        pad_ref[_CONV_HALO:_CONV_HALO + ROW_TILE, :] = glu(0, ROW_TILE)
        pad_ref[_CONV_HALO + ROW_TILE:2 * _CONV_HALO + ROW_TILE, :] = zeros
        finish([shift + p * _CONV_ROWS for p in range(n_parts)])

    @pl.when(jnp.logical_not(is_ctx))
    def _():
        stride = GRID_W + 2 * _CONV_HALO
        for p in range(n_parts):
            pad_ref[p * stride:p * stride + _CONV_HALO, :] = zeros
            pad_ref[p * stride + _CONV_HALO:p * stride + _CONV_HALO + GRID_W, :] = glu(p * GRID_W, (p + 1) * GRID_W)
            pad_ref[p * stride + _CONV_HALO + GRID_W:(p + 1) * stride, :] = zeros
        finish([p * stride + shift for p in range(n_parts)])


def _conv_branch(u_conv, dw_w, dw_b, ln_g, ln_b, n_ctx):
    n = u_conv.shape[0]
    assert _CONV_ROWS == GRID_W and ROW_TILE % GRID_W == 0
    vec = pl.BlockSpec((1, C_CONV), lambda i: (0, 0))
    return pl.pallas_call(
        functools.partial(_conv_kernel, n_ctx=n_ctx),
        grid=(n // ROW_TILE,),
        in_specs=[
            pl.BlockSpec((ROW_TILE, 2 * C_CONV), lambda i: (i, 0)),
            pl.BlockSpec((CONV_W, C_CONV), lambda i: (0, 0)),
            vec, vec, vec,
        ],
        out_specs=pl.BlockSpec((ROW_TILE, C_CONV), lambda i: (i, 0)),
        out_shape=jax.ShapeDtypeStruct((n, C_CONV), BF16),
        scratch_shapes=[pltpu.VMEM((ROW_TILE // GRID_W * (GRID_W + 2 * _CONV_HALO), C_CONV), F32)],
        compiler_params=_cparams(("parallel",)),
        name="conv_branch",
    )(u_conv, dw_w, dw_b.reshape(1, C_CONV), ln_g.reshape(1, C_CONV), ln_b.reshape(1, C_CONV))


def _dft_tables(t_len):
    def cs(n):
        k = np.arange(n, dtype=np.int64)
        ang = 2.0 * np.pi * ((k[:, None] * k[None, :]) % n).astype(np.float64) / n
        return np.cos(ang), np.sin(ang)
    cg, sg = cs(FNO_GW)
    eye = np.eye(C_FNO // FNO_GW)
    w1 = np.concatenate([np.kron(eye, cg), np.kron(eye, sg)], axis=1)
    ct, st = cs(t_len)
    w2 = np.concatenate([ct, -st], axis=1)
    return jnp.asarray(w1, dtype=F32).astype(BF16), jnp.asarray(w2, dtype=F32).astype(BF16)


def _fno_kernel(u_ref, w1_ref, w2_ref, o_ref, hs_ref, *, t_len, scale):
    @pl.when(pl.program_id(1) == 0)
    def _():
        hc = _dot(u_ref[...].astype(BF16), w1_ref[...])
        hs_ref[0:t_len, :] = hc[:, 0:C_FNO].astype(BF16)
        hs_ref[t_len:2 * t_len, :] = hc[:, C_FNO:2 * C_FNO].astype(BF16)

    o_ref[...] = (_dot(w2_ref[...], hs_ref[...]) * scale).astype(o_ref.dtype)


def _fno_branch(u_fno, row0, n_seq, t_len):
    w1, w2 = _dft_tables(t_len)
    tk = min(t_len, 512)
    blk0 = row0 // t_len
    return pl.pallas_call(
        functools.partial(_fno_kernel, t_len=t_len, scale=float(1.0 / np.sqrt(t_len * FNO_GW))),
        grid=(n_seq, t_len // tk),
        in_specs=[
            pl.BlockSpec((t_len, C_FNO), lambda b, j: (blk0 + b, 0)),
            pl.BlockSpec((C_FNO, 2 * C_FNO), lambda b, j: (0, 0)),
            pl.BlockSpec((tk, 2 * t_len), lambda b, j: (j, 0)),
        ],
        out_specs=pl.BlockSpec((tk, C_FNO), lambda b, j: (b * (t_len // tk) + j, 0)),
        out_shape=jax.ShapeDtypeStruct((n_seq * t_len, C_FNO), BF16),
        scratch_shapes=[pltpu.VMEM((2 * t_len, C_FNO), BF16)],
        compiler_params=_cparams(("parallel", "arbitrary")),
        name="fno_branch",
    )(u_fno, w1, w2)


def _prep_kernel(z_ref, zp_ref, zn_ref, mu_ref, gup_ref, w0_ref, dup_ref, a0_ref, aup_ref, kk_ref, bd_ref,
                 r_ref, k_ref, v_ref, kkn_ref, g_ref, lw_ref, a_ref, pad_ref, *, n_ctx, seq_s):
    i = pl.program_id(0)
    n_ctx_tiles = n_ctx // ROW_TILE
    per_seq = seq_s // ROW_TILE
    j = (i - n_ctx_tiles) % per_seq
    first = jnp.logical_or(i < n_ctx_tiles, j == 0)
    last = jnp.logical_or(i < n_ctx_tiles, j == per_seq - 1)
    pad_ref[8:8 + ROW_TILE, :] = z_ref[...]
    pad_ref[0:8, :] = jnp.where(first, 0.0, zp_ref[...])
    pad_ref[8 + ROW_TILE:16 + ROW_TILE, :] = jnp.where(last, 0.0, zn_ref[...])

    def shifted(lo, hi):
        z = pad_ref[8:8 + ROW_TILE, lo:hi]
        zp = pad_ref[7:7 + ROW_TILE, lo:hi]
        zn = pad_ref[9:9 + ROW_TILE, lo:hi]
        return z + mu_ref[:, lo:hi] * (0.5 * (zp + zn) - z)

    c = C_RWKV
    r_ref[...] = shifted(0, c)
    k = shifted(c, 2 * c)
    k_ref[...] = k
    v_ref[...] = shifted(2 * c, 3 * c)
    kx = k * kk_ref[...]
    nrm = jnp.sqrt(_seg_sum(kx * kx, bd_ref[...]))
    kkn_ref[...] = kx / jnp.maximum(nrm, 1e-12)
    o = 3 * c
    g_ref[...] = _dot(_sigmoid(shifted(o, o + G_RANK)).astype(BF16), gup_ref[...])
    o += G_RANK
    xw = _dot(jnp.tanh(shifted(o, o + 2 * W_RANK)).astype(BF16), dup_ref[...]) + w0_ref[...]
    soft = jnp.maximum(-xw, 0.0) + jnp.log(1.0 + jnp.exp(-jnp.abs(xw)))
    lw_ref[...] = -jnp.exp(-soft - 0.5)
    o += 2 * W_RANK
    xa = _dot(shifted(o, o + 2 * A_RANK).astype(BF16), aup_ref[...]) + a0_ref[...]
    a_ref[...] = _sigmoid(xa)


def _block_diag2(w):
    z = jnp.zeros_like(w[0])
    return jnp.concatenate([jnp.concatenate([w[0], z], axis=1), jnp.concatenate([z, w[1]], axis=1)], axis=0)


def _head_ones():
    return jnp.asarray(np.kron(np.eye(N_HEADS), np.ones((HEAD_DIM, HEAD_DIM))), dtype=BF16)


def _rwkv_prep(u_rw, mu, g_up, dec_w0, dec_up, iclr_a0, iclr_up, k_k, n_ctx, seq_s):
    n = u_rw.shape[0]
    halo = ROW_TILE // 8
    last_blk = n // 8 - 1
    full = lambda shape: pl.BlockSpec(shape, lambda i: (0,) * len(shape))
    c = C_RWKV
    out_w = (c, c, c, c, c, 2 * c, 2 * c)
    return pl.pallas_call(
        functools.partial(_prep_kernel, n_ctx=n_ctx, seq_s=seq_s),
        grid=(n // ROW_TILE,),
        in_specs=[
            pl.BlockSpec((ROW_TILE, N_SHIFT), lambda i: (i, 0)),
            pl.BlockSpec((8, N_SHIFT), lambda i: (jnp.maximum(i * halo - 1, 0), 0)),
            pl.BlockSpec((8, N_SHIFT), lambda i: (jnp.minimum((i + 1) * halo, last_blk), 0)),
            full((1, N_SHIFT)), full((G_RANK, c)), full((1, 2 * c)), full((2 * W_RANK, 2 * c)),
            full((1, 2 * c)), full((2 * A_RANK, 2 * c)), full((1, c)), full((c, c)),
        ],
        out_specs=[pl.BlockSpec((ROW_TILE, w), lambda i: (i, 0)) for w in out_w],
        out_shape=[jax.ShapeDtypeStruct((n, w), F32) for w in out_w],
        scratch_shapes=[pltpu.VMEM((ROW_TILE + 16, N_SHIFT), F32)],
        compiler_params=_cparams(("parallel",)),
        name="rwkv_prep",
    )(u_rw, u_rw, u_rw, mu.reshape(1, N_SHIFT), g_up.astype(BF16), dec_w0.reshape(1, 2 * c),
      _block_diag2(dec_up).astype(BF16), iclr_a0.reshape(1, 2 * c), _block_diag2(iclr_up).astype(BF16),
      k_k.reshape(1, c), _head_ones())


def _tri_inverse(lab, row, col):
    eye = (row == col).astype(F32)
    pair = jnp.logical_and(row // 2 == col // 2, row != col)
    t = eye + jnp.where(pair, lab, 0.0)
    n = 2
    while n < CHUNK:
        m = jnp.logical_and(row // (2 * n) == col // (2 * n), row // n != col // n)
        t = t + _dot_x3(t, _dot_x3(jnp.where(m, lab, 0.0), t))
        n *= 2
    return t


def _scan_kernel(tab_ref, rf_ref, kf_ref, vf_ref, nf_ref, lwf_ref, af_ref,
                 rb_ref, kb_ref, vb_ref, nb_ref, lwb_ref, ab_ref, ka_ref, s0_ref,
                 yf_ref, yb_ref, sfin_ref, st_ref):
    step = pl.program_id(1)
    is_first = tab_ref[3, step] == 1
    is_last = tab_ref[4, step] == 1
    hp = HEADS_PER_STEP
    hd = HEAD_DIM

    @pl.when(is_first)
    def _():
        st_ref[...] = s0_ref[0]

    row = lax.broadcasted_iota(jnp.int32, (CHUNK, CHUNK), 0)
    col = lax.broadcasted_iota(jnp.int32, (CHUNK, CHUNK), 1)
    ka = ka_ref[...]

    for d, refs in enumerate(((rf_ref, kf_ref, vf_ref, nf_ref, lwf_ref, af_ref, yf_ref),
                              (rb_ref, kb_ref, vb_ref, nb_ref, lwb_ref, ab_ref, yb_ref))):
        r_ref, k_ref, v_ref, n_ref, lw_ref, a_ref, y_ref = refs
        earlier = (col < row) if d == 0 else (col > row)
        upto = jnp.logical_or(earlier, row == col)
        tri = upto.astype(BF16)
        lw = lw_ref[...]
        h1 = lw.astype(BF16)
        r1 = lw - h1.astype(F32)
        h2 = r1.astype(BF16)
        h3 = (r1 - h2.astype(F32)).astype(BF16)
        cum = _dot(tri, h1) + (_dot(tri, h2) + _dot(tri, h3))
        e_in = jnp.exp(cum)
        e_ex = jnp.exp(cum - lw)
        e_ng = jnp.exp(-cum)
        a = a_ref[...]
        kkn = n_ref[...]
        v = v_ref[...]
        kd = k_ref[...] * (1.0 + (a - 1.0) * ka)
        at = -kkn * e_ex
        rt = r_ref[...] * e_in
        bt = kkn * a * e_ng
        kt = kd * e_ng
        end = CHUNK - 1 if d == 0 else 0
        g_end = e_in[end:end + 1, :]
        bh = bt * g_end
        kh = kt * g_end
        for h in range(hp):
            sl = slice(h * hd, (h + 1) * hd)
            s_prev = st_ref[d, h]
            v_h = v[:, sl]
            sc = _dot_x3_nt(jnp.concatenate([at[:, sl], rt[:, sl]], axis=0),
                            jnp.concatenate([bt[:, sl], kt[:, sl]], axis=0))
            lab = jnp.where(earlier, sc[0:CHUNK, 0:CHUNK], 0.0)
            lak = jnp.where(earlier, sc[0:CHUNK, CHUNK:2 * CHUNK], 0.0)
            mrb = jnp.where(upto, sc[CHUNK:2 * CHUNK, 0:CHUNK], 0.0)
            mrk = jnp.where(upto, sc[CHUNK:2 * CHUNK, CHUNK:2 * CHUNK], 0.0)
            t_inv = _tri_inverse(lab, row, col)
            x = _dot_x3_nt(at[:, sl], s_prev) + _dot_x3(lak, v_h)
            u = _dot_x3(t_inv, x)
            y = _dot_x3_nt(rt[:, sl], s_prev) + _dot_x3(mrb, u) + _dot_x3(mrk, v_h)
            y_ref[:, sl] = y
            uv_t = jnp.concatenate([u, v_h], axis=0).T
            s_new = s_prev * g_end[:, sl] + _dot_x3(uv_t, jnp.concatenate([bh[:, sl], kh[:, sl]], axis=0))
            st_ref[d, h] = s_new

    @pl.when(is_last)
    def _():
        sfin_ref[0] = st_ref[...]


def _scan_table(n_ctx_seq, t_ctx, n_s_seq, t_s):
    rows = []
    base = 0
    seq = 0
    for n_seq, t_len in ((n_ctx_seq, t_ctx), (n_s_seq, t_s)):
        n_c = t_len // CHUNK
        for _ in range(n_seq):
            for c in range(n_c):
                rows.append((base + c, base + n_c - 1 - c, seq, int(c == 0), int(c == n_c - 1)))
            base += n_c
            seq += 1
    return np.asarray(rows, dtype=np.int32).T.copy()


def _rwkv_scan(r, k, v, kkn, lw, a, k_a, s0, table):
    n = r.shape[0]
    n_seq = s0.shape[0]
    hp = HEADS_PER_STEP
    w = hp * HEAD_DIM
    n_hg = N_HEADS // hp
    n_steps = table.shape[1]

    def tok(which, lane_off=0):
        return pl.BlockSpec((CHUNK, w), lambda g, s, tab: (tab[which, s], g + lane_off))

    fwd = [tok(0), tok(0), tok(0), tok(0), tok(0), tok(0)]
    bwd = [tok(1), tok(1), tok(1), tok(1), tok(1, n_hg), tok(1, n_hg)]
    state_spec = pl.BlockSpec((1, 2, hp, HEAD_DIM, HEAD_DIM), lambda g, s, tab: (tab[2, s], 0, g, 0, 0))
    grid_spec = pltpu.PrefetchScalarGridSpec(
        num_scalar_prefetch=1,
        grid=(n_hg, n_steps),
        in_specs=fwd + bwd + [pl.BlockSpec((1, w), lambda g, s, tab: (0, g)), state_spec],
        out_specs=[tok(0), tok(1), state_spec],
        scratch_shapes=[pltpu.VMEM((2, hp, HEAD_DIM, HEAD_DIM), F32)],
    )
    return pl.pallas_call(
        _scan_kernel,
        grid_spec=grid_spec,
        out_shape=[jax.ShapeDtypeStruct((n, C_RWKV), F32), jax.ShapeDtypeStruct((n, C_RWKV), F32),
                   jax.ShapeDtypeStruct((n_seq, 2, N_HEADS, HEAD_DIM, HEAD_DIM), F32)],
        compiler_params=_cparams(("parallel", "arbitrary")),
        name="rwkv_scan",
    )(jnp.asarray(table), r, k, v, kkn, lw, a, r, k, v, kkn, lw, a, k_a.reshape(1, C_RWKV), s0)


def _merge_kernel(x_ref, c_ref, f_ref, yf_ref, yb_ref, r_ref, k_ref, v_ref, g_ref, ug_ref, mod_ref,
                  wc_ref, wf_ref, wr_ref, wo_ref, gng_ref, gnb_ref, rk_ref, bd_ref, n2_ref, wrt_ref, brt_ref,
                  x1_ref, h2_ref, comb_ref, *, n_ctx, seq_s):
    row = _mod_row(pl.program_id(0), ROW_TILE, n_ctx, seq_s)
    m = mod_ref[pl.ds(row, 1), :]
    g1 = m[:, 2 * D_MODEL:3 * D_MODEL]
    sh2 = m[:, 3 * D_MODEL:4 * D_MODEL]
    sc2 = m[:, 4 * D_MODEL:5 * D_MODEL]
    bd = bd_ref[...]
    inv = 1.0 / HEAD_DIM
    y = yf_ref[...] + yb_ref[...]
    d = y - _seg_sum(y, bd) * inv
    var = _seg_sum(d * d, bd) * inv
    yn = d * lax.rsqrt(var + GN_EPS) * gng_ref[...] + gnb_ref[...]
    v = v_ref[...]
    yn = yn + _seg_sum(r_ref[...] * k_ref[...] * rk_ref[...], bd) * v
    yr = _dot((yn * g_ref[...]).astype(BF16), wr_ref[...])
    yc = _dot(c_ref[...], wc_ref[...])
    yf = _dot(f_ref[...], wf_ref[...])
    dm = D_MODEL
    merged = (_sigmoid(ug_ref[:, 0:dm]) * yc + _sigmoid(ug_ref[:, dm:2 * dm]) * yf
              + _sigmoid(ug_ref[:, 2 * dm:3 * dm]) * yr)
    x1 = x_ref[...] + g1 * _dot(merged.astype(BF16), wo_ref[...])
    x1_ref[...] = x1
    h2 = x1 * lax.rsqrt(jnp.mean(x1 * x1, axis=-1, keepdims=True) + RMS_EPS) * n2_ref[...]
    h2 = h2 * (1.0 + sc2) + sh2
    h2_ref[...] = h2.astype(BF16)
    logits = _dot_x3(h2, wrt_ref[...]) + brt_ref[...]
    lane = lax.broadcasted_iota(jnp.int32, logits.shape, 1).astype(F32)
    neg = jnp.float32(-jnp.inf)
    lg = jnp.where(lane < N_EXPERTS, logits, neg)
    m1 = jnp.max(lg, axis=-1, keepdims=True)
    i1 = jnp.min(jnp.where(lg == m1, lane, float(LANES)), axis=-1, keepdims=True)
    lg2 = jnp.where(lane == i1, neg, lg)
    m2 = jnp.max(lg2, axis=-1, keepdims=True)
    i2 = jnp.min(jnp.where(lg2 == m2, lane, float(LANES)), axis=-1, keepdims=True)
    e = jnp.exp(m2 - m1)
    p1 = 1.0 / (1.0 + e)
    comb_ref[...] = jnp.where(lane == i1, p1, 0.0) + jnp.where(lane == i2, e * p1, 0.0)


def _merge(x, conv_h, fno_h, yf, yb, r, k, v, g, u_gate, mod, w_conv_out, w_fno_out, w_rwkv_out, w_o,
           gn_g, gn_b, r_k, gain2, w_router, b_router, n_ctx, seq_s):
    n = x.shape[0]
    c = C_RWKV
    tile = lambda w: pl.BlockSpec((ROW_TILE, w), lambda i: (i, 0))
    full = lambda shape: pl.BlockSpec(shape, lambda i: (0,) * len(shape))
    wrt = jnp.zeros((D_MODEL, LANES), F32).at[:, :N_EXPERTS].set(w_router)
    brt = jnp.zeros((1, LANES), F32).at[0, :N_EXPERTS].set(b_router)
    return pl.pallas_call(
        functools.partial(_merge_kernel, n_ctx=n_ctx, seq_s=seq_s),
        grid=(n // ROW_TILE,),
        in_specs=[
            tile(D_MODEL), tile(C_CONV), tile(C_FNO), tile(c), tile(c), tile(c), tile(c), tile(c), tile(c),
            tile(3 * D_MODEL), full((8, 6 * D_MODEL)),
            full((C_CONV, D_MODEL)), full((C_FNO, D_MODEL)), full((c, D_MODEL)), full((D_MODEL, D_MODEL)),
            full((1, c)), full((1, c)), full((1, c)), full((c, c)), full((1, D_MODEL)),
            full((D_MODEL, LANES)), full((1, LANES)),
        ],
        out_specs=[tile(D_MODEL), tile(D_MODEL), tile(LANES)],
        out_shape=[jax.ShapeDtypeStruct((n, D_MODEL), F32), jax.ShapeDtypeStruct((n, D_MODEL), BF16),
                   jax.ShapeDtypeStruct((n, LANES), F32)],
        compiler_params=_cparams(("parallel",)),
        name="merge",
    )(x, conv_h, fno_h, yf, yb, r, k, v, g, u_gate, mod,
      w_conv_out.astype(BF16), w_fno_out.astype(BF16), w_rwkv_out.astype(BF16), w_o.astype(BF16),
      gn_g.reshape(1, c), gn_b.reshape(1, c), r_k.reshape(1, c), _head_ones(), gain2.reshape(1, D_MODEL), wrt, brt)


def _ffn_kernel(x_ref, h_ref, mod_ref, wg_ref, wu_ref, wd_ref, o_ref, *, n_ctx, seq_s):
    row = _mod_row(pl.program_id(0), FFN_TILE, n_ctx, seq_s)
    g2 = mod_ref[pl.ds(row, 1), 5 * D_MODEL:6 * D_MODEL]
    h = h_ref[...]
    half = D_FF // 2
    acc = jnp.zeros((FFN_TILE, D_MODEL), F32)
    for p in range(2):
        sl = slice(p * half, (p + 1) * half)
        t = _silu(_dot(h, wg_ref[:, sl])) * _dot(h, wu_ref[:, sl])
        acc = acc + _dot(t.astype(BF16), wd_ref[sl, :])
    o_ref[...] = x_ref[...] + g2 * acc


def _ffn_dense(x1, h2, mod, w_gate, w_up, w_down, n_ctx, seq_s):
    n = x1.shape[0]
    tile = pl.BlockSpec((FFN_TILE, D_MODEL), lambda i: (i, 0))
    full = lambda shape: pl.BlockSpec(shape, lambda i: (0,) * len(shape))
    return pl.pallas_call(
        functools.partial(_ffn_kernel, n_ctx=n_ctx, seq_s=seq_s),
        grid=(n // FFN_TILE,),
        in_specs=[tile, tile, full((8, 6 * D_MODEL)), full((D_MODEL, D_FF)), full((D_MODEL, D_FF)),
                  full((D_FF, D_MODEL))],
        out_specs=tile,
        out_shape=jax.ShapeDtypeStruct((n, D_MODEL), F32),
        compiler_params=_cparams(("parallel",)),
        name="ffn_dense",
    )(x1, h2, mod, w_gate.astype(BF16), w_up.astype(BF16), w_down.astype(BF16))


def _moe_kernel(x_ref, h_ref, comb_ref, mod_ref, wg_ref, wu_ref, wd_ref, o_ref, acc_ref, *, n_ctx, seq_s):
    e = pl.program_id(1)

    @pl.when(e == 0)
    def _():
        acc_ref[...] = jnp.zeros_like(acc_ref)

    comb = comb_ref[...]
    lane = lax.broadcasted_iota(jnp.int32, comb.shape, 1)
    w_e = jnp.sum(jnp.where(lane == e, comb, 0.0), axis=-1, keepdims=True)
    h = h_ref[...]
    t = _silu(_dot(h, wg_ref[0])) * _dot(h, wu_ref[0])
    acc_ref[...] += w_e * _dot(t.astype(BF16), wd_ref[0])

    @pl.when(e == N_EXPERTS - 1)
    def _():
        row = _mod_row(pl.program_id(0), FFN_TILE, n_ctx, seq_s)
        g2 = mod_ref[pl.ds(row, 1), 5 * D_MODEL:6 * D_MODEL]
        o_ref[...] = x_ref[...] + g2 * acc_ref[...]


def _ffn_moe(x1, h2, comb, mod, w_gate, w_up, w_down, n_ctx, seq_s):
    n = x1.shape[0]
    tile = lambda w: pl.BlockSpec((FFN_TILE, w), lambda i, e: (i, 0))
    return pl.pallas_call(
        functools.partial(_moe_kernel, n_ctx=n_ctx, seq_s=seq_s),
        grid=(n // FFN_TILE, N_EXPERTS),
        in_specs=[tile(D_MODEL), tile(D_MODEL), tile(LANES),
                  pl.BlockSpec((8, 6 * D_MODEL), lambda i, e: (0, 0)),
                  pl.BlockSpec((1, D_MODEL, D_FF_E), lambda i, e: (e, 0, 0)),
                  pl.BlockSpec((1, D_MODEL, D_FF_E), lambda i, e: (e, 0, 0)),
                  pl.BlockSpec((1, D_FF_E, D_MODEL), lambda i, e: (e, 0, 0))],
        out_specs=tile(D_MODEL),
        out_shape=jax.ShapeDtypeStruct((n, D_MODEL), F32),
        scratch_shapes=[pltpu.VMEM((FFN_TILE, D_MODEL), F32)],
        compiler_params=_cparams(("parallel", "arbitrary")),
        name="ffn_moe",
    )(x1, h2, comb, mod, w_gate.astype(BF16), w_up.astype(BF16), w_down.astype(BF16))


def _final_kernel(x_ref, g_ref, o_ref):
    x = x_ref[...]
    o_ref[...] = x * lax.rsqrt(jnp.mean(x * x, axis=-1, keepdims=True) + RMS_EPS) * g_ref[...]


def _final_norm(x, gain, row0, n_rows):
    blk0 = row0 // FFN_TILE
    return pl.pallas_call(
        _final_kernel,
        grid=(n_rows // FFN_TILE,),
        in_specs=[pl.BlockSpec((FFN_TILE, D_MODEL), lambda i: (blk0 + i, 0)),
                  pl.BlockSpec((1, D_MODEL), lambda i: (0, 0))],
        out_specs=pl.BlockSpec((FFN_TILE, D_MODEL), lambda i: (i, 0)),
        out_shape=jax.ShapeDtypeStruct((n_rows, D_MODEL), F32),
        compiler_params=_cparams(("parallel",)),
        name="final_norm",
    )(x, gain.reshape(1, D_MODEL))


def kernel(x_prompt, x_sample, state_rwkv, c, c_ctx, norm1, norm2, w_ada, b_ada, w_in, dw_w, dw_b, conv_ln_g, conv_ln_b, w_conv_out, w_fno_out, shift_mu, g_up, dec_w0, dec_up, iclr_a0, iclr_up, k_k, k_a, r_k, gn_g, gn_b, w_rwkv_out, w_o, ffn_w_gate, ffn_w_up, ffn_w_down, w_router, b_router, moe_w_gate, moe_w_up, moe_w_down, final_norm):
    b_p, t_p, _ = x_prompt.shape
    b_s, t_s, _ = x_sample.shape
    depth = w_in.shape[0]
    n_ctx = b_p * t_p
    n_lat = b_s * t_s
    assert t_p == ROW_TILE and t_s % FFN_TILE == 0 and n_ctx % FFN_TILE == 0 and b_s <= CTX_ROW
    assert t_s % GRID_W == 0 and CHUNK == GRID_W

    x = jnp.concatenate([x_prompt.reshape(n_ctx, D_MODEL), x_sample.reshape(n_lat, D_MODEL)], axis=0)
    cond = jnp.zeros((8, D_MODEL), F32).at[:b_s].set(c).at[CTX_ROW].set(c_ctx)
    mods = _ada(cond, w_ada, b_ada)
    table = _scan_table(b_p, t_p, b_s, t_s)
    zero_state = jnp.zeros((b_p, 2, N_HEADS, HEAD_DIM, HEAD_DIM), F32)

    ctx_states = []
    for l in range(depth):
        mod = mods[l]
        u_conv, u_fno, u_rw, u_gate = _inproj(x, mod, norm1[l], w_in[l].astype(BF16), n_ctx, t_s)
        conv_h = _conv_branch(u_conv, dw_w[l], dw_b[l], conv_ln_g[l], conv_ln_b[l], n_ctx)
        fno_h = jnp.concatenate([_fno_branch(u_fno, 0, b_p, t_p), _fno_branch(u_fno, n_ctx, b_s, t_s)], axis=0)
        r, k, v, kkn, g, lw, a = _rwkv_prep(u_rw, shift_mu[l], g_up[l], dec_w0[l], dec_up[l], iclr_a0[l],
                                             iclr_up[l], k_k[l], n_ctx, t_s)
        s0 = jnp.concatenate([zero_state, state_rwkv[:, l]], axis=0)
        yf, yb, s_fin = _rwkv_scan(r, k, v, kkn, lw, a, k_a[l], s0, table)
        ctx_states.append(s_fin[:b_p])
        i = l // 2
        if l % 2 == 0:
            w_rt, b_rt = jnp.zeros((D_MODEL, N_EXPERTS), F32), jnp.zeros((N_EXPERTS,), F32)
        else:
            w_rt, b_rt = w_router[i], b_router[i]
        x1, h2, comb = _merge(x, conv_h, fno_h, yf, yb, r, k, v, g, u_gate, mod, w_conv_out[l], w_fno_out[l],
                              w_rwkv_out[l], w_o[l], gn_g[l], gn_b[l], r_k[l], norm2[l], w_rt, b_rt, n_ctx, t_s)
        if l % 2 == 0:
            x = _ffn_dense(x1, h2, mod, ffn_w_gate[i], ffn_w_up[i], ffn_w_down[i], n_ctx, t_s)
        else:
            x = _ffn_moe(x1, h2, comb, mod, moe_w_gate[i], moe_w_up[i], moe_w_down[i], n_ctx, t_s)

    y_prompt = _final_norm(x, final_norm, 0, n_ctx).reshape(b_p, t_p, D_MODEL)
    y_sample = _final_norm(x, final_norm, n_ctx, n_lat).reshape(b_s, t_s, D_MODEL)
    new_state = jnp.stack(ctx_states, axis=1).astype(x_prompt.dtype)
    return (y_prompt, y_sample, new_state)
```

```python
import functools

import numpy as np
import jax
import jax.numpy as jnp
from jax import lax
from jax.experimental import pallas as pl
from jax.experimental.pallas import tpu as pltpu

F32 = jnp.float32
BF16 = jnp.bfloat16

D_MODEL = 1024
GRID_W = 64
C_CONV = 256
CONV_W = 31
C_FNO = 256
FNO_GW = 64
N_HEADS = 8
HEAD_DIM = 64
C_RWKV = N_HEADS * HEAD_DIM
G_RANK = 128
W_RANK = 64
A_RANK = 64
D_FF = 2816
N_EXPERTS = 8
D_FF_E = 1408
RMS_EPS = 1e-6
LN_EPS = 1e-5
GN_EPS = 64e-5

OFF_FNO = 2 * C_CONV
OFF_RWKV = OFF_FNO + C_FNO
N_SHIFT = 3 * C_RWKV + G_RANK + 2 * (W_RANK + A_RANK)
OFF_GATE = OFF_RWKV + N_SHIFT
D_IN = OFF_GATE + 3 * D_MODEL

ROW_TILE = 256
FFN_TILE = 512
CHUNK = 64
HEADS_PER_STEP = 8
CTX_ROW = 4
LANES = 128
VMEM_LIMIT = 56 * 1024 * 1024


def _cparams(sem):
    return pltpu.CompilerParams(dimension_semantics=sem, vmem_limit_bytes=VMEM_LIMIT)


def _sigmoid(x):
    return 1.0 / (1.0 + jnp.exp(-x))


def _silu(x):
    return x * _sigmoid(x)


def _dot(a, b):
    return jnp.dot(a, b, preferred_element_type=F32)


def _split(x):
    hi = x.astype(BF16)
    lo = (x - hi.astype(F32)).astype(BF16)
    return hi, lo


def _dot_x3(a, b):
    ah, al = _split(a)
    bh, bl = _split(b)
    return _dot(ah, bh) + (_dot(ah, bl) + _dot(al, bh))


def _seg_sum(x, bd):
    hi, lo = _split(x)
    return _dot(hi, bd) + _dot(lo, bd)


def _mod_row(i, tile, n_ctx_rows, seq_s):
    n_ctx_tiles = n_ctx_rows // tile
    return jnp.where(i < n_ctx_tiles, CTX_ROW, (i - n_ctx_tiles) // (seq_s // tile))


def _ada_kernel(c_ref, w_ref, b_ref, o_ref):
    s = _silu(c_ref[...])
    o_ref[0] = _dot(s.astype(BF16), w_ref[0].astype(BF16)) + b_ref[0]


def _ada(cond, w_ada, b_ada):
    n_l = w_ada.shape[0]
    tn = 1536
    return pl.pallas_call(
        _ada_kernel,
        grid=(n_l, 6 * D_MODEL // tn),
        in_specs=[
            pl.BlockSpec((8, D_MODEL), lambda l, j: (0, 0)),
            pl.BlockSpec((1, D_MODEL, tn), lambda l, j: (l, 0, j)),
            pl.BlockSpec((1, 1, tn), lambda l, j: (l, 0, j)),
        ],
        out_specs=pl.BlockSpec((1, 8, tn), lambda l, j: (l, 0, j)),
        out_shape=jax.ShapeDtypeStruct((n_l, 8, 6 * D_MODEL), F32),
        compiler_params=_cparams(("parallel", "parallel")),
        name="ada",
    )(cond, w_ada, b_ada.reshape(n_l, 1, 6 * D_MODEL))


def _inproj_kernel(x_ref, mod_ref, g_ref, w_ref, oc_ref, of_ref, or_ref, og_ref, *, n_ctx, seq_s):
    row = _mod_row(pl.program_id(0), ROW_TILE, n_ctx, seq_s)
    m = mod_ref[pl.ds(row, 1), :]
    sh = m[:, 0:D_MODEL]
    sc = m[:, D_MODEL:2 * D_MODEL]
    x = x_ref[...]
    y = x * lax.rsqrt(jnp.mean(x * x, axis=-1, keepdims=True) + RMS_EPS) * g_ref[...]
    h = (y * (1.0 + sc) + sh).astype(BF16)
    oc_ref[...] = _dot(h, w_ref[:, 0:OFF_FNO])
    of_ref[...] = _dot(h, w_ref[:, OFF_FNO:OFF_RWKV])
    or_ref[...] = _dot(h, w_ref[:, OFF_RWKV:OFF_GATE])
    og_ref[...] = _dot(h, w_ref[:, OFF_GATE:D_IN])


def _inproj(x, mod, gain, w_in, n_ctx, seq_s):
    n = x.shape[0]
    widths = (OFF_FNO, C_FNO, N_SHIFT, 3 * D_MODEL)
    return pl.pallas_call(
        functools.partial(_inproj_kernel, n_ctx=n_ctx, seq_s=seq_s),
        grid=(n // ROW_TILE,),
        in_specs=[
            pl.BlockSpec((ROW_TILE, D_MODEL), lambda i: (i, 0)),
            pl.BlockSpec((8, 6 * D_MODEL), lambda i: (0, 0)),
            pl.BlockSpec((1, D_MODEL), lambda i: (0, 0)),
            pl.BlockSpec((D_MODEL, D_IN), lambda i: (0, 0)),
        ],
        out_specs=[pl.BlockSpec((ROW_TILE, w), lambda i: (i, 0)) for w in widths],
        out_shape=[jax.ShapeDtypeStruct((n, w), F32) for w in widths],
        compiler_params=_cparams(("parallel",)),
        name="inproj",
    )(x, mod, gain.reshape(1, D_MODEL), w_in)


_CONV_HALO = 16
_CONV_ROWS = 64


def _conv_kernel(u_ref, w_ref, b_ref, g_ref, be_ref, o_ref, pad_ref, *, n_ctx):
    is_ctx = pl.program_id(0) < n_ctx // ROW_TILE
    zeros = jnp.zeros((_CONV_HALO, C_CONV), F32)
    n_parts = ROW_TILE // _CONV_ROWS

    def glu(lo, hi):
        return u_ref[lo:hi, 0:C_CONV] * _sigmoid(u_ref[lo:hi, C_CONV:2 * C_CONV])

    def finish(starts):
        for p in range(n_parts):
            acc = jnp.zeros((_CONV_ROWS, C_CONV), F32)
            for j in range(CONV_W):
                acc = acc + w_ref[j:j + 1, :] * pad_ref[starts[p] + j:starts[p] + j + _CONV_ROWS, :]
            acc = acc + b_ref[...]
            mu = jnp.mean(acc, axis=-1, keepdims=True)
            d = acc - mu
            var = jnp.mean(d * d, axis=-1, keepdims=True)
            y = d * lax.rsqrt(var + LN_EPS) * g_ref[...] + be_ref[...]
            o_ref[p * _CONV_ROWS:(p + 1) * _CONV_ROWS, :] = _silu(y).astype(o_ref.dtype)

    shift = _CONV_HALO - CONV_W // 2

    @pl.when(is_ctx)
    def _():
        pad_ref[0:_CONV_HALO, :] = zeros
        pad_ref[_CONV_HALO:_CONV_HALO + ROW_TILE, :] = glu(0, ROW_TILE)
        pad_ref[_CONV_HALO + ROW_TILE:2 * _CONV_HALO + ROW_TILE, :] = zeros
        finish([shift + p * _CONV_ROWS for p in range(n_parts)])

    @pl.when(jnp.logical_not(is_ctx))
    def _():
        stride = GRID_W + 2 * _CONV_HALO
        for p in range(n_parts):
            pad_ref[p * stride:p * stride + _CONV_HALO, :] = zeros
            pad_ref[p * stride + _CONV_HALO:p * stride + _CONV_HALO + GRID_W, :] = glu(p * GRID_W, (p + 1) * GRID_W)
            pad_ref[p * stride + _CONV_HALO + GRID_W:(p + 1) * stride, :] = zeros
        finish([p * stride + shift for p in range(n_parts)])


def _conv_branch(u_conv, dw_w, dw_b, ln_g, ln_b, n_ctx):
    n = u_conv.shape[0]
    assert _CONV_ROWS == GRID_W and ROW_TILE % GRID_W == 0
    vec = pl.BlockSpec((1, C_CONV), lambda i: (0, 0))
    return pl.pallas_call(
        functools.partial(_conv_kernel, n_ctx=n_ctx),
        grid=(n // ROW_TILE,),
        in_specs=[
            pl.BlockSpec((ROW_TILE, 2 * C_CONV), lambda i: (i, 0)),
            pl.BlockSpec((CONV_W, C_CONV), lambda i: (0, 0)),
            vec, vec, vec,
        ],
        out_specs=pl.BlockSpec((ROW_TILE, C_CONV), lambda i: (i, 0)),
        out_shape=jax.ShapeDtypeStruct((n, C_CONV), BF16),
        scratch_shapes=[pltpu.VMEM((ROW_TILE // GRID_W * (GRID_W + 2 * _CONV_HALO), C_CONV), F32)],
        compiler_params=_cparams(("parallel",)),
        name="conv_branch",
    )(u_conv, dw_w, dw_b.reshape(1, C_CONV), ln_g.reshape(1, C_CONV), ln_b.reshape(1, C_CONV))


def _dft_tables(t_len):
    def cs(n):
        k = np.arange(n, dtype=np.int64)
        ang = 2.0 * np.pi * ((k[:, None] * k[None, :]) % n).astype(np.float64) / n
        return np.cos(ang), np.sin(ang)
    cg, sg = cs(FNO_GW)
    eye = np.eye(C_FNO // FNO_GW)
    w1 = np.concatenate([np.kron(eye, cg), np.kron(eye, sg)], axis=1)
    ct, st = cs(t_len)
    w2 = np.concatenate([ct, -st], axis=1)
    return jnp.asarray(w1, dtype=F32).astype(BF16), jnp.asarray(w2, dtype=F32).astype(BF16)


def _fno_kernel(u_ref, w1_ref, w2_ref, o_ref, hs_ref, *, t_len, scale):
    @pl.when(pl.program_id(1) == 0)
    def _():
        hc = _dot(u_ref[...].astype(BF16), w1_ref[...])
        hs_ref[0:t_len, :] = hc[:, 0:C_FNO].astype(BF16)
        hs_ref[t_len:2 * t_len, :] = hc[:, C_FNO:2 * C_FNO].astype(BF16)

    o_ref[...] = (_dot(w2_ref[...], hs_ref[...]) * scale).astype(o_ref.dtype)


def _fno_branch(u_fno, row0, n_seq, t_len):
    w1, w2 = _dft_tables(t_len)
    tk = min(t_len, 512)
    blk0 = row0 // t_len
    return pl.pallas_call(
        functools.partial(_fno_kernel, t_len=t_len, scale=float(1.0 / np.sqrt(t_len * FNO_GW))),
        grid=(n_seq, t_len // tk),
        in_specs=[
            pl.BlockSpec((t_len, C_FNO), lambda b, j: (blk0 + b, 0)),
            pl.BlockSpec((C_FNO, 2 * C_FNO), lambda b, j: (0, 0)),
            pl.BlockSpec((tk, 2 * t_len), lambda b, j: (j, 0)),
        ],
        out_specs=pl.BlockSpec((tk, C_FNO), lambda b, j: (b * (t_len // tk) + j, 0)),
        out_shape=jax.ShapeDtypeStruct((n_seq * t_len, C_FNO), BF16),
        scratch_shapes=[pltpu.VMEM((2 * t_len, C_FNO), BF16)],
        compiler_params=_cparams(("parallel", "arbitrary")),
        name="fno_branch",
    )(u_fno, w1, w2)


def _prep_kernel(z_ref, zp_ref, zn_ref, mu_ref, gup_ref, w0_ref, dup_ref, a0_ref, aup_ref, kk_ref, bd_ref,
                 r_ref, k_ref, v_ref, kkn_ref, g_ref, lw_ref, a_ref, pad_ref, *, n_ctx, seq_s):
    i = pl.program_id(0)
    n_ctx_tiles = n_ctx // ROW_TILE
    per_seq = seq_s // ROW_TILE
    j = (i - n_ctx_tiles) % per_seq
    first = jnp.logical_or(i < n_ctx_tiles, j == 0)
    last = jnp.logical_or(i < n_ctx_tiles, j == per_seq - 1)
    pad_ref[8:8 + ROW_TILE, :] = z_ref[...]
    pad_ref[0:8, :] = jnp.where(first, 0.0, zp_ref[...])
    pad_ref[8 + ROW_TILE:16 + ROW_TILE, :] = jnp.where(last, 0.0, zn_ref[...])

    def shifted(lo, hi):
        z = pad_ref[8:8 + ROW_TILE, lo:hi]
        zp = pad_ref[7:7 + ROW_TILE, lo:hi]
        zn = pad_ref[9:9 + ROW_TILE, lo:hi]
        return z + mu_ref[:, lo:hi] * (0.5 * (zp + zn) - z)

    c = C_RWKV
    r_ref[...] = shifted(0, c)
    k = shifted(c, 2 * c)
    k_ref[...] = k
    v_ref[...] = shifted(2 * c, 3 * c)
    kx = k * kk_ref[...]
    nrm = jnp.sqrt(_seg_sum(kx * kx, bd_ref[...]))
    kkn_ref[...] = kx / jnp.maximum(nrm, 1e-12)
    o = 3 * c
    g_ref[...] = _dot(_sigmoid(shifted(o, o + G_RANK)).astype(BF16), gup_ref[...])
    o += G_RANK
    xw = _dot(jnp.tanh(shifted(o, o + 2 * W_RANK)).astype(BF16), dup_ref[...]) + w0_ref[...]
    soft = jnp.maximum(-xw, 0.0) + jnp.log(1.0 + jnp.exp(-jnp.abs(xw)))
    lw_ref[...] = -jnp.exp(-soft - 0.5)
    o += 2 * W_RANK
    xa = _dot(shifted(o, o + 2 * A_RANK).astype(BF16), aup_ref[...]) + a0_ref[...]
    a_ref[...] = _sigmoid(xa)


def _block_diag2(w):
    z = jnp.zeros_like(w[0])
    return jnp.concatenate([jnp.concatenate([w[0], z], axis=1), jnp.concatenate([z, w[1]], axis=1)], axis=0)


def _head_ones():
    return jnp.asarray(np.kron(np.eye(N_HEADS), np.ones((HEAD_DIM, HEAD_DIM))), dtype=BF16)


def _rwkv_prep(u_rw, mu, g_up, dec_w0, dec_up, iclr_a0, iclr_up, k_k, n_ctx, seq_s):
    n = u_rw.shape[0]
    halo = ROW_TILE // 8
    last_blk = n // 8 - 1
    full = lambda shape: pl.BlockSpec(shape, lambda i: (0,) * len(shape))
    c = C_RWKV
    out_w = (c, c, c, c, c, 2 * c, 2 * c)
    return pl.pallas_call(
        functools.partial(_prep_kernel, n_ctx=n_ctx, seq_s=seq_s),
        grid=(n // ROW_TILE,),
        in_specs=[
            pl.BlockSpec((ROW_TILE, N_SHIFT), lambda i: (i, 0)),
            pl.BlockSpec((8, N_SHIFT), lambda i: (jnp.maximum(i * halo - 1, 0), 0)),
            pl.BlockSpec((8, N_SHIFT), lambda i: (jnp.minimum((i + 1) * halo, last_blk), 0)),
            full((1, N_SHIFT)), full((G_RANK, c)), full((1, 2 * c)), full((2 * W_RANK, 2 * c)),
            full((1, 2 * c)), full((2 * A_RANK, 2 * c)), full((1, c)), full((c, c)),
        ],
        out_specs=[pl.BlockSpec((ROW_TILE, w), lambda i: (i, 0)) for w in out_w],
        out_shape=[jax.ShapeDtypeStruct((n, w), F32) for w in out_w],
        scratch_shapes=[pltpu.VMEM((ROW_TILE + 16, N_SHIFT), F32)],
        compiler_params=_cparams(("parallel",)),
        name="rwkv_prep",
    )(u_rw, u_rw, u_rw, mu.reshape(1, N_SHIFT), g_up.astype(BF16), dec_w0.reshape(1, 2 * c),
      _block_diag2(dec_up).astype(BF16), iclr_a0.reshape(1, 2 * c), _block_diag2(iclr_up).astype(BF16),
      k_k.reshape(1, c), _head_ones())


def _bdot(a, b):
    return _dot(a.astype(BF16), b.astype(BF16))


def _bdot_nt(a, b):
    return lax.dot_general(a.astype(BF16), b.astype(BF16), (((1,), (1,)), ((), ())), preferred_element_type=F32)


def _scan_kernel(tab_ref, rf_ref, kf_ref, vf_ref, nf_ref, lwf_ref, af_ref,
                 rb_ref, kb_ref, vb_ref, nb_ref, lwb_ref, ab_ref, ka_ref, s0_ref,
                 yf_ref, yb_ref, sfin_ref, st_ref):
    step = pl.program_id(1)
    is_first = tab_ref[3, step] == 1
    is_last = tab_ref[4, step] == 1
    hd = HEAD_DIM

    @pl.when(is_first)
    def _():
        st_ref[...] = s0_ref[0]

    row = lax.broadcasted_iota(jnp.int32, (CHUNK, CHUNK), 0)
    col = lax.broadcasted_iota(jnp.int32, (CHUNK, CHUNK), 1)
    ka = ka_ref[...]

    chains = []
    for d, refs in enumerate(((rf_ref, kf_ref, vf_ref, nf_ref, lwf_ref, af_ref, yf_ref),
                              (rb_ref, kb_ref, vb_ref, nb_ref, lwb_ref, ab_ref, yb_ref))):
        r_ref, k_ref, v_ref, n_ref, lw_ref, a_ref, y_ref = refs
        earlier = (col < row) if d == 0 else (col > row)
        upto = jnp.logical_or(earlier, row == col)
        tri = upto.astype(BF16)
        lw = lw_ref[...]
        h1 = lw.astype(BF16)
        r1 = lw - h1.astype(F32)
        h2 = r1.astype(BF16)
        h3 = (r1 - h2.astype(F32)).astype(BF16)
        cum = _dot(tri, h1) + (_dot(tri, h2) + _dot(tri, h3))
        e_in = jnp.exp(cum)
        e_ex = jnp.exp(cum - lw)
        e_ng = jnp.exp(-cum)
        a = a_ref[...]
        kkn = n_ref[...]
        v = v_ref[...]
        kd = k_ref[...] * (1.0 + (a - 1.0) * ka)
        at = -kkn * e_ex
        rt = r_ref[...] * e_in
        bt = kkn * a * e_ng
        kt = kd * e_ng
        end = CHUNK - 1 if d == 0 else 0
        g_end = e_in[end:end + 1, :]
        bh = bt * g_end
        kh = kt * g_end
        for h in range(HEADS_PER_STEP):
            sl = slice(h * hd, (h + 1) * hd)
            chains.append(dict(d=d, h=h, sl=sl, y_ref=y_ref, earlier=earlier, upto=upto,
                               at=at[:, sl], rt=rt[:, sl], bt=bt[:, sl], kt=kt[:, sl], v=v[:, sl],
                               bh=bh[:, sl], kh=kh[:, sl], g=g_end[:, sl], s=st_ref[d, h]))

    for c in chains:
        sc = _bdot_nt(jnp.concatenate([c["at"], c["rt"]], axis=0), jnp.concatenate([c["bt"], c["kt"]], axis=0))
        c["lab"] = jnp.where(c["earlier"], sc[0:CHUNK, 0:CHUNK], 0.0)
        c["lak"] = jnp.where(c["earlier"], sc[0:CHUNK, CHUNK:2 * CHUNK], 0.0)
        c["mrb"] = jnp.where(c["upto"], sc[CHUNK:2 * CHUNK, 0:CHUNK], 0.0)
        c["mrk"] = jnp.where(c["upto"], sc[CHUNK:2 * CHUNK, CHUNK:2 * CHUNK], 0.0)

    eye = (row == col).astype(F32)
    pair = jnp.logical_and(row // 2 == col // 2, row != col)
    for c in chains:
        c["t"] = eye + jnp.where(pair, c["lab"], 0.0)
    n = 2
    while n < CHUNK:
        m = jnp.logical_and(row // (2 * n) == col // (2 * n), row // n != col // n)
        for c in chains:
            c["w"] = _bdot(jnp.where(m, c["lab"], 0.0), c["t"])
        for c in chains:
            c["t"] = c["t"] + _bdot(c["t"], c["w"])
        n *= 2

    for c in chains:
        c["x"] = _bdot_nt(c["at"], c["s"]) + _bdot(c["lak"], c["v"])
    for c in chains:
        c["y0"] = _bdot_nt(c["rt"], c["s"]) + _bdot(c["mrk"], c["v"])
    for c in chains:
        c["u"] = _bdot(c["t"], c["x"])
    for c in chains:
        c["y_ref"][:, c["sl"]] = c["y0"] + _bdot(c["mrb"], c["u"])
    for c in chains:
        uv_t = jnp.concatenate([c["u"], c["v"]], axis=0).T
        s_new = c["s"] * c["g"] + _bdot(uv_t, jnp.concatenate([c["bh"], c["kh"]], axis=0))
        st_ref[c["d"], c["h"]] = s_new

    @pl.when(is_last)
    def _():
        sfin_ref[0] = st_ref[...]


def _scan_table(n_ctx_seq, t_ctx, n_s_seq, t_s):
    rows = []
    base = 0
    seq = 0
    for n_seq, t_len in ((n_ctx_seq, t_ctx), (n_s_seq, t_s)):
        n_c = t_len // CHUNK
        for _ in range(n_seq):
            for c in range(n_c):
                rows.append((base + c, base + n_c - 1 - c, seq, int(c == 0), int(c == n_c - 1)))
            base += n_c
            seq += 1
    return np.asarray(rows, dtype=np.int32).T.copy()


def _rwkv_scan(r, k, v, kkn, lw, a, k_a, s0, table):
    n = r.shape[0]
    n_seq = s0.shape[0]
    hp = HEADS_PER_STEP
    w = hp * HEAD_DIM
    n_hg = N_HEADS // hp
    n_steps = table.shape[1]

    def tok(which, lane_off=0):
        return pl.BlockSpec((CHUNK, w), lambda g, s, tab: (tab[which, s], g + lane_off))

    fwd = [tok(0), tok(0), tok(0), tok(0), tok(0), tok(0)]
    bwd = [tok(1), tok(1), tok(1), tok(1), tok(1, n_hg), tok(1, n_hg)]
    state_spec = pl.BlockSpec((1, 2, hp, HEAD_DIM, HEAD_DIM), lambda g, s, tab: (tab[2, s], 0, g, 0, 0))
    grid_spec = pltpu.PrefetchScalarGridSpec(
        num_scalar_prefetch=1,
        grid=(n_hg, n_steps),
        in_specs=fwd + bwd + [pl.BlockSpec((1, w), lambda g, s, tab: (0, g)), state_spec],
        out_specs=[tok(0), tok(1), state_spec],
        scratch_shapes=[pltpu.VMEM((2, hp, HEAD_DIM, HEAD_DIM), F32)],
    )
    return pl.pallas_call(
        _scan_kernel,
        grid_spec=grid_spec,
        out_shape=[jax.ShapeDtypeStruct((n, C_RWKV), F32), jax.ShapeDtypeStruct((n, C_RWKV), F32),
                   jax.ShapeDtypeStruct((n_seq, 2, N_HEADS, HEAD_DIM, HEAD_DIM), F32)],
        compiler_params=_cparams(("parallel", "arbitrary")),
        name="rwkv_scan",
    )(jnp.asarray(table), r, k, v, kkn, lw, a, r, k, v, kkn, lw, a, k_a.reshape(1, C_RWKV), s0)


def _merge_kernel(x_ref, c_ref, f_ref, yf_ref, yb_ref, r_ref, k_ref, v_ref, g_ref, ug_ref, mod_ref,
                  wc_ref, wf_ref, wr_ref, wo_ref, gng_ref, gnb_ref, rk_ref, bd_ref, n2_ref, wrt_ref, brt_ref,
                  x1_ref, h2_ref, comb_ref, *, n_ctx, seq_s):
    row = _mod_row(pl.program_id(0), ROW_TILE, n_ctx, seq_s)
    m = mod_ref[pl.ds(row, 1), :]
    g1 = m[:, 2 * D_MODEL:3 * D_MODEL]
    sh2 = m[:, 3 * D_MODEL:4 * D_MODEL]
    sc2 = m[:, 4 * D_MODEL:5 * D_MODEL]
    bd = bd_ref[...]
    inv = 1.0 / HEAD_DIM
    y = yf_ref[...] + yb_ref[...]
    d = y - _seg_sum(y, bd) * inv
    var = _seg_sum(d * d, bd) * inv
    yn = d * lax.rsqrt(var + GN_EPS) * gng_ref[...] + gnb_ref[...]
    v = v_ref[...]
    yn = yn + _seg_sum(r_ref[...] * k_ref[...] * rk_ref[...], bd) * v
    yr = _dot((yn * g_ref[...]).astype(BF16), wr_ref[...])
    yc = _dot(c_ref[...], wc_ref[...])
    yf = _dot(f_ref[...], wf_ref[...])
    dm = D_MODEL
    merged = (_sigmoid(ug_ref[:, 0:dm]) * yc + _sigmoid(ug_ref[:, dm:2 * dm]) * yf
              + _sigmoid(ug_ref[:, 2 * dm:3 * dm]) * yr)
    x1 = x_ref[...] + g1 * _dot(merged.astype(BF16), wo_ref[...])
    x1_ref[...] = x1
    h2 = x1 * lax.rsqrt(jnp.mean(x1 * x1, axis=-1, keepdims=True) + RMS_EPS) * n2_ref[...]
    h2 = h2 * (1.0 + sc2) + sh2
    h2_ref[...] = h2.astype(BF16)
    logits = _dot_x3(h2, wrt_ref[...]) + brt_ref[...]
    lane = lax.broadcasted_iota(jnp.int32, logits.shape, 1).astype(F32)
    neg = jnp.float32(-jnp.inf)
    lg = jnp.where(lane < N_EXPERTS, logits, neg)
    m1 = jnp.max(lg, axis=-1, keepdims=True)
    i1 = jnp.min(jnp.where(lg == m1, lane, float(LANES)), axis=-1, keepdims=True)
    lg2 = jnp.where(lane == i1, neg, lg)
    m2 = jnp.max(lg2, axis=-1, keepdims=True)
    i2 = jnp.min(jnp.where(lg2 == m2, lane, float(LANES)), axis=-1, keepdims=True)
    e = jnp.exp(m2 - m1)
    p1 = 1.0 / (1.0 + e)
    comb_ref[...] = jnp.where(lane == i1, p1, 0.0) + jnp.where(lane == i2, e * p1, 0.0)


def _merge(x, conv_h, fno_h, yf, yb, r, k, v, g, u_gate, mod, w_conv_out, w_fno_out, w_rwkv_out, w_o,
           gn_g, gn_b, r_k, gain2, w_router, b_router, n_ctx, seq_s):
    n = x.shape[0]
    c = C_RWKV
    tile = lambda w: pl.BlockSpec((ROW_TILE, w), lambda i: (i, 0))
    full = lambda shape: pl.BlockSpec(shape, lambda i: (0,) * len(shape))
    wrt = jnp.zeros((D_MODEL, LANES), F32).at[:, :N_EXPERTS].set(w_router)
    brt = jnp.zeros((1, LANES), F32).at[0, :N_EXPERTS].set(b_router)
    return pl.pallas_call(
        functools.partial(_merge_kernel, n_ctx=n_ctx, seq_s=seq_s),
        grid=(n // ROW_TILE,),
        in_specs=[
            tile(D_MODEL), tile(C_CONV), tile(C_FNO), tile(c), tile(c), tile(c), tile(c), tile(c), tile(c),
            tile(3 * D_MODEL), full((8, 6 * D_MODEL)),
            full((C_CONV, D_MODEL)), full((C_FNO, D_MODEL)), full((c, D_MODEL)), full((D_MODEL, D_MODEL)),
            full((1, c)), full((1, c)), full((1, c)), full((c, c)), full((1, D_MODEL)),
            full((D_MODEL, LANES)), full((1, LANES)),
        ],
        out_specs=[tile(D_MODEL), tile(D_MODEL), tile(LANES)],
        out_shape=[jax.ShapeDtypeStruct((n, D_MODEL), F32), jax.ShapeDtypeStruct((n, D_MODEL), BF16),
                   jax.ShapeDtypeStruct((n, LANES), F32)],
        compiler_params=_cparams(("parallel",)),
        name="merge",
    )(x, conv_h, fno_h, yf, yb, r, k, v, g, u_gate, mod,
      w_conv_out.astype(BF16), w_fno_out.astype(BF16), w_rwkv_out.astype(BF16), w_o.astype(BF16),
      gn_g.reshape(1, c), gn_b.reshape(1, c), r_k.reshape(1, c), _head_ones(), gain2.reshape(1, D_MODEL), wrt, brt)


def _ffn_kernel(x_ref, h_ref, mod_ref, wg_ref, wu_ref, wd_ref, o_ref, *, n_ctx, seq_s):
    row = _mod_row(pl.program_id(0), FFN_TILE, n_ctx, seq_s)
    g2 = mod_ref[pl.ds(row, 1), 5 * D_MODEL:6 * D_MODEL]
    h = h_ref[...]
    half = D_FF // 2
    acc = jnp.zeros((FFN_TILE, D_MODEL), F32)
    for p in range(2):
        sl = slice(p * half, (p + 1) * half)
        t = _silu(_dot(h, wg_ref[:, sl])) * _dot(h, wu_ref[:, sl])
        acc = acc + _dot(t.astype(BF16), wd_ref[sl, :])
    o_ref[...] = x_ref[...] + g2 * acc


def _ffn_dense(x1, h2, mod, w_gate, w_up, w_down, n_ctx, seq_s):
    n = x1.shape[0]
    tile = pl.BlockSpec((FFN_TILE, D_MODEL), lambda i: (i, 0))
    full = lambda shape: pl.BlockSpec(shape, lambda i: (0,) * len(shape))
    return pl.pallas_call(
        functools.partial(_ffn_kernel, n_ctx=n_ctx, seq_s=seq_s),
        grid=(n // FFN_TILE,),
        in_specs=[tile, tile, full((8, 6 * D_MODEL)), full((D_MODEL, D_FF)), full((D_MODEL, D_FF)),
                  full((D_FF, D_MODEL))],
        out_specs=tile,
        out_shape=jax.ShapeDtypeStruct((n, D_MODEL), F32),
        compiler_params=_cparams(("parallel",)),
        name="ffn_dense",
    )(x1, h2, mod, w_gate.astype(BF16), w_up.astype(BF16), w_down.astype(BF16))


def _moe_kernel(x_ref, h_ref, comb_ref, mod_ref, wg_ref, wu_ref, wd_ref, o_ref, acc_ref, *, n_ctx, seq_s):
    e = pl.program_id(1)

    @pl.when(e == 0)
    def _():
        acc_ref[...] = jnp.zeros_like(acc_ref)

    comb = comb_ref[...]
    lane = lax.broadcasted_iota(jnp.int32, comb.shape, 1)
    w_e = jnp.sum(jnp.where(lane == e, comb, 0.0), axis=-1, keepdims=True)
    h = h_ref[...]
    t = _silu(_dot(h, wg_ref[0])) * _dot(h, wu_ref[0])
    acc_ref[...] += w_e * _dot(t.astype(BF16), wd_ref[0])

    @pl.when(e == N_EXPERTS - 1)
    def _():
        row = _mod_row(pl.program_id(0), FFN_TILE, n_ctx, seq_s)
        g2 = mod_ref[pl.ds(row, 1), 5 * D_MODEL:6 * D_MODEL]
        o_ref[...] = x_ref[...] + g2 * acc_ref[...]


def _ffn_moe(x1, h2, comb, mod, w_gate, w_up, w_down, n_ctx, seq_s):
    n = x1.shape[0]
    tile = lambda w: pl.BlockSpec((FFN_TILE, w), lambda i, e: (i, 0))
    return pl.pallas_call(
        functools.partial(_moe_kernel, n_ctx=n_ctx, seq_s=seq_s),
        grid=(n // FFN_TILE, N_EXPERTS),
        in_specs=[tile(D_MODEL), tile(D_MODEL), tile(LANES),
                  pl.BlockSpec((8, 6 * D_MODEL), lambda i, e: (0, 0)),
                  pl.BlockSpec((1, D_MODEL, D_FF_E), lambda i, e: (e, 0, 0)),
                  pl.BlockSpec((1, D_MODEL, D_FF_E), lambda i, e: (e, 0, 0)),
                  pl.BlockSpec((1, D_FF_E, D_MODEL), lambda i, e: (e, 0, 0))],
        out_specs=tile(D_MODEL),
        out_shape=jax.ShapeDtypeStruct((n, D_MODEL), F32),
        scratch_shapes=[pltpu.VMEM((FFN_TILE, D_MODEL), F32)],
        compiler_params=_cparams(("parallel", "arbitrary")),
        name="ffn_moe",
    )(x1, h2, comb, mod, w_gate.astype(BF16), w_up.astype(BF16), w_down.astype(BF16))


def _final_kernel(x_ref, g_ref, o_ref):
    x = x_ref[...]
    o_ref[...] = x * lax.rsqrt(jnp.mean(x * x, axis=-1, keepdims=True) + RMS_EPS) * g_ref[...]


def _final_norm(x, gain, row0, n_rows):
    blk0 = row0 // FFN_TILE
    return pl.pallas_call(
        _final_kernel,
        grid=(n_rows // FFN_TILE,),
        in_specs=[pl.BlockSpec((FFN_TILE, D_MODEL), lambda i: (blk0 + i, 0)),
                  pl.BlockSpec((1, D_MODEL), lambda i: (0, 0))],
        out_specs=pl.BlockSpec((FFN_TILE, D_MODEL), lambda i: (i, 0)),
        out_shape=jax.ShapeDtypeStruct((n_rows, D_MODEL), F32),
        compiler_params=_cparams(("parallel",)),
        name="final_norm",
    )(x, gain.reshape(1, D_MODEL))


def kernel(x_prompt, x_sample, state_rwkv, c, c_ctx, norm1, norm2, w_ada, b_ada, w_in, dw_w, dw_b, conv_ln_g, conv_ln_b, w_conv_out, w_fno_out, shift_mu, g_up, dec_w0, dec_up, iclr_a0, iclr_up, k_k, k_a, r_k, gn_g, gn_b, w_rwkv_out, w_o, ffn_w_gate, ffn_w_up, ffn_w_down, w_router, b_router, moe_w_gate, moe_w_up, moe_w_down, final_norm):
    b_p, t_p, _ = x_prompt.shape
    b_s, t_s, _ = x_sample.shape
    depth = w_in.shape[0]
    n_ctx = b_p * t_p
    n_lat = b_s * t_s
    assert t_p == ROW_TILE and t_s % FFN_TILE == 0 and n_ctx % FFN_TILE == 0 and b_s <= CTX_ROW
    assert t_s % GRID_W == 0 and CHUNK == GRID_W

    x = jnp.concatenate([x_prompt.reshape(n_ctx, D_MODEL), x_sample.reshape(n_lat, D_MODEL)], axis=0)
    cond = jnp.zeros((8, D_MODEL), F32).at[:b_s].set(c).at[CTX_ROW].set(c_ctx)
    mods = _ada(cond, w_ada, b_ada)
    table = _scan_table(b_p, t_p, b_s, t_s)
    zero_state = jnp.zeros((b_p, 2, N_HEADS, HEAD_DIM, HEAD_DIM), F32)

    ctx_states = []
    for l in range(depth):
        mod = mods[l]
        u_conv, u_fno, u_rw, u_gate = _inproj(x, mod, norm1[l], w_in[l].astype(BF16), n_ctx, t_s)
        conv_h = _conv_branch(u_conv, dw_w[l], dw_b[l], conv_ln_g[l], conv_ln_b[l], n_ctx)
        fno_h = jnp.concatenate([_fno_branch(u_fno, 0, b_p, t_p), _fno_branch(u_fno, n_ctx, b_s, t_s)], axis=0)
        r, k, v, kkn, g, lw, a = _rwkv_prep(u_rw, shift_mu[l], g_up[l], dec_w0[l], dec_up[l], iclr_a0[l],
                                             iclr_up[l], k_k[l], n_ctx, t_s)
        s0 = jnp.concatenate([zero_state, state_rwkv[:, l]], axis=0)
        yf, yb, s_fin = _rwkv_scan(r, k, v, kkn, lw, a, k_a[l], s0, table)
        ctx_states.append(s_fin[:b_p])
        i = l // 2
        if l % 2 == 0:
            w_rt, b_rt = jnp.zeros((D_MODEL, N_EXPERTS), F32), jnp.zeros((N_EXPERTS,), F32)
        else:
            w_rt, b_rt = w_router[i], b_router[i]
        x1, h2, comb = _merge(x, conv_h, fno_h, yf, yb, r, k, v, g, u_gate, mod, w_conv_out[l], w_fno_out[l],
                              w_rwkv_out[l], w_o[l], gn_g[l], gn_b[l], r_k[l], norm2[l], w_rt, b_rt, n_ctx, t_s)
        if l % 2 == 0:
            x = _ffn_dense(x1, h2, mod, ffn_w_gate[i], ffn_w_up[i], ffn_w_down[i], n_ctx, t_s)
        else:
            x = _ffn_moe(x1, h2, comb, mod, moe_w_gate[i], moe_w_up[i], moe_w_down[i], n_ctx, t_s)

    y_prompt = _final_norm(x, final_norm, 0, n_ctx).reshape(b_p, t_p, D_MODEL)
    y_sample = _final_norm(x, final_norm, n_ctx, n_lat).reshape(b_s, t_s, D_MODEL)
    new_state = jnp.stack(ctx_states, axis=1).astype(x_prompt.dtype)
    return (y_prompt, y_sample, new_state)
```

```python
import functools

import numpy as np
import jax
import jax.numpy as jnp
from jax import lax
from jax.experimental import pallas as pl
from jax.experimental.pallas import tpu as pltpu

F32 = jnp.float32
BF16 = jnp.bfloat16

D_MODEL = 1024
GRID_W = 64
C_CONV = 256
CONV_W = 31
C_FNO = 256
FNO_GW = 64
N_HEADS = 8
HEAD_DIM = 64
C_RWKV = N_HEADS * HEAD_DIM
G_RANK = 128
W_RANK = 64
A_RANK = 64
D_FF = 2816
N_EXPERTS = 8
D_FF_E = 1408
RMS_EPS = 1e-6
LN_EPS = 1e-5
GN_EPS = 64e-5

OFF_FNO = 2 * C_CONV
OFF_RWKV = OFF_FNO + C_FNO
N_SHIFT = 3 * C_RWKV + G_RANK + 2 * (W_RANK + A_RANK)
OFF_GATE = OFF_RWKV + N_SHIFT
D_IN = OFF_GATE + 3 * D_MODEL

ROW_TILE = 256
FFN_TILE = 512
CHUNK = 64
GROUP = 4
GROUP_W = GROUP * HEAD_DIM
SUMMARY_CHUNKS = 4
CTX_ROW = 4
LANES = 128
VMEM_LIMIT = 56 * 1024 * 1024


def _cparams(sem):
    return pltpu.CompilerParams(dimension_semantics=sem, vmem_limit_bytes=VMEM_LIMIT)


def _sigmoid(x):
    return 1.0 / (1.0 + jnp.exp(-x))


def _silu(x):
    return x * _sigmoid(x)


def _dot(a, b):
    return jnp.dot(a, b, preferred_element_type=F32)


def _split(x):
    hi = x.astype(BF16)
    lo = (x - hi.astype(F32)).astype(BF16)
    return hi, lo


def _dot_x3(a, b):
    ah, al = _split(a)
    bh, bl = _split(b)
    return _dot(ah, bh) + (_dot(ah, bl) + _dot(al, bh))


def _seg_sum(x, bd, split=True):
    if not split:
        return _dot(x.astype(BF16), bd)
    hi, lo = _split(x)
    return _dot(hi, bd) + _dot(lo, bd)


def _mod_row(i, tile, n_ctx_rows, seq_s):
    n_ctx_tiles = n_ctx_rows // tile
    return jnp.where(i < n_ctx_tiles, CTX_ROW, (i - n_ctx_tiles) // (seq_s // tile))


def _stream_specs(tile, width, n_ctx, lat_block0):
    n_ctx_tiles = n_ctx // tile
    return [pl.BlockSpec((tile, width), lambda i: (jnp.minimum(i, n_ctx_tiles - 1), 0)),
            pl.BlockSpec((tile, width), lambda i: (jnp.maximum(i - n_ctx_tiles, 0) + lat_block0, 0))]


def _stream_args(x, tile, n_ctx):
    if isinstance(x, tuple):
        return x[0], x[1], 0
    return x, x, n_ctx // tile


def _pick_stream(i, tile, n_ctx, ctx_ref, lat_ref):
    return jnp.where(i < n_ctx // tile, ctx_ref[...], lat_ref[...])


def _ada_kernel(c_ref, w_ref, b_ref, o_ref):
    s = _silu(c_ref[...])
    o_ref[0] = _dot(s.astype(BF16), w_ref[0].astype(BF16)) + b_ref[0]


def _ada(cond, w_ada, b_ada):
    n_l = w_ada.shape[0]
    tn = 1536
    return pl.pallas_call(
        _ada_kernel,
        grid=(n_l, 6 * D_MODEL // tn),
        in_specs=[
            pl.BlockSpec((8, D_MODEL), lambda l, j: (0, 0)),
            pl.BlockSpec((1, D_MODEL, tn), lambda l, j: (l, 0, j)),
            pl.BlockSpec((1, 1, tn), lambda l, j: (l, 0, j)),
        ],
        out_specs=pl.BlockSpec((1, 8, tn), lambda l, j: (l, 0, j)),
        out_shape=jax.ShapeDtypeStruct((n_l, 8, 6 * D_MODEL), F32),
        compiler_params=_cparams(("parallel", "parallel")),
        name="ada",
    )(cond, w_ada, b_ada.reshape(n_l, 1, 6 * D_MODEL))


def _inproj_kernel(xc_ref, xl_ref, mod_ref, g_ref, w_ref, oc_ref, of_ref, or_ref, og_ref, *, n_ctx, seq_s):
    row = _mod_row(pl.program_id(0), ROW_TILE, n_ctx, seq_s)
    m = mod_ref[pl.ds(row, 1), :]
    sh = m[:, 0:D_MODEL]
    sc = m[:, D_MODEL:2 * D_MODEL]
    x = _pick_stream(pl.program_id(0), ROW_TILE, n_ctx, xc_ref, xl_ref)
    y = x * lax.rsqrt(jnp.mean(x * x, axis=-1, keepdims=True) + RMS_EPS) * g_ref[...]
    h = (y * (1.0 + sc) + sh).astype(BF16)
    oc_ref[...] = _dot(h, w_ref[:, 0:OFF_FNO])
    of_ref[...] = _dot(h, w_ref[:, OFF_FNO:OFF_RWKV])
    or_ref[...] = _dot(h, w_ref[:, OFF_RWKV:OFF_GATE])
    og_ref[...] = _dot(h, w_ref[:, OFF_GATE:D_IN])


def _inproj(x, n, mod, gain, w_in, n_ctx, seq_s):
    xc, xl, lat0 = _stream_args(x, ROW_TILE, n_ctx)
    widths = (OFF_FNO, C_FNO, N_SHIFT, 3 * D_MODEL)
    return pl.pallas_call(
        functools.partial(_inproj_kernel, n_ctx=n_ctx, seq_s=seq_s),
        grid=(n // ROW_TILE,),
        in_specs=_stream_specs(ROW_TILE, D_MODEL, n_ctx, lat0) + [
            pl.BlockSpec((8, 6 * D_MODEL), lambda i: (0, 0)),
            pl.BlockSpec((1, D_MODEL), lambda i: (0, 0)),
            pl.BlockSpec((D_MODEL, D_IN), lambda i: (0, 0)),
        ],
        out_specs=[pl.BlockSpec((ROW_TILE, w), lambda i: (i, 0)) for w in widths],
        out_shape=[jax.ShapeDtypeStruct((n, w), F32) for w in widths],
        compiler_params=_cparams(("parallel",)),
        name="inproj",
    )(xc, xl, mod, gain.reshape(1, D_MODEL), w_in)


_CONV_HALO = 16
_CONV_ROWS = 64


def _conv_kernel(u_ref, w_ref, b_ref, g_ref, be_ref, o_ref, pad_ref, *, n_ctx):
    is_ctx = pl.program_id(0) < n_ctx // ROW_TILE
    zeros = jnp.zeros((_CONV_HALO, C_CONV), F32)
    n_parts = ROW_TILE // _CONV_ROWS

    def glu(lo, hi):
        return u_ref[lo:hi, 0:C_CONV] * _sigmoid(u_ref[lo:hi, C_CONV:2 * C_CONV])

    def finish(starts):
        for p in range(n_parts):
            acc = jnp.zeros((_CONV_ROWS, C_CONV), F32)
            for j in range(CONV_W):
                acc = acc + w_ref[j:j + 1, :] * pad_ref[starts[p] + j:starts[p] + j + _CONV_ROWS, :]
            acc = acc + b_ref[...]
            mu = jnp.mean(acc, axis=-1, keepdims=True)
            d = acc - mu
            var = jnp.mean(d * d, axis=-1, keepdims=True)
            y = d * lax.rsqrt(var + LN_EPS) * g_ref[...] + be_ref[...]
            o_ref[p * _CONV_ROWS:(p + 1) * _CONV_ROWS, :] = _silu(y).astype(o_ref.dtype)

    shift = _CONV_HALO - CONV_W // 2

    @pl.when(is_ctx)
    def _():
        pad_ref[0:_CONV_HALO, :] = zeros
        pad_ref[_CONV_HALO:_CONV_HALO + ROW_TILE, :] = glu(0, ROW_TILE)
        pad_ref[_CONV_HALO + ROW_TILE:2 * _CONV_HALO + ROW_TILE, :] = zeros
        finish([shift + p * _CONV_ROWS for p in range(n_parts)])

    @pl.when(jnp.logical_not(is_ctx))
    def _():
        stride = GRID_W + 2 * _CONV_HALO
        for p in range(n_parts):
            pad_ref[p * stride:p * stride + _CONV_HALO, :] = zeros
            pad_ref[p * stride + _CONV_HALO:p * stride + _CONV_HALO + GRID_W, :] = glu(p * GRID_W, (p + 1) * GRID_W)
            pad_ref[p * stride + _CONV_HALO + GRID_W:(p + 1) * stride, :] = zeros
        finish([p * stride + shift for p in range(n_parts)])


def _conv_branch(u_conv, dw_w, dw_b, ln_g, ln_b, n_ctx):
    n = u_conv.shape[0]
    assert _CONV_ROWS == GRID_W and ROW_TILE % GRID_W == 0
    vec = pl.BlockSpec((1, C_CONV), lambda i: (0, 0))
    return pl.pallas_call(
        functools.partial(_conv_kernel, n_ctx=n_ctx),
        grid=(n // ROW_TILE,),
        in_specs=[
            pl.BlockSpec((ROW_TILE, 2 * C_CONV), lambda i: (i, 0)),
            pl.BlockSpec((CONV_W, C_CONV), lambda i: (0, 0)),
            vec, vec, vec,
        ],
        out_specs=pl.BlockSpec((ROW_TILE, C_CONV), lambda i: (i, 0)),
        out_shape=jax.ShapeDtypeStruct((n, C_CONV), BF16),
        scratch_shapes=[pltpu.VMEM((ROW_TILE // GRID_W * (GRID_W + 2 * _CONV_HALO), C_CONV), F32)],
        compiler_params=_cparams(("parallel",)),
        name="conv_branch",
    )(u_conv, dw_w, dw_b.reshape(1, C_CONV), ln_g.reshape(1, C_CONV), ln_b.reshape(1, C_CONV))


def _dft_tables(t_len):
    def cs(n):
        k = np.arange(n, dtype=np.int64)
        ang = 2.0 * np.pi * ((k[:, None] * k[None, :]) % n).astype(np.float64) / n
        return np.cos(ang), np.sin(ang)
    cg, sg = cs(FNO_GW)
    eye = np.eye(C_FNO // FNO_GW)
    w1 = np.concatenate([np.kron(eye, cg), np.kron(eye, sg)], axis=1)
    ct, st = cs(t_len)
    w2 = np.concatenate([ct, -st], axis=1)
    return jnp.asarray(w1, dtype=F32).astype(BF16), jnp.asarray(w2, dtype=F32).astype(BF16)


def _fno_kernel(u_ref, w1_ref, w2_ref, o_ref, hs_ref, *, t_len, scale):
    @pl.when(pl.program_id(1) == 0)
    def _():
        hc = _dot(u_ref[...].astype(BF16), w1_ref[...])
        hs_ref[0:t_len, :] = hc[:, 0:C_FNO].astype(BF16)
        hs_ref[t_len:2 * t_len, :] = hc[:, C_FNO:2 * C_FNO].astype(BF16)

    o_ref[...] = (_dot(w2_ref[...], hs_ref[...]) * scale).astype(o_ref.dtype)


def _fno_branch(u_fno, row0, n_seq, t_len):
    w1, w2 = _dft_tables(t_len)
    tk = min(t_len, 512)
    blk0 = row0 // t_len
    return pl.pallas_call(
        functools.partial(_fno_kernel, t_len=t_len, scale=float(1.0 / np.sqrt(t_len * FNO_GW))),
        grid=(n_seq, t_len // tk),
        in_specs=[
            pl.BlockSpec((t_len, C_FNO), lambda b, j: (blk0 + b, 0)),
            pl.BlockSpec((C_FNO, 2 * C_FNO), lambda b, j: (0, 0)),
            pl.BlockSpec((tk, 2 * t_len), lambda b, j: (j, 0)),
        ],
        out_specs=pl.BlockSpec((tk, C_FNO), lambda b, j: (b * (t_len // tk) + j, 0)),
        out_shape=jax.ShapeDtypeStruct((n_seq * t_len, C_FNO), BF16),
        scratch_shapes=[pltpu.VMEM((2 * t_len, C_FNO), BF16)],
        compiler_params=_cparams(("parallel", "arbitrary")),
        name="fno_branch",
    )(u_fno, w1, w2)


def _prep_kernel(z_ref, zp_ref, zn_ref, mu_ref, gup_ref, w0_ref, dup_ref, a0_ref, aup_ref, kk_ref, bd_ref,
                 r_ref, k_ref, v_ref, kkn_ref, g_ref, lw_ref, a_ref, pad_ref, *, n_ctx, seq_s):
    i = pl.program_id(0)
    n_ctx_tiles = n_ctx // ROW_TILE
    per_seq = seq_s // ROW_TILE
    j = (i - n_ctx_tiles) % per_seq
    first = jnp.logical_or(i < n_ctx_tiles, j == 0)
    last = jnp.logical_or(i < n_ctx_tiles, j == per_seq - 1)
    pad_ref[8:8 + ROW_TILE, :] = z_ref[...]
    pad_ref[0:8, :] = jnp.where(first, 0.0, zp_ref[...])
    pad_ref[8 + ROW_TILE:16 + ROW_TILE, :] = jnp.where(last, 0.0, zn_ref[...])

    def shifted(lo, hi):
        z = pad_ref[8:8 + ROW_TILE, lo:hi]
        zp = pad_ref[7:7 + ROW_TILE, lo:hi]
        zn = pad_ref[9:9 + ROW_TILE, lo:hi]
        return z + mu_ref[:, lo:hi] * (0.5 * (zp + zn) - z)

    c = C_RWKV
    r_ref[...] = shifted(0, c)
    k = shifted(c, 2 * c)
    k_ref[...] = k
    v_ref[...] = shifted(2 * c, 3 * c)
    kx = k * kk_ref[...]
    nrm = jnp.sqrt(_seg_sum(kx * kx, bd_ref[...], split=False))
    kkn_ref[...] = kx / jnp.maximum(nrm, 1e-12)
    o = 3 * c
    g_ref[...] = _dot(_sigmoid(shifted(o, o + G_RANK)).astype(BF16), gup_ref[...])
    o += G_RANK
    xw = _dot(jnp.tanh(shifted(o, o + 2 * W_RANK)).astype(BF16), dup_ref[...]) + w0_ref[...]
    soft = jnp.maximum(-xw, 0.0) + jnp.log(1.0 + jnp.exp(-jnp.abs(xw)))
    lw_ref[...] = -jnp.exp(-soft - 0.5)
    o += 2 * W_RANK
    xa = _dot(shifted(o, o + 2 * A_RANK).astype(BF16), aup_ref[...]) + a0_ref[...]
    a_ref[...] = _sigmoid(xa)


def _block_diag2(w):
    z = jnp.zeros_like(w[0])
    return jnp.concatenate([jnp.concatenate([w[0], z], axis=1), jnp.concatenate([z, w[1]], axis=1)], axis=0)


def _head_ones():
    return jnp.asarray(np.kron(np.eye(N_HEADS), np.ones((HEAD_DIM, HEAD_DIM))), dtype=BF16)


def _rwkv_prep(u_rw, mu, g_up, dec_w0, dec_up, iclr_a0, iclr_up, k_k, n_ctx, seq_s):
    n = u_rw.shape[0]
    halo = ROW_TILE // 8
    last_blk = n // 8 - 1
    full = lambda shape: pl.BlockSpec(shape, lambda i: (0,) * len(shape))
    c = C_RWKV
    out_w = (c, c, c, c, c, 2 * c, 2 * c)
    return pl.pallas_call(
        functools.partial(_prep_kernel, n_ctx=n_ctx, seq_s=seq_s),
        grid=(n // ROW_TILE,),
        in_specs=[
            pl.BlockSpec((ROW_TILE, N_SHIFT), lambda i: (i, 0)),
            pl.BlockSpec((8, N_SHIFT), lambda i: (jnp.maximum(i * halo - 1, 0), 0)),
            pl.BlockSpec((8, N_SHIFT), lambda i: (jnp.minimum((i + 1) * halo, last_blk), 0)),
            full((1, N_SHIFT)), full((G_RANK, c)), full((1, 2 * c)), full((2 * W_RANK, 2 * c)),
            full((1, 2 * c)), full((2 * A_RANK, 2 * c)), full((1, c)), full((c, c)),
        ],
        out_specs=[pl.BlockSpec((ROW_TILE, w), lambda i: (i, 0)) for w in out_w],
        out_shape=[jax.ShapeDtypeStruct((n, w), F32) for w in out_w],
        scratch_shapes=[pltpu.VMEM((ROW_TILE + 16, N_SHIFT), F32)],
        compiler_params=_cparams(("parallel",)),
        name="rwkv_prep",
    )(u_rw, u_rw, u_rw, mu.reshape(1, N_SHIFT), g_up.astype(BF16), dec_w0.reshape(1, 2 * c),
      _block_diag2(dec_up).astype(BF16), iclr_a0.reshape(1, 2 * c), _block_diag2(iclr_up).astype(BF16),
      k_k.reshape(1, c), _head_ones())


def _block_diag(x, mask):
    xb = x.astype(BF16)
    return jnp.where(mask, jnp.concatenate([xb] * GROUP, axis=0), jnp.zeros((), BF16))


def _dot_nt(a, b):
    return lax.dot_general(a, b, (((1,), (1,)), ((), ())), preferred_element_type=F32)


def _diag_blocks(prod, lane_head):
    out = jnp.where(lane_head == 0, prod[0:HEAD_DIM], 0.0)
    for h in range(1, GROUP):
        out = out + jnp.where(lane_head == h, prod[h * HEAD_DIM:(h + 1) * HEAD_DIM], 0.0)
    return out


def _summary_kernel(r_ref, k_ref, v_ref, n_ref, lw_ref, a_ref, ka_ref, q_ref, y0_ref, p_ref, z_ref, g_ref):
    row = lax.broadcasted_iota(jnp.int32, (CHUNK, GROUP_W), 0)
    lane = lax.broadcasted_iota(jnp.int32, (CHUNK, GROUP_W), 1)
    col = lane % CHUNK
    lane_head = lane // HEAD_DIM
    bd_mask = (lax.broadcasted_iota(jnp.int32, (GROUP * CHUNK, GROUP_W), 0) // CHUNK
               == lax.broadcasted_iota(jnp.int32, (GROUP * CHUNK, GROUP_W), 1) // HEAD_DIM)
    bd = lambda x: _block_diag(x, bd_mask)
    ka = ka_ref[...]

    units = []
    for j in range(SUMMARY_CHUNKS):
        rows = slice(j * CHUNK, (j + 1) * CHUNK)
        r = r_ref[rows, :]
        k = k_ref[rows, :]
        v = v_ref[rows, :]
        kkn = n_ref[rows, :]
        for d in range(2):
            lanes = slice(d * C_RWKV, (d + 1) * C_RWKV)
            earlier = (col < row) if d == 0 else (col > row)
            upto = jnp.logical_or(earlier, row == col)
            tri = upto[:, 0:CHUNK].astype(BF16)
            lw = lw_ref[rows, lanes]
            h1 = lw.astype(BF16)
            r1 = lw - h1.astype(F32)
            h2 = r1.astype(BF16)
            h3 = (r1 - h2.astype(F32)).astype(BF16)
            cum = _dot(tri, h1) + (_dot(tri, h2) + _dot(tri, h3))
            e_in = jnp.exp(cum)
            e_ex = jnp.exp(cum - lw)
            e_ng = jnp.exp(-cum)
            a = a_ref[rows, lanes]
            kd = k * (1.0 + (a - 1.0) * ka)
            at = -kkn * e_ex
            rt = r * e_in
            bt = kkn * a * e_ng
            kt = kd * e_ng
            end = CHUNK - 1 if d == 0 else 0
            g_end = e_in[end:end + 1, :]
            g_ref[j, :, lanes] = g_end
            bh = bt * g_end
            kh = kt * g_end
            for q in range(C_RWKV // GROUP_W):
                sl = slice(q * GROUP_W, (q + 1) * GROUP_W)
                units.append(dict(rows=rows, out=slice(d * C_RWKV + q * GROUP_W, d * C_RWKV + (q + 1) * GROUP_W),
                                  earlier=earlier, upto=upto, at=at[:, sl], rt=rt[:, sl],
                                  ar=jnp.concatenate([at[:, sl], rt[:, sl]], axis=0).astype(BF16),
                                  bt=bt[:, sl], kt=kt[:, sl], v=v[:, sl], bh=bh[:, sl].astype(BF16),
                                  bk=jnp.concatenate([bh[:, sl], kh[:, sl]], axis=0).astype(BF16)))

    for u in units:
        sb = _dot_nt(u["ar"], bd(u["bt"]))
        sk = _dot_nt(u["ar"], bd(u["kt"]))
        u["lab"] = jnp.where(u["earlier"], sb[0:CHUNK], 0.0)
        u["mrb"] = jnp.where(u["upto"], sb[CHUNK:2 * CHUNK], 0.0).astype(BF16)
        u["lm"] = jnp.concatenate([jnp.where(u["earlier"], sk[0:CHUNK], 0.0),
                                   jnp.where(u["upto"], sk[CHUNK:2 * CHUNK], 0.0)], axis=0).astype(BF16)

    eye = (row == col).astype(F32)
    pair = jnp.logical_and(row // 2 == col // 2, row != col)
    for u in units:
        u["t"] = eye + jnp.where(pair, u["lab"], 0.0)
    n = 2
    while n < CHUNK:
        m = jnp.logical_and(row // (2 * n) == col // (2 * n), row // n != col // n)
        for u in units:
            u["w"] = _dot(jnp.where(m, u["lab"], 0.0).astype(BF16), bd(u["t"]))
        for u in units:
            u["t"] = u["t"] + _dot(u["t"].astype(BF16), bd(u["w"]))
        n *= 2

    for u in units:
        u["t"] = u["t"].astype(BF16)
        u["wm"] = _dot(u["t"], bd(u["at"]))
        u["lv"] = _dot(u["lm"], bd(u["v"]))
    for u in units:
        u["u0"] = _dot(u["t"], bd(u["lv"][0:CHUNK]))
    for u in units:
        q_ref[u["rows"], u["out"]] = (u["rt"] + _dot(u["mrb"], bd(u["wm"]))).astype(q_ref.dtype)
        y0_ref[u["rows"], u["out"]] = u["lv"][CHUNK:2 * CHUNK] + _dot(u["mrb"], bd(u["u0"]))
    for u in units:
        p = _dot(u["wm"].T.astype(BF16), u["bh"])
        p_ref[u["rows"], u["out"]] = _diag_blocks(p, lane_head).astype(p_ref.dtype)
        z = _dot(jnp.concatenate([u["u0"], u["v"]], axis=0).T.astype(BF16), u["bk"])
        z_ref[u["rows"], u["out"]] = _diag_blocks(z, lane_head)


def _chunk_summaries(r, k, v, kkn, lw, a, k_a):
    n = r.shape[0]
    tile_rows = SUMMARY_CHUNKS * CHUNK
    assert CHUNK == HEAD_DIM and C_RWKV % GROUP_W == 0 and n % tile_rows == 0
    tile = lambda w: pl.BlockSpec((tile_rows, w), lambda i: (i, 0))
    c = C_RWKV
    return pl.pallas_call(
        _summary_kernel,
        grid=(n // tile_rows,),
        in_specs=[tile(c), tile(c), tile(c), tile(c), tile(2 * c), tile(2 * c),
                  pl.BlockSpec((1, c), lambda i: (0, 0))],
        out_specs=[tile(2 * c), tile(2 * c), tile(2 * c), tile(2 * c),
                   pl.BlockSpec((SUMMARY_CHUNKS, 1, 2 * c), lambda i: (i, 0, 0))],
        out_shape=[jax.ShapeDtypeStruct((n, 2 * c), BF16), jax.ShapeDtypeStruct((n, 2 * c), F32),
                   jax.ShapeDtypeStruct((n, 2 * c), BF16), jax.ShapeDtypeStruct((n, 2 * c), F32),
                   jax.ShapeDtypeStruct((n // CHUNK, 1, 2 * c), F32)],
        compiler_params=_cparams(("parallel",)),
        name="rwkv_chunk_summaries",
    )(r, k, v, kkn, lw, a, k_a.reshape(1, c))


def _carry_kernel(tab_ref, qf_ref, y0f_ref, pf_ref, zf_ref, gf_ref, qb_ref, y0b_ref, pb_ref, zb_ref, gb_ref,
                  s0_ref, yf_ref, yb_ref, sfin_ref, st_ref):
    step = pl.program_id(0)

    @pl.when(tab_ref[3, step] == 1)
    def _():
        st_ref[...] = s0_ref[0]

    bd_mask = (lax.broadcasted_iota(jnp.int32, (GROUP * CHUNK, GROUP_W), 0) // CHUNK
               == lax.broadcasted_iota(jnp.int32, (GROUP * CHUNK, GROUP_W), 1) // HEAD_DIM)
    for d, (q_ref, y0_ref, p_ref, z_ref, g_ref, y_ref) in enumerate(
            ((qf_ref, y0f_ref, pf_ref, zf_ref, gf_ref, yf_ref), (qb_ref, y0b_ref, pb_ref, zb_ref, gb_ref, yb_ref))):
        for q in range(C_RWKV // GROUP_W):
            sl = slice(q * GROUP_W, (q + 1) * GROUP_W)
            s = st_ref[d, :, sl]
            y_ref[:, sl] = y0_ref[:, sl] + _dot_nt(q_ref[:, sl], _block_diag(s, bd_mask))
            st_ref[d, :, sl] = (s * g_ref[0, :, sl] + z_ref[:, sl]
                                + _dot(s.astype(BF16), _block_diag(p_ref[:, sl], bd_mask)))

    @pl.when(tab_ref[4, step] == 1)
    def _():
        sfin_ref[0] = st_ref[...]


def _scan_table(n_ctx_seq, t_ctx, n_s_seq, t_s):
    rows = []
    base = 0
    seq = 0
    for n_seq, t_len in ((n_ctx_seq, t_ctx), (n_s_seq, t_s)):
        n_c = t_len // CHUNK
        for _ in range(n_seq):
            for c in range(n_c):
                rows.append((base + c, base + n_c - 1 - c, seq, int(c == 0), int(c == n_c - 1)))
            base += n_c
            seq += 1
    return np.asarray(rows, dtype=np.int32).T.copy()


def _carry_states(q, y0, p, z, g, s0, table):
    n = q.shape[0]
    n_seq = s0.shape[0]
    c = C_RWKV

    def tok(which, d):
        return pl.BlockSpec((CHUNK, c), lambda s, tab: (tab[which, s], d))

    def decay(which, d):
        return pl.BlockSpec((1, 1, c), lambda s, tab: (tab[which, s], 0, d))

    state_spec = pl.BlockSpec((1, 2, HEAD_DIM, c), lambda s, tab: (tab[2, s], 0, 0, 0))
    grid_spec = pltpu.PrefetchScalarGridSpec(
        num_scalar_prefetch=1,
        grid=(table.shape[1],),
        in_specs=[tok(0, 0), tok(0, 0), tok(0, 0), tok(0, 0), decay(0, 0),
                  tok(1, 1), tok(1, 1), tok(1, 1), tok(1, 1), decay(1, 1), state_spec],
        out_specs=[tok(0, 0), tok(1, 0), state_spec],
        scratch_shapes=[pltpu.VMEM((2, HEAD_DIM, c), F32)],
    )
    return pl.pallas_call(
        _carry_kernel,
        grid_spec=grid_spec,
        out_shape=[jax.ShapeDtypeStruct((n, c), F32), jax.ShapeDtypeStruct((n, c), F32),
                   jax.ShapeDtypeStruct((n_seq, 2, HEAD_DIM, c), F32)],
        compiler_params=_cparams(("arbitrary",)),
        name="rwkv_carry",
    )(jnp.asarray(table), q, y0, p, z, g, q, y0, p, z, g, s0)


def _merge_kernel(xc_ref, xl_ref, c_ref, fc_ref, fl_ref, yf_ref, yb_ref, r_ref, k_ref, v_ref, g_ref, ug_ref, mod_ref,
                  wc_ref, wf_ref, wr_ref, wo_ref, gng_ref, gnb_ref, rk_ref, bd_ref, n2_ref, wrt_ref, brt_ref,
                  x1_ref, h2_ref, comb_ref, *, n_ctx, seq_s):
    row = _mod_row(pl.program_id(0), ROW_TILE, n_ctx, seq_s)
    m = mod_ref[pl.ds(row, 1), :]
    g1 = m[:, 2 * D_MODEL:3 * D_MODEL]
    sh2 = m[:, 3 * D_MODEL:4 * D_MODEL]
    sc2 = m[:, 4 * D_MODEL:5 * D_MODEL]
    bd = bd_ref[...]
    inv = 1.0 / HEAD_DIM
    y = yf_ref[...] + yb_ref[...]
    d = y - _seg_sum(y, bd) * inv
    var = _seg_sum(d * d, bd, split=False) * inv
    yn = d * lax.rsqrt(var + GN_EPS) * gng_ref[...] + gnb_ref[...]
    v = v_ref[...]
    yn = yn + _seg_sum(r_ref[...] * k_ref[...] * rk_ref[...], bd, split=False) * v
    yr = _dot((yn * g_ref[...]).astype(BF16), wr_ref[...])
    yc = _dot(c_ref[...], wc_ref[...])
    yf = _dot(_pick_stream(pl.program_id(0), ROW_TILE, n_ctx, fc_ref, fl_ref), wf_ref[...])
    dm = D_MODEL
    merged = (_sigmoid(ug_ref[:, 0:dm]) * yc + _sigmoid(ug_ref[:, dm:2 * dm]) * yf
              + _sigmoid(ug_ref[:, 2 * dm:3 * dm]) * yr)
    x1 = _pick_stream(pl.program_id(0), ROW_TILE, n_ctx, xc_ref, xl_ref) + g1 * _dot(merged.astype(BF16), wo_ref[...])
    x1_ref[...] = x1
    h2 = x1 * lax.rsqrt(jnp.mean(x1 * x1, axis=-1, keepdims=True) + RMS_EPS) * n2_ref[...]
    h2 = h2 * (1.0 + sc2) + sh2
    h2_ref[...] = h2.astype(BF16)
    logits = _dot_x3(h2, wrt_ref[...]) + brt_ref[...]
    lane = lax.broadcasted_iota(jnp.int32, logits.shape, 1).astype(F32)
    neg = jnp.float32(-jnp.inf)
    lg = jnp.where(lane < N_EXPERTS, logits, neg)
    m1 = jnp.max(lg, axis=-1, keepdims=True)
    i1 = jnp.min(jnp.where(lg == m1, lane, float(LANES)), axis=-1, keepdims=True)
    lg2 = jnp.where(lane == i1, neg, lg)
    m2 = jnp.max(lg2, axis=-1, keepdims=True)
    i2 = jnp.min(jnp.where(lg2 == m2, lane, float(LANES)), axis=-1, keepdims=True)
    e = jnp.exp(m2 - m1)
    p1 = 1.0 / (1.0 + e)
    comb_ref[...] = jnp.where(lane == i1, p1, 0.0) + jnp.where(lane == i2, e * p1, 0.0)


def _merge(x, n, conv_h, fno_h, yf, yb, r, k, v, g, u_gate, mod, w_conv_out, w_fno_out, w_rwkv_out, w_o,
           gn_g, gn_b, r_k, gain2, w_router, b_router, n_ctx, seq_s):
    xc, xl, x_lat0 = _stream_args(x, ROW_TILE, n_ctx)
    fc, fl, f_lat0 = _stream_args(fno_h, ROW_TILE, n_ctx)
    c = C_RWKV
    tile = lambda w: pl.BlockSpec((ROW_TILE, w), lambda i: (i, 0))
    full = lambda shape: pl.BlockSpec(shape, lambda i: (0,) * len(shape))
    wrt = jnp.zeros((D_MODEL, LANES), F32).at[:, :N_EXPERTS].set(w_router)
    brt = jnp.zeros((1, LANES), F32).at[0, :N_EXPERTS].set(b_router)
    return pl.pallas_call(
        functools.partial(_merge_kernel, n_ctx=n_ctx, seq_s=seq_s),
        grid=(n // ROW_TILE,),
        in_specs=_stream_specs(ROW_TILE, D_MODEL, n_ctx, x_lat0) + [tile(C_CONV)]
        + _stream_specs(ROW_TILE, C_FNO, n_ctx, f_lat0) + [
            tile(c), tile(c), tile(c), tile(c), tile(c), tile(c),
            tile(3 * D_MODEL), full((8, 6 * D_MODEL)),
            full((C_CONV, D_MODEL)), full((C_FNO, D_MODEL)), full((c, D_MODEL)), full((D_MODEL, D_MODEL)),
            full((1, c)), full((1, c)), full((1, c)), full((c, c)), full((1, D_MODEL)),
            full((D_MODEL, LANES)), full((1, LANES)),
        ],
        out_specs=[tile(D_MODEL), tile(D_MODEL), tile(LANES)],
        out_shape=[jax.ShapeDtypeStruct((n, D_MODEL), F32), jax.ShapeDtypeStruct((n, D_MODEL), BF16),
                   jax.ShapeDtypeStruct((n, LANES), F32)],
        compiler_params=_cparams(("parallel",)),
        name="merge",
    )(xc, xl, conv_h, fc, fl, yf, yb, r, k, v, g, u_gate, mod,
      w_conv_out.astype(BF16), w_fno_out.astype(BF16), w_rwkv_out.astype(BF16), w_o.astype(BF16),
      gn_g.reshape(1, c), gn_b.reshape(1, c), r_k.reshape(1, c), _head_ones(), gain2.reshape(1, D_MODEL), wrt, brt)


def _ffn_kernel(x_ref, h_ref, mod_ref, wg_ref, wu_ref, wd_ref, o_ref, *, n_ctx, seq_s):
    row = _mod_row(pl.program_id(0), FFN_TILE, n_ctx, seq_s)
    g2 = mod_ref[pl.ds(row, 1), 5 * D_MODEL:6 * D_MODEL]
    h = h_ref[...]
    half = D_FF // 2
    acc = jnp.zeros((FFN_TILE, D_MODEL), F32)
    for p in range(2):
        sl = slice(p * half, (p + 1) * half)
        t = _silu(_dot(h, wg_ref[:, sl])) * _dot(h, wu_ref[:, sl])
        acc = acc + _dot(t.astype(BF16), wd_ref[sl, :])
    o_ref[...] = x_ref[...] + g2 * acc


def _ffn_dense(x1, h2, mod, w_gate, w_up, w_down, n_ctx, seq_s):
    n = x1.shape[0]
    tile = pl.BlockSpec((FFN_TILE, D_MODEL), lambda i: (i, 0))
    full = lambda shape: pl.BlockSpec(shape, lambda i: (0,) * len(shape))
    return pl.pallas_call(
        functools.partial(_ffn_kernel, n_ctx=n_ctx, seq_s=seq_s),
        grid=(n // FFN_TILE,),
        in_specs=[tile, tile, full((8, 6 * D_MODEL)), full((D_MODEL, D_FF)), full((D_MODEL, D_FF)),
                  full((D_FF, D_MODEL))],
        out_specs=tile,
        out_shape=jax.ShapeDtypeStruct((n, D_MODEL), F32),
        compiler_params=_cparams(("parallel",)),
        name="ffn_dense",
    )(x1, h2, mod, w_gate.astype(BF16), w_up.astype(BF16), w_down.astype(BF16))


def _moe_kernel(x_ref, h_ref, comb_ref, mod_ref, wg_ref, wu_ref, wd_ref, o_ref, acc_ref, *, n_ctx, seq_s):
    e = pl.program_id(1)

    @pl.when(e == 0)
    def _():
        acc_ref[...] = jnp.zeros_like(acc_ref)

    comb = comb_ref[...]
    lane = lax.broadcasted_iota(jnp.int32, comb.shape, 1)
    w_e = jnp.sum(jnp.where(lane == e, comb, 0.0), axis=-1, keepdims=True)
    h = h_ref[...]
    t = _silu(_dot(h, wg_ref[0])) * _dot(h, wu_ref[0])
    acc_ref[...] += w_e * _dot(t.astype(BF16), wd_ref[0])

    @pl.when(e == N_EXPERTS - 1)
    def _():
        row = _mod_row(pl.program_id(0), FFN_TILE, n_ctx, seq_s)
        g2 = mod_ref[pl.ds(row, 1), 5 * D_MODEL:6 * D_MODEL]
        o_ref[...] = x_ref[...] + g2 * acc_ref[...]


def _ffn_moe(x1, h2, comb, mod, w_gate, w_up, w_down, n_ctx, seq_s):
    n = x1.shape[0]
    tile = lambda w: pl.BlockSpec((FFN_TILE, w), lambda i, e: (i, 0))
    return pl.pallas_call(
        functools.partial(_moe_kernel, n_ctx=n_ctx, seq_s=seq_s),
        grid=(n // FFN_TILE, N_EXPERTS),
        in_specs=[tile(D_MODEL), tile(D_MODEL), tile(LANES),
                  pl.BlockSpec((8, 6 * D_MODEL), lambda i, e: (0, 0)),
                  pl.BlockSpec((1, D_MODEL, D_FF_E), lambda i, e: (e, 0, 0)),
                  pl.BlockSpec((1, D_MODEL, D_FF_E), lambda i, e: (e, 0, 0)),
                  pl.BlockSpec((1, D_FF_E, D_MODEL), lambda i, e: (e, 0, 0))],
        out_specs=tile(D_MODEL),
        out_shape=jax.ShapeDtypeStruct((n, D_MODEL), F32),
        scratch_shapes=[pltpu.VMEM((FFN_TILE, D_MODEL), F32)],
        compiler_params=_cparams(("parallel", "arbitrary")),
        name="ffn_moe",
    )(x1, h2, comb, mod, w_gate.astype(BF16), w_up.astype(BF16), w_down.astype(BF16))


def _final_kernel(x_ref, g_ref, o_ref):
    x = x_ref[...]
    o_ref[...] = x * lax.rsqrt(jnp.mean(x * x, axis=-1, keepdims=True) + RMS_EPS) * g_ref[...]


def _final_norm(x, gain, row0, n_rows):
    blk0 = row0 // FFN_TILE
    return pl.pallas_call(
        _final_kernel,
        grid=(n_rows // FFN_TILE,),
        in_specs=[pl.BlockSpec((FFN_TILE, D_MODEL), lambda i: (blk0 + i, 0)),
                  pl.BlockSpec((1, D_MODEL), lambda i: (0, 0))],
        out_specs=pl.BlockSpec((FFN_TILE, D_MODEL), lambda i: (i, 0)),
        out_shape=jax.ShapeDtypeStruct((n_rows, D_MODEL), F32),
        compiler_params=_cparams(("parallel",)),
        name="final_norm",
    )(x, gain.reshape(1, D_MODEL))


def kernel(x_prompt, x_sample, state_rwkv, c, c_ctx, norm1, norm2, w_ada, b_ada, w_in, dw_w, dw_b, conv_ln_g, conv_ln_b, w_conv_out, w_fno_out, shift_mu, g_up, dec_w0, dec_up, iclr_a0, iclr_up, k_k, k_a, r_k, gn_g, gn_b, w_rwkv_out, w_o, ffn_w_gate, ffn_w_up, ffn_w_down, w_router, b_router, moe_w_gate, moe_w_up, moe_w_down, final_norm):
    b_p, t_p, _ = x_prompt.shape
    b_s, t_s, _ = x_sample.shape
    depth = w_in.shape[0]
    n_ctx = b_p * t_p
    n_lat = b_s * t_s
    assert t_p == ROW_TILE and t_s % FFN_TILE == 0 and n_ctx % FFN_TILE == 0 and b_s <= CTX_ROW
    assert t_s % GRID_W == 0 and CHUNK == GRID_W

    n = n_ctx + n_lat
    x = (x_prompt.reshape(n_ctx, D_MODEL), x_sample.reshape(n_lat, D_MODEL))
    cond = jnp.zeros((8, D_MODEL), F32).at[:b_s].set(c).at[CTX_ROW].set(c_ctx)
    mods = _ada(cond, w_ada, b_ada)
    table = _scan_table(b_p, t_p, b_s, t_s)
    zero_state = jnp.zeros((b_p, 2, HEAD_DIM, C_RWKV), F32)
    pack_state = lambda s: s.transpose(0, 1, 3, 2, 4).reshape(s.shape[0], 2, HEAD_DIM, C_RWKV)
    unpack_state = lambda s: s.reshape(s.shape[0], 2, HEAD_DIM, N_HEADS, HEAD_DIM).transpose(0, 1, 3, 2, 4)

    ctx_states = []
    for l in range(depth):
        mod = mods[l]
        u_conv, u_fno, u_rw, u_gate = _inproj(x, n, mod, norm1[l], w_in[l].astype(BF16), n_ctx, t_s)
        conv_h = _conv_branch(u_conv, dw_w[l], dw_b[l], conv_ln_g[l], conv_ln_b[l], n_ctx)
        fno_h = (_fno_branch(u_fno, 0, b_p, t_p), _fno_branch(u_fno, n_ctx, b_s, t_s))
        r, k, v, kkn, g, lw, a = _rwkv_prep(u_rw, shift_mu[l], g_up[l], dec_w0[l], dec_up[l], iclr_a0[l],
                                             iclr_up[l], k_k[l], n_ctx, t_s)
        s0 = jnp.concatenate([zero_state, pack_state(state_rwkv[:, l])], axis=0)
        yf, yb, s_fin = _carry_states(*_chunk_summaries(r, k, v, kkn, lw, a, k_a[l]), s0, table)
        ctx_states.append(unpack_state(s_fin[:b_p]))
        i = l // 2
        if l % 2 == 0:
            w_rt, b_rt = jnp.zeros((D_MODEL, N_EXPERTS), F32), jnp.zeros((N_EXPERTS,), F32)
        else:
            w_rt, b_rt = w_router[i], b_router[i]
        x1, h2, comb = _merge(x, n, conv_h, fno_h, yf, yb, r, k, v, g, u_gate, mod, w_conv_out[l], w_fno_out[l],
                              w_rwkv_out[l], w_o[l], gn_g[l], gn_b[l], r_k[l], norm2[l], w_rt, b_rt, n_ctx, t_s)
        if l % 2 == 0:
            x = _ffn_dense(x1, h2, mod, ffn_w_gate[i], ffn_w_up[i], ffn_w_down[i], n_ctx, t_s)
        else:
            x = _ffn_moe(x1, h2, comb, mod, moe_w_gate[i], moe_w_up[i], moe_w_down[i], n_ctx, t_s)

    y_prompt = _final_norm(x, final_norm, 0, n_ctx).reshape(b_p, t_p, D_MODEL)
    y_sample = _final_norm(x, final_norm, n_ctx, n_lat).reshape(b_s, t_s, D_MODEL)
    new_state = jnp.stack(ctx_states, axis=1).astype(x_prompt.dtype)
    return (y_prompt, y_sample, new_state)
```

```python
import functools

import numpy as np
import jax
import jax.numpy as jnp
from jax import lax
from jax.experimental import pallas as pl
from jax.experimental.pallas import tpu as pltpu

F32 = jnp.float32
BF16 = jnp.bfloat16

D_MODEL = 1024
GRID_W = 64
C_CONV = 256
CONV_W = 31
C_FNO = 256
FNO_GW = 64
N_HEADS = 8
HEAD_DIM = 64
C_RWKV = N_HEADS * HEAD_DIM
G_RANK = 128
W_RANK = 64
A_RANK = 64
D_FF = 2816
N_EXPERTS = 8
D_FF_E = 1408
RMS_EPS = 1e-6
LN_EPS = 1e-5
GN_EPS = 64e-5

OFF_FNO = 2 * C_CONV
OFF_RWKV = OFF_FNO + C_FNO
N_SHIFT = 3 * C_RWKV + G_RANK + 2 * (W_RANK + A_RANK)
OFF_GATE = OFF_RWKV + N_SHIFT
D_IN = OFF_GATE + 3 * D_MODEL

ROW_TILE = 256
FFN_TILE = 512
CHUNK = 64
GROUP = 4
GROUP_W = GROUP * HEAD_DIM
SUMMARY_CHUNKS = 4
CARRY_CHUNKS = 4
MOE_ROWS = 160
CTX_ROW = 4
LANES = 128
VMEM_LIMIT = 56 * 1024 * 1024


def _cparams(sem):
    return pltpu.CompilerParams(dimension_semantics=sem, vmem_limit_bytes=VMEM_LIMIT)


def _sigmoid(x):
    return 1.0 / (1.0 + jnp.exp(-x))


def _silu(x):
    return x * _sigmoid(x)


def _dot(a, b):
    return jnp.dot(a, b, preferred_element_type=F32)


def _split(x):
    hi = x.astype(BF16)
    lo = (x - hi.astype(F32)).astype(BF16)
    return hi, lo


def _dot_x3(a, b):
    ah, al = _split(a)
    bh, bl = _split(b)
    return _dot(ah, bh) + (_dot(ah, bl) + _dot(al, bh))


def _seg_sum(x, bd, split=True):
    if not split:
        return _dot(x.astype(BF16), bd)
    hi, lo = _split(x)
    return _dot(hi, bd) + _dot(lo, bd)


def _mod_row(i, tile, n_ctx_rows, seq_s):
    n_ctx_tiles = n_ctx_rows // tile
    return jnp.where(i < n_ctx_tiles, CTX_ROW, (i - n_ctx_tiles) // (seq_s // tile))


def _stream_specs(tile, width, n_ctx, lat_block0):
    n_ctx_tiles = n_ctx // tile
    return [pl.BlockSpec((tile, width), lambda i: (jnp.minimum(i, n_ctx_tiles - 1), 0)),
            pl.BlockSpec((tile, width), lambda i: (jnp.maximum(i - n_ctx_tiles, 0) + lat_block0, 0))]


def _stream_args(x, tile, n_ctx):
    if isinstance(x, tuple):
        return x[0], x[1], 0
    return x, x, n_ctx // tile


def _pick_stream(i, tile, n_ctx, ctx_ref, lat_ref):
    return jnp.where(i < n_ctx // tile, ctx_ref[...], lat_ref[...])


def _ada_kernel(c_ref, w_ref, b_ref, o_ref):
    s = _silu(c_ref[...])
    o_ref[0] = _dot(s.astype(BF16), w_ref[0].astype(BF16)) + b_ref[0]


def _ada(cond, w_ada, b_ada):
    n_l = w_ada.shape[0]
    tn = 1536
    return pl.pallas_call(
        _ada_kernel,
        grid=(n_l, 6 * D_MODEL // tn),
        in_specs=[
            pl.BlockSpec((8, D_MODEL), lambda l, j: (0, 0)),
            pl.BlockSpec((1, D_MODEL, tn), lambda l, j: (l, 0, j)),
            pl.BlockSpec((1, 1, tn), lambda l, j: (l, 0, j)),
        ],
        out_specs=pl.BlockSpec((1, 8, tn), lambda l, j: (l, 0, j)),
        out_shape=jax.ShapeDtypeStruct((n_l, 8, 6 * D_MODEL), F32),
        compiler_params=_cparams(("parallel", "parallel")),
        name="ada",
    )(cond, w_ada, b_ada.reshape(n_l, 1, 6 * D_MODEL))


def _inproj_kernel(xc_ref, xl_ref, mod_ref, g_ref, w_ref, oc_ref, of_ref, or_ref, og_ref, *, n_ctx, seq_s):
    row = _mod_row(pl.program_id(0), ROW_TILE, n_ctx, seq_s)
    m = mod_ref[pl.ds(row, 1), :]
    sh = m[:, 0:D_MODEL]
    sc = m[:, D_MODEL:2 * D_MODEL]
    x = _pick_stream(pl.program_id(0), ROW_TILE, n_ctx, xc_ref, xl_ref)
    y = x * lax.rsqrt(jnp.mean(x * x, axis=-1, keepdims=True) + RMS_EPS) * g_ref[...]
    h = (y * (1.0 + sc) + sh).astype(BF16)
    oc_ref[...] = _dot(h, w_ref[:, 0:OFF_FNO])
    of_ref[...] = _dot(h, w_ref[:, OFF_FNO:OFF_RWKV])
    or_ref[...] = _dot(h, w_ref[:, OFF_RWKV:OFF_GATE])
    og_ref[...] = _dot(h, w_ref[:, OFF_GATE:D_IN])


def _inproj(x, n, mod, gain, w_in, n_ctx, seq_s):
    xc, xl, lat0 = _stream_args(x, ROW_TILE, n_ctx)
    widths = (OFF_FNO, C_FNO, N_SHIFT, 3 * D_MODEL)
    return pl.pallas_call(
        functools.partial(_inproj_kernel, n_ctx=n_ctx, seq_s=seq_s),
        grid=(n // ROW_TILE,),
        in_specs=_stream_specs(ROW_TILE, D_MODEL, n_ctx, lat0) + [
            pl.BlockSpec((8, 6 * D_MODEL), lambda i: (0, 0)),
            pl.BlockSpec((1, D_MODEL), lambda i: (0, 0)),
            pl.BlockSpec((D_MODEL, D_IN), lambda i: (0, 0)),
        ],
        out_specs=[pl.BlockSpec((ROW_TILE, w), lambda i: (i, 0)) for w in widths],
        out_shape=[jax.ShapeDtypeStruct((n, w), F32) for w in widths],
        compiler_params=_cparams(("parallel",)),
        name="inproj",
    )(xc, xl, mod, gain.reshape(1, D_MODEL), w_in)


_CONV_HALO = 16
_CONV_ROWS = 64


def _conv_kernel(u_ref, w_ref, b_ref, g_ref, be_ref, o_ref, pad_ref, *, n_ctx):
    is_ctx = pl.program_id(0) < n_ctx // ROW_TILE
    zeros = jnp.zeros((_CONV_HALO, C_CONV), F32)
    n_parts = ROW_TILE // _CONV_ROWS

    def glu(lo, hi):
        return u_ref[lo:hi, 0:C_CONV] * _sigmoid(u_ref[lo:hi, C_CONV:2 * C_CONV])

    def finish(starts):
        for p in range(n_parts):
            acc = jnp.zeros((_CONV_ROWS, C_CONV), F32)
            for j in range(CONV_W):
                acc = acc + w_ref[j:j + 1, :] * pad_ref[starts[p] + j:starts[p] + j + _CONV_ROWS, :]
            acc = acc + b_ref[...]
            mu = jnp.mean(acc, axis=-1, keepdims=True)
            d = acc - mu
            var = jnp.mean(d * d, axis=-1, keepdims=True)
            y = d * lax.rsqrt(var + LN_EPS) * g_ref[...] + be_ref[...]
            o_ref[p * _CONV_ROWS:(p + 1) * _CONV_ROWS, :] = _silu(y).astype(o_ref.dtype)

    shift = _CONV_HALO - CONV_W // 2

    @pl.when(is_ctx)
    def _():
        pad_ref[0:_CONV_HALO, :] = zeros
        pad_ref[_CONV_HALO:_CONV_HALO + ROW_TILE, :] = glu(0, ROW_TILE)
        pad_ref[_CONV_HALO + ROW_TILE:2 * _CONV_HALO + ROW_TILE, :] = zeros
        finish([shift + p * _CONV_ROWS for p in range(n_parts)])

    @pl.when(jnp.logical_not(is_ctx))
    def _():
        stride = GRID_W + 2 * _CONV_HALO
        for p in range(n_parts):
            pad_ref[p * stride:p * stride + _CONV_HALO, :] = zeros
            pad_ref[p * stride + _CONV_HALO:p * stride + _CONV_HALO + GRID_W, :] = glu(p * GRID_W, (p + 1) * GRID_W)
            pad_ref[p * stride + _CONV_HALO + GRID_W:(p + 1) * stride, :] = zeros
        finish([p * stride + shift for p in range(n_parts)])


def _conv_branch(u_conv, dw_w, dw_b, ln_g, ln_b, n_ctx):
    n = u_conv.shape[0]
    assert _CONV_ROWS == GRID_W and ROW_TILE % GRID_W == 0
    vec = pl.BlockSpec((1, C_CONV), lambda i: (0, 0))
    return pl.pallas_call(
        functools.partial(_conv_kernel, n_ctx=n_ctx),
        grid=(n // ROW_TILE,),
        in_specs=[
            pl.BlockSpec((ROW_TILE, 2 * C_CONV), lambda i: (i, 0)),
            pl.BlockSpec((CONV_W, C_CONV), lambda i: (0, 0)),
            vec, vec, vec,
        ],
        out_specs=pl.BlockSpec((ROW_TILE, C_CONV), lambda i: (i, 0)),
        out_shape=jax.ShapeDtypeStruct((n, C_CONV), BF16),
        scratch_shapes=[pltpu.VMEM((ROW_TILE // GRID_W * (GRID_W + 2 * _CONV_HALO), C_CONV), F32)],
        compiler_params=_cparams(("parallel",)),
        name="conv_branch",
    )(u_conv, dw_w, dw_b.reshape(1, C_CONV), ln_g.reshape(1, C_CONV), ln_b.reshape(1, C_CONV))


def _dft_tables(t_len):
    def cs(n):
        k = np.arange(n, dtype=np.int64)
        ang = 2.0 * np.pi * ((k[:, None] * k[None, :]) % n).astype(np.float64) / n
        return np.cos(ang), np.sin(ang)
    cg, sg = cs(FNO_GW)
    eye = np.eye(C_FNO // FNO_GW)
    w1 = np.concatenate([np.kron(eye, cg), np.kron(eye, sg)], axis=1)
    ct, st = cs(t_len)
    w2 = np.concatenate([ct, -st], axis=1)
    return jnp.asarray(w1, dtype=F32).astype(BF16), jnp.asarray(w2, dtype=F32).astype(BF16)


def _fno_kernel(u_ref, w1_ref, w2_ref, o_ref, hs_ref, *, t_len, scale):
    @pl.when(pl.program_id(1) == 0)
    def _():
        hc = _dot(u_ref[...].astype(BF16), w1_ref[...])
        hs_ref[0:t_len, :] = hc[:, 0:C_FNO].astype(BF16)
        hs_ref[t_len:2 * t_len, :] = hc[:, C_FNO:2 * C_FNO].astype(BF16)

    o_ref[...] = (_dot(w2_ref[...], hs_ref[...]) * scale).astype(o_ref.dtype)


def _fno_branch(u_fno, row0, n_seq, t_len):
    w1, w2 = _dft_tables(t_len)
    tk = min(t_len, 512)
    blk0 = row0 // t_len
    return pl.pallas_call(
        functools.partial(_fno_kernel, t_len=t_len, scale=float(1.0 / np.sqrt(t_len * FNO_GW))),
        grid=(n_seq, t_len // tk),
        in_specs=[
            pl.BlockSpec((t_len, C_FNO), lambda b, j: (blk0 + b, 0)),
            pl.BlockSpec((C_FNO, 2 * C_FNO), lambda b, j: (0, 0)),
            pl.BlockSpec((tk, 2 * t_len), lambda b, j: (j, 0)),
        ],
        out_specs=pl.BlockSpec((tk, C_FNO), lambda b, j: (b * (t_len // tk) + j, 0)),
        out_shape=jax.ShapeDtypeStruct((n_seq * t_len, C_FNO), BF16),
        scratch_shapes=[pltpu.VMEM((2 * t_len, C_FNO), BF16)],
        compiler_params=_cparams(("parallel", "arbitrary")),
        name="fno_branch",
    )(u_fno, w1, w2)


def _prep_kernel(z_ref, zp_ref, zn_ref, mu_ref, gup_ref, w0_ref, dup_ref, a0_ref, aup_ref, kk_ref, bd_ref,
                 r_ref, k_ref, v_ref, kkn_ref, g_ref, lw_ref, a_ref, pad_ref, *, n_ctx, seq_s):
    i = pl.program_id(0)
    n_ctx_tiles = n_ctx // ROW_TILE
    per_seq = seq_s // ROW_TILE
    j = (i - n_ctx_tiles) % per_seq
    first = jnp.logical_or(i < n_ctx_tiles, j == 0)
    last = jnp.logical_or(i < n_ctx_tiles, j == per_seq - 1)
    pad_ref[8:8 + ROW_TILE, :] = z_ref[...]
    pad_ref[0:8, :] = jnp.where(first, 0.0, zp_ref[...])
    pad_ref[8 + ROW_TILE:16 + ROW_TILE, :] = jnp.where(last, 0.0, zn_ref[...])

    def shifted(lo, hi):
        z = pad_ref[8:8 + ROW_TILE, lo:hi]
        zp = pad_ref[7:7 + ROW_TILE, lo:hi]
        zn = pad_ref[9:9 + ROW_TILE, lo:hi]
        return z + mu_ref[:, lo:hi] * (0.5 * (zp + zn) - z)

    c = C_RWKV
    r_ref[...] = shifted(0, c)
    k = shifted(c, 2 * c)
    k_ref[...] = k
    v_ref[...] = shifted(2 * c, 3 * c)
    kx = k * kk_ref[...]
    nrm = jnp.sqrt(_seg_sum(kx * kx, bd_ref[...], split=False))
    kkn_ref[...] = kx / jnp.maximum(nrm, 1e-12)
    o = 3 * c
    g_ref[...] = _dot(_sigmoid(shifted(o, o + G_RANK)).astype(BF16), gup_ref[...])
    o += G_RANK
    xw = _dot(jnp.tanh(shifted(o, o + 2 * W_RANK)).astype(BF16), dup_ref[...]) + w0_ref[...]
    soft = jnp.maximum(-xw, 0.0) + jnp.log(1.0 + jnp.exp(-jnp.abs(xw)))
    lw_ref[...] = -jnp.exp(-soft - 0.5)
    o += 2 * W_RANK
    xa = _dot(shifted(o, o + 2 * A_RANK).astype(BF16), aup_ref[...]) + a0_ref[...]
    a_ref[...] = _sigmoid(xa)


def _block_diag2(w):
    z = jnp.zeros_like(w[0])
    return jnp.concatenate([jnp.concatenate([w[0], z], axis=1), jnp.concatenate([z, w[1]], axis=1)], axis=0)


def _head_ones():
    return jnp.asarray(np.kron(np.eye(N_HEADS), np.ones((HEAD_DIM, HEAD_DIM))), dtype=BF16)


def _rwkv_prep(u_rw, mu, g_up, dec_w0, dec_up, iclr_a0, iclr_up, k_k, n_ctx, seq_s):
    n = u_rw.shape[0]
    halo = ROW_TILE // 8
    last_blk = n // 8 - 1
    full = lambda shape: pl.BlockSpec(shape, lambda i: (0,) * len(shape))
    c = C_RWKV
    out_w = (c, c, c, c, c, 2 * c, 2 * c)
    return pl.pallas_call(
        functools.partial(_prep_kernel, n_ctx=n_ctx, seq_s=seq_s),
        grid=(n // ROW_TILE,),
        in_specs=[
            pl.BlockSpec((ROW_TILE, N_SHIFT), lambda i: (i, 0)),
            pl.BlockSpec((8, N_SHIFT), lambda i: (jnp.maximum(i * halo - 1, 0), 0)),
            pl.BlockSpec((8, N_SHIFT), lambda i: (jnp.minimum((i + 1) * halo, last_blk), 0)),
            full((1, N_SHIFT)), full((G_RANK, c)), full((1, 2 * c)), full((2 * W_RANK, 2 * c)),
            full((1, 2 * c)), full((2 * A_RANK, 2 * c)), full((1, c)), full((c, c)),
        ],
        out_specs=[pl.BlockSpec((ROW_TILE, w), lambda i: (i, 0)) for w in out_w],
        out_shape=[jax.ShapeDtypeStruct((n, w), F32) for w in out_w],
        scratch_shapes=[pltpu.VMEM((ROW_TILE + 16, N_SHIFT), F32)],
        compiler_params=_cparams(("parallel",)),
        name="rwkv_prep",
    )(u_rw, u_rw, u_rw, mu.reshape(1, N_SHIFT), g_up.astype(BF16), dec_w0.reshape(1, 2 * c),
      _block_diag2(dec_up).astype(BF16), iclr_a0.reshape(1, 2 * c), _block_diag2(iclr_up).astype(BF16),
      k_k.reshape(1, c), _head_ones())


def _block_diag(x, mask):
    xb = x.astype(BF16)
    return jnp.where(mask, jnp.concatenate([xb] * GROUP, axis=0), jnp.zeros((), BF16))


def _dot_nt(a, b):
    return lax.dot_general(a, b, (((1,), (1,)), ((), ())), preferred_element_type=F32)


def _diag_blocks(prod, lane_head):
    out = jnp.where(lane_head == 0, prod[0:HEAD_DIM], 0.0)
    for h in range(1, GROUP):
        out = out + jnp.where(lane_head == h, prod[h * HEAD_DIM:(h + 1) * HEAD_DIM], 0.0)
    return out


def _summary_kernel(r_ref, k_ref, v_ref, n_ref, lw_ref, a_ref, ka_ref, q_ref, y0_ref, p_ref, z_ref, g_ref):
    row = lax.broadcasted_iota(jnp.int32, (CHUNK, GROUP_W), 0)
    lane = lax.broadcasted_iota(jnp.int32, (CHUNK, GROUP_W), 1)
    col = lane % CHUNK
    lane_head = lane // HEAD_DIM
    bd_mask = (lax.broadcasted_iota(jnp.int32, (GROUP * CHUNK, GROUP_W), 0) // CHUNK
               == lax.broadcasted_iota(jnp.int32, (GROUP * CHUNK, GROUP_W), 1) // HEAD_DIM)
    bd = lambda x: _block_diag(x, bd_mask)
    ka = ka_ref[...]

    units = []
    for j in range(SUMMARY_CHUNKS):
        rows = slice(j * CHUNK, (j + 1) * CHUNK)
        r = r_ref[rows, :]
        k = k_ref[rows, :]
        v = v_ref[rows, :]
        kkn = n_ref[rows, :]
        for d in range(2):
            lanes = slice(d * C_RWKV, (d + 1) * C_RWKV)
            earlier = (col < row) if d == 0 else (col > row)
            upto = jnp.logical_or(earlier, row == col)
            tri = upto[:, 0:CHUNK].astype(BF16)
            lw = lw_ref[rows, lanes]
            h1 = lw.astype(BF16)
            r1 = lw - h1.astype(F32)
            h2 = r1.astype(BF16)
            h3 = (r1 - h2.astype(F32)).astype(BF16)
            cum = _dot(tri, h1) + (_dot(tri, h2) + _dot(tri, h3))
            e_in = jnp.exp(cum)
            e_ex = jnp.exp(cum - lw)
            e_ng = jnp.exp(-cum)
            a = a_ref[rows, lanes]
            kd = k * (1.0 + (a - 1.0) * ka)
            at = -kkn * e_ex
            rt = r * e_in
            bt = kkn * a * e_ng
            kt = kd * e_ng
            end = CHUNK - 1 if d == 0 else 0
            g_end = e_in[end:end + 1, :]
            g_ref[j, :, lanes] = g_end
            bh = bt * g_end
            kh = kt * g_end
            for q in range(C_RWKV // GROUP_W):
                sl = slice(q * GROUP_W, (q + 1) * GROUP_W)
                units.append(dict(rows=rows, out=slice(d * C_RWKV + q * GROUP_W, d * C_RWKV + (q + 1) * GROUP_W),
                                  earlier=earlier, upto=upto, at=at[:, sl], rt=rt[:, sl],
                                  ar=jnp.concatenate([at[:, sl], rt[:, sl]], axis=0).astype(BF16),
                                  bt=bt[:, sl], kt=kt[:, sl], v=v[:, sl], bh=bh[:, sl].astype(BF16),
                                  bk=jnp.concatenate([bh[:, sl], kh[:, sl]], axis=0).astype(BF16)))

    for u in units:
        sb = _dot_nt(u["ar"], bd(u["bt"]))
        sk = _dot_nt(u["ar"], bd(u["kt"]))
        u["lab"] = jnp.where(u["earlier"], sb[0:CHUNK], 0.0)
        u["mrb"] = jnp.where(u["upto"], sb[CHUNK:2 * CHUNK], 0.0).astype(BF16)
        u["lm"] = jnp.concatenate([jnp.where(u["earlier"], sk[0:CHUNK], 0.0),
                                   jnp.where(u["upto"], sk[CHUNK:2 * CHUNK], 0.0)], axis=0).astype(BF16)

    eye = (row == col).astype(F32)
    pair = jnp.logical_and(row // 2 == col // 2, row != col)
    for u in units:
        u["t"] = eye + jnp.where(pair, u["lab"], 0.0)
    n = 2
    while n < CHUNK:
        m = jnp.logical_and(row // (2 * n) == col // (2 * n), row // n != col // n)
        for u in units:
            u["w"] = _dot(jnp.where(m, u["lab"], 0.0).astype(BF16), bd(u["t"]))
        for u in units:
            u["t"] = u["t"] + _dot(u["t"].astype(BF16), bd(u["w"]))
        n *= 2

    for u in units:
        u["t"] = u["t"].astype(BF16)
        u["wm"] = _dot(u["t"], bd(u["at"]))
        u["lv"] = _dot(u["lm"], bd(u["v"]))
    for u in units:
        u["u0"] = _dot(u["t"], bd(u["lv"][0:CHUNK]))
    for u in units:
        q_ref[u["rows"], u["out"]] = (u["rt"] + _dot(u["mrb"], bd(u["wm"]))).astype(q_ref.dtype)
        y0_ref[u["rows"], u["out"]] = u["lv"][CHUNK:2 * CHUNK] + _dot(u["mrb"], bd(u["u0"]))
    for u in units:
        p = _dot(u["wm"].T.astype(BF16), u["bh"])
        p_ref[u["rows"], u["out"]] = _diag_blocks(p, lane_head).astype(p_ref.dtype)
        z = _dot(jnp.concatenate([u["u0"], u["v"]], axis=0).T.astype(BF16), u["bk"])
        z_ref[u["rows"], u["out"]] = _diag_blocks(z, lane_head)


def _chunk_summaries(r, k, v, kkn, lw, a, k_a):
    n = r.shape[0]
    tile_rows = SUMMARY_CHUNKS * CHUNK
    assert CHUNK == HEAD_DIM and C_RWKV % GROUP_W == 0 and n % tile_rows == 0
    tile = lambda w: pl.BlockSpec((tile_rows, w), lambda i: (i, 0))
    c = C_RWKV
    return pl.pallas_call(
        _summary_kernel,
        grid=(n // tile_rows,),
        in_specs=[tile(c), tile(c), tile(c), tile(c), tile(2 * c), tile(2 * c),
                  pl.BlockSpec((1, c), lambda i: (0, 0))],
        out_specs=[tile(2 * c), tile(2 * c), tile(2 * c), tile(2 * c),
                   pl.BlockSpec((SUMMARY_CHUNKS, 1, 2 * c), lambda i: (i, 0, 0))],
        out_shape=[jax.ShapeDtypeStruct((n, 2 * c), BF16), jax.ShapeDtypeStruct((n, 2 * c), F32),
                   jax.ShapeDtypeStruct((n, 2 * c), BF16), jax.ShapeDtypeStruct((n, 2 * c), F32),
                   jax.ShapeDtypeStruct((n // CHUNK, 1, 2 * c), F32)],
        compiler_params=_cparams(("parallel",)),
        name="rwkv_chunk_summaries",
    )(r, k, v, kkn, lw, a, k_a.reshape(1, c))


def _carry_kernel(tab_ref, qf_ref, y0f_ref, pf_ref, zf_ref, gf_ref, qb_ref, y0b_ref, pb_ref, zb_ref, gb_ref,
                  s0_ref, yf_ref, yb_ref, sfin_ref, st_ref):
    step = pl.program_id(0)

    @pl.when(tab_ref[3, step] == 1)
    def _():
        st_ref[...] = s0_ref[0]

    bd_mask = (lax.broadcasted_iota(jnp.int32, (GROUP * CHUNK, GROUP_W), 0) // CHUNK
               == lax.broadcasted_iota(jnp.int32, (GROUP * CHUNK, GROUP_W), 1) // HEAD_DIM)
    for i in range(CARRY_CHUNKS):
        for d, (q_ref, y0_ref, p_ref, z_ref, g_ref, y_ref) in enumerate(
                ((qf_ref, y0f_ref, pf_ref, zf_ref, gf_ref, yf_ref), (qb_ref, y0b_ref, pb_ref, zb_ref, gb_ref, yb_ref))):
            j = i if d == 0 else CARRY_CHUNKS - 1 - i
            rows = slice(j * CHUNK, (j + 1) * CHUNK)
            for q in range(C_RWKV // GROUP_W):
                sl = slice(q * GROUP_W, (q + 1) * GROUP_W)
                s = st_ref[d, :, sl]
                y_ref[rows, sl] = y0_ref[rows, sl] + _dot_nt(q_ref[rows, sl], _block_diag(s, bd_mask))
                st_ref[d, :, sl] = (s * g_ref[j, :, sl] + z_ref[rows, sl]
                                    + _dot(s.astype(BF16), _block_diag(p_ref[rows, sl], bd_mask)))

    @pl.when(tab_ref[4, step] == 1)
    def _():
        sfin_ref[0] = st_ref[...]


def _scan_table(n_ctx_seq, t_ctx, n_s_seq, t_s):
    rows = []
    base = 0
    seq = 0
    for n_seq, t_len in ((n_ctx_seq, t_ctx), (n_s_seq, t_s)):
        assert t_len % (CARRY_CHUNKS * CHUNK) == 0
        n_c = t_len // (CARRY_CHUNKS * CHUNK)
        for _ in range(n_seq):
            for c in range(n_c):
                rows.append((base + c, base + n_c - 1 - c, seq, int(c == 0), int(c == n_c - 1)))
            base += n_c
            seq += 1
    return np.asarray(rows, dtype=np.int32).T.copy()


def _carry_states(q, y0, p, z, g, s0, table):
    n = q.shape[0]
    n_seq = s0.shape[0]
    c = C_RWKV

    def tok(which, d):
        return pl.BlockSpec((CARRY_CHUNKS * CHUNK, c), lambda s, tab: (tab[which, s], d))

    def decay(which, d):
        return pl.BlockSpec((CARRY_CHUNKS, 1, c), lambda s, tab: (tab[which, s], 0, d))

    state_spec = pl.BlockSpec((1, 2, HEAD_DIM, c), lambda s, tab: (tab[2, s], 0, 0, 0))
    grid_spec = pltpu.PrefetchScalarGridSpec(
        num_scalar_prefetch=1,
        grid=(table.shape[1],),
        in_specs=[tok(0, 0), tok(0, 0), tok(0, 0), tok(0, 0), decay(0, 0),
                  tok(1, 1), tok(1, 1), tok(1, 1), tok(1, 1), decay(1, 1), state_spec],
        out_specs=[tok(0, 0), tok(1, 0), state_spec],
        scratch_shapes=[pltpu.VMEM((2, HEAD_DIM, c), F32)],
    )
    return pl.pallas_call(
        _carry_kernel,
        grid_spec=grid_spec,
        out_shape=[jax.ShapeDtypeStruct((n, c), F32), jax.ShapeDtypeStruct((n, c), F32),
                   jax.ShapeDtypeStruct((n_seq, 2, HEAD_DIM, c), F32)],
        compiler_params=_cparams(("arbitrary",)),
        name="rwkv_carry",
    )(jnp.asarray(table), q, y0, p, z, g, q, y0, p, z, g, s0)


def _merge_kernel(xc_ref, xl_ref, c_ref, fc_ref, fl_ref, yf_ref, yb_ref, r_ref, k_ref, v_ref, g_ref, ug_ref, mod_ref,
                  wc_ref, wf_ref, wr_ref, wo_ref, gng_ref, gnb_ref, rk_ref, bd_ref, n2_ref, wrt_ref, brt_ref,
                  x1_ref, h2_ref, comb_ref, *, n_ctx, seq_s):
    row = _mod_row(pl.program_id(0), ROW_TILE, n_ctx, seq_s)
    m = mod_ref[pl.ds(row, 1), :]
    g1 = m[:, 2 * D_MODEL:3 * D_MODEL]
    sh2 = m[:, 3 * D_MODEL:4 * D_MODEL]
    sc2 = m[:, 4 * D_MODEL:5 * D_MODEL]
    bd = bd_ref[...]
    inv = 1.0 / HEAD_DIM
    y = yf_ref[...] + yb_ref[...]
    d = y - _seg_sum(y, bd) * inv
    var = _seg_sum(d * d, bd, split=False) * inv
    yn = d * lax.rsqrt(var + GN_EPS) * gng_ref[...] + gnb_ref[...]
    v = v_ref[...]
    yn = yn + _seg_sum(r_ref[...] * k_ref[...] * rk_ref[...], bd, split=False) * v
    yr = _dot((yn * g_ref[...]).astype(BF16), wr_ref[...])
    yc = _dot(c_ref[...], wc_ref[...])
    yf = _dot(_pick_stream(pl.program_id(0), ROW_TILE, n_ctx, fc_ref, fl_ref), wf_ref[...])
    dm = D_MODEL
    merged = (_sigmoid(ug_ref[:, 0:dm]) * yc + _sigmoid(ug_ref[:, dm:2 * dm]) * yf
              + _sigmoid(ug_ref[:, 2 * dm:3 * dm]) * yr)
    x1 = _pick_stream(pl.program_id(0), ROW_TILE, n_ctx, xc_ref, xl_ref) + g1 * _dot(merged.astype(BF16), wo_ref[...])
    x1_ref[...] = x1
    h2 = x1 * lax.rsqrt(jnp.mean(x1 * x1, axis=-1, keepdims=True) + RMS_EPS) * n2_ref[...]
    h2 = h2 * (1.0 + sc2) + sh2
    h2_ref[...] = h2.astype(BF16)
    logits = _dot_x3(h2, wrt_ref[...]) + brt_ref[...]
    lane = lax.broadcasted_iota(jnp.int32, logits.shape, 1).astype(F32)
    neg = jnp.float32(-jnp.inf)
    lg = jnp.where(lane < N_EXPERTS, logits, neg)
    m1 = jnp.max(lg, axis=-1, keepdims=True)
    i1 = jnp.min(jnp.where(lg == m1, lane, float(LANES)), axis=-1, keepdims=True)
    lg2 = jnp.where(lane == i1, neg, lg)
    m2 = jnp.max(lg2, axis=-1, keepdims=True)
    i2 = jnp.min(jnp.where(lg2 == m2, lane, float(LANES)), axis=-1, keepdims=True)
    e = jnp.exp(m2 - m1)
    p1 = 1.0 / (1.0 + e)
    comb_ref[...] = jnp.where(lane == i1, p1, 0.0) + jnp.where(lane == i2, e * p1, 0.0)


def _merge(x, n, conv_h, fno_h, yf, yb, r, k, v, g, u_gate, mod, w_conv_out, w_fno_out, w_rwkv_out, w_o,
           gn_g, gn_b, r_k, gain2, w_router, b_router, n_ctx, seq_s):
    xc, xl, x_lat0 = _stream_args(x, ROW_TILE, n_ctx)
    fc, fl, f_lat0 = _stream_args(fno_h, ROW_TILE, n_ctx)
    c = C_RWKV
    tile = lambda w: pl.BlockSpec((ROW_TILE, w), lambda i: (i, 0))
    full = lambda shape: pl.BlockSpec(shape, lambda i: (0,) * len(shape))
    wrt = jnp.zeros((D_MODEL, LANES), F32).at[:, :N_EXPERTS].set(w_router)
    brt = jnp.zeros((1, LANES), F32).at[0, :N_EXPERTS].set(b_router)
    return pl.pallas_call(
        functools.partial(_merge_kernel, n_ctx=n_ctx, seq_s=seq_s),
        grid=(n // ROW_TILE,),
        in_specs=_stream_specs(ROW_TILE, D_MODEL, n_ctx, x_lat0) + [tile(C_CONV)]
        + _stream_specs(ROW_TILE, C_FNO, n_ctx, f_lat0) + [
            tile(c), tile(c), tile(c), tile(c), tile(c), tile(c),
            tile(3 * D_MODEL), full((8, 6 * D_MODEL)),
            full((C_CONV, D_MODEL)), full((C_FNO, D_MODEL)), full((c, D_MODEL)), full((D_MODEL, D_MODEL)),
            full((1, c)), full((1, c)), full((1, c)), full((c, c)), full((1, D_MODEL)),
            full((D_MODEL, LANES)), full((1, LANES)),
        ],
        out_specs=[tile(D_MODEL), tile(D_MODEL), tile(LANES)],
        out_shape=[jax.ShapeDtypeStruct((n, D_MODEL), F32), jax.ShapeDtypeStruct((n, D_MODEL), BF16),
                   jax.ShapeDtypeStruct((n, LANES), F32)],
        compiler_params=_cparams(("parallel",)),
        name="merge",
    )(xc, xl, conv_h, fc, fl, yf, yb, r, k, v, g, u_gate, mod,
      w_conv_out.astype(BF16), w_fno_out.astype(BF16), w_rwkv_out.astype(BF16), w_o.astype(BF16),
      gn_g.reshape(1, c), gn_b.reshape(1, c), r_k.reshape(1, c), _head_ones(), gain2.reshape(1, D_MODEL), wrt, brt)


def _ffn_kernel(x_ref, h_ref, mod_ref, wg_ref, wu_ref, wd_ref, o_ref, *, n_ctx, seq_s):
    row = _mod_row(pl.program_id(0), FFN_TILE, n_ctx, seq_s)
    g2 = mod_ref[pl.ds(row, 1), 5 * D_MODEL:6 * D_MODEL]
    h = h_ref[...]
    half = D_FF // 2
    acc = jnp.zeros((FFN_TILE, D_MODEL), F32)
    for p in range(2):
        sl = slice(p * half, (p + 1) * half)
        t = _silu(_dot(h, wg_ref[:, sl])) * _dot(h, wu_ref[:, sl])
        acc = acc + _dot(t.astype(BF16), wd_ref[sl, :])
    o_ref[...] = x_ref[...] + g2 * acc


def _ffn_dense(x1, h2, mod, w_gate, w_up, w_down, n_ctx, seq_s):
    n = x1.shape[0]
    tile = pl.BlockSpec((FFN_TILE, D_MODEL), lambda i: (i, 0))
    full = lambda shape: pl.BlockSpec(shape, lambda i: (0,) * len(shape))
    return pl.pallas_call(
        functools.partial(_ffn_kernel, n_ctx=n_ctx, seq_s=seq_s),
        grid=(n // FFN_TILE,),
        in_specs=[tile, tile, full((8, 6 * D_MODEL)), full((D_MODEL, D_FF)), full((D_MODEL, D_FF)),
                  full((D_FF, D_MODEL))],
        out_specs=tile,
        out_shape=jax.ShapeDtypeStruct((n, D_MODEL), F32),
        compiler_params=_cparams(("parallel",)),
        name="ffn_dense",
    )(x1, h2, mod, w_gate.astype(BF16), w_up.astype(BF16), w_down.astype(BF16))


def _moe_kernel(x_ref, h_ref, comb_ref, mod_ref, wg_ref, wu_ref, wd_ref, o_ref,
                acc_ref, rank_c_ref, rank_r_ref, comb_t_ref, *, n_ctx, seq_s):
    e = pl.program_id(1)
    t_rows = FFN_TILE

    @pl.when(e == 0)
    def _():
        acc_ref[...] = jnp.zeros_like(acc_ref)
        comb = comb_ref[...]
        before = lax.broadcasted_iota(jnp.int32, (t_rows, t_rows), 1) < lax.broadcasted_iota(jnp.int32, (t_rows, t_rows), 0)
        rank_c_ref[...] = _dot(before.astype(BF16), (comb > 0.0).astype(BF16))
        comb_t = comb.T
        comb_t_ref[...] = comb_t
        rank_r_ref[...] = _dot_nt((comb_t > 0.0).astype(BF16), before.astype(BF16))

    comb = comb_ref[...]
    lane = lax.broadcasted_iota(jnp.int32, comb.shape, 1)
    w_col = jnp.sum(jnp.where(lane == e, comb, 0.0), axis=-1, keepdims=True)
    rank_col = jnp.sum(jnp.where(lane == e, rank_c_ref[...], 0.0), axis=-1, keepdims=True)
    w_row = comb_t_ref[pl.ds(e, 1), :]
    rank_row = rank_r_ref[pl.ds(e, 1), :]
    count = jnp.sum((w_row > 0.0).astype(jnp.int32))

    def body(j, carry):
        base = (j * MOE_ROWS).astype(F32)
        slot_r = lax.broadcasted_iota(jnp.int32, (MOE_ROWS, t_rows), 0).astype(F32) + base
        gather = jnp.where(jnp.logical_and(rank_row == slot_r, w_row > 0.0), 1.0, 0.0).astype(BF16)
        hg = _dot(gather, h_ref[...]).astype(BF16)
        t = _silu(_dot(hg, wg_ref[0])) * _dot(hg, wu_ref[0])
        y_hi, y_lo = _split(_dot(t.astype(BF16), wd_ref[0]))
        slot_c = lax.broadcasted_iota(jnp.int32, (t_rows, MOE_ROWS), 1).astype(F32) + base
        scatter = jnp.where(jnp.logical_and(rank_col == slot_c, w_col > 0.0), 1.0, 0.0).astype(BF16)
        acc_ref[...] += w_col * (_dot(scatter, y_hi) + _dot(scatter, y_lo))
        return carry

    lax.fori_loop(0, (count + MOE_ROWS - 1) // MOE_ROWS, body, 0)

    @pl.when(e == N_EXPERTS - 1)
    def _():
        row = _mod_row(pl.program_id(0), FFN_TILE, n_ctx, seq_s)
        g2 = mod_ref[pl.ds(row, 1), 5 * D_MODEL:6 * D_MODEL]
        o_ref[...] = x_ref[...] + g2 * acc_ref[...]


def _ffn_moe(x1, h2, comb, mod, w_gate, w_up, w_down, n_ctx, seq_s):
    n = x1.shape[0]
    tile = lambda w: pl.BlockSpec((FFN_TILE, w), lambda i, e: (i, 0))
    return pl.pallas_call(
        functools.partial(_moe_kernel, n_ctx=n_ctx, seq_s=seq_s),
        grid=(n // FFN_TILE, N_EXPERTS),
        in_specs=[tile(D_MODEL), tile(D_MODEL), tile(LANES),
                  pl.BlockSpec((8, 6 * D_MODEL), lambda i, e: (0, 0)),
                  pl.BlockSpec((1, D_MODEL, D_FF_E), lambda i, e: (e, 0, 0)),
                  pl.BlockSpec((1, D_MODEL, D_FF_E), lambda i, e: (e, 0, 0)),
                  pl.BlockSpec((1, D_FF_E, D_MODEL), lambda i, e: (e, 0, 0))],
        out_specs=tile(D_MODEL),
        out_shape=jax.ShapeDtypeStruct((n, D_MODEL), F32),
        scratch_shapes=[pltpu.VMEM((FFN_TILE, D_MODEL), F32), pltpu.VMEM((FFN_TILE, LANES), F32),
                        pltpu.VMEM((LANES, FFN_TILE), F32), pltpu.VMEM((LANES, FFN_TILE), F32)],
        compiler_params=_cparams(("parallel", "arbitrary")),
        name="ffn_moe",
    )(x1, h2, comb, mod, w_gate.astype(BF16), w_up.astype(BF16), w_down.astype(BF16))


def _final_kernel(x_ref, g_ref, o_ref):
    x = x_ref[...]
    o_ref[...] = x * lax.rsqrt(jnp.mean(x * x, axis=-1, keepdims=True) + RMS_EPS) * g_ref[...]


def _final_norm(x, gain, row0, n_rows):
    blk0 = row0 // FFN_TILE
    return pl.pallas_call(
        _final_kernel,
        grid=(n_rows // FFN_TILE,),
        in_specs=[pl.BlockSpec((FFN_TILE, D_MODEL), lambda i: (blk0 + i, 0)),
                  pl.BlockSpec((1, D_MODEL), lambda i: (0, 0))],
        out_specs=pl.BlockSpec((FFN_TILE, D_MODEL), lambda i: (i, 0)),
        out_shape=jax.ShapeDtypeStruct((n_rows, D_MODEL), F32),
        compiler_params=_cparams(("parallel",)),
        name="final_norm",
    )(x, gain.reshape(1, D_MODEL))


def kernel(x_prompt, x_sample, state_rwkv, c, c_ctx, norm1, norm2, w_ada, b_ada, w_in, dw_w, dw_b, conv_ln_g, conv_ln_b, w_conv_out, w_fno_out, shift_mu, g_up, dec_w0, dec_up, iclr_a0, iclr_up, k_k, k_a, r_k, gn_g, gn_b, w_rwkv_out, w_o, ffn_w_gate, ffn_w_up, ffn_w_down, w_router, b_router, moe_w_gate, moe_w_up, moe_w_down, final_norm):
    b_p, t_p, _ = x_prompt.shape
    b_s, t_s, _ = x_sample.shape
    depth = w_in.shape[0]
    n_ctx = b_p * t_p
    n_lat = b_s * t_s
    assert t_p == ROW_TILE and t_s % FFN_TILE == 0 and n_ctx % FFN_TILE == 0 and b_s <= CTX_ROW
    assert t_s % GRID_W == 0 and CHUNK == GRID_W

    n = n_ctx + n_lat
    x = (x_prompt.reshape(n_ctx, D_MODEL), x_sample.reshape(n_lat, D_MODEL))
    cond = jnp.zeros((8, D_MODEL), F32).at[:b_s].set(c).at[CTX_ROW].set(c_ctx)
    mods = _ada(cond, w_ada, b_ada)
    table = _scan_table(b_p, t_p, b_s, t_s)
    zero_state = jnp.zeros((b_p, 2, HEAD_DIM, C_RWKV), F32)
    pack_state = lambda s: s.transpose(0, 1, 3, 2, 4).reshape(s.shape[0], 2, HEAD_DIM, C_RWKV)
    unpack_state = lambda s: s.reshape(s.shape[0], 2, HEAD_DIM, N_HEADS, HEAD_DIM).transpose(0, 1, 3, 2, 4)

    ctx_states = []
    for l in range(depth):
        mod = mods[l]
        u_conv, u_fno, u_rw, u_gate = _inproj(x, n, mod, norm1[l], w_in[l].astype(BF16), n_ctx, t_s)
        conv_h = _conv_branch(u_conv, dw_w[l], dw_b[l], conv_ln_g[l], conv_ln_b[l], n_ctx)
        fno_h = (_fno_branch(u_fno, 0, b_p, t_p), _fno_branch(u_fno, n_ctx, b_s, t_s))
        r, k, v, kkn, g, lw, a = _rwkv_prep(u_rw, shift_mu[l], g_up[l], dec_w0[l], dec_up[l], iclr_a0[l],
                                             iclr_up[l], k_k[l], n_ctx, t_s)
        s0 = jnp.concatenate([zero_state, pack_state(state_rwkv[:, l])], axis=0)
        yf, yb, s_fin = _carry_states(*_chunk_summaries(r, k, v, kkn, lw, a, k_a[l]), s0, table)
        ctx_states.append(unpack_state(s_fin[:b_p]))
        i = l // 2
        if l % 2 == 0:
            w_rt, b_rt = jnp.zeros((D_MODEL, N_EXPERTS), F32), jnp.zeros((N_EXPERTS,), F32)
        else:
            w_rt, b_rt = w_router[i], b_router[i]
        x1, h2, comb = _merge(x, n, conv_h, fno_h, yf, yb, r, k, v, g, u_gate, mod, w_conv_out[l], w_fno_out[l],
                              w_rwkv_out[l], w_o[l], gn_g[l], gn_b[l], r_k[l], norm2[l], w_rt, b_rt, n_ctx, t_s)
        if l % 2 == 0:
            x = _ffn_dense(x1, h2, mod, ffn_w_gate[i], ffn_w_up[i], ffn_w_down[i], n_ctx, t_s)
        else:
            x = _ffn_moe(x1, h2, comb, mod, moe_w_gate[i], moe_w_up[i], moe_w_down[i], n_ctx, t_s)

    y_prompt = _final_norm(x, final_norm, 0, n_ctx).reshape(b_p, t_p, D_MODEL)
    y_sample = _final_norm(x, final_norm, n_ctx, n_lat).reshape(b_s, t_s, D_MODEL)
    new_state = jnp.stack(ctx_states, axis=1).astype(x_prompt.dtype)
    return (y_prompt, y_sample, new_state)
```

```python
import functools

import numpy as np
import jax
import jax.numpy as jnp
from jax import lax
from jax.experimental import pallas as pl
from jax.experimental.pallas import tpu as pltpu

F32 = jnp.float32
BF16 = jnp.bfloat16

D_MODEL = 1024
GRID_W = 64
C_CONV = 256
CONV_W = 31
C_FNO = 256
FNO_GW = 64
N_HEADS = 8
HEAD_DIM = 64
C_RWKV = N_HEADS * HEAD_DIM
G_RANK = 128
W_RANK = 64
A_RANK = 64
D_FF = 2816
N_EXPERTS = 8
D_FF_E = 1408
RMS_EPS = 1e-6
LN_EPS = 1e-5
GN_EPS = 64e-5

OFF_FNO = 2 * C_CONV
OFF_RWKV = OFF_FNO + C_FNO
N_SHIFT = 3 * C_RWKV + G_RANK + 2 * (W_RANK + A_RANK)
OFF_GATE = OFF_RWKV + N_SHIFT
D_IN = OFF_GATE + 3 * D_MODEL

ROW_TILE = 256
FFN_TILE = 512
CHUNK = 64
GROUP = 4
GROUP_W = GROUP * HEAD_DIM
SUMMARY_CHUNKS = 4
CARRY_CHUNKS = 4
MOE_ROWS = 160
CTX_ROW = 4
LANES = 128
VMEM_LIMIT = 56 * 1024 * 1024


def _cparams(sem):
    return pltpu.CompilerParams(dimension_semantics=sem, vmem_limit_bytes=VMEM_LIMIT)


def _sigmoid(x):
    return 1.0 / (1.0 + jnp.exp(-x))


def _silu(x):
    return x * _sigmoid(x)


def _dot(a, b):
    return jnp.dot(a, b, preferred_element_type=F32)


def _split(x):
    hi = x.astype(BF16)
    lo = (x - hi.astype(F32)).astype(BF16)
    return hi, lo


def _dot_x3(a, b):
    ah, al = _split(a)
    bh, bl = _split(b)
    return _dot(ah, bh) + (_dot(ah, bl) + _dot(al, bh))


def _seg_sum(x, bd, split=True):
    if not split:
        return _dot(x.astype(BF16), bd)
    hi, lo = _split(x)
    return _dot(hi, bd) + _dot(lo, bd)


def _mod_row(i, tile, n_ctx_rows, seq_s):
    n_ctx_tiles = n_ctx_rows // tile
    return jnp.where(i < n_ctx_tiles, CTX_ROW, (i - n_ctx_tiles) // (seq_s // tile))


def _stream_specs(tile, width, n_ctx, lat_block0):
    n_ctx_tiles = n_ctx // tile
    return [pl.BlockSpec((tile, width), lambda i: (jnp.minimum(i, n_ctx_tiles - 1), 0)),
            pl.BlockSpec((tile, width), lambda i: (jnp.maximum(i - n_ctx_tiles, 0) + lat_block0, 0))]


def _stream_args(x, tile, n_ctx):
    if isinstance(x, tuple):
        return x[0], x[1], 0
    return x, x, n_ctx // tile


def _pick_stream(i, tile, n_ctx, ctx_ref, lat_ref):
    return jnp.where(i < n_ctx // tile, ctx_ref[...], lat_ref[...])


def _ada_kernel(c_ref, w_ref, b_ref, o_ref):
    s = _silu(c_ref[...])
    o_ref[0] = _dot(s.astype(BF16), w_ref[0].astype(BF16)) + b_ref[0]


def _ada(cond, w_ada, b_ada):
    n_l = w_ada.shape[0]
    tn = 1536
    return pl.pallas_call(
        _ada_kernel,
        grid=(n_l, 6 * D_MODEL // tn),
        in_specs=[
            pl.BlockSpec((8, D_MODEL), lambda l, j: (0, 0)),
            pl.BlockSpec((1, D_MODEL, tn), lambda l, j: (l, 0, j)),
            pl.BlockSpec((1, 1, tn), lambda l, j: (l, 0, j)),
        ],
        out_specs=pl.BlockSpec((1, 8, tn), lambda l, j: (l, 0, j)),
        out_shape=jax.ShapeDtypeStruct((n_l, 8, 6 * D_MODEL), F32),
        compiler_params=_cparams(("parallel", "parallel")),
        name="ada",
    )(cond, w_ada, b_ada.reshape(n_l, 1, 6 * D_MODEL))


def _inproj_kernel(xc_ref, xl_ref, mod_ref, g_ref, w_ref, oc_ref, of_ref, or_ref, *, n_ctx, seq_s):
    row = _mod_row(pl.program_id(0), ROW_TILE, n_ctx, seq_s)
    m = mod_ref[pl.ds(row, 1), :]
    sh = m[:, 0:D_MODEL]
    sc = m[:, D_MODEL:2 * D_MODEL]
    x = _pick_stream(pl.program_id(0), ROW_TILE, n_ctx, xc_ref, xl_ref)
    y = x * lax.rsqrt(jnp.mean(x * x, axis=-1, keepdims=True) + RMS_EPS) * g_ref[...]
    h = (y * (1.0 + sc) + sh).astype(BF16)
    oc_ref[...] = _dot(h, w_ref[:, 0:OFF_FNO])
    of_ref[...] = _dot(h, w_ref[:, OFF_FNO:OFF_RWKV])
    or_ref[...] = _dot(h, w_ref[:, OFF_RWKV:OFF_GATE])


def _inproj(x, n, mod, gain, w_in, n_ctx, seq_s):
    xc, xl, lat0 = _stream_args(x, ROW_TILE, n_ctx)
    widths = (OFF_FNO, C_FNO, N_SHIFT)
    return pl.pallas_call(
        functools.partial(_inproj_kernel, n_ctx=n_ctx, seq_s=seq_s),
        grid=(n // ROW_TILE,),
        in_specs=_stream_specs(ROW_TILE, D_MODEL, n_ctx, lat0) + [
            pl.BlockSpec((8, 6 * D_MODEL), lambda i: (0, 0)),
            pl.BlockSpec((1, D_MODEL), lambda i: (0, 0)),
            pl.BlockSpec((D_MODEL, OFF_GATE), lambda i: (0, 0)),
        ],
        out_specs=[pl.BlockSpec((ROW_TILE, w), lambda i: (i, 0)) for w in widths],
        out_shape=[jax.ShapeDtypeStruct((n, w), F32) for w in widths],
        compiler_params=_cparams(("parallel",)),
        name="inproj",
    )(xc, xl, mod, gain.reshape(1, D_MODEL), w_in)


_CONV_HALO = 16
_CONV_ROWS = 64


def _conv_kernel(u_ref, w_ref, b_ref, g_ref, be_ref, o_ref, pad_ref, sh_ref, *, n_ctx):
    is_ctx = pl.program_id(0) < n_ctx // ROW_TILE
    zeros = jnp.zeros((_CONV_HALO, C_CONV), F32)
    n_parts = ROW_TILE // _CONV_ROWS

    def glu(lo, hi):
        return u_ref[lo:hi, 0:C_CONV] * _sigmoid(u_ref[lo:hi, C_CONV:2 * C_CONV])

    def finish(starts, n_rows):
        for s in range(8):
            sh_ref[s, 0:n_rows - 8, :] = pad_ref[s:s + n_rows - 8, :]
        for p in range(n_parts):
            acc = jnp.zeros((_CONV_ROWS, C_CONV), F32)
            for j in range(CONV_W):
                o = starts[p] + j
                acc = acc + w_ref[j:j + 1, :] * sh_ref[o % 8, o - o % 8:o - o % 8 + _CONV_ROWS, :]
            acc = acc + b_ref[...]
            mu = jnp.mean(acc, axis=-1, keepdims=True)
            d = acc - mu
            var = jnp.mean(d * d, axis=-1, keepdims=True)
            y = d * lax.rsqrt(var + LN_EPS) * g_ref[...] + be_ref[...]
            o_ref[p * _CONV_ROWS:(p + 1) * _CONV_ROWS, :] = _silu(y).astype(o_ref.dtype)

    shift = _CONV_HALO - CONV_W // 2

    @pl.when(is_ctx)
    def _():
        pad_ref[0:_CONV_HALO, :] = zeros
        pad_ref[_CONV_HALO:_CONV_HALO + ROW_TILE, :] = glu(0, ROW_TILE)
        pad_ref[_CONV_HALO + ROW_TILE:2 * _CONV_HALO + ROW_TILE, :] = zeros
        finish([shift + p * _CONV_ROWS for p in range(n_parts)], ROW_TILE + 2 * _CONV_HALO)

    @pl.when(jnp.logical_not(is_ctx))
    def _():
        stride = GRID_W + 2 * _CONV_HALO
        for p in range(n_parts):
            pad_ref[p * stride:p * stride + _CONV_HALO, :] = zeros
            pad_ref[p * stride + _CONV_HALO:p * stride + _CONV_HALO + GRID_W, :] = glu(p * GRID_W, (p + 1) * GRID_W)
            pad_ref[p * stride + _CONV_HALO + GRID_W:(p + 1) * stride, :] = zeros
        finish([p * stride + shift for p in range(n_parts)], n_parts * stride)


def _conv_branch(u_conv, dw_w, dw_b, ln_g, ln_b, n_ctx):
    n = u_conv.shape[0]
    assert _CONV_ROWS == GRID_W and ROW_TILE % GRID_W == 0
    vec = pl.BlockSpec((1, C_CONV), lambda i: (0, 0))
    return pl.pallas_call(
        functools.partial(_conv_kernel, n_ctx=n_ctx),
        grid=(n // ROW_TILE,),
        in_specs=[
            pl.BlockSpec((ROW_TILE, 2 * C_CONV), lambda i: (i, 0)),
            pl.BlockSpec((CONV_W, C_CONV), lambda i: (0, 0)),
            vec, vec, vec,
        ],
        out_specs=pl.BlockSpec((ROW_TILE, C_CONV), lambda i: (i, 0)),
        out_shape=jax.ShapeDtypeStruct((n, C_CONV), BF16),
        scratch_shapes=[pltpu.VMEM((ROW_TILE // GRID_W * (GRID_W + 2 * _CONV_HALO), C_CONV), F32),
                        pltpu.VMEM((8, ROW_TILE // GRID_W * (GRID_W + 2 * _CONV_HALO), C_CONV), F32)],
        compiler_params=_cparams(("parallel",)),
        name="conv_branch",
    )(u_conv, dw_w, dw_b.reshape(1, C_CONV), ln_g.reshape(1, C_CONV), ln_b.reshape(1, C_CONV))


def _dft_tables(t_len):
    def cs(n):
        k = np.arange(n, dtype=np.int64)
        ang = 2.0 * np.pi * ((k[:, None] * k[None, :]) % n).astype(np.float64) / n
        return np.cos(ang), np.sin(ang)
    cg, sg = cs(FNO_GW)
    eye = np.eye(C_FNO // FNO_GW)
    w1 = np.concatenate([np.kron(eye, cg), np.kron(eye, sg)], axis=1)
    ct, st = cs(t_len)
    w2 = np.concatenate([ct, -st], axis=1)
    return jnp.asarray(w1, dtype=F32).astype(BF16), jnp.asarray(w2, dtype=F32).astype(BF16)


def _fno_kernel(u_ref, w1_ref, w2_ref, o_ref, hs_ref, *, t_len, scale):
    @pl.when(pl.program_id(1) == 0)
    def _():
        hc = _dot(u_ref[...].astype(BF16), w1_ref[...])
        hs_ref[0:t_len, :] = hc[:, 0:C_FNO].astype(BF16)
        hs_ref[t_len:2 * t_len, :] = hc[:, C_FNO:2 * C_FNO].astype(BF16)

    o_ref[...] = (_dot(w2_ref[...], hs_ref[...]) * scale).astype(o_ref.dtype)


def _fno_branch(u_fno, row0, n_seq, t_len):
    w1, w2 = _dft_tables(t_len)
    tk = min(t_len, 512)
    blk0 = row0 // t_len
    return pl.pallas_call(
        functools.partial(_fno_kernel, t_len=t_len, scale=float(1.0 / np.sqrt(t_len * FNO_GW))),
        grid=(n_seq, t_len // tk),
        in_specs=[
            pl.BlockSpec((t_len, C_FNO), lambda b, j: (blk0 + b, 0)),
            pl.BlockSpec((C_FNO, 2 * C_FNO), lambda b, j: (0, 0)),
            pl.BlockSpec((tk, 2 * t_len), lambda b, j: (j, 0)),
        ],
        out_specs=pl.BlockSpec((tk, C_FNO), lambda b, j: (b * (t_len // tk) + j, 0)),
        out_shape=jax.ShapeDtypeStruct((n_seq * t_len, C_FNO), BF16),
        scratch_shapes=[pltpu.VMEM((2 * t_len, C_FNO), BF16)],
        compiler_params=_cparams(("parallel", "arbitrary")),
        name="fno_branch",
    )(u_fno, w1, w2)


def _prep_kernel(z_ref, zp_ref, zn_ref, mu_ref, gup_ref, w0_ref, dup_ref, a0_ref, aup_ref, kk_ref, bd_ref,
                 r_ref, k_ref, v_ref, kkn_ref, g_ref, lw_ref, a_ref, pad_ref, *, n_ctx, seq_s):
    i = pl.program_id(0)
    n_ctx_tiles = n_ctx // ROW_TILE
    per_seq = seq_s // ROW_TILE
    j = (i - n_ctx_tiles) % per_seq
    first = jnp.logical_or(i < n_ctx_tiles, j == 0)
    last = jnp.logical_or(i < n_ctx_tiles, j == per_seq - 1)
    pad_ref[8:8 + ROW_TILE, :] = z_ref[...]
    pad_ref[0:8, :] = jnp.where(first, 0.0, zp_ref[...])
    pad_ref[8 + ROW_TILE:16 + ROW_TILE, :] = jnp.where(last, 0.0, zn_ref[...])

    def shifted(lo, hi):
        z = pad_ref[8:8 + ROW_TILE, lo:hi]
        zp = pad_ref[7:7 + ROW_TILE, lo:hi]
        zn = pad_ref[9:9 + ROW_TILE, lo:hi]
        return z + mu_ref[:, lo:hi] * (0.5 * (zp + zn) - z)

    c = C_RWKV
    r_ref[...] = shifted(0, c)
    k = shifted(c, 2 * c)
    k_ref[...] = k
    v_ref[...] = shifted(2 * c, 3 * c)
    kx = k * kk_ref[...]
    nrm = jnp.sqrt(_seg_sum(kx * kx, bd_ref[...], split=False))
    kkn_ref[...] = kx / jnp.maximum(nrm, 1e-12)
    o = 3 * c
    g_ref[...] = _dot(_sigmoid(shifted(o, o + G_RANK)).astype(BF16), gup_ref[...])
    o += G_RANK
    xw = _dot(jnp.tanh(shifted(o, o + 2 * W_RANK)).astype(BF16), dup_ref[...]) + w0_ref[...]
    soft = jnp.maximum(-xw, 0.0) + jnp.log(1.0 + jnp.exp(-jnp.abs(xw)))
    lw_ref[...] = -jnp.exp(-soft - 0.5)
    o += 2 * W_RANK
    xa = _dot(shifted(o, o + 2 * A_RANK).astype(BF16), aup_ref[...]) + a0_ref[...]
    a_ref[...] = _sigmoid(xa)


def _block_diag2(w):
    z = jnp.zeros_like(w[0])
    return jnp.concatenate([jnp.concatenate([w[0], z], axis=1), jnp.concatenate([z, w[1]], axis=1)], axis=0)


def _head_ones():
    return jnp.asarray(np.kron(np.eye(N_HEADS), np.ones((HEAD_DIM, HEAD_DIM))), dtype=BF16)


def _rwkv_prep(u_rw, mu, g_up, dec_w0, dec_up, iclr_a0, iclr_up, k_k, n_ctx, seq_s):
    n = u_rw.shape[0]
    halo = ROW_TILE // 8
    last_blk = n // 8 - 1
    full = lambda shape: pl.BlockSpec(shape, lambda i: (0,) * len(shape))
    c = C_RWKV
    out_w = (c, c, c, c, c, 2 * c, 2 * c)
    return pl.pallas_call(
        functools.partial(_prep_kernel, n_ctx=n_ctx, seq_s=seq_s),
        grid=(n // ROW_TILE,),
        in_specs=[
            pl.BlockSpec((ROW_TILE, N_SHIFT), lambda i: (i, 0)),
            pl.BlockSpec((8, N_SHIFT), lambda i: (jnp.maximum(i * halo - 1, 0), 0)),
            pl.BlockSpec((8, N_SHIFT), lambda i: (jnp.minimum((i + 1) * halo, last_blk), 0)),
            full((1, N_SHIFT)), full((G_RANK, c)), full((1, 2 * c)), full((2 * W_RANK, 2 * c)),
            full((1, 2 * c)), full((2 * A_RANK, 2 * c)), full((1, c)), full((c, c)),
        ],
        out_specs=[pl.BlockSpec((ROW_TILE, w), lambda i: (i, 0)) for w in out_w],
        out_shape=[jax.ShapeDtypeStruct((n, w), F32) for w in out_w],
        scratch_shapes=[pltpu.VMEM((ROW_TILE + 16, N_SHIFT), F32)],
        compiler_params=_cparams(("parallel",)),
        name="rwkv_prep",
    )(u_rw, u_rw, u_rw, mu.reshape(1, N_SHIFT), g_up.astype(BF16), dec_w0.reshape(1, 2 * c),
      _block_diag2(dec_up).astype(BF16), iclr_a0.reshape(1, 2 * c), _block_diag2(iclr_up).astype(BF16),
      k_k.reshape(1, c), _head_ones())


def _block_diag(x, mask):
    xb = x.astype(BF16)
    return jnp.where(mask, jnp.concatenate([xb] * GROUP, axis=0), jnp.zeros((), BF16))


def _dot_nt(a, b):
    return lax.dot_general(a, b, (((1,), (1,)), ((), ())), preferred_element_type=F32)


def _diag_blocks(prod, lane_head):
    out = jnp.where(lane_head == 0, prod[0:HEAD_DIM], 0.0)
    for h in range(1, GROUP):
        out = out + jnp.where(lane_head == h, prod[h * HEAD_DIM:(h + 1) * HEAD_DIM], 0.0)
    return out


def _summary_kernel(r_ref, k_ref, v_ref, n_ref, lw_ref, a_ref, ka_ref, q_ref, y0_ref, p_ref, z_ref, g_ref):
    row = lax.broadcasted_iota(jnp.int32, (CHUNK, GROUP_W), 0)
    lane = lax.broadcasted_iota(jnp.int32, (CHUNK, GROUP_W), 1)
    col = lane % CHUNK
    lane_head = lane // HEAD_DIM
    bd_mask = (lax.broadcasted_iota(jnp.int32, (GROUP * CHUNK, GROUP_W), 0) // CHUNK
               == lax.broadcasted_iota(jnp.int32, (GROUP * CHUNK, GROUP_W), 1) // HEAD_DIM)
    bd = lambda x: _block_diag(x, bd_mask)
    ka = ka_ref[...]

    units = []
    for j in range(SUMMARY_CHUNKS):
        rows = slice(j * CHUNK, (j + 1) * CHUNK)
        r = r_ref[rows, :]
        k = k_ref[rows, :]
        v = v_ref[rows, :]
        kkn = n_ref[rows, :]
        for d in range(2):
            lanes = slice(d * C_RWKV, (d + 1) * C_RWKV)
            earlier = (col < row) if d == 0 else (col > row)
            upto = jnp.logical_or(earlier, row == col)
            tri = upto[:, 0:CHUNK].astype(BF16)
            lw = lw_ref[rows, lanes]
            h1 = lw.astype(BF16)
            r1 = lw - h1.astype(F32)
            h2 = r1.astype(BF16)
            h3 = (r1 - h2.astype(F32)).astype(BF16)
            cum = _dot(tri, h1) + (_dot(tri, h2) + _dot(tri, h3))
            e_in = jnp.exp(cum)
            e_ex = jnp.exp(cum - lw)
            e_ng = jnp.exp(-cum)
            a = a_ref[rows, lanes]
            kd = k * (1.0 + (a - 1.0) * ka)
            at = -kkn * e_ex
            rt = r * e_in
            bt = kkn * a * e_ng
            kt = kd * e_ng
            end = CHUNK - 1 if d == 0 else 0
            g_end = e_in[end:end + 1, :]
            g_ref[j, :, lanes] = g_end
            bh = bt * g_end
            kh = kt * g_end
            for q in range(C_RWKV // GROUP_W):
                sl = slice(q * GROUP_W, (q + 1) * GROUP_W)
                units.append(dict(rows=rows, out=slice(d * C_RWKV + q * GROUP_W, d * C_RWKV + (q + 1) * GROUP_W),
                                  earlier=earlier, upto=upto, at=at[:, sl], rt=rt[:, sl],
                                  ar=jnp.concatenate([at[:, sl], rt[:, sl]], axis=0).astype(BF16),
                                  bt=bt[:, sl], kt=kt[:, sl], v=v[:, sl], bh=bh[:, sl].astype(BF16),
                                  bk=jnp.concatenate([bh[:, sl], kh[:, sl]], axis=0).astype(BF16)))

    for u in units:
        sb = _dot_nt(u["ar"], bd(u["bt"]))
        sk = _dot_nt(u["ar"], bd(u["kt"]))
        u["lab"] = jnp.where(u["earlier"], sb[0:CHUNK], 0.0)
        u["mrb"] = jnp.where(u["upto"], sb[CHUNK:2 * CHUNK], 0.0).astype(BF16)
        u["lm"] = jnp.concatenate([jnp.where(u["earlier"], sk[0:CHUNK], 0.0),
                                   jnp.where(u["upto"], sk[CHUNK:2 * CHUNK], 0.0)], axis=0).astype(BF16)

    eye = (row == col).astype(F32)
    pair = jnp.logical_and(row // 2 == col // 2, row != col)
    for u in units:
        u["t"] = eye + jnp.where(pair, u["lab"], 0.0)
    n = 2
    while n < CHUNK:
        m = jnp.logical_and(row // (2 * n) == col // (2 * n), row // n != col // n)
        for u in units:
            u["w"] = _dot(jnp.where(m, u["lab"], 0.0).astype(BF16), bd(u["t"]))
        for u in units:
            u["t"] = u["t"] + _dot(u["t"].astype(BF16), bd(u["w"]))
        n *= 2

    for u in units:
        u["t"] = u["t"].astype(BF16)
        u["wm"] = _dot(u["t"], bd(u["at"]))
        u["lv"] = _dot(u["lm"], bd(u["v"]))
    for u in units:
        u["u0"] = _dot(u["t"], bd(u["lv"][0:CHUNK]))
    for u in units:
        q_ref[u["rows"], u["out"]] = (u["rt"] + _dot(u["mrb"], bd(u["wm"]))).astype(q_ref.dtype)
        y0_ref[u["rows"], u["out"]] = u["lv"][CHUNK:2 * CHUNK] + _dot(u["mrb"], bd(u["u0"]))
    for u in units:
        p = _dot(u["wm"].T.astype(BF16), u["bh"])
        p_ref[u["rows"], u["out"]] = _diag_blocks(p, lane_head).astype(p_ref.dtype)
        z = _dot(jnp.concatenate([u["u0"], u["v"]], axis=0).T.astype(BF16), u["bk"])
        z_ref[u["rows"], u["out"]] = _diag_blocks(z, lane_head)


def _chunk_summaries(r, k, v, kkn, lw, a, k_a):
    n = r.shape[0]
    tile_rows = SUMMARY_CHUNKS * CHUNK
    assert CHUNK == HEAD_DIM and C_RWKV % GROUP_W == 0 and n % tile_rows == 0
    tile = lambda w: pl.BlockSpec((tile_rows, w), lambda i: (i, 0))
    c = C_RWKV
    return pl.pallas_call(
        _summary_kernel,
        grid=(n // tile_rows,),
        in_specs=[tile(c), tile(c), tile(c), tile(c), tile(2 * c), tile(2 * c),
                  pl.BlockSpec((1, c), lambda i: (0, 0))],
        out_specs=[tile(2 * c), tile(2 * c), tile(2 * c), tile(2 * c),
                   pl.BlockSpec((SUMMARY_CHUNKS, 1, 2 * c), lambda i: (i, 0, 0))],
        out_shape=[jax.ShapeDtypeStruct((n, 2 * c), BF16), jax.ShapeDtypeStruct((n, 2 * c), F32),
                   jax.ShapeDtypeStruct((n, 2 * c), BF16), jax.ShapeDtypeStruct((n, 2 * c), F32),
                   jax.ShapeDtypeStruct((n // CHUNK, 1, 2 * c), F32)],
        compiler_params=_cparams(("parallel",)),
        name="rwkv_chunk_summaries",
    )(r, k, v, kkn, lw, a, k_a.reshape(1, c))


def _carry_kernel(tab_ref, qf_ref, y0f_ref, pf_ref, zf_ref, gf_ref, qb_ref, y0b_ref, pb_ref, zb_ref, gb_ref,
                  s0_ref, yf_ref, yb_ref, sfin_ref, st_ref):
    step = pl.program_id(0)

    @pl.when(tab_ref[3, step] == 1)
    def _():
        st_ref[...] = s0_ref[0]

    bd_mask = (lax.broadcasted_iota(jnp.int32, (GROUP * CHUNK, GROUP_W), 0) // CHUNK
               == lax.broadcasted_iota(jnp.int32, (GROUP * CHUNK, GROUP_W), 1) // HEAD_DIM)
    for i in range(CARRY_CHUNKS):
        for d, (q_ref, y0_ref, p_ref, z_ref, g_ref, y_ref) in enumerate(
                ((qf_ref, y0f_ref, pf_ref, zf_ref, gf_ref, yf_ref), (qb_ref, y0b_ref, pb_ref, zb_ref, gb_ref, yb_ref))):
            j = i if d == 0 else CARRY_CHUNKS - 1 - i
            rows = slice(j * CHUNK, (j + 1) * CHUNK)
            for q in range(C_RWKV // GROUP_W):
                sl = slice(q * GROUP_W, (q + 1) * GROUP_W)
                s = st_ref[d, :, sl]
                y_ref[rows, sl] = y0_ref[rows, sl] + _dot_nt(q_ref[rows, sl], _block_diag(s, bd_mask))
                st_ref[d, :, sl] = (s * g_ref[j, :, sl] + z_ref[rows, sl]
                                    + _dot(s.astype(BF16), _block_diag(p_ref[rows, sl], bd_mask)))

    @pl.when(tab_ref[4, step] == 1)
    def _():
        sfin_ref[0] = st_ref[...]


def _scan_table(n_ctx_seq, t_ctx, n_s_seq, t_s):
    rows = []
    base = 0
    seq = 0
    for n_seq, t_len in ((n_ctx_seq, t_ctx), (n_s_seq, t_s)):
        assert t_len % (CARRY_CHUNKS * CHUNK) == 0
        n_c = t_len // (CARRY_CHUNKS * CHUNK)
        for _ in range(n_seq):
            for c in range(n_c):
                rows.append((base + c, base + n_c - 1 - c, seq, int(c == 0), int(c == n_c - 1)))
            base += n_c
            seq += 1
    return np.asarray(rows, dtype=np.int32).T.copy()


def _carry_states(q, y0, p, z, g, s0, table):
    n = q.shape[0]
    n_seq = s0.shape[0]
    c = C_RWKV

    def tok(which, d):
        return pl.BlockSpec((CARRY_CHUNKS * CHUNK, c), lambda s, tab: (tab[which, s], d))

    def decay(which, d):
        return pl.BlockSpec((CARRY_CHUNKS, 1, c), lambda s, tab: (tab[which, s], 0, d))

    state_spec = pl.BlockSpec((1, 2, HEAD_DIM, c), lambda s, tab: (tab[2, s], 0, 0, 0))
    grid_spec = pltpu.PrefetchScalarGridSpec(
        num_scalar_prefetch=1,
        grid=(table.shape[1],),
        in_specs=[tok(0, 0), tok(0, 0), tok(0, 0), tok(0, 0), decay(0, 0),
                  tok(1, 1), tok(1, 1), tok(1, 1), tok(1, 1), decay(1, 1), state_spec],
        out_specs=[tok(0, 0), tok(1, 0), state_spec],
        scratch_shapes=[pltpu.VMEM((2, HEAD_DIM, c), F32)],
    )
    return pl.pallas_call(
        _carry_kernel,
        grid_spec=grid_spec,
        out_shape=[jax.ShapeDtypeStruct((n, c), F32), jax.ShapeDtypeStruct((n, c), F32),
                   jax.ShapeDtypeStruct((n_seq, 2, HEAD_DIM, c), F32)],
        compiler_params=_cparams(("arbitrary",)),
        name="rwkv_carry",
    )(jnp.asarray(table), q, y0, p, z, g, q, y0, p, z, g, s0)


def _merge_kernel(xc_ref, xl_ref, c_ref, fc_ref, fl_ref, yf_ref, yb_ref, r_ref, k_ref, v_ref, g_ref, n1_ref, wgt_ref, mod_ref,
                  wc_ref, wf_ref, wr_ref, wo_ref, gng_ref, gnb_ref, rk_ref, bd_ref, n2_ref, wrt_ref, brt_ref,
                  x1_ref, h2_ref, comb_ref, *, n_ctx, seq_s):
    row = _mod_row(pl.program_id(0), ROW_TILE, n_ctx, seq_s)
    m = mod_ref[pl.ds(row, 1), :]
    sh1 = m[:, 0:D_MODEL]
    sc1 = m[:, D_MODEL:2 * D_MODEL]
    g1 = m[:, 2 * D_MODEL:3 * D_MODEL]
    sh2 = m[:, 3 * D_MODEL:4 * D_MODEL]
    sc2 = m[:, 4 * D_MODEL:5 * D_MODEL]
    bd = bd_ref[...]
    inv = 1.0 / HEAD_DIM
    y = yf_ref[...] + yb_ref[...]
    d = y - _seg_sum(y, bd) * inv
    var = _seg_sum(d * d, bd, split=False) * inv
    yn = d * lax.rsqrt(var + GN_EPS) * gng_ref[...] + gnb_ref[...]
    v = v_ref[...]
    yn = yn + _seg_sum(r_ref[...] * k_ref[...] * rk_ref[...], bd, split=False) * v
    yr = _dot((yn * g_ref[...]).astype(BF16), wr_ref[...])
    yc = _dot(c_ref[...], wc_ref[...])
    yf = _dot(_pick_stream(pl.program_id(0), ROW_TILE, n_ctx, fc_ref, fl_ref), wf_ref[...])
    x = _pick_stream(pl.program_id(0), ROW_TILE, n_ctx, xc_ref, xl_ref)
    hn = x * lax.rsqrt(jnp.mean(x * x, axis=-1, keepdims=True) + RMS_EPS) * n1_ref[...]
    hn = (hn * (1.0 + sc1) + sh1).astype(BF16)
    dm = D_MODEL
    merged = (_sigmoid(_dot(hn, wgt_ref[:, 0:dm])) * yc + _sigmoid(_dot(hn, wgt_ref[:, dm:2 * dm])) * yf
              + _sigmoid(_dot(hn, wgt_ref[:, 2 * dm:3 * dm])) * yr)
    x1 = x + g1 * _dot(merged.astype(BF16), wo_ref[...])
    x1_ref[...] = x1
    h2 = x1 * lax.rsqrt(jnp.mean(x1 * x1, axis=-1, keepdims=True) + RMS_EPS) * n2_ref[...]
    h2 = h2 * (1.0 + sc2) + sh2
    h2_ref[...] = h2.astype(BF16)
    logits = _dot_x3(h2, wrt_ref[...]) + brt_ref[...]
    lane = lax.broadcasted_iota(jnp.int32, logits.shape, 1).astype(F32)
    neg = jnp.float32(-jnp.inf)
    lg = jnp.where(lane < N_EXPERTS, logits, neg)
    m1 = jnp.max(lg, axis=-1, keepdims=True)
    i1 = jnp.min(jnp.where(lg == m1, lane, float(LANES)), axis=-1, keepdims=True)
    lg2 = jnp.where(lane == i1, neg, lg)
    m2 = jnp.max(lg2, axis=-1, keepdims=True)
    i2 = jnp.min(jnp.where(lg2 == m2, lane, float(LANES)), axis=-1, keepdims=True)
    e = jnp.exp(m2 - m1)
    p1 = 1.0 / (1.0 + e)
    comb_ref[...] = jnp.where(lane == i1, p1, 0.0) + jnp.where(lane == i2, e * p1, 0.0)


def _merge(x, n, conv_h, fno_h, yf, yb, r, k, v, g, gain1, w_gates, mod, w_conv_out, w_fno_out, w_rwkv_out, w_o,
           gn_g, gn_b, r_k, gain2, w_router, b_router, n_ctx, seq_s):
    xc, xl, x_lat0 = _stream_args(x, ROW_TILE, n_ctx)
    fc, fl, f_lat0 = _stream_args(fno_h, ROW_TILE, n_ctx)
    c = C_RWKV
    tile = lambda w: pl.BlockSpec((ROW_TILE, w), lambda i: (i, 0))
    full = lambda shape: pl.BlockSpec(shape, lambda i: (0,) * len(shape))
    wrt = jnp.zeros((D_MODEL, LANES), F32).at[:, :N_EXPERTS].set(w_router)
    brt = jnp.zeros((1, LANES), F32).at[0, :N_EXPERTS].set(b_router)
    return pl.pallas_call(
        functools.partial(_merge_kernel, n_ctx=n_ctx, seq_s=seq_s),
        grid=(n // ROW_TILE,),
        in_specs=_stream_specs(ROW_TILE, D_MODEL, n_ctx, x_lat0) + [tile(C_CONV)]
        + _stream_specs(ROW_TILE, C_FNO, n_ctx, f_lat0) + [
            tile(c), tile(c), tile(c), tile(c), tile(c), tile(c),
            full((1, D_MODEL)), full((D_MODEL, 3 * D_MODEL)), full((8, 6 * D_MODEL)),
            full((C_CONV, D_MODEL)), full((C_FNO, D_MODEL)), full((c, D_MODEL)), full((D_MODEL, D_MODEL)),
            full((1, c)), full((1, c)), full((1, c)), full((c, c)), full((1, D_MODEL)),
            full((D_MODEL, LANES)), full((1, LANES)),
        ],
        out_specs=[tile(D_MODEL), tile(D_MODEL), tile(LANES)],
        out_shape=[jax.ShapeDtypeStruct((n, D_MODEL), F32), jax.ShapeDtypeStruct((n, D_MODEL), BF16),
                   jax.ShapeDtypeStruct((n, LANES), F32)],
        compiler_params=_cparams(("parallel",)),
        name="merge",
    )(xc, xl, conv_h, fc, fl, yf, yb, r, k, v, g, gain1.reshape(1, D_MODEL), w_gates, mod,
      w_conv_out.astype(BF16), w_fno_out.astype(BF16), w_rwkv_out.astype(BF16), w_o.astype(BF16),
      gn_g.reshape(1, c), gn_b.reshape(1, c), r_k.reshape(1, c), _head_ones(), gain2.reshape(1, D_MODEL), wrt, brt)


def _ffn_kernel(x_ref, h_ref, mod_ref, wg_ref, wu_ref, wd_ref, o_ref, *, n_ctx, seq_s):
    row = _mod_row(pl.program_id(0), FFN_TILE, n_ctx, seq_s)
    g2 = mod_ref[pl.ds(row, 1), 5 * D_MODEL:6 * D_MODEL]
    h = h_ref[...]
    half = D_FF // 2
    acc = jnp.zeros((FFN_TILE, D_MODEL), F32)
    for p in range(2):
        sl = slice(p * half, (p + 1) * half)
        t = _silu(_dot(h, wg_ref[:, sl])) * _dot(h, wu_ref[:, sl])
        acc = acc + _dot(t.astype(BF16), wd_ref[sl, :])
    o_ref[...] = x_ref[...] + g2 * acc


def _ffn_dense(x1, h2, mod, w_gate, w_up, w_down, n_ctx, seq_s):
    n = x1.shape[0]
    tile = pl.BlockSpec((FFN_TILE, D_MODEL), lambda i: (i, 0))
    full = lambda shape: pl.BlockSpec(shape, lambda i: (0,) * len(shape))
    return pl.pallas_call(
        functools.partial(_ffn_kernel, n_ctx=n_ctx, seq_s=seq_s),
        grid=(n // FFN_TILE,),
        in_specs=[tile, tile, full((8, 6 * D_MODEL)), full((D_MODEL, D_FF)), full((D_MODEL, D_FF)),
                  full((D_FF, D_MODEL))],
        out_specs=tile,
        out_shape=jax.ShapeDtypeStruct((n, D_MODEL), F32),
        compiler_params=_cparams(("parallel",)),
        name="ffn_dense",
    )(x1, h2, mod, w_gate.astype(BF16), w_up.astype(BF16), w_down.astype(BF16))


def _moe_kernel(x_ref, h_ref, comb_ref, mod_ref, wg_ref, wu_ref, wd_ref, fin_ref, *rest, n_ctx, seq_s, final):
    outs, (acc_ref, rank_c_ref, rank_r_ref, comb_t_ref) = rest[:-4], rest[-4:]
    e = pl.program_id(1)
    t_rows = FFN_TILE

    @pl.when(e == 0)
    def _():
        acc_ref[...] = jnp.zeros_like(acc_ref)
        comb = comb_ref[...]
        before = lax.broadcasted_iota(jnp.int32, (t_rows, t_rows), 1) < lax.broadcasted_iota(jnp.int32, (t_rows, t_rows), 0)
        rank_c_ref[...] = _dot(before.astype(BF16), (comb > 0.0).astype(BF16))
        comb_t = comb.T
        comb_t_ref[...] = comb_t
        rank_r_ref[...] = _dot_nt((comb_t > 0.0).astype(BF16), before.astype(BF16))

    comb = comb_ref[...]
    lane = lax.broadcasted_iota(jnp.int32, comb.shape, 1)
    w_col = jnp.sum(jnp.where(lane == e, comb, 0.0), axis=-1, keepdims=True)
    rank_col = jnp.sum(jnp.where(lane == e, rank_c_ref[...], 0.0), axis=-1, keepdims=True)
    w_row = comb_t_ref[pl.ds(e, 1), :]
    rank_row = rank_r_ref[pl.ds(e, 1), :]
    count = jnp.sum((w_row > 0.0).astype(jnp.int32))

    def body(j, carry):
        base = (j * MOE_ROWS).astype(F32)
        slot_r = lax.broadcasted_iota(jnp.int32, (MOE_ROWS, t_rows), 0).astype(F32) + base
        gather = jnp.where(jnp.logical_and(rank_row == slot_r, w_row > 0.0), 1.0, 0.0).astype(BF16)
        hg = _dot(gather, h_ref[...]).astype(BF16)
        t = _silu(_dot(hg, wg_ref[0])) * _dot(hg, wu_ref[0])
        y_hi, y_lo = _split(_dot(t.astype(BF16), wd_ref[0]))
        slot_c = lax.broadcasted_iota(jnp.int32, (t_rows, MOE_ROWS), 1).astype(F32) + base
        scatter = jnp.where(jnp.logical_and(rank_col == slot_c, w_col > 0.0), 1.0, 0.0).astype(BF16)
        acc_ref[...] += w_col * (_dot(scatter, y_hi) + _dot(scatter, y_lo))
        return carry

    lax.fori_loop(0, (count + MOE_ROWS - 1) // MOE_ROWS, body, 0)

    @pl.when(e == N_EXPERTS - 1)
    def _():
        row = _mod_row(pl.program_id(0), FFN_TILE, n_ctx, seq_s)
        g2 = mod_ref[pl.ds(row, 1), 5 * D_MODEL:6 * D_MODEL]
        y = x_ref[...] + g2 * acc_ref[...]
        if not final:
            outs[0][...] = y
        else:
            y = y * lax.rsqrt(jnp.mean(y * y, axis=-1, keepdims=True) + RMS_EPS) * fin_ref[...]
            is_ctx = pl.program_id(0) < n_ctx // FFN_TILE

            @pl.when(is_ctx)
            def _():
                outs[0][...] = y

            @pl.when(jnp.logical_not(is_ctx))
            def _():
                outs[1][...] = y


def _ffn_moe(x1, h2, comb, mod, w_gate, w_up, w_down, n_ctx, seq_s, final_gain=None):
    n = x1.shape[0]
    final = final_gain is not None
    tile = lambda w: pl.BlockSpec((FFN_TILE, w), lambda i, e: (i, 0))
    if final:
        n_ctx_tiles = n_ctx // FFN_TILE
        out_specs = [pl.BlockSpec((FFN_TILE, D_MODEL), lambda i, e: (jnp.minimum(i, n_ctx_tiles - 1), 0)),
                     pl.BlockSpec((FFN_TILE, D_MODEL), lambda i, e: (jnp.maximum(i - n_ctx_tiles, 0), 0))]
        out_shape = [jax.ShapeDtypeStruct((n_ctx, D_MODEL), F32), jax.ShapeDtypeStruct((n - n_ctx, D_MODEL), F32)]
        gain = final_gain.reshape(1, D_MODEL)
    else:
        out_specs = [tile(D_MODEL)]
        out_shape = [jax.ShapeDtypeStruct((n, D_MODEL), F32)]
        gain = jnp.ones((1, D_MODEL), F32)
    out = pl.pallas_call(
        functools.partial(_moe_kernel, n_ctx=n_ctx, seq_s=seq_s, final=final),
        grid=(n // FFN_TILE, N_EXPERTS),
        in_specs=[tile(D_MODEL), tile(D_MODEL), tile(LANES),
                  pl.BlockSpec((8, 6 * D_MODEL), lambda i, e: (0, 0)),
                  pl.BlockSpec((1, D_MODEL, D_FF_E), lambda i, e: (e, 0, 0)),
                  pl.BlockSpec((1, D_MODEL, D_FF_E), lambda i, e: (e, 0, 0)),
                  pl.BlockSpec((1, D_FF_E, D_MODEL), lambda i, e: (e, 0, 0)),
                  pl.BlockSpec((1, D_MODEL), lambda i, e: (0, 0))],
        out_specs=out_specs,
        out_shape=out_shape,
        scratch_shapes=[pltpu.VMEM((FFN_TILE, D_MODEL), F32), pltpu.VMEM((FFN_TILE, LANES), F32),
                        pltpu.VMEM((LANES, FFN_TILE), F32), pltpu.VMEM((LANES, FFN_TILE), F32)],
        compiler_params=_cparams(("arbitrary", "arbitrary") if final else ("parallel", "arbitrary")),
        name="ffn_moe",
    )(x1, h2, comb, mod, w_gate.astype(BF16), w_up.astype(BF16), w_down.astype(BF16), gain)
    return tuple(out) if final else out[0]


def _final_kernel(x_ref, g_ref, o_ref):
    x = x_ref[...]
    o_ref[...] = x * lax.rsqrt(jnp.mean(x * x, axis=-1, keepdims=True) + RMS_EPS) * g_ref[...]


def _final_norm(x, gain, row0, n_rows):
    blk0 = row0 // FFN_TILE
    return pl.pallas_call(
        _final_kernel,
        grid=(n_rows // FFN_TILE,),
        in_specs=[pl.BlockSpec((FFN_TILE, D_MODEL), lambda i: (blk0 + i, 0)),
                  pl.BlockSpec((1, D_MODEL), lambda i: (0, 0))],
        out_specs=pl.BlockSpec((FFN_TILE, D_MODEL), lambda i: (i, 0)),
        out_shape=jax.ShapeDtypeStruct((n_rows, D_MODEL), F32),
        compiler_params=_cparams(("parallel",)),
        name="final_norm",
    )(x, gain.reshape(1, D_MODEL))


def kernel(x_prompt, x_sample, state_rwkv, c, c_ctx, norm1, norm2, w_ada, b_ada, w_in, dw_w, dw_b, conv_ln_g, conv_ln_b, w_conv_out, w_fno_out, shift_mu, g_up, dec_w0, dec_up, iclr_a0, iclr_up, k_k, k_a, r_k, gn_g, gn_b, w_rwkv_out, w_o, ffn_w_gate, ffn_w_up, ffn_w_down, w_router, b_router, moe_w_gate, moe_w_up, moe_w_down, final_norm):
    b_p, t_p, _ = x_prompt.shape
    b_s, t_s, _ = x_sample.shape
    depth = w_in.shape[0]
    n_ctx = b_p * t_p
    n_lat = b_s * t_s
    assert t_p == ROW_TILE and t_s % FFN_TILE == 0 and n_ctx % FFN_TILE == 0 and b_s <= CTX_ROW
    assert t_s % GRID_W == 0 and CHUNK == GRID_W

    n = n_ctx + n_lat
    x = (x_prompt.reshape(n_ctx, D_MODEL), x_sample.reshape(n_lat, D_MODEL))
    cond = jnp.zeros((8, D_MODEL), F32).at[:b_s].set(c).at[CTX_ROW].set(c_ctx)
    mods = _ada(cond, w_ada, b_ada)
    table = _scan_table(b_p, t_p, b_s, t_s)
    zero_state = jnp.zeros((b_p, 2, HEAD_DIM, C_RWKV), F32)
    pack_state = lambda s: s.transpose(0, 1, 3, 2, 4).reshape(s.shape[0], 2, HEAD_DIM, C_RWKV)
    unpack_state = lambda s: s.reshape(s.shape[0], 2, HEAD_DIM, N_HEADS, HEAD_DIM).transpose(0, 1, 3, 2, 4)

    ctx_states = []
    for l in range(depth):
        mod = mods[l]
        w_in_l = w_in[l].astype(BF16)
        u_conv, u_fno, u_rw = _inproj(x, n, mod, norm1[l], w_in_l[:, :OFF_GATE], n_ctx, t_s)
        conv_h = _conv_branch(u_conv, dw_w[l], dw_b[l], conv_ln_g[l], conv_ln_b[l], n_ctx)
        fno_h = (_fno_branch(u_fno, 0, b_p, t_p), _fno_branch(u_fno, n_ctx, b_s, t_s))
        r, k, v, kkn, g, lw, a = _rwkv_prep(u_rw, shift_mu[l], g_up[l], dec_w0[l], dec_up[l], iclr_a0[l],
                                             iclr_up[l], k_k[l], n_ctx, t_s)
        s0 = jnp.concatenate([zero_state, pack_state(state_rwkv[:, l])], axis=0)
        yf, yb, s_fin = _carry_states(*_chunk_summaries(r, k, v, kkn, lw, a, k_a[l]), s0, table)
        ctx_states.append(unpack_state(s_fin[:b_p]))
        i = l // 2
        if l % 2 == 0:
            w_rt, b_rt = jnp.zeros((D_MODEL, N_EXPERTS), F32), jnp.zeros((N_EXPERTS,), F32)
        else:
            w_rt, b_rt = w_router[i], b_router[i]
        x1, h2, comb = _merge(x, n, conv_h, fno_h, yf, yb, r, k, v, g, norm1[l], w_in_l[:, OFF_GATE:], mod, w_conv_out[l], w_fno_out[l],
                              w_rwkv_out[l], w_o[l], gn_g[l], gn_b[l], r_k[l], norm2[l], w_rt, b_rt, n_ctx, t_s)
        if l % 2 == 0:
            x = _ffn_dense(x1, h2, mod, ffn_w_gate[i], ffn_w_up[i], ffn_w_down[i], n_ctx, t_s)
        else:
            x = _ffn_moe(x1, h2, comb, mod, moe_w_gate[i], moe_w_up[i], moe_w_down[i], n_ctx, t_s,
                         final_gain=final_norm if l == depth - 1 else None)

    if isinstance(x, tuple):
        y_prompt, y_sample = x
    else:
        y_prompt, y_sample = _final_norm(x, final_norm, 0, n_ctx), _final_norm(x, final_norm, n_ctx, n_lat)
    y_prompt = y_prompt.reshape(b_p, t_p, D_MODEL)
    y_sample = y_sample.reshape(b_s, t_s, D_MODEL)
    new_state = jnp.stack(ctx_states, axis=1).astype(x_prompt.dtype)
    return (y_prompt, y_sample, new_state)
```

```python
import functools

import numpy as np
import jax
import jax.numpy as jnp
from jax import lax
from jax.experimental import pallas as pl
from jax.experimental.pallas import tpu as pltpu

F32 = jnp.float32
BF16 = jnp.bfloat16

D_MODEL = 1024
GRID_W = 64
C_CONV = 256
CONV_W = 31
C_FNO = 256
FNO_GW = 64
N_HEADS = 8
HEAD_DIM = 64
C_RWKV = N_HEADS * HEAD_DIM
G_RANK = 128
W_RANK = 64
A_RANK = 64
D_FF = 2816
N_EXPERTS = 8
D_FF_E = 1408
RMS_EPS = 1e-6
LN_EPS = 1e-5
GN_EPS = 64e-5

OFF_FNO = 2 * C_CONV
OFF_RWKV = OFF_FNO + C_FNO
N_SHIFT = 3 * C_RWKV + G_RANK + 2 * (W_RANK + A_RANK)
OFF_GATE = OFF_RWKV + N_SHIFT
D_IN = OFF_GATE + 3 * D_MODEL

ROW_TILE = 256
FFN_TILE = 512
CHUNK = 64
GROUP = 4
GROUP_W = GROUP * HEAD_DIM
SUMMARY_CHUNKS = 4
CARRY_CHUNKS = 4
MOE_ROWS = 160
CTX_ROW = 4
LANES = 128
VMEM_LIMIT = 56 * 1024 * 1024


def _cparams(sem):
    return pltpu.CompilerParams(dimension_semantics=sem, vmem_limit_bytes=VMEM_LIMIT)


def _sigmoid(x):
    return 1.0 / (1.0 + jnp.exp(-x))


def _silu(x):
    return x * _sigmoid(x)


def _dot(a, b):
    return jnp.dot(a, b, preferred_element_type=F32)


def _split(x):
    hi = x.astype(BF16)
    lo = (x - hi.astype(F32)).astype(BF16)
    return hi, lo


def _dot_x3(a, b):
    ah, al = _split(a)
    bh, bl = _split(b)
    return _dot(ah, bh) + (_dot(ah, bl) + _dot(al, bh))


def _seg_sum(x, bd):
    return _dot(x.astype(BF16), bd)


def _mod_row(i, tile, n_ctx_rows, seq_s):
    n_ctx_tiles = n_ctx_rows // tile
    return jnp.where(i < n_ctx_tiles, CTX_ROW, (i - n_ctx_tiles) // (seq_s // tile))


def _stream_specs(tile, width, n_ctx, lat_block0):
    n_ctx_tiles = n_ctx // tile
    return [pl.BlockSpec((tile, width), lambda i: (jnp.minimum(i, n_ctx_tiles - 1), 0)),
            pl.BlockSpec((tile, width), lambda i: (jnp.maximum(i - n_ctx_tiles, 0) + lat_block0, 0))]


def _stream_args(x, tile, n_ctx):
    if isinstance(x, tuple):
        return x[0], x[1], 0
    return x, x, n_ctx // tile


def _pick_stream(i, tile, n_ctx, ctx_ref, lat_ref):
    return jnp.where(i < n_ctx // tile, ctx_ref[...], lat_ref[...])


def _ada_kernel(c_ref, w_ref, b_ref, o_ref):
    s = _silu(c_ref[...])
    o_ref[0] = _dot(s.astype(BF16), w_ref[0].astype(BF16)) + b_ref[0]


def _ada(cond, w_ada, b_ada):
    n_l = w_ada.shape[0]
    tn = 1536
    return pl.pallas_call(
        _ada_kernel,
        grid=(n_l, 6 * D_MODEL // tn),
        in_specs=[
            pl.BlockSpec((8, D_MODEL), lambda l, j: (0, 0)),
            pl.BlockSpec((1, D_MODEL, tn), lambda l, j: (l, 0, j)),
            pl.BlockSpec((1, 1, tn), lambda l, j: (l, 0, j)),
        ],
        out_specs=pl.BlockSpec((1, 8, tn), lambda l, j: (l, 0, j)),
        out_shape=jax.ShapeDtypeStruct((n_l, 8, 6 * D_MODEL), F32),
        compiler_params=_cparams(("parallel", "parallel")),
        name="ada",
    )(cond, w_ada, b_ada.reshape(n_l, 1, 6 * D_MODEL))


def _inproj_kernel(xc_ref, xl_ref, mod_ref, g_ref, w_ref, oc_ref, of_ref, or_ref, *, n_ctx, seq_s):
    row = _mod_row(pl.program_id(0), ROW_TILE, n_ctx, seq_s)
    m = mod_ref[pl.ds(row, 1), :]
    sh = m[:, 0:D_MODEL]
    sc = m[:, D_MODEL:2 * D_MODEL]
    x = _pick_stream(pl.program_id(0), ROW_TILE, n_ctx, xc_ref, xl_ref)
    y = x * lax.rsqrt(jnp.mean(x * x, axis=-1, keepdims=True) + RMS_EPS) * g_ref[...]
    h = (y * (1.0 + sc) + sh).astype(BF16)
    oc_ref[...] = _dot(h, w_ref[:, 0:OFF_FNO])
    of_ref[...] = _dot(h, w_ref[:, OFF_FNO:OFF_RWKV])
    or_ref[...] = _dot(h, w_ref[:, OFF_RWKV:OFF_GATE])


def _inproj(x, n, mod, gain, w_in, n_ctx, seq_s):
    xc, xl, lat0 = _stream_args(x, ROW_TILE, n_ctx)
    widths = (OFF_FNO, C_FNO, N_SHIFT)
    return pl.pallas_call(
        functools.partial(_inproj_kernel, n_ctx=n_ctx, seq_s=seq_s),
        grid=(n // ROW_TILE,),
        in_specs=_stream_specs(ROW_TILE, D_MODEL, n_ctx, lat0) + [
            pl.BlockSpec((8, 6 * D_MODEL), lambda i: (0, 0)),
            pl.BlockSpec((1, D_MODEL), lambda i: (0, 0)),
            pl.BlockSpec((D_MODEL, OFF_GATE), lambda i: (0, 0)),
        ],
        out_specs=[pl.BlockSpec((ROW_TILE, w), lambda i: (i, 0)) for w in widths],
        out_shape=[jax.ShapeDtypeStruct((n, w), F32) for w in widths],
        compiler_params=_cparams(("parallel",)),
        name="inproj",
    )(xc, xl, mod, gain.reshape(1, D_MODEL), w_in)


_CONV_HALO = 16
_CONV_ROWS = 64


def _conv_kernel(u_ref, w_ref, b_ref, g_ref, be_ref, o_ref, pad_ref, sh_ref, *, n_ctx):
    is_ctx = pl.program_id(0) < n_ctx // ROW_TILE
    zeros = jnp.zeros((_CONV_HALO, C_CONV), F32)
    n_parts = ROW_TILE // _CONV_ROWS

    def glu(lo, hi):
        return u_ref[lo:hi, 0:C_CONV] * _sigmoid(u_ref[lo:hi, C_CONV:2 * C_CONV])

    def finish(starts, n_rows):
        for s in range(8):
            sh_ref[s, 0:n_rows - 8, :] = pad_ref[s:s + n_rows - 8, :]
        for p in range(n_parts):
            acc = jnp.zeros((_CONV_ROWS, C_CONV), F32)
            for j in range(CONV_W):
                o = starts[p] + j
                acc = acc + w_ref[j:j + 1, :] * sh_ref[o % 8, o - o % 8:o - o % 8 + _CONV_ROWS, :]
            acc = acc + b_ref[...]
            mu = jnp.mean(acc, axis=-1, keepdims=True)
            d = acc - mu
            var = jnp.mean(d * d, axis=-1, keepdims=True)
            y = d * lax.rsqrt(var + LN_EPS) * g_ref[...] + be_ref[...]
            o_ref[p * _CONV_ROWS:(p + 1) * _CONV_ROWS, :] = _silu(y).astype(o_ref.dtype)

    shift = _CONV_HALO - CONV_W // 2

    @pl.when(is_ctx)
    def _():
        pad_ref[0:_CONV_HALO, :] = zeros
        pad_ref[_CONV_HALO:_CONV_HALO + ROW_TILE, :] = glu(0, ROW_TILE)
        pad_ref[_CONV_HALO + ROW_TILE:2 * _CONV_HALO + ROW_TILE, :] = zeros
        finish([shift + p * _CONV_ROWS for p in range(n_parts)], ROW_TILE + 2 * _CONV_HALO)

    @pl.when(jnp.logical_not(is_ctx))
    def _():
        stride = GRID_W + 2 * _CONV_HALO
        for p in range(n_parts):
            pad_ref[p * stride:p * stride + _CONV_HALO, :] = zeros
            pad_ref[p * stride + _CONV_HALO:p * stride + _CONV_HALO + GRID_W, :] = glu(p * GRID_W, (p + 1) * GRID_W)
            pad_ref[p * stride + _CONV_HALO + GRID_W:(p + 1) * stride, :] = zeros
        finish([p * stride + shift for p in range(n_parts)], n_parts * stride)


def _conv_branch(u_conv, dw_w, dw_b, ln_g, ln_b, n_ctx):
    n = u_conv.shape[0]
    assert _CONV_ROWS == GRID_W and ROW_TILE % GRID_W == 0
    vec = pl.BlockSpec((1, C_CONV), lambda i: (0, 0))
    return pl.pallas_call(
        functools.partial(_conv_kernel, n_ctx=n_ctx),
        grid=(n // ROW_TILE,),
        in_specs=[
            pl.BlockSpec((ROW_TILE, 2 * C_CONV), lambda i: (i, 0)),
            pl.BlockSpec((CONV_W, C_CONV), lambda i: (0, 0)),
            vec, vec, vec,
        ],
        out_specs=pl.BlockSpec((ROW_TILE, C_CONV), lambda i: (i, 0)),
        out_shape=jax.ShapeDtypeStruct((n, C_CONV), BF16),
        scratch_shapes=[pltpu.VMEM((ROW_TILE // GRID_W * (GRID_W + 2 * _CONV_HALO), C_CONV), F32),
                        pltpu.VMEM((8, ROW_TILE // GRID_W * (GRID_W + 2 * _CONV_HALO), C_CONV), F32)],
        compiler_params=_cparams(("parallel",)),
        name="conv_branch",
    )(u_conv, dw_w, dw_b.reshape(1, C_CONV), ln_g.reshape(1, C_CONV), ln_b.reshape(1, C_CONV))


def _dft_tables(t_len):
    def cs(n):
        k = np.arange(n, dtype=np.int64)
        ang = 2.0 * np.pi * ((k[:, None] * k[None, :]) % n).astype(np.float64) / n
        return np.cos(ang), np.sin(ang)
    cg, sg = cs(FNO_GW)
    eye = np.eye(C_FNO // FNO_GW)
    w1 = np.concatenate([np.kron(eye, cg), np.kron(eye, sg)], axis=1)
    ct, st = cs(t_len)
    w2 = np.concatenate([ct, -st], axis=1)
    return jnp.asarray(w1, dtype=F32).astype(BF16), jnp.asarray(w2, dtype=F32).astype(BF16)


def _fno_kernel(u_ref, w1_ref, w2_ref, o_ref, hs_ref, *, t_len, scale):
    @pl.when(pl.program_id(1) == 0)
    def _():
        hc = _dot(u_ref[...].astype(BF16), w1_ref[...])
        hs_ref[0:t_len, :] = hc[:, 0:C_FNO].astype(BF16)
        hs_ref[t_len:2 * t_len, :] = hc[:, C_FNO:2 * C_FNO].astype(BF16)

    o_ref[...] = (_dot(w2_ref[...], hs_ref[...]) * scale).astype(o_ref.dtype)


def _fno_branch(u_fno, row0, n_seq, t_len):
    w1, w2 = _dft_tables(t_len)
    tk = min(t_len, 512)
    blk0 = row0 // t_len
    return pl.pallas_call(
        functools.partial(_fno_kernel, t_len=t_len, scale=float(1.0 / np.sqrt(t_len * FNO_GW))),
        grid=(n_seq, t_len // tk),
        in_specs=[
            pl.BlockSpec((t_len, C_FNO), lambda b, j: (blk0 + b, 0)),
            pl.BlockSpec((C_FNO, 2 * C_FNO), lambda b, j: (0, 0)),
            pl.BlockSpec((tk, 2 * t_len), lambda b, j: (j, 0)),
        ],
        out_specs=pl.BlockSpec((tk, C_FNO), lambda b, j: (b * (t_len // tk) + j, 0)),
        out_shape=jax.ShapeDtypeStruct((n_seq * t_len, C_FNO), BF16),
        scratch_shapes=[pltpu.VMEM((2 * t_len, C_FNO), BF16)],
        compiler_params=_cparams(("parallel", "arbitrary")),
        name="fno_branch",
    )(u_fno, w1, w2)


def _prep_pieces(i, z_ref, zp_ref, zn_ref, mu_ref, gup_ref, w0_ref, dup_ref, a0_ref, aup_ref, kk_ref, bd_ref,
                 r_refs, k_refs, v_refs, kkn_ref, g_ref, lw_ref, a_ref, pad_ref, n_ctx, seq_s):
    n_ctx_tiles = n_ctx // ROW_TILE
    per_seq = seq_s // ROW_TILE
    j = (i - n_ctx_tiles) % per_seq
    first = jnp.logical_or(i < n_ctx_tiles, j == 0)
    last = jnp.logical_or(i < n_ctx_tiles, j == per_seq - 1)
    c = C_RWKV

    def shifted(lo, hi):
        z = pad_ref[8:8 + ROW_TILE, lo:hi]
        zp = pad_ref[7:7 + ROW_TILE, lo:hi]
        zn = pad_ref[9:9 + ROW_TILE, lo:hi]
        return z + mu_ref[:, lo:hi] * (0.5 * (zp + zn) - z)

    def fill():
        pad_ref[8:8 + ROW_TILE, :] = z_ref[...]
        pad_ref[0:8, :] = jnp.where(first, 0.0, zp_ref[...])
        pad_ref[8 + ROW_TILE:16 + ROW_TILE, :] = jnp.where(last, 0.0, zn_ref[...])

    def receptance():
        r = shifted(0, c)
        for ref in r_refs:
            ref[...] = r

    def key():
        k = shifted(c, 2 * c)
        for ref in k_refs:
            ref[...] = k
        kx = k * kk_ref[...]
        nrm = jnp.sqrt(_seg_sum(kx * kx, bd_ref[...]))
        kkn_ref[...] = kx / jnp.maximum(nrm, 1e-12)

    def value():
        v = shifted(2 * c, 3 * c)
        for ref in v_refs:
            ref[...] = v

    def gate():
        o = 3 * c
        g_ref[...] = _dot(_sigmoid(shifted(o, o + G_RANK)).astype(BF16), gup_ref[...])

    def decay():
        o = 3 * c + G_RANK
        xw = _dot(jnp.tanh(shifted(o, o + 2 * W_RANK)).astype(BF16), dup_ref[...]) + w0_ref[...]
        soft = jnp.maximum(-xw, 0.0) + jnp.log(1.0 + jnp.exp(-jnp.abs(xw)))
        lw_ref[...] = -jnp.exp(-soft - 0.5)

    def iclr():
        o = 3 * c + G_RANK + 2 * W_RANK
        xa = _dot(shifted(o, o + 2 * A_RANK).astype(BF16), aup_ref[...]) + a0_ref[...]
        a_ref[...] = _sigmoid(xa)

    return [fill, receptance, key, value, gate, decay, iclr]


def _block_diag2(w):
    z = jnp.zeros_like(w[0])
    return jnp.concatenate([jnp.concatenate([w[0], z], axis=1), jnp.concatenate([z, w[1]], axis=1)], axis=0)


def _head_ones():
    return jnp.asarray(np.kron(np.eye(N_HEADS), np.ones((HEAD_DIM, HEAD_DIM))), dtype=BF16)


def _block_diag(x, mask):
    xb = x.astype(BF16)
    return jnp.where(mask, jnp.concatenate([xb] * GROUP, axis=0), jnp.zeros((), BF16))


def _dot_nt(a, b):
    return lax.dot_general(a, b, (((1,), (1,)), ((), ())), preferred_element_type=F32)


def _diag_blocks(prod, lane_head):
    out = jnp.where(lane_head == 0, prod[0:HEAD_DIM], 0.0)
    for h in range(1, GROUP):
        out = out + jnp.where(lane_head == h, prod[h * HEAD_DIM:(h + 1) * HEAD_DIM], 0.0)
    return out


def _summary_body(r_ref, k_ref, v_ref, n_ref, lw_ref, a_ref, ka_ref, q_ref, y0_ref, p_ref, z_ref, g_ref, fillers=()):
    fillers = list(fillers)
    fill_one = lambda: fillers.pop(0)() if fillers else None
    row = lax.broadcasted_iota(jnp.int32, (CHUNK, GROUP_W), 0)
    lane = lax.broadcasted_iota(jnp.int32, (CHUNK, GROUP_W), 1)
    col = lane % CHUNK
    lane_head = lane // HEAD_DIM
    bd_mask = (lax.broadcasted_iota(jnp.int32, (GROUP * CHUNK, GROUP_W), 0) // CHUNK
               == lax.broadcasted_iota(jnp.int32, (GROUP * CHUNK, GROUP_W), 1) // HEAD_DIM)
    bd = lambda x: _block_diag(x, bd_mask)
    ka = ka_ref[...]

    units = []
    for j in range(SUMMARY_CHUNKS):
        rows = slice(j * CHUNK, (j + 1) * CHUNK)
        r = r_ref[rows, :]
        k = k_ref[rows, :]
        v = v_ref[rows, :]
        kkn = n_ref[rows, :]
        for d in range(2):
            lanes = slice(d * C_RWKV, (d + 1) * C_RWKV)
            earlier = (col < row) if d == 0 else (col > row)
            upto = jnp.logical_or(earlier, row == col)
            tri = upto[:, 0:CHUNK].astype(BF16)
            lw = lw_ref[rows, lanes]
            h1 = lw.astype(BF16)
            r1 = lw - h1.astype(F32)
            h2 = r1.astype(BF16)
            h3 = (r1 - h2.astype(F32)).astype(BF16)
            cum = _dot(tri, h1) + (_dot(tri, h2) + _dot(tri, h3))
            e_in = jnp.exp(cum)
            e_ex = jnp.exp(cum - lw)
            e_ng = jnp.exp(-cum)
            a = a_ref[rows, lanes]
            kd = k * (1.0 + (a - 1.0) * ka)
            at = -kkn * e_ex
            rt = r * e_in
            bt = kkn * a * e_ng
            kt = kd * e_ng
            end = CHUNK - 1 if d == 0 else 0
            g_end = e_in[end:end + 1, :]
            g_ref[j, :, lanes] = g_end
            bh = bt * g_end
            kh = kt * g_end
            for q in range(C_RWKV // GROUP_W):
                sl = slice(q * GROUP_W, (q + 1) * GROUP_W)
                units.append(dict(rows=rows, out=slice(d * C_RWKV + q * GROUP_W, d * C_RWKV + (q + 1) * GROUP_W),
                                  earlier=earlier, upto=upto, at=at[:, sl], rt=rt[:, sl],
                                  ar=jnp.concatenate([at[:, sl], rt[:, sl]], axis=0).astype(BF16),
                                  bt=bt[:, sl], kt=kt[:, sl], v=v[:, sl], bh=bh[:, sl].astype(BF16),
                                  bk=jnp.concatenate([bh[:, sl], kh[:, sl]], axis=0).astype(BF16)))

    for u in units:
        sb = _dot_nt(u["ar"], bd(u["bt"]))
        sk = _dot_nt(u["ar"], bd(u["kt"]))
        u["lab"] = jnp.where(u["earlier"], sb[0:CHUNK], 0.0)
        u["mrb"] = jnp.where(u["upto"], sb[CHUNK:2 * CHUNK], 0.0).astype(BF16)
        u["lm"] = jnp.concatenate([jnp.where(u["earlier"], sk[0:CHUNK], 0.0),
                                   jnp.where(u["upto"], sk[CHUNK:2 * CHUNK], 0.0)], axis=0).astype(BF16)

    fill_one()
    eye = (row == col).astype(F32)
    pair = jnp.logical_and(row // 2 == col // 2, row != col)
    for u in units:
        u["t"] = eye + jnp.where(pair, u["lab"], 0.0)
    n = 2
    while n < CHUNK:
        m = jnp.logical_and(row // (2 * n) == col // (2 * n), row // n != col // n)
        for u in units:
            u["w"] = _dot(jnp.where(m, u["lab"], 0.0).astype(BF16), bd(u["t"]))
        for u in units:
            u["t"] = u["t"] + _dot(u["t"].astype(BF16), bd(u["w"]))
        fill_one()
        n *= 2

    for u in units:
        u["t"] = u["t"].astype(BF16)
        u["wm"] = _dot(u["t"], bd(u["at"]))
        u["lv"] = _dot(u["lm"], bd(u["v"]))
    for u in units:
        u["u0"] = _dot(u["t"], bd(u["lv"][0:CHUNK]))
    for u in units:
        q_ref[u["rows"], u["out"]] = (u["rt"] + _dot(u["mrb"], bd(u["wm"]))).astype(q_ref.dtype)
        y0_ref[u["rows"], u["out"]] = u["lv"][CHUNK:2 * CHUNK] + _dot(u["mrb"], bd(u["u0"]))
    for u in units:
        p = _dot(u["wm"].T.astype(BF16), u["bh"])
        p_ref[u["rows"], u["out"]] = _diag_blocks(p, lane_head).astype(p_ref.dtype)
        z = _dot(jnp.concatenate([u["u0"], u["v"]], axis=0).T.astype(BF16), u["bk"])
        z_ref[u["rows"], u["out"]] = _diag_blocks(z, lane_head)
    while fillers:
        fill_one()


def _rwkv_kernel(z_ref, zp_ref, zn_ref, mu_ref, gup_ref, w0_ref, dup_ref, a0_ref, aup_ref, kk_ref, bd_ref, ka_ref,
                 r_ref, k_ref, v_ref, g_ref, q_ref, y0_ref, p_ref, zz_ref, gd_ref,
                 pad_ref, r_n, k_n, v_n, n_n, lw_n, a_n, r_c, k_c, v_c, n_c, lw_c, a_c, *, n_ctx, seq_s, n_tiles):
    s = pl.program_id(0)
    new = (r_n, k_n, v_n, n_n, lw_n, a_n)
    cur = (r_c, k_c, v_c, n_c, lw_c, a_c)

    @pl.when(s == 0)
    def _():
        for ref in cur:
            ref[...] = jnp.zeros_like(ref)

    pieces = _prep_pieces(jnp.minimum(s, n_tiles - 1), z_ref, zp_ref, zn_ref, mu_ref, gup_ref, w0_ref, dup_ref,
                          a0_ref, aup_ref, kk_ref, bd_ref, (r_n, r_ref), (k_n, k_ref), (v_n, v_ref), n_n, g_ref,
                          lw_n, a_n, pad_ref, n_ctx, seq_s)
    pieces[0]()
    _summary_body(r_c, k_c, v_c, n_c, lw_c, a_c, ka_ref, q_ref, y0_ref, p_ref, zz_ref, gd_ref, fillers=pieces[1:])
    for dst, src in zip(cur, new):
        dst[...] = src[...]


def _rwkv_project_and_summarise(u_rw, mu, g_up, dec_w0, dec_up, iclr_a0, iclr_up, k_k, k_a, n_ctx, seq_s):
    n = u_rw.shape[0]
    assert ROW_TILE == SUMMARY_CHUNKS * CHUNK and CHUNK == HEAD_DIM and C_RWKV % GROUP_W == 0
    n_tiles = n // ROW_TILE
    halo = ROW_TILE // 8
    last_blk = n // 8 - 1
    c = C_RWKV
    full = lambda shape: pl.BlockSpec(shape, lambda s: (0,) * len(shape))
    this = lambda s: jnp.minimum(s, n_tiles - 1)
    prev = lambda s: jnp.maximum(s - 1, 0)
    proj = lambda w: pl.BlockSpec((ROW_TILE, w), lambda s: (this(s), 0))
    summ = lambda w: pl.BlockSpec((ROW_TILE, w), lambda s: (prev(s), 0))
    vm = lambda w: pltpu.VMEM((ROW_TILE, w), F32)
    return pl.pallas_call(
        functools.partial(_rwkv_kernel, n_ctx=n_ctx, seq_s=seq_s, n_tiles=n_tiles),
        grid=(n_tiles + 1,),
        in_specs=[
            proj(N_SHIFT),
            pl.BlockSpec((8, N_SHIFT), lambda s: (jnp.maximum(this(s) * halo - 1, 0), 0)),
            pl.BlockSpec((8, N_SHIFT), lambda s: (jnp.minimum((this(s) + 1) * halo, last_blk), 0)),
            full((1, N_SHIFT)), full((G_RANK, c)), full((1, 2 * c)), full((2 * W_RANK, 2 * c)),
            full((1, 2 * c)), full((2 * A_RANK, 2 * c)), full((1, c)), full((c, c)), full((1, c)),
        ],
        out_specs=[proj(c), proj(c), proj(c), proj(c), summ(2 * c), summ(2 * c), summ(2 * c), summ(2 * c),
                   pl.BlockSpec((SUMMARY_CHUNKS, 1, 2 * c), lambda s: (prev(s), 0, 0))],
        out_shape=[jax.ShapeDtypeStruct((n, c), F32)] * 4
        + [jax.ShapeDtypeStruct((n, 2 * c), BF16), jax.ShapeDtypeStruct((n, 2 * c), F32),
           jax.ShapeDtypeStruct((n, 2 * c), BF16), jax.ShapeDtypeStruct((n, 2 * c), F32),
           jax.ShapeDtypeStruct((n // CHUNK, 1, 2 * c), F32)],
        scratch_shapes=[pltpu.VMEM((ROW_TILE + 16, N_SHIFT), F32)]
        + [vm(c), vm(c), vm(c), vm(c), vm(2 * c), vm(2 * c)] * 2,
        compiler_params=_cparams(("arbitrary",)),
        name="rwkv_project_summarise",
    )(u_rw, u_rw, u_rw, mu.reshape(1, N_SHIFT), g_up.astype(BF16), dec_w0.reshape(1, 2 * c),
      _block_diag2(dec_up).astype(BF16), iclr_a0.reshape(1, 2 * c), _block_diag2(iclr_up).astype(BF16),
      k_k.reshape(1, c), _head_ones(), k_a.reshape(1, c))


def _carry_kernel(tab_ref, qf_ref, y0f_ref, pf_ref, zf_ref, gf_ref, qb_ref, y0b_ref, pb_ref, zb_ref, gb_ref,
                  s0_ref, yf_ref, yb_ref, sfin_ref, st_ref):
    step = pl.program_id(0)

    @pl.when(tab_ref[3, step] == 1)
    def _():
        st_ref[...] = s0_ref[0]

    bd_mask = (lax.broadcasted_iota(jnp.int32, (GROUP * CHUNK, GROUP_W), 0) // CHUNK
               == lax.broadcasted_iota(jnp.int32, (GROUP * CHUNK, GROUP_W), 1) // HEAD_DIM)
    for i in range(CARRY_CHUNKS):
        for d, (q_ref, y0_ref, p_ref, z_ref, g_ref, y_ref) in enumerate(
                ((qf_ref, y0f_ref, pf_ref, zf_ref, gf_ref, yf_ref), (qb_ref, y0b_ref, pb_ref, zb_ref, gb_ref, yb_ref))):
            j = i if d == 0 else CARRY_CHUNKS - 1 - i
            rows = slice(j * CHUNK, (j + 1) * CHUNK)
            for q in range(C_RWKV // GROUP_W):
                sl = slice(q * GROUP_W, (q + 1) * GROUP_W)
                s = st_ref[d, :, sl]
                y_ref[rows, sl] = y0_ref[rows, sl] + _dot_nt(q_ref[rows, sl], _block_diag(s, bd_mask))
                st_ref[d, :, sl] = (s * g_ref[j, :, sl] + z_ref[rows, sl]
                                    + _dot(s.astype(BF16), _block_diag(p_ref[rows, sl], bd_mask)))

    @pl.when(tab_ref[4, step] == 1)
    def _():
        sfin_ref[0] = st_ref[...]


def _scan_table(n_ctx_seq, t_ctx, n_s_seq, t_s):
    rows = []
    base = 0
    seq = 0
    for n_seq, t_len in ((n_ctx_seq, t_ctx), (n_s_seq, t_s)):
        assert t_len % (CARRY_CHUNKS * CHUNK) == 0
        n_c = t_len // (CARRY_CHUNKS * CHUNK)
        for _ in range(n_seq):
            for c in range(n_c):
                rows.append((base + c, base + n_c - 1 - c, seq, int(c == 0), int(c == n_c - 1)))
            base += n_c
            seq += 1
    return np.asarray(rows, dtype=np.int32).T.copy()


def _carry_states(q, y0, p, z, g, s0, table):
    n = q.shape[0]
    n_seq = s0.shape[0]
    c = C_RWKV

    def tok(which, d):
        return pl.BlockSpec((CARRY_CHUNKS * CHUNK, c), lambda s, tab: (tab[which, s], d))

    def decay(which, d):
        return pl.BlockSpec((CARRY_CHUNKS, 1, c), lambda s, tab: (tab[which, s], 0, d))

    state_spec = pl.BlockSpec((1, 2, HEAD_DIM, c), lambda s, tab: (tab[2, s], 0, 0, 0))
    grid_spec = pltpu.PrefetchScalarGridSpec(
        num_scalar_prefetch=1,
        grid=(table.shape[1],),
        in_specs=[tok(0, 0), tok(0, 0), tok(0, 0), tok(0, 0), decay(0, 0),
                  tok(1, 1), tok(1, 1), tok(1, 1), tok(1, 1), decay(1, 1), state_spec],
        out_specs=[tok(0, 0), tok(1, 0), state_spec],
        scratch_shapes=[pltpu.VMEM((2, HEAD_DIM, c), F32)],
    )
    return pl.pallas_call(
        _carry_kernel,
        grid_spec=grid_spec,
        out_shape=[jax.ShapeDtypeStruct((n, c), F32), jax.ShapeDtypeStruct((n, c), F32),
                   jax.ShapeDtypeStruct((n_seq, 2, HEAD_DIM, c), F32)],
        compiler_params=_cparams(("arbitrary",)),
        name="rwkv_carry",
    )(jnp.asarray(table), q, y0, p, z, g, q, y0, p, z, g, s0)


def _merge_kernel(xc_ref, xl_ref, c_ref, fc_ref, fl_ref, yf_ref, yb_ref, r_ref, k_ref, v_ref, g_ref, n1_ref, wgt_ref, mod_ref,
                  wc_ref, wf_ref, wr_ref, wo_ref, gng_ref, gnb_ref, rk_ref, bd_ref, n2_ref, wrt_ref, brt_ref,
                  x1_ref, h2_ref, comb_ref, *, n_ctx, seq_s):
    row = _mod_row(pl.program_id(0), ROW_TILE, n_ctx, seq_s)
    m = mod_ref[pl.ds(row, 1), :]
    sh1 = m[:, 0:D_MODEL]
    sc1 = m[:, D_MODEL:2 * D_MODEL]
    g1 = m[:, 2 * D_MODEL:3 * D_MODEL]
    sh2 = m[:, 3 * D_MODEL:4 * D_MODEL]
    sc2 = m[:, 4 * D_MODEL:5 * D_MODEL]
    bd = bd_ref[...]
    inv = 1.0 / HEAD_DIM
    y = yf_ref[...] + yb_ref[...]
    y_hi, y_lo = _split(y)
    d = y - (_dot(y_hi, bd) + _dot(y_lo, bd)) * inv
    var = _seg_sum(d * d, bd) * inv
    yn = d * lax.rsqrt(var + GN_EPS) * gng_ref[...] + gnb_ref[...]
    v = v_ref[...]
    yn = yn + _seg_sum(r_ref[...] * k_ref[...] * rk_ref[...], bd) * v
    yr = _dot((yn * g_ref[...]).astype(BF16), wr_ref[...])
    yc = _dot(c_ref[...], wc_ref[...])
    yf = _dot(_pick_stream(pl.program_id(0), ROW_TILE, n_ctx, fc_ref, fl_ref), wf_ref[...])
    x = _pick_stream(pl.program_id(0), ROW_TILE, n_ctx, xc_ref, xl_ref)
    hn = x * lax.rsqrt(jnp.mean(x * x, axis=-1, keepdims=True) + RMS_EPS) * n1_ref[...]
    hn = (hn * (1.0 + sc1) + sh1).astype(BF16)
    dm = D_MODEL
    merged = (_sigmoid(_dot(hn, wgt_ref[:, 0:dm])) * yc + _sigmoid(_dot(hn, wgt_ref[:, dm:2 * dm])) * yf
              + _sigmoid(_dot(hn, wgt_ref[:, 2 * dm:3 * dm])) * yr)
    x1 = x + g1 * _dot(merged.astype(BF16), wo_ref[...])
    x1_ref[...] = x1
    h2 = x1 * lax.rsqrt(jnp.mean(x1 * x1, axis=-1, keepdims=True) + RMS_EPS) * n2_ref[...]
    h2 = h2 * (1.0 + sc2) + sh2
    h2_ref[...] = h2.astype(BF16)
    logits = _dot_x3(h2, wrt_ref[...]) + brt_ref[...]
    lane = lax.broadcasted_iota(jnp.int32, logits.shape, 1).astype(F32)
    neg = jnp.float32(-jnp.inf)
    lg = jnp.where(lane < N_EXPERTS, logits, neg)
    m1 = jnp.max(lg, axis=-1, keepdims=True)
    i1 = jnp.min(jnp.where(lg == m1, lane, float(LANES)), axis=-1, keepdims=True)
    lg2 = jnp.where(lane == i1, neg, lg)
    m2 = jnp.max(lg2, axis=-1, keepdims=True)
    i2 = jnp.min(jnp.where(lg2 == m2, lane, float(LANES)), axis=-1, keepdims=True)
    e = jnp.exp(m2 - m1)
    p1 = 1.0 / (1.0 + e)
    comb_ref[...] = jnp.where(lane == i1, p1, 0.0) + jnp.where(lane == i2, e * p1, 0.0)


def _merge(x, n, conv_h, fno_h, yf, yb, r, k, v, g, gain1, w_gates, mod, w_conv_out, w_fno_out, w_rwkv_out, w_o,
           gn_g, gn_b, r_k, gain2, w_router, b_router, n_ctx, seq_s):
    xc, xl, x_lat0 = _stream_args(x, ROW_TILE, n_ctx)
    fc, fl, f_lat0 = _stream_args(fno_h, ROW_TILE, n_ctx)
    c = C_RWKV
    tile = lambda w: pl.BlockSpec((ROW_TILE, w), lambda i: (i, 0))
    full = lambda shape: pl.BlockSpec(shape, lambda i: (0,) * len(shape))
    wrt = jnp.zeros((D_MODEL, LANES), F32).at[:, :N_EXPERTS].set(w_router)
    brt = jnp.zeros((1, LANES), F32).at[0, :N_EXPERTS].set(b_router)
    return pl.pallas_call(
        functools.partial(_merge_kernel, n_ctx=n_ctx, seq_s=seq_s),
        grid=(n // ROW_TILE,),
        in_specs=_stream_specs(ROW_TILE, D_MODEL, n_ctx, x_lat0) + [tile(C_CONV)]
        + _stream_specs(ROW_TILE, C_FNO, n_ctx, f_lat0) + [
            tile(c), tile(c), tile(c), tile(c), tile(c), tile(c),
            full((1, D_MODEL)), full((D_MODEL, 3 * D_MODEL)), full((8, 6 * D_MODEL)),
            full((C_CONV, D_MODEL)), full((C_FNO, D_MODEL)), full((c, D_MODEL)), full((D_MODEL, D_MODEL)),
            full((1, c)), full((1, c)), full((1, c)), full((c, c)), full((1, D_MODEL)),
            full((D_MODEL, LANES)), full((1, LANES)),
        ],
        out_specs=[tile(D_MODEL), tile(D_MODEL), tile(LANES)],
        out_shape=[jax.ShapeDtypeStruct((n, D_MODEL), F32), jax.ShapeDtypeStruct((n, D_MODEL), BF16),
                   jax.ShapeDtypeStruct((n, LANES), F32)],
        compiler_params=_cparams(("parallel",)),
        name="merge",
    )(xc, xl, conv_h, fc, fl, yf, yb, r, k, v, g, gain1.reshape(1, D_MODEL), w_gates, mod,
      w_conv_out.astype(BF16), w_fno_out.astype(BF16), w_rwkv_out.astype(BF16), w_o.astype(BF16),
      gn_g.reshape(1, c), gn_b.reshape(1, c), r_k.reshape(1, c), _head_ones(), gain2.reshape(1, D_MODEL), wrt, brt)


def _ffn_kernel(x_ref, h_ref, mod_ref, wg_ref, wu_ref, wd_ref, o_ref, *, n_ctx, seq_s):
    row = _mod_row(pl.program_id(0), FFN_TILE, n_ctx, seq_s)
    g2 = mod_ref[pl.ds(row, 1), 5 * D_MODEL:6 * D_MODEL]
    h = h_ref[...]
    half = D_FF // 2
    acc = jnp.zeros((FFN_TILE, D_MODEL), F32)
    for p in range(2):
        sl = slice(p * half, (p + 1) * half)
        t = _silu(_dot(h, wg_ref[:, sl])) * _dot(h, wu_ref[:, sl])
        acc = acc + _dot(t.astype(BF16), wd_ref[sl, :])
    o_ref[...] = x_ref[...] + g2 * acc


def _ffn_dense(x1, h2, mod, w_gate, w_up, w_down, n_ctx, seq_s):
    n = x1.shape[0]
    tile = pl.BlockSpec((FFN_TILE, D_MODEL), lambda i: (i, 0))
    full = lambda shape: pl.BlockSpec(shape, lambda i: (0,) * len(shape))
    return pl.pallas_call(
        functools.partial(_ffn_kernel, n_ctx=n_ctx, seq_s=seq_s),
        grid=(n // FFN_TILE,),
        in_specs=[tile, tile, full((8, 6 * D_MODEL)), full((D_MODEL, D_FF)), full((D_MODEL, D_FF)),
                  full((D_FF, D_MODEL))],
        out_specs=tile,
        out_shape=jax.ShapeDtypeStruct((n, D_MODEL), F32),
        compiler_params=_cparams(("parallel",)),
        name="ffn_dense",
    )(x1, h2, mod, w_gate.astype(BF16), w_up.astype(BF16), w_down.astype(BF16))


def _moe_kernel(x_ref, h_ref, comb_ref, mod_ref, wg_ref, wu_ref, wd_ref, fin_ref, *rest, n_ctx, seq_s, final):
    outs, (acc_ref, rank_c_ref, rank_r_ref, comb_t_ref) = rest[:-4], rest[-4:]
    e = pl.program_id(1)
    t_rows = FFN_TILE

    @pl.when(e == 0)
    def _():
        acc_ref[...] = jnp.zeros_like(acc_ref)
        comb = comb_ref[...]
        before = lax.broadcasted_iota(jnp.int32, (t_rows, t_rows), 1) < lax.broadcasted_iota(jnp.int32, (t_rows, t_rows), 0)
        rank_c_ref[...] = _dot(before.astype(BF16), (comb > 0.0).astype(BF16))
        comb_t = comb.T
        comb_t_ref[...] = comb_t
        rank_r_ref[...] = _dot_nt((comb_t > 0.0).astype(BF16), before.astype(BF16))

    comb = comb_ref[...]
    lane = lax.broadcasted_iota(jnp.int32, comb.shape, 1)
    w_col = jnp.sum(jnp.where(lane == e, comb, 0.0), axis=-1, keepdims=True)
    rank_col = jnp.sum(jnp.where(lane == e, rank_c_ref[...], 0.0), axis=-1, keepdims=True)
    w_row = comb_t_ref[pl.ds(e, 1), :]
    rank_row = rank_r_ref[pl.ds(e, 1), :]
    count = jnp.sum((w_row > 0.0).astype(jnp.int32))

    def body(j, carry):
        base = (j * MOE_ROWS).astype(F32)
        slot_r = lax.broadcasted_iota(jnp.int32, (MOE_ROWS, t_rows), 0).astype(F32) + base
        gather = jnp.where(jnp.logical_and(rank_row == slot_r, w_row > 0.0), 1.0, 0.0).astype(BF16)
        hg = _dot(gather, h_ref[...]).astype(BF16)
        t = _silu(_dot(hg, wg_ref[0])) * _dot(hg, wu_ref[0])
        y = _dot(t.astype(BF16), wd_ref[0]).astype(BF16)
        slot_c = lax.broadcasted_iota(jnp.int32, (t_rows, MOE_ROWS), 1).astype(F32) + base
        scatter = jnp.where(jnp.logical_and(rank_col == slot_c, w_col > 0.0), 1.0, 0.0).astype(BF16)
        acc_ref[...] += w_col * _dot(scatter, y)
        return carry

    lax.fori_loop(0, (count + MOE_ROWS - 1) // MOE_ROWS, body, 0)

    @pl.when(e == N_EXPERTS - 1)
    def _():
        row = _mod_row(pl.program_id(0), FFN_TILE, n_ctx, seq_s)
        g2 = mod_ref[pl.ds(row, 1), 5 * D_MODEL:6 * D_MODEL]
        y = x_ref[...] + g2 * acc_ref[...]
        if not final:
            outs[0][...] = y
        else:
            y = y * lax.rsqrt(jnp.mean(y * y, axis=-1, keepdims=True) + RMS_EPS) * fin_ref[...]
            is_ctx = pl.program_id(0) < n_ctx // FFN_TILE

            @pl.when(is_ctx)
            def _():
                outs[0][...] = y

            @pl.when(jnp.logical_not(is_ctx))
            def _():
                outs[1][...] = y


def _ffn_moe(x1, h2, comb, mod, w_gate, w_up, w_down, n_ctx, seq_s, final_gain=None):
    n = x1.shape[0]
    final = final_gain is not None
    tile = lambda w: pl.BlockSpec((FFN_TILE, w), lambda i, e: (i, 0))
    if final:
        n_ctx_tiles = n_ctx // FFN_TILE
        out_specs = [pl.BlockSpec((FFN_TILE, D_MODEL), lambda i, e: (jnp.minimum(i, n_ctx_tiles - 1), 0)),
                     pl.BlockSpec((FFN_TILE, D_MODEL), lambda i, e: (jnp.maximum(i - n_ctx_tiles, 0), 0))]
        out_shape = [jax.ShapeDtypeStruct((n_ctx, D_MODEL), F32), jax.ShapeDtypeStruct((n - n_ctx, D_MODEL), F32)]
        gain = final_gain.reshape(1, D_MODEL)
    else:
        out_specs = [tile(D_MODEL)]
        out_shape = [jax.ShapeDtypeStruct((n, D_MODEL), F32)]
        gain = jnp.ones((1, D_MODEL), F32)
    out = pl.pallas_call(
        functools.partial(_moe_kernel, n_ctx=n_ctx, seq_s=seq_s, final=final),
        grid=(n // FFN_TILE, N_EXPERTS),
        in_specs=[tile(D_MODEL), tile(D_MODEL), tile(LANES),
                  pl.BlockSpec((8, 6 * D_MODEL), lambda i, e: (0, 0)),
                  pl.BlockSpec((1, D_MODEL, D_FF_E), lambda i, e: (e, 0, 0)),
                  pl.BlockSpec((1, D_MODEL, D_FF_E), lambda i, e: (e, 0, 0)),
                  pl.BlockSpec((1, D_FF_E, D_MODEL), lambda i, e: (e, 0, 0)),
                  pl.BlockSpec((1, D_MODEL), lambda i, e: (0, 0))],
        out_specs=out_specs,
        out_shape=out_shape,
        scratch_shapes=[pltpu.VMEM((FFN_TILE, D_MODEL), F32), pltpu.VMEM((FFN_TILE, LANES), F32),
                        pltpu.VMEM((LANES, FFN_TILE), F32), pltpu.VMEM((LANES, FFN_TILE), F32)],
        compiler_params=_cparams(("arbitrary", "arbitrary") if final else ("parallel", "arbitrary")),
        name="ffn_moe",
    )(x1, h2, comb, mod, w_gate.astype(BF16), w_up.astype(BF16), w_down.astype(BF16), gain)
    return tuple(out) if final else out[0]


def _final_kernel(x_ref, g_ref, o_ref):
    x = x_ref[...]
    o_ref[...] = x * lax.rsqrt(jnp.mean(x * x, axis=-1, keepdims=True) + RMS_EPS) * g_ref[...]


def _final_norm(x, gain, row0, n_rows):
    blk0 = row0 // FFN_TILE
    return pl.pallas_call(
        _final_kernel,
        grid=(n_rows // FFN_TILE,),
        in_specs=[pl.BlockSpec((FFN_TILE, D_MODEL), lambda i: (blk0 + i, 0)),
                  pl.BlockSpec((1, D_MODEL), lambda i: (0, 0))],
        out_specs=pl.BlockSpec((FFN_TILE, D_MODEL), lambda i: (i, 0)),
        out_shape=jax.ShapeDtypeStruct((n_rows, D_MODEL), F32),
        compiler_params=_cparams(("parallel",)),
        name="final_norm",
    )(x, gain.reshape(1, D_MODEL))


def kernel(x_prompt, x_sample, state_rwkv, c, c_ctx, norm1, norm2, w_ada, b_ada, w_in, dw_w, dw_b, conv_ln_g, conv_ln_b, w_conv_out, w_fno_out, shift_mu, g_up, dec_w0, dec_up, iclr_a0, iclr_up, k_k, k_a, r_k, gn_g, gn_b, w_rwkv_out, w_o, ffn_w_gate, ffn_w_up, ffn_w_down, w_router, b_router, moe_w_gate, moe_w_up, moe_w_down, final_norm):
    b_p, t_p, _ = x_prompt.shape
    b_s, t_s, _ = x_sample.shape
    depth = w_in.shape[0]
    n_ctx = b_p * t_p
    n_lat = b_s * t_s
    assert t_p == ROW_TILE and t_s % FFN_TILE == 0 and n_ctx % FFN_TILE == 0 and b_s <= CTX_ROW
    assert t_s % GRID_W == 0 and CHUNK == GRID_W

    n = n_ctx + n_lat
    x = (x_prompt.reshape(n_ctx, D_MODEL), x_sample.reshape(n_lat, D_MODEL))
    cond = jnp.zeros((8, D_MODEL), F32).at[:b_s].set(c).at[CTX_ROW].set(c_ctx)
    mods = _ada(cond, w_ada, b_ada)
    table = _scan_table(b_p, t_p, b_s, t_s)
    zero_state = jnp.zeros((b_p, 2, HEAD_DIM, C_RWKV), F32)
    pack_state = lambda s: s.transpose(0, 1, 3, 2, 4).reshape(s.shape[0], 2, HEAD_DIM, C_RWKV)
    unpack_state = lambda s: s.reshape(s.shape[0], 2, HEAD_DIM, N_HEADS, HEAD_DIM).transpose(0, 1, 3, 2, 4)

    ctx_states = []
    for l in range(depth):
        mod = mods[l]
        u_conv, u_fno, u_rw = _inproj(x, n, mod, norm1[l], w_in[l, :, :OFF_GATE].astype(BF16), n_ctx, t_s)
        conv_h = _conv_branch(u_conv, dw_w[l], dw_b[l], conv_ln_g[l], conv_ln_b[l], n_ctx)
        fno_h = (_fno_branch(u_fno, 0, b_p, t_p), _fno_branch(u_fno, n_ctx, b_s, t_s))
        r, k, v, g, *summaries = _rwkv_project_and_summarise(u_rw, shift_mu[l], g_up[l], dec_w0[l], dec_up[l],
                                                             iclr_a0[l], iclr_up[l], k_k[l], k_a[l], n_ctx, t_s)
        s0 = jnp.concatenate([zero_state, pack_state(state_rwkv[:, l])], axis=0)
        yf, yb, s_fin = _carry_states(*summaries, s0, table)
        ctx_states.append(unpack_state(s_fin[:b_p]))
        i = l // 2
        if l % 2 == 0:
            w_rt, b_rt = jnp.zeros((D_MODEL, N_EXPERTS), F32), jnp.zeros((N_EXPERTS,), F32)
        else:
            w_rt, b_rt = w_router[i], b_router[i]
        x1, h2, comb = _merge(x, n, conv_h, fno_h, yf, yb, r, k, v, g, norm1[l], w_in[l, :, OFF_GATE:].astype(BF16), mod, w_conv_out[l], w_fno_out[l],
                              w_rwkv_out[l], w_o[l], gn_g[l], gn_b[l], r_k[l], norm2[l], w_rt, b_rt, n_ctx, t_s)
        if l % 2 == 0:
            x = _ffn_dense(x1, h2, mod, ffn_w_gate[i], ffn_w_up[i], ffn_w_down[i], n_ctx, t_s)
        else:
            x = _ffn_moe(x1, h2, comb, mod, moe_w_gate[i], moe_w_up[i], moe_w_down[i], n_ctx, t_s,
                         final_gain=final_norm if l == depth - 1 else None)

    if isinstance(x, tuple):
        y_prompt, y_sample = x
    else:
        y_prompt, y_sample = _final_norm(x, final_norm, 0, n_ctx), _final_norm(x, final_norm, n_ctx, n_lat)
    y_prompt = y_prompt.reshape(b_p, t_p, D_MODEL)
    y_sample = y_sample.reshape(b_s, t_s, D_MODEL)
    new_state = jnp.stack(ctx_states, axis=1).astype(x_prompt.dtype)
    return (y_prompt, y_sample, new_state)
```

```python
import functools

import numpy as np
import jax
import jax.numpy as jnp
from jax import lax
from jax.experimental import pallas as pl
from jax.experimental.pallas import tpu as pltpu

F32 = jnp.float32
BF16 = jnp.bfloat16

D_MODEL = 1024
GRID_W = 64
C_CONV = 256
CONV_W = 31
C_FNO = 256
FNO_GW = 64
N_HEADS = 8
HEAD_DIM = 64
C_RWKV = N_HEADS * HEAD_DIM
G_RANK = 128
W_RANK = 64
A_RANK = 64
D_FF = 2816
N_EXPERTS = 8
D_FF_E = 1408
RMS_EPS = 1e-6
LN_EPS = 1e-5
GN_EPS = 64e-5

OFF_FNO = 2 * C_CONV
OFF_RWKV = OFF_FNO + C_FNO
N_SHIFT = 3 * C_RWKV + G_RANK + 2 * (W_RANK + A_RANK)
OFF_GATE = OFF_RWKV + N_SHIFT
D_IN = OFF_GATE + 3 * D_MODEL

ROW_TILE = 256
FFN_TILE = 512
CHUNK = 64
GROUP = 4
GROUP_W = GROUP * HEAD_DIM
SUMMARY_CHUNKS = 4
CARRY_CHUNKS = 4
MOE_TILE = 1024
MOE_ROWS = 288
CTX_ROW = 4
LANES = 128
VMEM_LIMIT = 56 * 1024 * 1024


def _cparams(sem):
    return pltpu.CompilerParams(dimension_semantics=sem, vmem_limit_bytes=VMEM_LIMIT)


def _sigmoid(x):
    return 1.0 / (1.0 + jnp.exp(-x))


def _silu(x):
    return x * _sigmoid(x)


def _dot(a, b):
    return jnp.dot(a, b, preferred_element_type=F32)


def _split(x):
    hi = x.astype(BF16)
    lo = (x - hi.astype(F32)).astype(BF16)
    return hi, lo


def _dot_x3(a, b):
    ah, al = _split(a)
    bh, bl = _split(b)
    return _dot(ah, bh) + (_dot(ah, bl) + _dot(al, bh))


def _seg_sum(x, bd):
    return _dot(x.astype(BF16), bd)


def _mod_row(i, tile, n_ctx_rows, seq_s):
    n_ctx_tiles = n_ctx_rows // tile
    return jnp.where(i < n_ctx_tiles, CTX_ROW, (i - n_ctx_tiles) // (seq_s // tile))


def _stream_specs(tile, width, n_ctx, lat_block0):
    n_ctx_tiles = n_ctx // tile
    return [pl.BlockSpec((tile, width), lambda i: (jnp.minimum(i, n_ctx_tiles - 1), 0)),
            pl.BlockSpec((tile, width), lambda i: (jnp.maximum(i - n_ctx_tiles, 0) + lat_block0, 0))]


def _stream_args(x, tile, n_ctx):
    if isinstance(x, tuple):
        return x[0], x[1], 0
    return x, x, n_ctx // tile


def _pick_stream(i, tile, n_ctx, ctx_ref, lat_ref):
    return jnp.where(i < n_ctx // tile, ctx_ref[...], lat_ref[...])


def _ada_kernel(c_ref, w_ref, b_ref, o_ref):
    s = _silu(c_ref[...])
    o_ref[0] = _dot(s.astype(BF16), w_ref[0].astype(BF16)) + b_ref[0]


def _ada(cond, w_ada, b_ada):
    n_l = w_ada.shape[0]
    tn = 1536
    return pl.pallas_call(
        _ada_kernel,
        grid=(n_l, 6 * D_MODEL // tn),
        in_specs=[
            pl.BlockSpec((8, D_MODEL), lambda l, j: (0, 0)),
            pl.BlockSpec((1, D_MODEL, tn), lambda l, j: (l, 0, j)),
            pl.BlockSpec((1, 1, tn), lambda l, j: (l, 0, j)),
        ],
        out_specs=pl.BlockSpec((1, 8, tn), lambda l, j: (l, 0, j)),
        out_shape=jax.ShapeDtypeStruct((n_l, 8, 6 * D_MODEL), F32),
        compiler_params=_cparams(("parallel", "parallel")),
        name="ada",
    )(cond, w_ada, b_ada.reshape(n_l, 1, 6 * D_MODEL))


def _inproj_kernel(xc_ref, xl_ref, mod_ref, g_ref, w_ref, oc_ref, of_ref, or_ref, *, n_ctx, seq_s):
    row = _mod_row(pl.program_id(0), ROW_TILE, n_ctx, seq_s)
    m = mod_ref[pl.ds(row, 1), :]
    sh = m[:, 0:D_MODEL]
    sc = m[:, D_MODEL:2 * D_MODEL]
    x = _pick_stream(pl.program_id(0), ROW_TILE, n_ctx, xc_ref, xl_ref)
    y = x * lax.rsqrt(jnp.mean(x * x, axis=-1, keepdims=True) + RMS_EPS) * g_ref[...]
    h = (y * (1.0 + sc) + sh).astype(BF16)
    oc_ref[...] = _dot(h, w_ref[:, 0:OFF_FNO])
    of_ref[...] = _dot(h, w_ref[:, OFF_FNO:OFF_RWKV])
    or_ref[...] = _dot(h, w_ref[:, OFF_RWKV:OFF_GATE])


def _inproj(x, n, mod, gain, w_in, n_ctx, seq_s):
    xc, xl, lat0 = _stream_args(x, ROW_TILE, n_ctx)
    widths = (OFF_FNO, C_FNO, N_SHIFT)
    return pl.pallas_call(
        functools.partial(_inproj_kernel, n_ctx=n_ctx, seq_s=seq_s),
        grid=(n // ROW_TILE,),
        in_specs=_stream_specs(ROW_TILE, D_MODEL, n_ctx, lat0) + [
            pl.BlockSpec((8, 6 * D_MODEL), lambda i: (0, 0)),
            pl.BlockSpec((1, D_MODEL), lambda i: (0, 0)),
            pl.BlockSpec((D_MODEL, OFF_GATE), lambda i: (0, 0)),
        ],
        out_specs=[pl.BlockSpec((ROW_TILE, w), lambda i: (i, 0)) for w in widths],
        out_shape=[jax.ShapeDtypeStruct((n, w), F32) for w in widths],
        compiler_params=_cparams(("parallel",)),
        name="inproj",
    )(xc, xl, mod, gain.reshape(1, D_MODEL), w_in)


_CONV_HALO = 16
_CONV_ROWS = 64


def _conv_kernel(u_ref, w_ref, b_ref, g_ref, be_ref, o_ref, pad_ref, sh_ref, *, n_ctx):
    is_ctx = pl.program_id(0) < n_ctx // ROW_TILE
    zeros = jnp.zeros((_CONV_HALO, C_CONV), F32)
    n_parts = ROW_TILE // _CONV_ROWS

    def glu(lo, hi):
        return u_ref[lo:hi, 0:C_CONV] * _sigmoid(u_ref[lo:hi, C_CONV:2 * C_CONV])

    def finish(starts, n_rows):
        for s in range(8):
            sh_ref[s, 0:n_rows - 8, :] = pad_ref[s:s + n_rows - 8, :]
        for p in range(n_parts):
            acc = jnp.zeros((_CONV_ROWS, C_CONV), F32)
            for j in range(CONV_W):
                o = starts[p] + j
                acc = acc + w_ref[j:j + 1, :] * sh_ref[o % 8, o - o % 8:o - o % 8 + _CONV_ROWS, :]
            acc = acc + b_ref[...]
            mu = jnp.mean(acc, axis=-1, keepdims=True)
            d = acc - mu
            var = jnp.mean(d * d, axis=-1, keepdims=True)
            y = d * lax.rsqrt(var + LN_EPS) * g_ref[...] + be_ref[...]
            o_ref[p * _CONV_ROWS:(p + 1) * _CONV_ROWS, :] = _silu(y).astype(o_ref.dtype)

    shift = _CONV_HALO - CONV_W // 2

    @pl.when(is_ctx)
    def _():
        pad_ref[0:_CONV_HALO, :] = zeros
        pad_ref[_CONV_HALO:_CONV_HALO + ROW_TILE, :] = glu(0, ROW_TILE)
        pad_ref[_CONV_HALO + ROW_TILE:2 * _CONV_HALO + ROW_TILE, :] = zeros
        finish([shift + p * _CONV_ROWS for p in range(n_parts)], ROW_TILE + 2 * _CONV_HALO)

    @pl.when(jnp.logical_not(is_ctx))
    def _():
        stride = GRID_W + 2 * _CONV_HALO
        for p in range(n_parts):
            pad_ref[p * stride:p * stride + _CONV_HALO, :] = zeros
            pad_ref[p * stride + _CONV_HALO:p * stride + _CONV_HALO + GRID_W, :] = glu(p * GRID_W, (p + 1) * GRID_W)
            pad_ref[p * stride + _CONV_HALO + GRID_W:(p + 1) * stride, :] = zeros
        finish([p * stride + shift for p in range(n_parts)], n_parts * stride)


def _conv_branch(u_conv, dw_w, dw_b, ln_g, ln_b, n_ctx):
    n = u_conv.shape[0]
    assert _CONV_ROWS == GRID_W and ROW_TILE % GRID_W == 0
    vec = pl.BlockSpec((1, C_CONV), lambda i: (0, 0))
    return pl.pallas_call(
        functools.partial(_conv_kernel, n_ctx=n_ctx),
        grid=(n // ROW_TILE,),
        in_specs=[
            pl.BlockSpec((ROW_TILE, 2 * C_CONV), lambda i: (i, 0)),
            pl.BlockSpec((CONV_W, C_CONV), lambda i: (0, 0)),
            vec, vec, vec,
        ],
        out_specs=pl.BlockSpec((ROW_TILE, C_CONV), lambda i: (i, 0)),
        out_shape=jax.ShapeDtypeStruct((n, C_CONV), BF16),
        scratch_shapes=[pltpu.VMEM((ROW_TILE // GRID_W * (GRID_W + 2 * _CONV_HALO), C_CONV), F32),
                        pltpu.VMEM((8, ROW_TILE // GRID_W * (GRID_W + 2 * _CONV_HALO), C_CONV), F32)],
        compiler_params=_cparams(("parallel",)),
        name="conv_branch",
    )(u_conv, dw_w, dw_b.reshape(1, C_CONV), ln_g.reshape(1, C_CONV), ln_b.reshape(1, C_CONV))


def _dft_tables(t_len):
    def cs(n):
        k = np.arange(n, dtype=np.int64)
        ang = 2.0 * np.pi * ((k[:, None] * k[None, :]) % n).astype(np.float64) / n
        return np.cos(ang), np.sin(ang)
    cg, sg = cs(FNO_GW)
    eye = np.eye(C_FNO // FNO_GW)
    w1 = np.concatenate([np.kron(eye, cg), np.kron(eye, sg)], axis=1)
    ct, st = cs(t_len)
    w2 = np.concatenate([ct, -st], axis=1)
    return jnp.asarray(w1, dtype=F32).astype(BF16), jnp.asarray(w2, dtype=F32).astype(BF16)


def _fno_kernel(u_ref, w1_ref, w2_ref, o_ref, hs_ref, *, t_len, scale):
    @pl.when(pl.program_id(1) == 0)
    def _():
        hc = _dot(u_ref[...].astype(BF16), w1_ref[...])
        hs_ref[0:t_len, :] = hc[:, 0:C_FNO].astype(BF16)
        hs_ref[t_len:2 * t_len, :] = hc[:, C_FNO:2 * C_FNO].astype(BF16)

    o_ref[...] = (_dot(w2_ref[...], hs_ref[...]) * scale).astype(o_ref.dtype)


def _fno_branch(u_fno, row0, n_seq, t_len):
    w1, w2 = _dft_tables(t_len)
    tk = min(t_len, 512)
    blk0 = row0 // t_len
    return pl.pallas_call(
        functools.partial(_fno_kernel, t_len=t_len, scale=float(1.0 / np.sqrt(t_len * FNO_GW))),
        grid=(n_seq, t_len // tk),
        in_specs=[
            pl.BlockSpec((t_len, C_FNO), lambda b, j: (blk0 + b, 0)),
            pl.BlockSpec((C_FNO, 2 * C_FNO), lambda b, j: (0, 0)),
            pl.BlockSpec((tk, 2 * t_len), lambda b, j: (j, 0)),
        ],
        out_specs=pl.BlockSpec((tk, C_FNO), lambda b, j: (b * (t_len // tk) + j, 0)),
        out_shape=jax.ShapeDtypeStruct((n_seq * t_len, C_FNO), BF16),
        scratch_shapes=[pltpu.VMEM((2 * t_len, C_FNO), BF16)],
        compiler_params=_cparams(("parallel", "arbitrary")),
        name="fno_branch",
    )(u_fno, w1, w2)


def _prep_pieces(i, z_ref, zp_ref, zn_ref, mu_ref, gup_ref, w0_ref, dup_ref, a0_ref, aup_ref, kk_ref, bd_ref,
                 r_refs, k_refs, v_refs, kkn_ref, g_ref, lw_ref, a_ref, pad_ref, n_ctx, seq_s):
    n_ctx_tiles = n_ctx // ROW_TILE
    per_seq = seq_s // ROW_TILE
    j = (i - n_ctx_tiles) % per_seq
    first = jnp.logical_or(i < n_ctx_tiles, j == 0)
    last = jnp.logical_or(i < n_ctx_tiles, j == per_seq - 1)
    c = C_RWKV

    def shifted(lo, hi):
        z = pad_ref[8:8 + ROW_TILE, lo:hi]
        zp = pad_ref[7:7 + ROW_TILE, lo:hi]
        zn = pad_ref[9:9 + ROW_TILE, lo:hi]
        return z + mu_ref[:, lo:hi] * (0.5 * (zp + zn) - z)

    def fill():
        pad_ref[8:8 + ROW_TILE, :] = z_ref[...]
        pad_ref[0:8, :] = jnp.where(first, 0.0, zp_ref[...])
        pad_ref[8 + ROW_TILE:16 + ROW_TILE, :] = jnp.where(last, 0.0, zn_ref[...])

    def receptance():
        r = shifted(0, c)
        for ref in r_refs:
            ref[...] = r

    def key():
        k = shifted(c, 2 * c)
        for ref in k_refs:
            ref[...] = k
        kx = k * kk_ref[...]
        nrm = jnp.sqrt(_seg_sum(kx * kx, bd_ref[...]))
        kkn_ref[...] = kx / jnp.maximum(nrm, 1e-12)

    def value():
        v = shifted(2 * c, 3 * c)
        for ref in v_refs:
            ref[...] = v

    def gate():
        o = 3 * c
        g_ref[...] = _dot(_sigmoid(shifted(o, o + G_RANK)).astype(BF16), gup_ref[...])

    def decay():
        o = 3 * c + G_RANK
        xw = _dot(jnp.tanh(shifted(o, o + 2 * W_RANK)).astype(BF16), dup_ref[...]) + w0_ref[...]
        soft = jnp.maximum(-xw, 0.0) + jnp.log(1.0 + jnp.exp(-jnp.abs(xw)))
        lw_ref[...] = -jnp.exp(-soft - 0.5)

    def iclr():
        o = 3 * c + G_RANK + 2 * W_RANK
        xa = _dot(shifted(o, o + 2 * A_RANK).astype(BF16), aup_ref[...]) + a0_ref[...]
        a_ref[...] = _sigmoid(xa)

    return [fill, receptance, key, value, gate, decay, iclr]


def _block_diag2(w):
    z = jnp.zeros_like(w[0])
    return jnp.concatenate([jnp.concatenate([w[0], z], axis=1), jnp.concatenate([z, w[1]], axis=1)], axis=0)


def _head_ones():
    return jnp.asarray(np.kron(np.eye(N_HEADS), np.ones((HEAD_DIM, HEAD_DIM))), dtype=BF16)


def _block_diag(x, mask):
    xb = x.astype(BF16)
    return jnp.where(mask, jnp.concatenate([xb] * GROUP, axis=0), jnp.zeros((), BF16))


def _dot_nt(a, b):
    return lax.dot_general(a, b, (((1,), (1,)), ((), ())), preferred_element_type=F32)


def _diag_blocks(prod, lane_head):
    out = jnp.where(lane_head == 0, prod[0:HEAD_DIM], 0.0)
    for h in range(1, GROUP):
        out = out + jnp.where(lane_head == h, prod[h * HEAD_DIM:(h + 1) * HEAD_DIM], 0.0)
    return out


def _summary_body(r_ref, k_ref, v_ref, n_ref, lw_ref, a_ref, ka_ref, q_ref, y0_ref, p_ref, z_ref, g_ref, fillers=()):
    fillers = list(fillers)
    fill_one = lambda: fillers.pop(0)() if fillers else None
    row = lax.broadcasted_iota(jnp.int32, (CHUNK, GROUP_W), 0)
    lane = lax.broadcasted_iota(jnp.int32, (CHUNK, GROUP_W), 1)
    col = lane % CHUNK
    lane_head = lane // HEAD_DIM
    bd_mask = (lax.broadcasted_iota(jnp.int32, (GROUP * CHUNK, GROUP_W), 0) // CHUNK
               == lax.broadcasted_iota(jnp.int32, (GROUP * CHUNK, GROUP_W), 1) // HEAD_DIM)
    bd = lambda x: _block_diag(x, bd_mask)
    ka = ka_ref[...]

    units = []
    for j in range(SUMMARY_CHUNKS):
        rows = slice(j * CHUNK, (j + 1) * CHUNK)
        r = r_ref[rows, :]
        k = k_ref[rows, :]
        v = v_ref[rows, :]
        kkn = n_ref[rows, :]
        for d in range(2):
            lanes = slice(d * C_RWKV, (d + 1) * C_RWKV)
            earlier = (col < row) if d == 0 else (col > row)
            upto = jnp.logical_or(earlier, row == col)
            tri = upto[:, 0:CHUNK].astype(BF16)
            lw = lw_ref[rows, lanes]
            h1 = lw.astype(BF16)
            r1 = lw - h1.astype(F32)
            h2 = r1.astype(BF16)
            h3 = (r1 - h2.astype(F32)).astype(BF16)
            cum = _dot(tri, h1) + (_dot(tri, h2) + _dot(tri, h3))
            e_in = jnp.exp(cum)
            e_ex = jnp.exp(cum - lw)
            e_ng = jnp.exp(-cum)
            a = a_ref[rows, lanes]
            kd = k * (1.0 + (a - 1.0) * ka)
            at = -kkn * e_ex
            rt = r * e_in
            bt = kkn * a * e_ng
            kt = kd * e_ng
            end = CHUNK - 1 if d == 0 else 0
            g_end = e_in[end:end + 1, :]
            g_ref[j, :, lanes] = g_end
            bh = bt * g_end
            kh = kt * g_end
            for q in range(C_RWKV // GROUP_W):
                sl = slice(q * GROUP_W, (q + 1) * GROUP_W)
                units.append(dict(rows=rows, out=slice(d * C_RWKV + q * GROUP_W, d * C_RWKV + (q + 1) * GROUP_W),
                                  earlier=earlier, upto=upto, at=at[:, sl], rt=rt[:, sl],
                                  ar=jnp.concatenate([at[:, sl], rt[:, sl]], axis=0).astype(BF16),
                                  bt=bt[:, sl], kt=kt[:, sl], v=v[:, sl], bh=bh[:, sl].astype(BF16),
                                  bk=jnp.concatenate([bh[:, sl], kh[:, sl]], axis=0).astype(BF16)))

    for u in units:
        sb = _dot_nt(u["ar"], bd(u["bt"]))
        sk = _dot_nt(u["ar"], bd(u["kt"]))
        u["lab"] = jnp.where(u["earlier"], sb[0:CHUNK], 0.0)
        u["mrb"] = jnp.where(u["upto"], sb[CHUNK:2 * CHUNK], 0.0).astype(BF16)
        u["lm"] = jnp.concatenate([jnp.where(u["earlier"], sk[0:CHUNK], 0.0),
                                   jnp.where(u["upto"], sk[CHUNK:2 * CHUNK], 0.0)], axis=0).astype(BF16)

    fill_one()
    eye = (row == col).astype(F32)
    pair = jnp.logical_and(row // 2 == col // 2, row != col)
    for u in units:
        u["t"] = eye + jnp.where(pair, u["lab"], 0.0)
    n = 2
    while n < CHUNK:
        m = jnp.logical_and(row // (2 * n) == col // (2 * n), row // n != col // n)
        for u in units:
            u["w"] = _dot(jnp.where(m, u["lab"], 0.0).astype(BF16), bd(u["t"]))
        for u in units:
            u["t"] = u["t"] + _dot(u["t"].astype(BF16), bd(u["w"]))
        fill_one()
        n *= 2

    for u in units:
        u["t"] = u["t"].astype(BF16)
        u["wm"] = _dot(u["t"], bd(u["at"]))
        u["lv"] = _dot(u["lm"], bd(u["v"]))
    for u in units:
        u["u0"] = _dot(u["t"], bd(u["lv"][0:CHUNK]))
    for u in units:
        q_ref[u["rows"], u["out"]] = (u["rt"] + _dot(u["mrb"], bd(u["wm"]))).astype(q_ref.dtype)
        y0_ref[u["rows"], u["out"]] = u["lv"][CHUNK:2 * CHUNK] + _dot(u["mrb"], bd(u["u0"]))
    for u in units:
        p = _dot(u["wm"].T.astype(BF16), u["bh"])
        p_ref[u["rows"], u["out"]] = _diag_blocks(p, lane_head).astype(p_ref.dtype)
        z = _dot(jnp.concatenate([u["u0"], u["v"]], axis=0).T.astype(BF16), u["bk"])
        z_ref[u["rows"], u["out"]] = _diag_blocks(z, lane_head)
    while fillers:
        fill_one()


def _rwkv_kernel(z_ref, zp_ref, zn_ref, mu_ref, gup_ref, w0_ref, dup_ref, a0_ref, aup_ref, kk_ref, bd_ref, ka_ref,
                 r_ref, k_ref, v_ref, g_ref, q_ref, y0_ref, p_ref, zz_ref, gd_ref,
                 pad_ref, r_n, k_n, v_n, n_n, lw_n, a_n, r_c, k_c, v_c, n_c, lw_c, a_c, *, n_ctx, seq_s, n_tiles):
    s = pl.program_id(0)
    new = (r_n, k_n, v_n, n_n, lw_n, a_n)
    cur = (r_c, k_c, v_c, n_c, lw_c, a_c)

    @pl.when(s == 0)
    def _():
        for ref in cur:
            ref[...] = jnp.zeros_like(ref)

    pieces = _prep_pieces(jnp.minimum(s, n_tiles - 1), z_ref, zp_ref, zn_ref, mu_ref, gup_ref, w0_ref, dup_ref,
                          a0_ref, aup_ref, kk_ref, bd_ref, (r_n, r_ref), (k_n, k_ref), (v_n, v_ref), n_n, g_ref,
                          lw_n, a_n, pad_ref, n_ctx, seq_s)
    pieces[0]()
    _summary_body(r_c, k_c, v_c, n_c, lw_c, a_c, ka_ref, q_ref, y0_ref, p_ref, zz_ref, gd_ref, fillers=pieces[1:])
    for dst, src in zip(cur, new):
        dst[...] = src[...]


def _rwkv_project_and_summarise(u_rw, mu, g_up, dec_w0, dec_up, iclr_a0, iclr_up, k_k, k_a, n_ctx, seq_s):
    n = u_rw.shape[0]
    assert ROW_TILE == SUMMARY_CHUNKS * CHUNK and CHUNK == HEAD_DIM and C_RWKV % GROUP_W == 0
    n_tiles = n // ROW_TILE
    halo = ROW_TILE // 8
    last_blk = n // 8 - 1
    c = C_RWKV
    full = lambda shape: pl.BlockSpec(shape, lambda s: (0,) * len(shape))
    this = lambda s: jnp.minimum(s, n_tiles - 1)
    prev = lambda s: jnp.maximum(s - 1, 0)
    proj = lambda w: pl.BlockSpec((ROW_TILE, w), lambda s: (this(s), 0))
    summ = lambda w: pl.BlockSpec((ROW_TILE, w), lambda s: (prev(s), 0))
    vm = lambda w: pltpu.VMEM((ROW_TILE, w), F32)
    return pl.pallas_call(
        functools.partial(_rwkv_kernel, n_ctx=n_ctx, seq_s=seq_s, n_tiles=n_tiles),
        grid=(n_tiles + 1,),
        in_specs=[
            proj(N_SHIFT),
            pl.BlockSpec((8, N_SHIFT), lambda s: (jnp.maximum(this(s) * halo - 1, 0), 0)),
            pl.BlockSpec((8, N_SHIFT), lambda s: (jnp.minimum((this(s) + 1) * halo, last_blk), 0)),
            full((1, N_SHIFT)), full((G_RANK, c)), full((1, 2 * c)), full((2 * W_RANK, 2 * c)),
            full((1, 2 * c)), full((2 * A_RANK, 2 * c)), full((1, c)), full((c, c)), full((1, c)),
        ],
        out_specs=[proj(c), proj(c), proj(c), proj(c), summ(2 * c), summ(2 * c), summ(2 * c), summ(2 * c),
                   pl.BlockSpec((SUMMARY_CHUNKS, 1, 2 * c), lambda s: (prev(s), 0, 0))],
        out_shape=[jax.ShapeDtypeStruct((n, c), F32)] * 4
        + [jax.ShapeDtypeStruct((n, 2 * c), BF16), jax.ShapeDtypeStruct((n, 2 * c), F32),
           jax.ShapeDtypeStruct((n, 2 * c), BF16), jax.ShapeDtypeStruct((n, 2 * c), F32),
           jax.ShapeDtypeStruct((n // CHUNK, 1, 2 * c), F32)],
        scratch_shapes=[pltpu.VMEM((ROW_TILE + 16, N_SHIFT), F32)]
        + [vm(c), vm(c), vm(c), vm(c), vm(2 * c), vm(2 * c)] * 2,
        compiler_params=_cparams(("arbitrary",)),
        name="rwkv_project_summarise",
    )(u_rw, u_rw, u_rw, mu.reshape(1, N_SHIFT), g_up.astype(BF16), dec_w0.reshape(1, 2 * c),
      _block_diag2(dec_up).astype(BF16), iclr_a0.reshape(1, 2 * c), _block_diag2(iclr_up).astype(BF16),
      k_k.reshape(1, c), _head_ones(), k_a.reshape(1, c))


def _carry_kernel(tab_ref, qf_ref, y0f_ref, pf_ref, zf_ref, gf_ref, qb_ref, y0b_ref, pb_ref, zb_ref, gb_ref,
                  s0_ref, yf_ref, yb_ref, sfin_ref, st_ref):
    step = pl.program_id(0)

    @pl.when(tab_ref[3, step] == 1)
    def _():
        st_ref[...] = s0_ref[0]

    bd_mask = (lax.broadcasted_iota(jnp.int32, (GROUP * CHUNK, GROUP_W), 0) // CHUNK
               == lax.broadcasted_iota(jnp.int32, (GROUP * CHUNK, GROUP_W), 1) // HEAD_DIM)
    for i in range(CARRY_CHUNKS):
        for d, (q_ref, y0_ref, p_ref, z_ref, g_ref, y_ref) in enumerate(
                ((qf_ref, y0f_ref, pf_ref, zf_ref, gf_ref, yf_ref), (qb_ref, y0b_ref, pb_ref, zb_ref, gb_ref, yb_ref))):
            j = i if d == 0 else CARRY_CHUNKS - 1 - i
            rows = slice(j * CHUNK, (j + 1) * CHUNK)
            for q in range(C_RWKV // GROUP_W):
                sl = slice(q * GROUP_W, (q + 1) * GROUP_W)
                s = st_ref[d, :, sl]
                y_ref[rows, sl] = y0_ref[rows, sl] + _dot_nt(q_ref[rows, sl], _block_diag(s, bd_mask))
                st_ref[d, :, sl] = (s * g_ref[j, :, sl] + z_ref[rows, sl]
                                    + _dot(s.astype(BF16), _block_diag(p_ref[rows, sl], bd_mask)))

    @pl.when(tab_ref[4, step] == 1)
    def _():
        sfin_ref[0] = st_ref[...]


def _scan_table(n_ctx_seq, t_ctx, n_s_seq, t_s):
    rows = []
    base = 0
    seq = 0
    for n_seq, t_len in ((n_ctx_seq, t_ctx), (n_s_seq, t_s)):
        assert t_len % (CARRY_CHUNKS * CHUNK) == 0
        n_c = t_len // (CARRY_CHUNKS * CHUNK)
        for _ in range(n_seq):
            for c in range(n_c):
                rows.append((base + c, base + n_c - 1 - c, seq, int(c == 0), int(c == n_c - 1)))
            base += n_c
            seq += 1
    return np.asarray(rows, dtype=np.int32).T.copy()


def _carry_states(q, y0, p, z, g, s0, table):
    n = q.shape[0]
    n_seq = s0.shape[0]
    c = C_RWKV

    def tok(which, d):
        return pl.BlockSpec((CARRY_CHUNKS * CHUNK, c), lambda s, tab: (tab[which, s], d))

    def decay(which, d):
        return pl.BlockSpec((CARRY_CHUNKS, 1, c), lambda s, tab: (tab[which, s], 0, d))

    state_spec = pl.BlockSpec((1, 2, HEAD_DIM, c), lambda s, tab: (tab[2, s], 0, 0, 0))
    grid_spec = pltpu.PrefetchScalarGridSpec(
        num_scalar_prefetch=1,
        grid=(table.shape[1],),
        in_specs=[tok(0, 0), tok(0, 0), tok(0, 0), tok(0, 0), decay(0, 0),
                  tok(1, 1), tok(1, 1), tok(1, 1), tok(1, 1), decay(1, 1), state_spec],
        out_specs=[tok(0, 0), tok(1, 0), state_spec],
        scratch_shapes=[pltpu.VMEM((2, HEAD_DIM, c), F32)],
    )
    return pl.pallas_call(
        _carry_kernel,
        grid_spec=grid_spec,
        out_shape=[jax.ShapeDtypeStruct((n, c), F32), jax.ShapeDtypeStruct((n, c), F32),
                   jax.ShapeDtypeStruct((n_seq, 2, HEAD_DIM, c), F32)],
        compiler_params=_cparams(("arbitrary",)),
        name="rwkv_carry",
    )(jnp.asarray(table), q, y0, p, z, g, q, y0, p, z, g, s0)


def _merge_kernel(xc_ref, xl_ref, c_ref, fc_ref, fl_ref, yf_ref, yb_ref, r_ref, k_ref, v_ref, g_ref, n1_ref, wgt_ref, mod_ref,
                  wc_ref, wf_ref, wr_ref, wo_ref, gng_ref, gnb_ref, rk_ref, bd_ref, n2_ref, *rest, n_ctx, seq_s, route):
    if route:
        wrt_ref, brt_ref, x1_ref, h2_ref, comb_ref = rest
    else:
        x1_ref, h2_ref = rest
    row = _mod_row(pl.program_id(0), ROW_TILE, n_ctx, seq_s)
    m = mod_ref[pl.ds(row, 1), :]
    sh1 = m[:, 0:D_MODEL]
    sc1 = m[:, D_MODEL:2 * D_MODEL]
    g1 = m[:, 2 * D_MODEL:3 * D_MODEL]
    sh2 = m[:, 3 * D_MODEL:4 * D_MODEL]
    sc2 = m[:, 4 * D_MODEL:5 * D_MODEL]
    bd = bd_ref[...]
    inv = 1.0 / HEAD_DIM
    y = yf_ref[...] + yb_ref[...]
    y_hi, y_lo = _split(y)
    d = y - (_dot(y_hi, bd) + _dot(y_lo, bd)) * inv
    var = _seg_sum(d * d, bd) * inv
    yn = d * lax.rsqrt(var + GN_EPS) * gng_ref[...] + gnb_ref[...]
    v = v_ref[...]
    yn = yn + _seg_sum(r_ref[...] * k_ref[...] * rk_ref[...], bd) * v
    yr = _dot((yn * g_ref[...]).astype(BF16), wr_ref[...])
    yc = _dot(c_ref[...], wc_ref[...])
    yf = _dot(_pick_stream(pl.program_id(0), ROW_TILE, n_ctx, fc_ref, fl_ref), wf_ref[...])
    x = _pick_stream(pl.program_id(0), ROW_TILE, n_ctx, xc_ref, xl_ref)
    hn = x * lax.rsqrt(jnp.mean(x * x, axis=-1, keepdims=True) + RMS_EPS) * n1_ref[...]
    hn = (hn * (1.0 + sc1) + sh1).astype(BF16)
    dm = D_MODEL
    merged = (_sigmoid(_dot(hn, wgt_ref[:, 0:dm])) * yc + _sigmoid(_dot(hn, wgt_ref[:, dm:2 * dm])) * yf
              + _sigmoid(_dot(hn, wgt_ref[:, 2 * dm:3 * dm])) * yr)
    x1 = x + g1 * _dot(merged.astype(BF16), wo_ref[...])
    x1_ref[...] = x1
    h2 = x1 * lax.rsqrt(jnp.mean(x1 * x1, axis=-1, keepdims=True) + RMS_EPS) * n2_ref[...]
    h2 = h2 * (1.0 + sc2) + sh2
    h2_ref[...] = h2.astype(BF16)
    if not route:
        return
    logits = _dot_x3(h2, wrt_ref[...]) + brt_ref[...]
    lane = lax.broadcasted_iota(jnp.int32, logits.shape, 1).astype(F32)
    neg = jnp.float32(-jnp.inf)
    lg = jnp.where(lane < N_EXPERTS, logits, neg)
    m1 = jnp.max(lg, axis=-1, keepdims=True)
    i1 = jnp.min(jnp.where(lg == m1, lane, float(LANES)), axis=-1, keepdims=True)
    lg2 = jnp.where(lane == i1, neg, lg)
    m2 = jnp.max(lg2, axis=-1, keepdims=True)
    i2 = jnp.min(jnp.where(lg2 == m2, lane, float(LANES)), axis=-1, keepdims=True)
    e = jnp.exp(m2 - m1)
    p1 = 1.0 / (1.0 + e)
    comb_ref[...] = jnp.where(lane == i1, p1, 0.0) + jnp.where(lane == i2, e * p1, 0.0)


def _merge(x, n, conv_h, fno_h, yf, yb, r, k, v, g, gain1, w_gates, mod, w_conv_out, w_fno_out, w_rwkv_out, w_o,
           gn_g, gn_b, r_k, gain2, n_ctx, seq_s, router=None):
    xc, xl, x_lat0 = _stream_args(x, ROW_TILE, n_ctx)
    fc, fl, f_lat0 = _stream_args(fno_h, ROW_TILE, n_ctx)
    c = C_RWKV
    tile = lambda w: pl.BlockSpec((ROW_TILE, w), lambda i: (i, 0))
    full = lambda shape: pl.BlockSpec(shape, lambda i: (0,) * len(shape))
    route = router is not None
    route_args, route_specs, route_out, route_shape = [], [], [], []
    if route:
        route_args = [jnp.zeros((D_MODEL, LANES), F32).at[:, :N_EXPERTS].set(router[0]),
                      jnp.zeros((1, LANES), F32).at[0, :N_EXPERTS].set(router[1])]
        route_specs = [full((D_MODEL, LANES)), full((1, LANES))]
        route_out = [tile(LANES)]
        route_shape = [jax.ShapeDtypeStruct((n, LANES), F32)]
    return pl.pallas_call(
        functools.partial(_merge_kernel, n_ctx=n_ctx, seq_s=seq_s, route=route),
        grid=(n // ROW_TILE,),
        in_specs=_stream_specs(ROW_TILE, D_MODEL, n_ctx, x_lat0) + [tile(C_CONV)]
        + _stream_specs(ROW_TILE, C_FNO, n_ctx, f_lat0) + [
            tile(c), tile(c), tile(c), tile(c), tile(c), tile(c),
            full((1, D_MODEL)), full((D_MODEL, 3 * D_MODEL)), full((8, 6 * D_MODEL)),
            full((C_CONV, D_MODEL)), full((C_FNO, D_MODEL)), full((c, D_MODEL)), full((D_MODEL, D_MODEL)),
            full((1, c)), full((1, c)), full((1, c)), full((c, c)), full((1, D_MODEL)),
        ] + route_specs,
        out_specs=[tile(D_MODEL), tile(D_MODEL)] + route_out,
        out_shape=[jax.ShapeDtypeStruct((n, D_MODEL), F32), jax.ShapeDtypeStruct((n, D_MODEL), BF16)] + route_shape,
        compiler_params=_cparams(("parallel",)),
        name="merge",
    )(xc, xl, conv_h, fc, fl, yf, yb, r, k, v, g, gain1.reshape(1, D_MODEL), w_gates, mod,
      w_conv_out.astype(BF16), w_fno_out.astype(BF16), w_rwkv_out.astype(BF16), w_o.astype(BF16),
      gn_g.reshape(1, c), gn_b.reshape(1, c), r_k.reshape(1, c), _head_ones(), gain2.reshape(1, D_MODEL), *route_args)


def _ffn_kernel(x_ref, h_ref, mod_ref, wg_ref, wu_ref, wd_ref, o_ref, *, n_ctx, seq_s):
    row = _mod_row(pl.program_id(0), FFN_TILE, n_ctx, seq_s)
    g2 = mod_ref[pl.ds(row, 1), 5 * D_MODEL:6 * D_MODEL]
    h = h_ref[...]
    half = D_FF // 2
    acc = jnp.zeros((FFN_TILE, D_MODEL), F32)
    for p in range(2):
        sl = slice(p * half, (p + 1) * half)
        t = _silu(_dot(h, wg_ref[:, sl])) * _dot(h, wu_ref[:, sl])
        acc = acc + _dot(t.astype(BF16), wd_ref[sl, :])
    o_ref[...] = x_ref[...] + g2 * acc


def _ffn_dense(x1, h2, mod, w_gate, w_up, w_down, n_ctx, seq_s):
    n = x1.shape[0]
    tile = pl.BlockSpec((FFN_TILE, D_MODEL), lambda i: (i, 0))
    full = lambda shape: pl.BlockSpec(shape, lambda i: (0,) * len(shape))
    return pl.pallas_call(
        functools.partial(_ffn_kernel, n_ctx=n_ctx, seq_s=seq_s),
        grid=(n // FFN_TILE,),
        in_specs=[tile, tile, full((8, 6 * D_MODEL)), full((D_MODEL, D_FF)), full((D_MODEL, D_FF)),
                  full((D_FF, D_MODEL))],
        out_specs=tile,
        out_shape=jax.ShapeDtypeStruct((n, D_MODEL), F32),
        compiler_params=_cparams(("parallel",)),
        name="ffn_dense",
    )(x1, h2, mod, w_gate.astype(BF16), w_up.astype(BF16), w_down.astype(BF16))


def _moe_kernel(x_ref, h_ref, comb_ref, mod_ref, wg_ref, wu_ref, wd_ref, fin_ref, *rest, n_ctx, seq_s, final):
    outs, (acc_ref, rank_c_ref, rank_r_ref, comb_t_ref) = rest[:-4], rest[-4:]
    e = pl.program_id(1)
    t_rows = MOE_TILE

    @pl.when(e == 0)
    def _():
        acc_ref[...] = jnp.zeros_like(acc_ref)
        comb = comb_ref[...]
        before = lax.broadcasted_iota(jnp.int32, (t_rows, t_rows), 1) < lax.broadcasted_iota(jnp.int32, (t_rows, t_rows), 0)
        rank_c_ref[...] = _dot(before.astype(BF16), (comb > 0.0).astype(BF16))
        comb_t = comb.T
        comb_t_ref[...] = comb_t
        rank_r_ref[...] = _dot_nt((comb_t > 0.0).astype(BF16), before.astype(BF16))

    comb = comb_ref[...]
    lane = lax.broadcasted_iota(jnp.int32, comb.shape, 1)
    w_col = jnp.sum(jnp.where(lane == e, comb, 0.0), axis=-1, keepdims=True)
    rank_col = jnp.sum(jnp.where(lane == e, rank_c_ref[...], 0.0), axis=-1, keepdims=True)
    w_row = comb_t_ref[pl.ds(e, 1), :]
    rank_row = rank_r_ref[pl.ds(e, 1), :]
    count = jnp.sum((w_row > 0.0).astype(jnp.int32))

    def body(j, carry):
        base = (j * MOE_ROWS).astype(F32)
        slot_r = lax.broadcasted_iota(jnp.int32, (MOE_ROWS, t_rows), 0).astype(F32) + base
        gather = jnp.where(jnp.logical_and(rank_row == slot_r, w_row > 0.0), 1.0, 0.0).astype(BF16)
        hg = _dot(gather, h_ref[...]).astype(BF16)
        t = _silu(_dot(hg, wg_ref[0])) * _dot(hg, wu_ref[0])
        y = _dot(t.astype(BF16), wd_ref[0]).astype(BF16)
        slot_c = lax.broadcasted_iota(jnp.int32, (t_rows, MOE_ROWS), 1).astype(F32) + base
        scatter = jnp.where(jnp.logical_and(rank_col == slot_c, w_col > 0.0), 1.0, 0.0).astype(BF16)
        acc_ref[...] += w_col * _dot(scatter, y)
        return carry

    lax.fori_loop(0, (count + MOE_ROWS - 1) // MOE_ROWS, body, 0)

    @pl.when(e == N_EXPERTS - 1)
    def _():
        row = _mod_row(pl.program_id(0), MOE_TILE, n_ctx, seq_s)
        g2 = mod_ref[pl.ds(row, 1), 5 * D_MODEL:6 * D_MODEL]
        y = x_ref[...] + g2 * acc_ref[...]
        if not final:
            outs[0][...] = y
        else:
            y = y * lax.rsqrt(jnp.mean(y * y, axis=-1, keepdims=True) + RMS_EPS) * fin_ref[...]
            is_ctx = pl.program_id(0) < n_ctx // MOE_TILE

            @pl.when(is_ctx)
            def _():
                outs[0][...] = y

            @pl.when(jnp.logical_not(is_ctx))
            def _():
                outs[1][...] = y


def _ffn_moe(x1, h2, comb, mod, w_gate, w_up, w_down, n_ctx, seq_s, final_gain=None):
    n = x1.shape[0]
    final = final_gain is not None
    once = pl.Buffered(1)
    tile = lambda w: pl.BlockSpec((MOE_TILE, w), lambda i, e: (i, 0), pipeline_mode=once)
    if final:
        n_ctx_tiles = n_ctx // MOE_TILE
        out_specs = [pl.BlockSpec((MOE_TILE, D_MODEL), lambda i, e: (jnp.minimum(i, n_ctx_tiles - 1), 0),
                                  pipeline_mode=once),
                     pl.BlockSpec((MOE_TILE, D_MODEL), lambda i, e: (jnp.maximum(i - n_ctx_tiles, 0), 0),
                                  pipeline_mode=once)]
        out_shape = [jax.ShapeDtypeStruct((n_ctx, D_MODEL), F32), jax.ShapeDtypeStruct((n - n_ctx, D_MODEL), F32)]
        gain = final_gain.reshape(1, D_MODEL)
    else:
        out_specs = [tile(D_MODEL)]
        out_shape = [jax.ShapeDtypeStruct((n, D_MODEL), F32)]
        gain = jnp.ones((1, D_MODEL), F32)
    out = pl.pallas_call(
        functools.partial(_moe_kernel, n_ctx=n_ctx, seq_s=seq_s, final=final),
        grid=(n // MOE_TILE, N_EXPERTS),
        in_specs=[tile(D_MODEL), tile(D_MODEL), tile(LANES),
                  pl.BlockSpec((8, 6 * D_MODEL), lambda i, e: (0, 0)),
                  pl.BlockSpec((1, D_MODEL, D_FF_E), lambda i, e: (e, 0, 0)),
                  pl.BlockSpec((1, D_MODEL, D_FF_E), lambda i, e: (e, 0, 0)),
                  pl.BlockSpec((1, D_FF_E, D_MODEL), lambda i, e: (e, 0, 0)),
                  pl.BlockSpec((1, D_MODEL), lambda i, e: (0, 0))],
        out_specs=out_specs,
        out_shape=out_shape,
        scratch_shapes=[pltpu.VMEM((MOE_TILE, D_MODEL), F32), pltpu.VMEM((MOE_TILE, LANES), F32),
                        pltpu.VMEM((LANES, MOE_TILE), F32), pltpu.VMEM((LANES, MOE_TILE), F32)],
        compiler_params=_cparams(("arbitrary", "arbitrary") if final else ("parallel", "arbitrary")),
        name="ffn_moe",
    )(x1, h2, comb, mod, w_gate.astype(BF16), w_up.astype(BF16), w_down.astype(BF16), gain)
    return tuple(out) if final else out[0]


def _final_kernel(x_ref, g_ref, o_ref):
    x = x_ref[...]
    o_ref[...] = x * lax.rsqrt(jnp.mean(x * x, axis=-1, keepdims=True) + RMS_EPS) * g_ref[...]


def _final_norm(x, gain, row0, n_rows):
    blk0 = row0 // FFN_TILE
    return pl.pallas_call(
        _final_kernel,
        grid=(n_rows // FFN_TILE,),
        in_specs=[pl.BlockSpec((FFN_TILE, D_MODEL), lambda i: (blk0 + i, 0)),
                  pl.BlockSpec((1, D_MODEL), lambda i: (0, 0))],
        out_specs=pl.BlockSpec((FFN_TILE, D_MODEL), lambda i: (i, 0)),
        out_shape=jax.ShapeDtypeStruct((n_rows, D_MODEL), F32),
        compiler_params=_cparams(("parallel",)),
        name="final_norm",
    )(x, gain.reshape(1, D_MODEL))


def kernel(x_prompt, x_sample, state_rwkv, c, c_ctx, norm1, norm2, w_ada, b_ada, w_in, dw_w, dw_b, conv_ln_g, conv_ln_b, w_conv_out, w_fno_out, shift_mu, g_up, dec_w0, dec_up, iclr_a0, iclr_up, k_k, k_a, r_k, gn_g, gn_b, w_rwkv_out, w_o, ffn_w_gate, ffn_w_up, ffn_w_down, w_router, b_router, moe_w_gate, moe_w_up, moe_w_down, final_norm):
    b_p, t_p, _ = x_prompt.shape
    b_s, t_s, _ = x_sample.shape
    depth = w_in.shape[0]
    n_ctx = b_p * t_p
    n_lat = b_s * t_s
    assert t_p == ROW_TILE and t_s % MOE_TILE == 0 and n_ctx % MOE_TILE == 0 and b_s <= CTX_ROW
    assert MOE_TILE % FFN_TILE == 0
    assert t_s % GRID_W == 0 and CHUNK == GRID_W

    n = n_ctx + n_lat
    x = (x_prompt.reshape(n_ctx, D_MODEL), x_sample.reshape(n_lat, D_MODEL))
    cond = jnp.zeros((8, D_MODEL), F32).at[:b_s].set(c).at[CTX_ROW].set(c_ctx)
    mods = _ada(cond, w_ada, b_ada)
    table = _scan_table(b_p, t_p, b_s, t_s)
    zero_state = jnp.zeros((b_p, 2, HEAD_DIM, C_RWKV), F32)
    pack_state = lambda s: s.transpose(0, 1, 3, 2, 4).reshape(s.shape[0], 2, HEAD_DIM, C_RWKV)
    unpack_state = lambda s: s.reshape(s.shape[0], 2, HEAD_DIM, N_HEADS, HEAD_DIM).transpose(0, 1, 3, 2, 4)

    ctx_states = []
    for l in range(depth):
        mod = mods[l]
        u_conv, u_fno, u_rw = _inproj(x, n, mod, norm1[l], w_in[l, :, :OFF_GATE].astype(BF16), n_ctx, t_s)
        conv_h = _conv_branch(u_conv, dw_w[l], dw_b[l], conv_ln_g[l], conv_ln_b[l], n_ctx)
        fno_h = (_fno_branch(u_fno, 0, b_p, t_p), _fno_branch(u_fno, n_ctx, b_s, t_s))
        r, k, v, g, *summaries = _rwkv_project_and_summarise(u_rw, shift_mu[l], g_up[l], dec_w0[l], dec_up[l],
                                                             iclr_a0[l], iclr_up[l], k_k[l], k_a[l], n_ctx, t_s)
        s0 = jnp.concatenate([zero_state, pack_state(state_rwkv[:, l])], axis=0)
        yf, yb, s_fin = _carry_states(*summaries, s0, table)
        ctx_states.append(unpack_state(s_fin[:b_p]))
        i = l // 2
        x1, h2, *comb = _merge(x, n, conv_h, fno_h, yf, yb, r, k, v, g, norm1[l], w_in[l, :, OFF_GATE:].astype(BF16), mod, w_conv_out[l], w_fno_out[l],
                              w_rwkv_out[l], w_o[l], gn_g[l], gn_b[l], r_k[l], norm2[l], n_ctx, t_s,
                               router=(w_router[i], b_router[i]) if l % 2 == 1 else None)
        if l % 2 == 0:
            x = _ffn_dense(x1, h2, mod, ffn_w_gate[i], ffn_w_up[i], ffn_w_down[i], n_ctx, t_s)
        else:
            x = _ffn_moe(x1, h2, comb[0], mod, moe_w_gate[i], moe_w_up[i], moe_w_down[i], n_ctx, t_s,
                         final_gain=final_norm if l == depth - 1 else None)

    if isinstance(x, tuple):
        y_prompt, y_sample = x
    else:
        y_prompt, y_sample = _final_norm(x, final_norm, 0, n_ctx), _final_norm(x, final_norm, n_ctx, n_lat)
    y_prompt = y_prompt.reshape(b_p, t_p, D_MODEL)
    y_sample = y_sample.reshape(b_s, t_s, D_MODEL)
    new_state = jnp.stack(ctx_states, axis=1).astype(x_prompt.dtype)
    return (y_prompt, y_sample, new_state)
```

```python
import functools

import numpy as np
import jax
import jax.numpy as jnp
from jax import lax
from jax.experimental import pallas as pl
from jax.experimental.pallas import tpu as pltpu

F32 = jnp.float32
BF16 = jnp.bfloat16

D_MODEL = 1024
GRID_W = 64
C_CONV = 256
CONV_W = 31
C_FNO = 256
FNO_GW = 64
N_HEADS = 8
HEAD_DIM = 64
C_RWKV = N_HEADS * HEAD_DIM
G_RANK = 128
W_RANK = 64
A_RANK = 64
D_FF = 2816
N_EXPERTS = 8
D_FF_E = 1408
RMS_EPS = 1e-6
LN_EPS = 1e-5
GN_EPS = 64e-5

OFF_FNO = 2 * C_CONV
OFF_RWKV = OFF_FNO + C_FNO
N_SHIFT = 3 * C_RWKV + G_RANK + 2 * (W_RANK + A_RANK)
OFF_GATE = OFF_RWKV + N_SHIFT
D_IN = OFF_GATE + 3 * D_MODEL

ROW_TILE = 256
FFN_TILE = 512
WIDE_TILE = 512
CHUNK = 64
GROUP = 4
GROUP_W = GROUP * HEAD_DIM
SUMMARY_CHUNKS = 4
CARRY_CHUNKS = 4
MOE_TILE = 512
MOE_ROWS = 160
CTX_ROW = 4
LANES = 128
VMEM_LIMIT = 56 * 1024 * 1024


def _cparams(sem):
    return pltpu.CompilerParams(dimension_semantics=sem, vmem_limit_bytes=VMEM_LIMIT)


def _sigmoid(x):
    return 1.0 / (1.0 + jnp.exp(-x))


def _silu(x):
    return x * _sigmoid(x)


def _dot(a, b):
    return jnp.dot(a, b, preferred_element_type=F32)


def _split(x):
    hi = x.astype(BF16)
    lo = (x - hi.astype(F32)).astype(BF16)
    return hi, lo


def _dot_x3(a, b):
    ah, al = _split(a)
    bh, bl = _split(b)
    return _dot(ah, bh) + (_dot(ah, bl) + _dot(al, bh))


def _seg_sum(x, bd):
    return _dot(x.astype(BF16), bd)


def _mod_row(i, tile, n_ctx_rows, seq_s):
    n_ctx_tiles = n_ctx_rows // tile
    return jnp.where(i < n_ctx_tiles, CTX_ROW, (i - n_ctx_tiles) // (seq_s // tile))


def _stream_specs(tile, width, n_ctx, lat_block0):
    n_ctx_tiles = n_ctx // tile
    return [pl.BlockSpec((tile, width), lambda i: (jnp.minimum(i, n_ctx_tiles - 1), 0)),
            pl.BlockSpec((tile, width), lambda i: (jnp.maximum(i - n_ctx_tiles, 0) + lat_block0, 0))]


def _stream_args(x, tile, n_ctx):
    if isinstance(x, tuple):
        return x[0], x[1], 0
    return x, x, n_ctx // tile


def _pick_stream(i, tile, n_ctx, ctx_ref, lat_ref):
    return jnp.where(i < n_ctx // tile, ctx_ref[...], lat_ref[...])


def _ada_kernel(c_ref, w_ref, b_ref, o_ref):
    s = _silu(c_ref[...])
    o_ref[0] = _dot(s.astype(BF16), w_ref[0].astype(BF16)) + b_ref[0]


def _ada(cond, w_ada, b_ada):
    n_l = w_ada.shape[0]
    tn = 1536
    return pl.pallas_call(
        _ada_kernel,
        grid=(n_l, 6 * D_MODEL // tn),
        in_specs=[
            pl.BlockSpec((8, D_MODEL), lambda l, j: (0, 0)),
            pl.BlockSpec((1, D_MODEL, tn), lambda l, j: (l, 0, j)),
            pl.BlockSpec((1, 1, tn), lambda l, j: (l, 0, j)),
        ],
        out_specs=pl.BlockSpec((1, 8, tn), lambda l, j: (l, 0, j)),
        out_shape=jax.ShapeDtypeStruct((n_l, 8, 6 * D_MODEL), F32),
        compiler_params=_cparams(("parallel", "parallel")),
        name="ada",
    )(cond, w_ada, b_ada.reshape(n_l, 1, 6 * D_MODEL))


def _inproj_kernel(xc_ref, xl_ref, mod_ref, g_ref, w_ref, oc_ref, of_ref, or_ref, *, n_ctx, seq_s):
    row = _mod_row(pl.program_id(0), WIDE_TILE, n_ctx, seq_s)
    m = mod_ref[pl.ds(row, 1), :]
    sh = m[:, 0:D_MODEL]
    sc = m[:, D_MODEL:2 * D_MODEL]
    x = _pick_stream(pl.program_id(0), WIDE_TILE, n_ctx, xc_ref, xl_ref)
    y = x * lax.rsqrt(jnp.mean(x * x, axis=-1, keepdims=True) + RMS_EPS) * g_ref[...]
    h = (y * (1.0 + sc) + sh).astype(BF16)
    oc_ref[...] = _dot(h, w_ref[:, 0:OFF_FNO])
    of_ref[...] = _dot(h, w_ref[:, OFF_FNO:OFF_RWKV])
    or_ref[...] = _dot(h, w_ref[:, OFF_RWKV:OFF_GATE])


def _inproj(x, n, mod, gain, w_in, n_ctx, seq_s):
    xc, xl, lat0 = _stream_args(x, WIDE_TILE, n_ctx)
    widths = (OFF_FNO, C_FNO, N_SHIFT)
    return pl.pallas_call(
        functools.partial(_inproj_kernel, n_ctx=n_ctx, seq_s=seq_s),
        grid=(n // WIDE_TILE,),
        in_specs=_stream_specs(WIDE_TILE, D_MODEL, n_ctx, lat0) + [
            pl.BlockSpec((8, 6 * D_MODEL), lambda i: (0, 0)),
            pl.BlockSpec((1, D_MODEL), lambda i: (0, 0)),
            pl.BlockSpec((D_MODEL, OFF_GATE), lambda i: (0, 0), pipeline_mode=pl.Buffered(1)),
        ],
        out_specs=[pl.BlockSpec((WIDE_TILE, w), lambda i: (i, 0)) for w in widths],
        out_shape=[jax.ShapeDtypeStruct((n, w), F32) for w in widths],
        compiler_params=_cparams(("parallel",)),
        name="inproj",
    )(xc, xl, mod, gain.reshape(1, D_MODEL), w_in)


_CONV_HALO = 16
_CONV_ROWS = 64


def _conv_kernel(u_ref, w_ref, b_ref, g_ref, be_ref, o_ref, pad_ref, sh_ref, *, n_ctx):
    is_ctx = pl.program_id(0) < n_ctx // ROW_TILE
    zeros = jnp.zeros((_CONV_HALO, C_CONV), F32)
    n_parts = ROW_TILE // _CONV_ROWS

    def glu(lo, hi):
        return u_ref[lo:hi, 0:C_CONV] * _sigmoid(u_ref[lo:hi, C_CONV:2 * C_CONV])

    def finish(starts, n_rows):
        for s in range(8):
            sh_ref[s, 0:n_rows - 8, :] = pad_ref[s:s + n_rows - 8, :]
        for p in range(n_parts):
            acc = jnp.zeros((_CONV_ROWS, C_CONV), F32)
            for j in range(CONV_W):
                o = starts[p] + j
                acc = acc + w_ref[j:j + 1, :] * sh_ref[o % 8, o - o % 8:o - o % 8 + _CONV_ROWS, :]
            acc = acc + b_ref[...]
            mu = jnp.mean(acc, axis=-1, keepdims=True)
            d = acc - mu
            var = jnp.mean(d * d, axis=-1, keepdims=True)
            y = d * lax.rsqrt(var + LN_EPS) * g_ref[...] + be_ref[...]
            o_ref[p * _CONV_ROWS:(p + 1) * _CONV_ROWS, :] = _silu(y).astype(o_ref.dtype)

    shift = _CONV_HALO - CONV_W // 2

    @pl.when(is_ctx)
    def _():
        pad_ref[0:_CONV_HALO, :] = zeros
        pad_ref[_CONV_HALO:_CONV_HALO + ROW_TILE, :] = glu(0, ROW_TILE)
        pad_ref[_CONV_HALO + ROW_TILE:2 * _CONV_HALO + ROW_TILE, :] = zeros
        finish([shift + p * _CONV_ROWS for p in range(n_parts)], ROW_TILE + 2 * _CONV_HALO)

    @pl.when(jnp.logical_not(is_ctx))
    def _():
        stride = GRID_W + 2 * _CONV_HALO
        for p in range(n_parts):
            pad_ref[p * stride:p * stride + _CONV_HALO, :] = zeros
            pad_ref[p * stride + _CONV_HALO:p * stride + _CONV_HALO + GRID_W, :] = glu(p * GRID_W, (p + 1) * GRID_W)
            pad_ref[p * stride + _CONV_HALO + GRID_W:(p + 1) * stride, :] = zeros
        finish([p * stride + shift for p in range(n_parts)], n_parts * stride)


def _conv_branch(u_conv, dw_w, dw_b, ln_g, ln_b, n_ctx):
    n = u_conv.shape[0]
    assert _CONV_ROWS == GRID_W and ROW_TILE % GRID_W == 0
    vec = pl.BlockSpec((1, C_CONV), lambda i: (0, 0))
    return pl.pallas_call(
        functools.partial(_conv_kernel, n_ctx=n_ctx),
        grid=(n // ROW_TILE,),
        in_specs=[
            pl.BlockSpec((ROW_TILE, 2 * C_CONV), lambda i: (i, 0)),
            pl.BlockSpec((CONV_W, C_CONV), lambda i: (0, 0)),
            vec, vec, vec,
        ],
        out_specs=pl.BlockSpec((ROW_TILE, C_CONV), lambda i: (i, 0)),
        out_shape=jax.ShapeDtypeStruct((n, C_CONV), BF16),
        scratch_shapes=[pltpu.VMEM((ROW_TILE // GRID_W * (GRID_W + 2 * _CONV_HALO), C_CONV), F32),
                        pltpu.VMEM((8, ROW_TILE // GRID_W * (GRID_W + 2 * _CONV_HALO), C_CONV), F32)],
        compiler_params=_cparams(("parallel",)),
        name="conv_branch",
    )(u_conv, dw_w, dw_b.reshape(1, C_CONV), ln_g.reshape(1, C_CONV), ln_b.reshape(1, C_CONV))


def _dft_tables(t_len):
    def cs(n):
        k = np.arange(n, dtype=np.int64)
        ang = 2.0 * np.pi * ((k[:, None] * k[None, :]) % n).astype(np.float64) / n
        return np.cos(ang), np.sin(ang)
    cg, sg = cs(FNO_GW)
    eye = np.eye(C_FNO // FNO_GW)
    w1 = np.concatenate([np.kron(eye, cg), np.kron(eye, sg)], axis=1)
    ct, st = cs(t_len)
    w2 = np.concatenate([ct, -st], axis=1)
    return jnp.asarray(w1, dtype=F32).astype(BF16), jnp.asarray(w2, dtype=F32).astype(BF16)


def _fno_kernel(u_ref, w1_ref, w2_ref, o_ref, hs_ref, *, t_len, scale):
    @pl.when(pl.program_id(1) == 0)
    def _():
        hc = _dot(u_ref[...].astype(BF16), w1_ref[...])
        hs_ref[0:t_len, :] = hc[:, 0:C_FNO].astype(BF16)
        hs_ref[t_len:2 * t_len, :] = hc[:, C_FNO:2 * C_FNO].astype(BF16)

    o_ref[...] = (_dot(w2_ref[...], hs_ref[...]) * scale).astype(o_ref.dtype)


def _fno_branch(u_fno, row0, n_seq, t_len):
    w1, w2 = _dft_tables(t_len)
    tk = min(t_len, 512)
    blk0 = row0 // t_len
    return pl.pallas_call(
        functools.partial(_fno_kernel, t_len=t_len, scale=float(1.0 / np.sqrt(t_len * FNO_GW))),
        grid=(n_seq, t_len // tk),
        in_specs=[
            pl.BlockSpec((t_len, C_FNO), lambda b, j: (blk0 + b, 0)),
            pl.BlockSpec((C_FNO, 2 * C_FNO), lambda b, j: (0, 0)),
            pl.BlockSpec((tk, 2 * t_len), lambda b, j: (j, 0)),
        ],
        out_specs=pl.BlockSpec((tk, C_FNO), lambda b, j: (b * (t_len // tk) + j, 0)),
        out_shape=jax.ShapeDtypeStruct((n_seq * t_len, C_FNO), BF16),
        scratch_shapes=[pltpu.VMEM((2 * t_len, C_FNO), BF16)],
        compiler_params=_cparams(("parallel", "arbitrary")),
        name="fno_branch",
    )(u_fno, w1, w2)


def _prep_pieces(i, z_ref, zp_ref, zn_ref, mu_ref, gup_ref, w0_ref, dup_ref, a0_ref, aup_ref, kk_ref, bd_ref,
                 r_refs, k_refs, v_refs, kkn_ref, g_ref, lw_ref, a_ref, pad_ref, n_ctx, seq_s):
    n_ctx_tiles = n_ctx // ROW_TILE
    per_seq = seq_s // ROW_TILE
    j = (i - n_ctx_tiles) % per_seq
    first = jnp.logical_or(i < n_ctx_tiles, j == 0)
    last = jnp.logical_or(i < n_ctx_tiles, j == per_seq - 1)
    c = C_RWKV

    def shifted(lo, hi):
        z = pad_ref[8:8 + ROW_TILE, lo:hi]
        zp = pad_ref[7:7 + ROW_TILE, lo:hi]
        zn = pad_ref[9:9 + ROW_TILE, lo:hi]
        return z + mu_ref[:, lo:hi] * (0.5 * (zp + zn) - z)

    def fill():
        pad_ref[8:8 + ROW_TILE, :] = z_ref[...]
        pad_ref[0:8, :] = jnp.where(first, 0.0, zp_ref[...])
        pad_ref[8 + ROW_TILE:16 + ROW_TILE, :] = jnp.where(last, 0.0, zn_ref[...])

    def receptance():
        r = shifted(0, c)
        for ref in r_refs:
            ref[...] = r

    def key():
        k = shifted(c, 2 * c)
        for ref in k_refs:
            ref[...] = k
        kx = k * kk_ref[...]
        nrm = jnp.sqrt(_seg_sum(kx * kx, bd_ref[...]))
        kkn_ref[...] = kx / jnp.maximum(nrm, 1e-12)

    def value():
        v = shifted(2 * c, 3 * c)
        for ref in v_refs:
            ref[...] = v

    def gate():
        o = 3 * c
        g_ref[...] = _dot(_sigmoid(shifted(o, o + G_RANK)).astype(BF16), gup_ref[...])

    def decay():
        o = 3 * c + G_RANK
        xw = _dot(jnp.tanh(shifted(o, o + 2 * W_RANK)).astype(BF16), dup_ref[...]) + w0_ref[...]
        soft = jnp.maximum(-xw, 0.0) + jnp.log(1.0 + jnp.exp(-jnp.abs(xw)))
        lw_ref[...] = -jnp.exp(-soft - 0.5)

    def iclr():
        o = 3 * c + G_RANK + 2 * W_RANK
        xa = _dot(shifted(o, o + 2 * A_RANK).astype(BF16), aup_ref[...]) + a0_ref[...]
        a_ref[...] = _sigmoid(xa)

    return [fill, receptance, key, value, gate, decay, iclr]


def _block_diag2(w):
    z = jnp.zeros_like(w[0])
    return jnp.concatenate([jnp.concatenate([w[0], z], axis=1), jnp.concatenate([z, w[1]], axis=1)], axis=0)


def _head_ones():
    return jnp.asarray(np.kron(np.eye(N_HEADS), np.ones((HEAD_DIM, HEAD_DIM))), dtype=BF16)


def _block_diag(x, mask):
    xb = x.astype(BF16)
    return jnp.where(mask, jnp.concatenate([xb] * GROUP, axis=0), jnp.zeros((), BF16))


def _dot_nt(a, b):
    return lax.dot_general(a, b, (((1,), (1,)), ((), ())), preferred_element_type=F32)


def _diag_blocks(prod, lane_head):
    out = jnp.where(lane_head == 0, prod[0:HEAD_DIM], 0.0)
    for h in range(1, GROUP):
        out = out + jnp.where(lane_head == h, prod[h * HEAD_DIM:(h + 1) * HEAD_DIM], 0.0)
    return out


def _summary_body(r_ref, k_ref, v_ref, n_ref, lw_ref, a_ref, ka_ref, q_ref, y0_ref, p_ref, z_ref, g_ref, fillers=()):
    fillers = list(fillers)
    fill_one = lambda: fillers.pop(0)() if fillers else None
    row = lax.broadcasted_iota(jnp.int32, (CHUNK, GROUP_W), 0)
    lane = lax.broadcasted_iota(jnp.int32, (CHUNK, GROUP_W), 1)
    col = lane % CHUNK
    lane_head = lane // HEAD_DIM
    bd_mask = (lax.broadcasted_iota(jnp.int32, (GROUP * CHUNK, GROUP_W), 0) // CHUNK
               == lax.broadcasted_iota(jnp.int32, (GROUP * CHUNK, GROUP_W), 1) // HEAD_DIM)
    bd = lambda x: _block_diag(x, bd_mask)
    ka = ka_ref[...]

    units = []
    for j in range(SUMMARY_CHUNKS):
        rows = slice(j * CHUNK, (j + 1) * CHUNK)
        r = r_ref[rows, :]
        k = k_ref[rows, :]
        v = v_ref[rows, :]
        kkn = n_ref[rows, :]
        for d in range(2):
            lanes = slice(d * C_RWKV, (d + 1) * C_RWKV)
            earlier = (col < row) if d == 0 else (col > row)
            upto = jnp.logical_or(earlier, row == col)
            tri = upto[:, 0:CHUNK].astype(BF16)
            lw = lw_ref[rows, lanes]
            h1 = lw.astype(BF16)
            r1 = lw - h1.astype(F32)
            h2 = r1.astype(BF16)
            h3 = (r1 - h2.astype(F32)).astype(BF16)
            cum = _dot(tri, h1) + (_dot(tri, h2) + _dot(tri, h3))
            e_in = jnp.exp(cum)
            e_ex = jnp.exp(cum - lw)
            e_ng = jnp.exp(-cum)
            a = a_ref[rows, lanes]
            kd = k * (1.0 + (a - 1.0) * ka)
            at = -kkn * e_ex
            rt = r * e_in
            bt = kkn * a * e_ng
            kt = kd * e_ng
            end = CHUNK - 1 if d == 0 else 0
            g_end = e_in[end:end + 1, :]
            g_ref[j, :, lanes] = g_end
            bh = bt * g_end
            kh = kt * g_end
            for q in range(C_RWKV // GROUP_W):
                sl = slice(q * GROUP_W, (q + 1) * GROUP_W)
                units.append(dict(rows=rows, out=slice(d * C_RWKV + q * GROUP_W, d * C_RWKV + (q + 1) * GROUP_W),
                                  earlier=earlier, upto=upto, at=at[:, sl], rt=rt[:, sl],
                                  ar=jnp.concatenate([at[:, sl], rt[:, sl]], axis=0).astype(BF16),
                                  bt=bt[:, sl], kt=kt[:, sl], v=v[:, sl], bh=bh[:, sl].astype(BF16),
                                  bk=jnp.concatenate([bh[:, sl], kh[:, sl]], axis=0).astype(BF16)))

    for u in units:
        sb = _dot_nt(u["ar"], bd(u["bt"]))
        sk = _dot_nt(u["ar"], bd(u["kt"]))
        u["lab"] = jnp.where(u["earlier"], sb[0:CHUNK], 0.0)
        u["mrb"] = jnp.where(u["upto"], sb[CHUNK:2 * CHUNK], 0.0).astype(BF16)
        u["lm"] = jnp.concatenate([jnp.where(u["earlier"], sk[0:CHUNK], 0.0),
                                   jnp.where(u["upto"], sk[CHUNK:2 * CHUNK], 0.0)], axis=0).astype(BF16)

    fill_one()
    eye = (row == col).astype(F32)
    pair = jnp.logical_and(row // 2 == col // 2, row != col)
    for u in units:
        u["t"] = eye + jnp.where(pair, u["lab"], 0.0)
    n = 2
    while n < CHUNK:
        m = jnp.logical_and(row // (2 * n) == col // (2 * n), row // n != col // n)
        for u in units:
            u["w"] = _dot(jnp.where(m, u["lab"], 0.0).astype(BF16), bd(u["t"]))
        for u in units:
            u["t"] = u["t"] + _dot(u["t"].astype(BF16), bd(u["w"]))
        fill_one()
        n *= 2

    for u in units:
        u["t"] = u["t"].astype(BF16)
        u["wm"] = _dot(u["t"], bd(u["at"]))
        u["lv"] = _dot(u["lm"], bd(u["v"]))
    for u in units:
        u["u0"] = _dot(u["t"], bd(u["lv"][0:CHUNK]))
    for u in units:
        q_ref[u["rows"], u["out"]] = (u["rt"] + _dot(u["mrb"], bd(u["wm"]))).astype(q_ref.dtype)
        y0_ref[u["rows"], u["out"]] = u["lv"][CHUNK:2 * CHUNK] + _dot(u["mrb"], bd(u["u0"]))
    for u in units:
        p = _dot(u["wm"].T.astype(BF16), u["bh"])
        p_ref[u["rows"], u["out"]] = _diag_blocks(p, lane_head).astype(p_ref.dtype)
        z = _dot(jnp.concatenate([u["u0"], u["v"]], axis=0).T.astype(BF16), u["bk"])
        z_ref[u["rows"], u["out"]] = _diag_blocks(z, lane_head)
    while fillers:
        fill_one()


def _rwkv_kernel(z_ref, zp_ref, zn_ref, mu_ref, gup_ref, w0_ref, dup_ref, a0_ref, aup_ref, kk_ref, bd_ref, ka_ref,
                 r_ref, k_ref, v_ref, g_ref, q_ref, y0_ref, p_ref, zz_ref, gd_ref,
                 pad_ref, r_n, k_n, v_n, n_n, lw_n, a_n, r_c, k_c, v_c, n_c, lw_c, a_c, *, n_ctx, seq_s, n_tiles):
    s = pl.program_id(0)
    new = (r_n, k_n, v_n, n_n, lw_n, a_n)
    cur = (r_c, k_c, v_c, n_c, lw_c, a_c)

    @pl.when(s == 0)
    def _():
        for ref in cur:
            ref[...] = jnp.zeros_like(ref)

    pieces = _prep_pieces(jnp.minimum(s, n_tiles - 1), z_ref, zp_ref, zn_ref, mu_ref, gup_ref, w0_ref, dup_ref,
                          a0_ref, aup_ref, kk_ref, bd_ref, (r_n, r_ref), (k_n, k_ref), (v_n, v_ref), n_n, g_ref,
                          lw_n, a_n, pad_ref, n_ctx, seq_s)
    pieces[0]()
    _summary_body(r_c, k_c, v_c, n_c, lw_c, a_c, ka_ref, q_ref, y0_ref, p_ref, zz_ref, gd_ref, fillers=pieces[1:])
    for dst, src in zip(cur, new):
        dst[...] = src[...]


def _rwkv_project_and_summarise(u_rw, mu, g_up, dec_w0, dec_up, iclr_a0, iclr_up, k_k, k_a, n_ctx, seq_s):
    n = u_rw.shape[0]
    assert ROW_TILE == SUMMARY_CHUNKS * CHUNK and CHUNK == HEAD_DIM and C_RWKV % GROUP_W == 0
    n_tiles = n // ROW_TILE
    halo = ROW_TILE // 8
    last_blk = n // 8 - 1
    c = C_RWKV
    full = lambda shape: pl.BlockSpec(shape, lambda s: (0,) * len(shape))
    this = lambda s: jnp.minimum(s, n_tiles - 1)
    prev = lambda s: jnp.maximum(s - 1, 0)
    proj = lambda w: pl.BlockSpec((ROW_TILE, w), lambda s: (this(s), 0))
    summ = lambda w: pl.BlockSpec((ROW_TILE, w), lambda s: (prev(s), 0))
    vm = lambda w: pltpu.VMEM((ROW_TILE, w), F32)
    return pl.pallas_call(
        functools.partial(_rwkv_kernel, n_ctx=n_ctx, seq_s=seq_s, n_tiles=n_tiles),
        grid=(n_tiles + 1,),
        in_specs=[
            proj(N_SHIFT),
            pl.BlockSpec((8, N_SHIFT), lambda s: (jnp.maximum(this(s) * halo - 1, 0), 0)),
            pl.BlockSpec((8, N_SHIFT), lambda s: (jnp.minimum((this(s) + 1) * halo, last_blk), 0)),
            full((1, N_SHIFT)), full((G_RANK, c)), full((1, 2 * c)), full((2 * W_RANK, 2 * c)),
            full((1, 2 * c)), full((2 * A_RANK, 2 * c)), full((1, c)), full((c, c)), full((1, c)),
        ],
        out_specs=[proj(c), proj(c), proj(c), proj(c), summ(2 * c), summ(2 * c), summ(2 * c), summ(2 * c),
                   pl.BlockSpec((SUMMARY_CHUNKS, 1, 2 * c), lambda s: (prev(s), 0, 0))],
        out_shape=[jax.ShapeDtypeStruct((n, c), F32)] * 4
        + [jax.ShapeDtypeStruct((n, 2 * c), BF16), jax.ShapeDtypeStruct((n, 2 * c), F32),
           jax.ShapeDtypeStruct((n, 2 * c), BF16), jax.ShapeDtypeStruct((n, 2 * c), F32),
           jax.ShapeDtypeStruct((n // CHUNK, 1, 2 * c), F32)],
        scratch_shapes=[pltpu.VMEM((ROW_TILE + 16, N_SHIFT), F32)]
        + [vm(c), vm(c), vm(c), vm(c), vm(2 * c), vm(2 * c)] * 2,
        compiler_params=_cparams(("arbitrary",)),
        name="rwkv_project_summarise",
    )(u_rw, u_rw, u_rw, mu.reshape(1, N_SHIFT), g_up.astype(BF16), dec_w0.reshape(1, 2 * c),
      _block_diag2(dec_up).astype(BF16), iclr_a0.reshape(1, 2 * c), _block_diag2(iclr_up).astype(BF16),
      k_k.reshape(1, c), _head_ones(), k_a.reshape(1, c))


def _carry_kernel(tab_ref, qf_ref, y0f_ref, pf_ref, zf_ref, gf_ref, qb_ref, y0b_ref, pb_ref, zb_ref, gb_ref,
                  s0_ref, yf_ref, yb_ref, sfin_ref, st_ref):
    step = pl.program_id(0)

    @pl.when(tab_ref[3, step] == 1)
    def _():
        st_ref[...] = s0_ref[0]

    bd_mask = (lax.broadcasted_iota(jnp.int32, (GROUP * CHUNK, GROUP_W), 0) // CHUNK
               == lax.broadcasted_iota(jnp.int32, (GROUP * CHUNK, GROUP_W), 1) // HEAD_DIM)
    for i in range(CARRY_CHUNKS):
        for d, (q_ref, y0_ref, p_ref, z_ref, g_ref, y_ref) in enumerate(
                ((qf_ref, y0f_ref, pf_ref, zf_ref, gf_ref, yf_ref), (qb_ref, y0b_ref, pb_ref, zb_ref, gb_ref, yb_ref))):
            j = i if d == 0 else CARRY_CHUNKS - 1 - i
            rows = slice(j * CHUNK, (j + 1) * CHUNK)
            for q in range(C_RWKV // GROUP_W):
                sl = slice(q * GROUP_W, (q + 1) * GROUP_W)
                s = st_ref[d, :, sl]
                y_ref[rows, sl] = y0_ref[rows, sl] + _dot_nt(q_ref[rows, sl], _block_diag(s, bd_mask))
                st_ref[d, :, sl] = (s * g_ref[j, :, sl] + z_ref[rows, sl]
                                    + _dot(s.astype(BF16), _block_diag(p_ref[rows, sl], bd_mask)))

    @pl.when(tab_ref[4, step] == 1)
    def _():
        sfin_ref[0] = st_ref[...]


def _scan_table(n_ctx_seq, t_ctx, n_s_seq, t_s):
    rows = []
    base = 0
    seq = 0
    for n_seq, t_len in ((n_ctx_seq, t_ctx), (n_s_seq, t_s)):
        assert t_len % (CARRY_CHUNKS * CHUNK) == 0
        n_c = t_len // (CARRY_CHUNKS * CHUNK)
        for _ in range(n_seq):
            for c in range(n_c):
                rows.append((base + c, base + n_c - 1 - c, seq, int(c == 0), int(c == n_c - 1)))
            base += n_c
            seq += 1
    return np.asarray(rows, dtype=np.int32).T.copy()


def _carry_states(q, y0, p, z, g, s0, table):
    n = q.shape[0]
    n_seq = s0.shape[0]
    c = C_RWKV

    def tok(which, d):
        return pl.BlockSpec((CARRY_CHUNKS * CHUNK, c), lambda s, tab: (tab[which, s], d))

    def decay(which, d):
        return pl.BlockSpec((CARRY_CHUNKS, 1, c), lambda s, tab: (tab[which, s], 0, d))

    state_spec = pl.BlockSpec((1, 2, HEAD_DIM, c), lambda s, tab: (tab[2, s], 0, 0, 0))
    grid_spec = pltpu.PrefetchScalarGridSpec(
        num_scalar_prefetch=1,
        grid=(table.shape[1],),
        in_specs=[tok(0, 0), tok(0, 0), tok(0, 0), tok(0, 0), decay(0, 0),
                  tok(1, 1), tok(1, 1), tok(1, 1), tok(1, 1), decay(1, 1), state_spec],
        out_specs=[tok(0, 0), tok(1, 0), state_spec],
        scratch_shapes=[pltpu.VMEM((2, HEAD_DIM, c), F32)],
    )
    return pl.pallas_call(
        _carry_kernel,
        grid_spec=grid_spec,
        out_shape=[jax.ShapeDtypeStruct((n, c), F32), jax.ShapeDtypeStruct((n, c), F32),
                   jax.ShapeDtypeStruct((n_seq, 2, HEAD_DIM, c), F32)],
        compiler_params=_cparams(("arbitrary",)),
        name="rwkv_carry",
    )(jnp.asarray(table), q, y0, p, z, g, q, y0, p, z, g, s0)


def _merge_kernel(xc_ref, xl_ref, c_ref, fc_ref, fl_ref, yf_ref, yb_ref, r_ref, k_ref, v_ref, g_ref, n1_ref, wgt_ref, mod_ref,
                  wc_ref, wf_ref, wr_ref, wo_ref, gng_ref, gnb_ref, rk_ref, bd_ref, n2_ref, *rest, n_ctx, seq_s, route):
    if route:
        wrt_ref, brt_ref, x1_ref, h2_ref, comb_ref = rest
    else:
        x1_ref, h2_ref = rest
    row = _mod_row(pl.program_id(0), WIDE_TILE, n_ctx, seq_s)
    m = mod_ref[pl.ds(row, 1), :]
    sh1 = m[:, 0:D_MODEL]
    sc1 = m[:, D_MODEL:2 * D_MODEL]
    g1 = m[:, 2 * D_MODEL:3 * D_MODEL]
    sh2 = m[:, 3 * D_MODEL:4 * D_MODEL]
    sc2 = m[:, 4 * D_MODEL:5 * D_MODEL]
    bd = bd_ref[...]
    inv = 1.0 / HEAD_DIM
    y = yf_ref[...] + yb_ref[...]
    y_hi, y_lo = _split(y)
    d = y - (_dot(y_hi, bd) + _dot(y_lo, bd)) * inv
    var = _seg_sum(d * d, bd) * inv
    yn = d * lax.rsqrt(var + GN_EPS) * gng_ref[...] + gnb_ref[...]
    v = v_ref[...]
    yn = yn + _seg_sum(r_ref[...] * k_ref[...] * rk_ref[...], bd) * v
    yr = _dot((yn * g_ref[...]).astype(BF16), wr_ref[...])
    yc = _dot(c_ref[...], wc_ref[...])
    yf = _dot(_pick_stream(pl.program_id(0), WIDE_TILE, n_ctx, fc_ref, fl_ref), wf_ref[...])
    x = _pick_stream(pl.program_id(0), WIDE_TILE, n_ctx, xc_ref, xl_ref)
    hn = x * lax.rsqrt(jnp.mean(x * x, axis=-1, keepdims=True) + RMS_EPS) * n1_ref[...]
    hn = (hn * (1.0 + sc1) + sh1).astype(BF16)
    dm = D_MODEL
    merged = (_sigmoid(_dot(hn, wgt_ref[:, 0:dm])) * yc + _sigmoid(_dot(hn, wgt_ref[:, dm:2 * dm])) * yf
              + _sigmoid(_dot(hn, wgt_ref[:, 2 * dm:3 * dm])) * yr)
    x1 = x + g1 * _dot(merged.astype(BF16), wo_ref[...])
    x1_ref[...] = x1
    h2 = x1 * lax.rsqrt(jnp.mean(x1 * x1, axis=-1, keepdims=True) + RMS_EPS) * n2_ref[...]
    h2 = h2 * (1.0 + sc2) + sh2
    h2_ref[...] = h2.astype(BF16)
    if not route:
        return
    logits = _dot_x3(h2, wrt_ref[...]) + brt_ref[...]
    lane = lax.broadcasted_iota(jnp.int32, logits.shape, 1).astype(F32)
    neg = jnp.float32(-jnp.inf)
    lg = jnp.where(lane < N_EXPERTS, logits, neg)
    m1 = jnp.max(lg, axis=-1, keepdims=True)
    i1 = jnp.min(jnp.where(lg == m1, lane, float(LANES)), axis=-1, keepdims=True)
    lg2 = jnp.where(lane == i1, neg, lg)
    m2 = jnp.max(lg2, axis=-1, keepdims=True)
    i2 = jnp.min(jnp.where(lg2 == m2, lane, float(LANES)), axis=-1, keepdims=True)
    e = jnp.exp(m2 - m1)
    p1 = 1.0 / (1.0 + e)
    comb_ref[...] = jnp.where(lane == i1, p1, 0.0) + jnp.where(lane == i2, e * p1, 0.0)


def _merge(x, n, conv_h, fno_h, yf, yb, r, k, v, g, gain1, w_gates, mod, w_conv_out, w_fno_out, w_rwkv_out, w_o,
           gn_g, gn_b, r_k, gain2, n_ctx, seq_s, router=None):
    xc, xl, x_lat0 = _stream_args(x, WIDE_TILE, n_ctx)
    fc, fl, f_lat0 = _stream_args(fno_h, WIDE_TILE, n_ctx)
    c = C_RWKV
    tile = lambda w: pl.BlockSpec((WIDE_TILE, w), lambda i: (i, 0))
    full = lambda shape: pl.BlockSpec(shape, lambda i: (0,) * len(shape), pipeline_mode=pl.Buffered(1))
    route = router is not None
    route_args, route_specs, route_out, route_shape = [], [], [], []
    if route:
        route_args = [jnp.zeros((D_MODEL, LANES), F32).at[:, :N_EXPERTS].set(router[0]),
                      jnp.zeros((1, LANES), F32).at[0, :N_EXPERTS].set(router[1])]
        route_specs = [full((D_MODEL, LANES)), full((1, LANES))]
        route_out = [tile(LANES)]
        route_shape = [jax.ShapeDtypeStruct((n, LANES), F32)]
    return pl.pallas_call(
        functools.partial(_merge_kernel, n_ctx=n_ctx, seq_s=seq_s, route=route),
        grid=(n // WIDE_TILE,),
        in_specs=_stream_specs(WIDE_TILE, D_MODEL, n_ctx, x_lat0) + [tile(C_CONV)]
        + _stream_specs(WIDE_TILE, C_FNO, n_ctx, f_lat0) + [
            tile(c), tile(c), tile(c), tile(c), tile(c), tile(c),
            full((1, D_MODEL)), full((D_MODEL, 3 * D_MODEL)), full((8, 6 * D_MODEL)),
            full((C_CONV, D_MODEL)), full((C_FNO, D_MODEL)), full((c, D_MODEL)), full((D_MODEL, D_MODEL)),
            full((1, c)), full((1, c)), full((1, c)), full((c, c)), full((1, D_MODEL)),
        ] + route_specs,
        out_specs=[tile(D_MODEL), tile(D_MODEL)] + route_out,
        out_shape=[jax.ShapeDtypeStruct((n, D_MODEL), F32), jax.ShapeDtypeStruct((n, D_MODEL), BF16)] + route_shape,
        compiler_params=_cparams(("parallel",)),
        name="merge",
    )(xc, xl, conv_h, fc, fl, yf, yb, r, k, v, g, gain1.reshape(1, D_MODEL), w_gates, mod,
      w_conv_out.astype(BF16), w_fno_out.astype(BF16), w_rwkv_out.astype(BF16), w_o.astype(BF16),
      gn_g.reshape(1, c), gn_b.reshape(1, c), r_k.reshape(1, c), _head_ones(), gain2.reshape(1, D_MODEL), *route_args)


def _ffn_kernel(x_ref, h_ref, mod_ref, wg_ref, wu_ref, wd_ref, o_ref, *, n_ctx, seq_s):
    row = _mod_row(pl.program_id(0), FFN_TILE, n_ctx, seq_s)
    g2 = mod_ref[pl.ds(row, 1), 5 * D_MODEL:6 * D_MODEL]
    h = h_ref[...]
    half = D_FF // 2
    acc = jnp.zeros((FFN_TILE, D_MODEL), F32)
    for p in range(2):
        sl = slice(p * half, (p + 1) * half)
        t = _silu(_dot(h, wg_ref[:, sl])) * _dot(h, wu_ref[:, sl])
        acc = acc + _dot(t.astype(BF16), wd_ref[sl, :])
    o_ref[...] = x_ref[...] + g2 * acc


def _ffn_dense(x1, h2, mod, w_gate, w_up, w_down, n_ctx, seq_s):
    n = x1.shape[0]
    tile = pl.BlockSpec((FFN_TILE, D_MODEL), lambda i: (i, 0))
    full = lambda shape: pl.BlockSpec(shape, lambda i: (0,) * len(shape))
    return pl.pallas_call(
        functools.partial(_ffn_kernel, n_ctx=n_ctx, seq_s=seq_s),
        grid=(n // FFN_TILE,),
        in_specs=[tile, tile, full((8, 6 * D_MODEL)), full((D_MODEL, D_FF)), full((D_MODEL, D_FF)),
                  full((D_FF, D_MODEL))],
        out_specs=tile,
        out_shape=jax.ShapeDtypeStruct((n, D_MODEL), F32),
        compiler_params=_cparams(("parallel",)),
        name="ffn_dense",
    )(x1, h2, mod, w_gate.astype(BF16), w_up.astype(BF16), w_down.astype(BF16))


def _moe_kernel(x_ref, h_ref, comb_ref, mod_ref, wg_ref, wu_ref, wd_ref, fin_ref, *rest, n_ctx, seq_s, final):
    outs, (acc_ref, rank_c_ref, rank_r_ref, comb_t_ref) = rest[:-4], rest[-4:]
    e = pl.program_id(1)
    t_rows = MOE_TILE

    @pl.when(e == 0)
    def _():
        acc_ref[...] = jnp.zeros_like(acc_ref)
        comb = comb_ref[...]
        before = lax.broadcasted_iota(jnp.int32, (t_rows, t_rows), 1) < lax.broadcasted_iota(jnp.int32, (t_rows, t_rows), 0)
        rank_c_ref[...] = _dot(before.astype(BF16), (comb > 0.0).astype(BF16))
        comb_t = comb.T
        comb_t_ref[...] = comb_t
        rank_r_ref[...] = _dot_nt((comb_t > 0.0).astype(BF16), before.astype(BF16))

    comb = comb_ref[...]
    lane = lax.broadcasted_iota(jnp.int32, comb.shape, 1)
    w_col = jnp.sum(jnp.where(lane == e, comb, 0.0), axis=-1, keepdims=True)
    rank_col = jnp.sum(jnp.where(lane == e, rank_c_ref[...], 0.0), axis=-1, keepdims=True)
    w_row = comb_t_ref[pl.ds(e, 1), :]
    rank_row = rank_r_ref[pl.ds(e, 1), :]
    count = jnp.sum((w_row > 0.0).astype(jnp.int32))

    def body(j, carry):
        base = (j * MOE_ROWS).astype(F32)
        slot_r = lax.broadcasted_iota(jnp.int32, (MOE_ROWS, t_rows), 0).astype(F32) + base
        gather = jnp.where(jnp.logical_and(rank_row == slot_r, w_row > 0.0), 1.0, 0.0).astype(BF16)
        hg = _dot(gather, h_ref[...]).astype(BF16)
        t = _silu(_dot(hg, wg_ref[0])) * _dot(hg, wu_ref[0])
        y = _dot(t.astype(BF16), wd_ref[0]).astype(BF16)
        slot_c = lax.broadcasted_iota(jnp.int32, (t_rows, MOE_ROWS), 1).astype(F32) + base
        scatter = jnp.where(jnp.logical_and(rank_col == slot_c, w_col > 0.0), 1.0, 0.0).astype(BF16)
        acc_ref[...] += w_col * _dot(scatter, y)
        return carry

    lax.fori_loop(0, (count + MOE_ROWS - 1) // MOE_ROWS, body, 0)

    @pl.when(e == N_EXPERTS - 1)
    def _():
        row = _mod_row(pl.program_id(0), MOE_TILE, n_ctx, seq_s)
        g2 = mod_ref[pl.ds(row, 1), 5 * D_MODEL:6 * D_MODEL]
        y = x_ref[...] + g2 * acc_ref[...]
        if not final:
            outs[0][...] = y
        else:
            y = y * lax.rsqrt(jnp.mean(y * y, axis=-1, keepdims=True) + RMS_EPS) * fin_ref[...]
            is_ctx = pl.program_id(0) < n_ctx // MOE_TILE

            @pl.when(is_ctx)
            def _():
                outs[0][...] = y

            @pl.when(jnp.logical_not(is_ctx))
            def _():
                outs[1][...] = y


def _ffn_moe(x1, h2, comb, mod, w_gate, w_up, w_down, n_ctx, seq_s, final_gain=None):
    n = x1.shape[0]
    final = final_gain is not None
    tile = lambda w: pl.BlockSpec((MOE_TILE, w), lambda i, e: (i, 0))
    if final:
        n_ctx_tiles = n_ctx // MOE_TILE
        out_specs = [pl.BlockSpec((MOE_TILE, D_MODEL), lambda i, e: (jnp.minimum(i, n_ctx_tiles - 1), 0)),
                     pl.BlockSpec((MOE_TILE, D_MODEL), lambda i, e: (jnp.maximum(i - n_ctx_tiles, 0), 0))]
        out_shape = [jax.ShapeDtypeStruct((n_ctx, D_MODEL), F32), jax.ShapeDtypeStruct((n - n_ctx, D_MODEL), F32)]
        gain = final_gain.reshape(1, D_MODEL)
    else:
        out_specs = [tile(D_MODEL)]
        out_shape = [jax.ShapeDtypeStruct((n, D_MODEL), F32)]
        gain = jnp.ones((1, D_MODEL), F32)
    out = pl.pallas_call(
        functools.partial(_moe_kernel, n_ctx=n_ctx, seq_s=seq_s, final=final),
        grid=(n // MOE_TILE, N_EXPERTS),
        in_specs=[tile(D_MODEL), tile(D_MODEL), tile(LANES),
                  pl.BlockSpec((8, 6 * D_MODEL), lambda i, e: (0, 0)),
                  pl.BlockSpec((1, D_MODEL, D_FF_E), lambda i, e: (e, 0, 0)),
                  pl.BlockSpec((1, D_MODEL, D_FF_E), lambda i, e: (e, 0, 0)),
                  pl.BlockSpec((1, D_FF_E, D_MODEL), lambda i, e: (e, 0, 0)),
                  pl.BlockSpec((1, D_MODEL), lambda i, e: (0, 0))],
        out_specs=out_specs,
        out_shape=out_shape,
        scratch_shapes=[pltpu.VMEM((MOE_TILE, D_MODEL), F32), pltpu.VMEM((MOE_TILE, LANES), F32),
                        pltpu.VMEM((LANES, MOE_TILE), F32), pltpu.VMEM((LANES, MOE_TILE), F32)],
        compiler_params=_cparams(("arbitrary", "arbitrary") if final else ("parallel", "arbitrary")),
        name="ffn_moe",
    )(x1, h2, comb, mod, w_gate.astype(BF16), w_up.astype(BF16), w_down.astype(BF16), gain)
    return tuple(out) if final else out[0]


def _final_kernel(x_ref, g_ref, o_ref):
    x = x_ref[...]
    o_ref[...] = x * lax.rsqrt(jnp.mean(x * x, axis=-1, keepdims=True) + RMS_EPS) * g_ref[...]


def _final_norm(x, gain, row0, n_rows):
    blk0 = row0 // FFN_TILE
    return pl.pallas_call(
        _final_kernel,
        grid=(n_rows // FFN_TILE,),
        in_specs=[pl.BlockSpec((FFN_TILE, D_MODEL), lambda i: (blk0 + i, 0)),
                  pl.BlockSpec((1, D_MODEL), lambda i: (0, 0))],
        out_specs=pl.BlockSpec((FFN_TILE, D_MODEL), lambda i: (i, 0)),
        out_shape=jax.ShapeDtypeStruct((n_rows, D_MODEL), F32),
        compiler_params=_cparams(("parallel",)),
        name="final_norm",
    )(x, gain.reshape(1, D_MODEL))


def kernel(x_prompt, x_sample, state_rwkv, c, c_ctx, norm1, norm2, w_ada, b_ada, w_in, dw_w, dw_b, conv_ln_g, conv_ln_b, w_conv_out, w_fno_out, shift_mu, g_up, dec_w0, dec_up, iclr_a0, iclr_up, k_k, k_a, r_k, gn_g, gn_b, w_rwkv_out, w_o, ffn_w_gate, ffn_w_up, ffn_w_down, w_router, b_router, moe_w_gate, moe_w_up, moe_w_down, final_norm):
    b_p, t_p, _ = x_prompt.shape
    b_s, t_s, _ = x_sample.shape
    depth = w_in.shape[0]
    n_ctx = b_p * t_p
    n_lat = b_s * t_s
    assert t_p == ROW_TILE and t_s % MOE_TILE == 0 and n_ctx % MOE_TILE == 0 and b_s <= CTX_ROW
    assert MOE_TILE % FFN_TILE == 0 and n_ctx % WIDE_TILE == 0 and t_s % WIDE_TILE == 0
    assert t_s % GRID_W == 0 and CHUNK == GRID_W

    n = n_ctx + n_lat
    x = (x_prompt.reshape(n_ctx, D_MODEL), x_sample.reshape(n_lat, D_MODEL))
    cond = jnp.zeros((8, D_MODEL), F32).at[:b_s].set(c).at[CTX_ROW].set(c_ctx)
    mods = _ada(cond, w_ada, b_ada)
    table = _scan_table(b_p, t_p, b_s, t_s)
    zero_state = jnp.zeros((b_p, 2, HEAD_DIM, C_RWKV), F32)
    pack_state = lambda s: s.transpose(0, 1, 3, 2, 4).reshape(s.shape[0], 2, HEAD_DIM, C_RWKV)
    unpack_state = lambda s: s.reshape(s.shape[0], 2, HEAD_DIM, N_HEADS, HEAD_DIM).transpose(0, 1, 3, 2, 4)

    ctx_states = []
    for l in range(depth):
        mod = mods[l]
        u_conv, u_fno, u_rw = _inproj(x, n, mod, norm1[l], w_in[l, :, :OFF_GATE].astype(BF16), n_ctx, t_s)
        conv_h = _conv_branch(u_conv, dw_w[l], dw_b[l], conv_ln_g[l], conv_ln_b[l], n_ctx)
        fno_h = (_fno_branch(u_fno, 0, b_p, t_p), _fno_branch(u_fno, n_ctx, b_s, t_s))
        r, k, v, g, *summaries = _rwkv_project_and_summarise(u_rw, shift_mu[l], g_up[l], dec_w0[l], dec_up[l],
                                                             iclr_a0[l], iclr_up[l], k_k[l], k_a[l], n_ctx, t_s)
        s0 = jnp.concatenate([zero_state, pack_state(state_rwkv[:, l])], axis=0)
        yf, yb, s_fin = _carry_states(*summaries, s0, table)
        ctx_states.append(unpack_state(s_fin[:b_p]))
        i = l // 2
        x1, h2, *comb = _merge(x, n, conv_h, fno_h, yf, yb, r, k, v, g, norm1[l], w_in[l, :, OFF_GATE:].astype(BF16), mod, w_conv_out[l], w_fno_out[l],
                              w_rwkv_out[l], w_o[l], gn_g[l], gn_b[l], r_k[l], norm2[l], n_ctx, t_s,
                               router=(w_router[i], b_router[i]) if l % 2 == 1 else None)
        if l % 2 == 0:
            x = _ffn_dense(x1, h2, mod, ffn_w_gate[i], ffn_w_up[i], ffn_w_down[i], n_ctx, t_s)
        else:
            x = _ffn_moe(x1, h2, comb[0], mod, moe_w_gate[i], moe_w_up[i], moe_w_down[i], n_ctx, t_s,
                         final_gain=final_norm if l == depth - 1 else None)

    if isinstance(x, tuple):
        y_prompt, y_sample = x
    else:
        y_prompt, y_sample = _final_norm(x, final_norm, 0, n_ctx), _final_norm(x, final_norm, n_ctx, n_lat)
    y_prompt = y_prompt.reshape(b_p, t_p, D_MODEL)
    y_sample = y_sample.reshape(b_s, t_s, D_MODEL)
    new_state = jnp.stack(ctx_states, axis=1).astype(x_prompt.dtype)
    return (y_prompt, y_sample, new_state)
```

```python
import functools

import numpy as np
import jax
import jax.numpy as jnp
from jax import lax
from jax.experimental import pallas as pl
from jax.experimental.pallas import tpu as pltpu

F32 = jnp.float32
BF16 = jnp.bfloat16

D_MODEL = 1024
GRID_W = 64
C_CONV = 256
CONV_W = 31
C_FNO = 256
FNO_GW = 64
N_HEADS = 8
HEAD_DIM = 64
C_RWKV = N_HEADS * HEAD_DIM
G_RANK = 128
W_RANK = 64
A_RANK = 64
D_FF = 2816
N_EXPERTS = 8
D_FF_E = 1408
RMS_EPS = 1e-6
LN_EPS = 1e-5
GN_EPS = 64e-5

OFF_FNO = 2 * C_CONV
OFF_RWKV = OFF_FNO + C_FNO
N_SHIFT = 3 * C_RWKV + G_RANK + 2 * (W_RANK + A_RANK)
OFF_GATE = OFF_RWKV + N_SHIFT
D_IN = OFF_GATE + 3 * D_MODEL

ROW_TILE = 256
FFN_TILE = 512
WIDE_TILE = 512
CHUNK = 64
GROUP = 4
GROUP_W = GROUP * HEAD_DIM
SUMMARY_CHUNKS = 4
CARRY_CHUNKS = 4
MOE_TILE = 1024
MOE_PART = 512
MOE_ROWS = 160
CTX_ROW = 4
LANES = 128
VMEM_LIMIT = 56 * 1024 * 1024


def _cparams(sem):
    return pltpu.CompilerParams(dimension_semantics=sem, vmem_limit_bytes=VMEM_LIMIT)


def _sigmoid(x):
    return 1.0 / (1.0 + jnp.exp(-x))


def _silu(x):
    return x * _sigmoid(x)


def _dot(a, b):
    return jnp.dot(a, b, preferred_element_type=F32)


def _split(x):
    hi = x.astype(BF16)
    lo = (x - hi.astype(F32)).astype(BF16)
    return hi, lo


def _dot_x3(a, b):
    ah, al = _split(a)
    bh, bl = _split(b)
    return _dot(ah, bh) + (_dot(ah, bl) + _dot(al, bh))


def _seg_sum(x, bd):
    return _dot(x.astype(BF16), bd)


def _mod_row(i, tile, n_ctx_rows, seq_s):
    n_ctx_tiles = n_ctx_rows // tile
    return jnp.where(i < n_ctx_tiles, CTX_ROW, (i - n_ctx_tiles) // (seq_s // tile))


def _stream_specs(tile, width, n_ctx, lat_block0):
    n_ctx_tiles = n_ctx // tile
    return [pl.BlockSpec((tile, width), lambda i: (jnp.minimum(i, n_ctx_tiles - 1), 0)),
            pl.BlockSpec((tile, width), lambda i: (jnp.maximum(i - n_ctx_tiles, 0) + lat_block0, 0))]


def _stream_args(x, tile, n_ctx):
    if isinstance(x, tuple):
        return x[0], x[1], 0
    return x, x, n_ctx // tile


def _pick_stream(i, tile, n_ctx, ctx_ref, lat_ref):
    return jnp.where(i < n_ctx // tile, ctx_ref[...], lat_ref[...])


def _ada_kernel(c_ref, w_ref, b_ref, o_ref):
    s = _silu(c_ref[...])
    o_ref[0] = _dot(s.astype(BF16), w_ref[0].astype(BF16)) + b_ref[0]


def _ada(cond, w_ada, b_ada):
    n_l = w_ada.shape[0]
    tn = 1536
    return pl.pallas_call(
        _ada_kernel,
        grid=(n_l, 6 * D_MODEL // tn),
        in_specs=[
            pl.BlockSpec((8, D_MODEL), lambda l, j: (0, 0)),
            pl.BlockSpec((1, D_MODEL, tn), lambda l, j: (l, 0, j)),
            pl.BlockSpec((1, 1, tn), lambda l, j: (l, 0, j)),
        ],
        out_specs=pl.BlockSpec((1, 8, tn), lambda l, j: (l, 0, j)),
        out_shape=jax.ShapeDtypeStruct((n_l, 8, 6 * D_MODEL), F32),
        compiler_params=_cparams(("parallel", "parallel")),
        name="ada",
    )(cond, w_ada, b_ada.reshape(n_l, 1, 6 * D_MODEL))


def _inproj_kernel(xc_ref, xl_ref, mod_ref, g_ref, w_ref, oc_ref, of_ref, or_ref, *, n_ctx, seq_s):
    row = _mod_row(pl.program_id(0), WIDE_TILE, n_ctx, seq_s)
    m = mod_ref[pl.ds(row, 1), :]
    sh = m[:, 0:D_MODEL]
    sc = m[:, D_MODEL:2 * D_MODEL]
    x = _pick_stream(pl.program_id(0), WIDE_TILE, n_ctx, xc_ref, xl_ref)
    y = x * lax.rsqrt(jnp.mean(x * x, axis=-1, keepdims=True) + RMS_EPS) * g_ref[...]
    h = (y * (1.0 + sc) + sh).astype(BF16)
    oc_ref[...] = _dot(h, w_ref[:, 0:OFF_FNO])
    of_ref[...] = _dot(h, w_ref[:, OFF_FNO:OFF_RWKV])
    or_ref[...] = _dot(h, w_ref[:, OFF_RWKV:OFF_GATE])


def _inproj(x, n, mod, gain, w_in, n_ctx, seq_s):
    xc, xl, lat0 = _stream_args(x, WIDE_TILE, n_ctx)
    widths = (OFF_FNO, C_FNO, N_SHIFT)
    return pl.pallas_call(
        functools.partial(_inproj_kernel, n_ctx=n_ctx, seq_s=seq_s),
        grid=(n // WIDE_TILE,),
        in_specs=_stream_specs(WIDE_TILE, D_MODEL, n_ctx, lat0) + [
            pl.BlockSpec((8, 6 * D_MODEL), lambda i: (0, 0)),
            pl.BlockSpec((1, D_MODEL), lambda i: (0, 0)),
            pl.BlockSpec((D_MODEL, OFF_GATE), lambda i: (0, 0), pipeline_mode=pl.Buffered(1)),
        ],
        out_specs=[pl.BlockSpec((WIDE_TILE, w), lambda i: (i, 0)) for w in widths],
        out_shape=[jax.ShapeDtypeStruct((n, w), F32) for w in widths],
        compiler_params=_cparams(("parallel",)),
        name="inproj",
    )(xc, xl, mod, gain.reshape(1, D_MODEL), w_in)


_CONV_HALO = 16
_CONV_ROWS = 64


def _conv_kernel(u_ref, w_ref, b_ref, g_ref, be_ref, o_ref, pad_ref, sh_ref, *, n_ctx):
    is_ctx = pl.program_id(0) < n_ctx // ROW_TILE
    zeros = jnp.zeros((_CONV_HALO, C_CONV), F32)
    n_parts = ROW_TILE // _CONV_ROWS

    def glu(lo, hi):
        return u_ref[lo:hi, 0:C_CONV] * _sigmoid(u_ref[lo:hi, C_CONV:2 * C_CONV])

    def finish(starts, n_rows):
        for s in range(8):
            sh_ref[s, 0:n_rows - 8, :] = pad_ref[s:s + n_rows - 8, :]
        for p in range(n_parts):
            acc = jnp.zeros((_CONV_ROWS, C_CONV), F32)
            for j in range(CONV_W):
                o = starts[p] + j
                acc = acc + w_ref[j:j + 1, :] * sh_ref[o % 8, o - o % 8:o - o % 8 + _CONV_ROWS, :]
            acc = acc + b_ref[...]
            mu = jnp.mean(acc, axis=-1, keepdims=True)
            d = acc - mu
            var = jnp.mean(d * d, axis=-1, keepdims=True)
            y = d * lax.rsqrt(var + LN_EPS) * g_ref[...] + be_ref[...]
            o_ref[p * _CONV_ROWS:(p + 1) * _CONV_ROWS, :] = _silu(y).astype(o_ref.dtype)

    shift = _CONV_HALO - CONV_W // 2

    @pl.when(is_ctx)
    def _():
        pad_ref[0:_CONV_HALO, :] = zeros
        pad_ref[_CONV_HALO:_CONV_HALO + ROW_TILE, :] = glu(0, ROW_TILE)
        pad_ref[_CONV_HALO + ROW_TILE:2 * _CONV_HALO + ROW_TILE, :] = zeros
        finish([shift + p * _CONV_ROWS for p in range(n_parts)], ROW_TILE + 2 * _CONV_HALO)

    @pl.when(jnp.logical_not(is_ctx))
    def _():
        stride = GRID_W + 2 * _CONV_HALO
        for p in range(n_parts):
            pad_ref[p * stride:p * stride + _CONV_HALO, :] = zeros
            pad_ref[p * stride + _CONV_HALO:p * stride + _CONV_HALO + GRID_W, :] = glu(p * GRID_W, (p + 1) * GRID_W)
            pad_ref[p * stride + _CONV_HALO + GRID_W:(p + 1) * stride, :] = zeros
        finish([p * stride + shift for p in range(n_parts)], n_parts * stride)


def _conv_branch(u_conv, dw_w, dw_b, ln_g, ln_b, n_ctx):
    n = u_conv.shape[0]
    assert _CONV_ROWS == GRID_W and ROW_TILE % GRID_W == 0
    vec = pl.BlockSpec((1, C_CONV), lambda i: (0, 0))
    return pl.pallas_call(
        functools.partial(_conv_kernel, n_ctx=n_ctx),
        grid=(n // ROW_TILE,),
        in_specs=[
            pl.BlockSpec((ROW_TILE, 2 * C_CONV), lambda i: (i, 0)),
            pl.BlockSpec((CONV_W, C_CONV), lambda i: (0, 0)),
            vec, vec, vec,
        ],
        out_specs=pl.BlockSpec((ROW_TILE, C_CONV), lambda i: (i, 0)),
        out_shape=jax.ShapeDtypeStruct((n, C_CONV), BF16),
        scratch_shapes=[pltpu.VMEM((ROW_TILE // GRID_W * (GRID_W + 2 * _CONV_HALO), C_CONV), F32),
                        pltpu.VMEM((8, ROW_TILE // GRID_W * (GRID_W + 2 * _CONV_HALO), C_CONV), F32)],
        compiler_params=_cparams(("parallel",)),
        name="conv_branch",
    )(u_conv, dw_w, dw_b.reshape(1, C_CONV), ln_g.reshape(1, C_CONV), ln_b.reshape(1, C_CONV))


def _dft_tables(t_len):
    def cs(n):
        k = np.arange(n, dtype=np.int64)
        ang = 2.0 * np.pi * ((k[:, None] * k[None, :]) % n).astype(np.float64) / n
        return np.cos(ang), np.sin(ang)
    cg, sg = cs(FNO_GW)
    eye = np.eye(C_FNO // FNO_GW)
    w1 = np.concatenate([np.kron(eye, cg), np.kron(eye, sg)], axis=1)
    ct, st = cs(t_len)
    w2 = np.concatenate([ct, -st], axis=1)
    return jnp.asarray(w1, dtype=F32).astype(BF16), jnp.asarray(w2, dtype=F32).astype(BF16)


def _fno_kernel(u_ref, w1_ref, w2_ref, o_ref, hs_ref, *, t_len, scale):
    @pl.when(pl.program_id(1) == 0)
    def _():
        hc = _dot(u_ref[...].astype(BF16), w1_ref[...])
        hs_ref[0:t_len, :] = hc[:, 0:C_FNO].astype(BF16)
        hs_ref[t_len:2 * t_len, :] = hc[:, C_FNO:2 * C_FNO].astype(BF16)

    o_ref[...] = (_dot(w2_ref[...], hs_ref[...]) * scale).astype(o_ref.dtype)


def _fno_branch(u_fno, row0, n_seq, t_len):
    w1, w2 = _dft_tables(t_len)
    tk = min(t_len, 512)
    blk0 = row0 // t_len
    return pl.pallas_call(
        functools.partial(_fno_kernel, t_len=t_len, scale=float(1.0 / np.sqrt(t_len * FNO_GW))),
        grid=(n_seq, t_len // tk),
        in_specs=[
            pl.BlockSpec((t_len, C_FNO), lambda b, j: (blk0 + b, 0)),
            pl.BlockSpec((C_FNO, 2 * C_FNO), lambda b, j: (0, 0)),
            pl.BlockSpec((tk, 2 * t_len), lambda b, j: (j, 0)),
        ],
        out_specs=pl.BlockSpec((tk, C_FNO), lambda b, j: (b * (t_len // tk) + j, 0)),
        out_shape=jax.ShapeDtypeStruct((n_seq * t_len, C_FNO), BF16),
        scratch_shapes=[pltpu.VMEM((2 * t_len, C_FNO), BF16)],
        compiler_params=_cparams(("parallel", "arbitrary")),
        name="fno_branch",
    )(u_fno, w1, w2)


def _prep_pieces(i, z_ref, zp_ref, zn_ref, mu_ref, gup_ref, w0_ref, dup_ref, a0_ref, aup_ref, kk_ref, bd_ref,
                 r_refs, k_refs, v_refs, kkn_ref, g_ref, lw_ref, a_ref, pad_ref, n_ctx, seq_s):
    n_ctx_tiles = n_ctx // ROW_TILE
    per_seq = seq_s // ROW_TILE
    j = (i - n_ctx_tiles) % per_seq
    first = jnp.logical_or(i < n_ctx_tiles, j == 0)
    last = jnp.logical_or(i < n_ctx_tiles, j == per_seq - 1)
    c = C_RWKV

    def shifted(lo, hi):
        z = pad_ref[8:8 + ROW_TILE, lo:hi]
        zp = pad_ref[7:7 + ROW_TILE, lo:hi]
        zn = pad_ref[9:9 + ROW_TILE, lo:hi]
        return z + mu_ref[:, lo:hi] * (0.5 * (zp + zn) - z)

    def fill():
        pad_ref[8:8 + ROW_TILE, :] = z_ref[...]
        pad_ref[0:8, :] = jnp.where(first, 0.0, zp_ref[...])
        pad_ref[8 + ROW_TILE:16 + ROW_TILE, :] = jnp.where(last, 0.0, zn_ref[...])

    def receptance():
        r = shifted(0, c)
        for ref in r_refs:
            ref[...] = r

    def key():
        k = shifted(c, 2 * c)
        for ref in k_refs:
            ref[...] = k
        kx = k * kk_ref[...]
        nrm = jnp.sqrt(_seg_sum(kx * kx, bd_ref[...]))
        kkn_ref[...] = kx / jnp.maximum(nrm, 1e-12)

    def value():
        v = shifted(2 * c, 3 * c)
        for ref in v_refs:
            ref[...] = v

    def gate():
        o = 3 * c
        g_ref[...] = _dot(_sigmoid(shifted(o, o + G_RANK)).astype(BF16), gup_ref[...])

    def decay():
        o = 3 * c + G_RANK
        xw = _dot(jnp.tanh(shifted(o, o + 2 * W_RANK)).astype(BF16), dup_ref[...]) + w0_ref[...]
        soft = jnp.maximum(-xw, 0.0) + jnp.log(1.0 + jnp.exp(-jnp.abs(xw)))
        lw_ref[...] = -jnp.exp(-soft - 0.5)

    def iclr():
        o = 3 * c + G_RANK + 2 * W_RANK
        xa = _dot(shifted(o, o + 2 * A_RANK).astype(BF16), aup_ref[...]) + a0_ref[...]
        a_ref[...] = _sigmoid(xa)

    return [fill, receptance, key, value, gate, decay, iclr]


def _block_diag2(w):
    z = jnp.zeros_like(w[0])
    return jnp.concatenate([jnp.concatenate([w[0], z], axis=1), jnp.concatenate([z, w[1]], axis=1)], axis=0)


def _head_ones():
    return jnp.asarray(np.kron(np.eye(N_HEADS), np.ones((HEAD_DIM, HEAD_DIM))), dtype=BF16)


def _block_diag(x, mask):
    xb = x.astype(BF16)
    return jnp.where(mask, jnp.concatenate([xb] * GROUP, axis=0), jnp.zeros((), BF16))


def _dot_nt(a, b):
    return lax.dot_general(a, b, (((1,), (1,)), ((), ())), preferred_element_type=F32)


def _diag_blocks(prod, lane_head):
    out = jnp.where(lane_head == 0, prod[0:HEAD_DIM], 0.0)
    for h in range(1, GROUP):
        out = out + jnp.where(lane_head == h, prod[h * HEAD_DIM:(h + 1) * HEAD_DIM], 0.0)
    return out


def _summary_body(r_ref, k_ref, v_ref, n_ref, lw_ref, a_ref, ka_ref, q_ref, y0_ref, p_ref, z_ref, g_ref, fillers=()):
    fillers = list(fillers)
    fill_one = lambda: fillers.pop(0)() if fillers else None
    row = lax.broadcasted_iota(jnp.int32, (CHUNK, GROUP_W), 0)
    lane = lax.broadcasted_iota(jnp.int32, (CHUNK, GROUP_W), 1)
    col = lane % CHUNK
    lane_head = lane // HEAD_DIM
    bd_mask = (lax.broadcasted_iota(jnp.int32, (GROUP * CHUNK, GROUP_W), 0) // CHUNK
               == lax.broadcasted_iota(jnp.int32, (GROUP * CHUNK, GROUP_W), 1) // HEAD_DIM)
    bd = lambda x: _block_diag(x, bd_mask)
    ka = ka_ref[...]

    units = []
    for j in range(SUMMARY_CHUNKS):
        rows = slice(j * CHUNK, (j + 1) * CHUNK)
        r = r_ref[rows, :]
        k = k_ref[rows, :]
        v = v_ref[rows, :]
        kkn = n_ref[rows, :]
        for d in range(2):
            lanes = slice(d * C_RWKV, (d + 1) * C_RWKV)
            earlier = (col < row) if d == 0 else (col > row)
            upto = jnp.logical_or(earlier, row == col)
            tri = upto[:, 0:CHUNK].astype(BF16)
            lw = lw_ref[rows, lanes]
            h1 = lw.astype(BF16)
            r1 = lw - h1.astype(F32)
            h2 = r1.astype(BF16)
            h3 = (r1 - h2.astype(F32)).astype(BF16)
            cum = _dot(tri, h1) + (_dot(tri, h2) + _dot(tri, h3))
            e_in = jnp.exp(cum)
            e_ex = jnp.exp(cum - lw)
            e_ng = jnp.exp(-cum)
            a = a_ref[rows, lanes]
            kd = k * (1.0 + (a - 1.0) * ka)
            at = -kkn * e_ex
            rt = r * e_in
            bt = kkn * a * e_ng
            kt = kd * e_ng
            end = CHUNK - 1 if d == 0 else 0
            g_end = e_in[end:end + 1, :]
            g_ref[j, :, lanes] = g_end
            bh = bt * g_end
            kh = kt * g_end
            for q in range(C_RWKV // GROUP_W):
                sl = slice(q * GROUP_W, (q + 1) * GROUP_W)
                units.append(dict(rows=rows, out=slice(d * C_RWKV + q * GROUP_W, d * C_RWKV + (q + 1) * GROUP_W),
                                  earlier=earlier, upto=upto, at=at[:, sl], rt=rt[:, sl],
                                  ar=jnp.concatenate([at[:, sl], rt[:, sl]], axis=0).astype(BF16),
                                  bt=bt[:, sl], kt=kt[:, sl], v=v[:, sl], bh=bh[:, sl].astype(BF16),
                                  bk=jnp.concatenate([bh[:, sl], kh[:, sl]], axis=0).astype(BF16)))

    for u in units:
        sb = _dot_nt(u["ar"], bd(u["bt"]))
        sk = _dot_nt(u["ar"], bd(u["kt"]))
        u["lab"] = jnp.where(u["earlier"], sb[0:CHUNK], 0.0)
        u["mrb"] = jnp.where(u["upto"], sb[CHUNK:2 * CHUNK], 0.0).astype(BF16)
        u["lm"] = jnp.concatenate([jnp.where(u["earlier"], sk[0:CHUNK], 0.0),
                                   jnp.where(u["upto"], sk[CHUNK:2 * CHUNK], 0.0)], axis=0).astype(BF16)

    fill_one()
    eye = (row == col).astype(F32)
    pair = jnp.logical_and(row // 2 == col // 2, row != col)
    for u in units:
        u["t"] = eye + jnp.where(pair, u["lab"], 0.0)
    n = 2
    while n < CHUNK:
        m = jnp.logical_and(row // (2 * n) == col // (2 * n), row // n != col // n)
        for u in units:
            u["w"] = _dot(jnp.where(m, u["lab"], 0.0).astype(BF16), bd(u["t"]))
        for u in units:
            u["t"] = u["t"] + _dot(u["t"].astype(BF16), bd(u["w"]))
        fill_one()
        n *= 2

    for u in units:
        u["t"] = u["t"].astype(BF16)
        u["wm"] = _dot(u["t"], bd(u["at"]))
        u["lv"] = _dot(u["lm"], bd(u["v"]))
    for u in units:
        u["u0"] = _dot(u["t"], bd(u["lv"][0:CHUNK]))
    for u in units:
        q_ref[u["rows"], u["out"]] = (u["rt"] + _dot(u["mrb"], bd(u["wm"]))).astype(q_ref.dtype)
        y0_ref[u["rows"], u["out"]] = u["lv"][CHUNK:2 * CHUNK] + _dot(u["mrb"], bd(u["u0"]))
    for u in units:
        p = _dot(u["wm"].T.astype(BF16), u["bh"])
        p_ref[u["rows"], u["out"]] = _diag_blocks(p, lane_head).astype(p_ref.dtype)
        z = _dot(jnp.concatenate([u["u0"], u["v"]], axis=0).T.astype(BF16), u["bk"])
        z_ref[u["rows"], u["out"]] = _diag_blocks(z, lane_head)
    while fillers:
        fill_one()


def _rwkv_kernel(z_ref, zp_ref, zn_ref, mu_ref, gup_ref, w0_ref, dup_ref, a0_ref, aup_ref, kk_ref, bd_ref, ka_ref,
                 r_ref, k_ref, v_ref, g_ref, q_ref, y0_ref, p_ref, zz_ref, gd_ref,
                 pad_ref, r_n, k_n, v_n, n_n, lw_n, a_n, r_c, k_c, v_c, n_c, lw_c, a_c, *, n_ctx, seq_s, n_tiles):
    s = pl.program_id(0)
    new = (r_n, k_n, v_n, n_n, lw_n, a_n)
    cur = (r_c, k_c, v_c, n_c, lw_c, a_c)

    @pl.when(s == 0)
    def _():
        for ref in cur:
            ref[...] = jnp.zeros_like(ref)

    pieces = _prep_pieces(jnp.minimum(s, n_tiles - 1), z_ref, zp_ref, zn_ref, mu_ref, gup_ref, w0_ref, dup_ref,
                          a0_ref, aup_ref, kk_ref, bd_ref, (r_n, r_ref), (k_n, k_ref), (v_n, v_ref), n_n, g_ref,
                          lw_n, a_n, pad_ref, n_ctx, seq_s)
    pieces[0]()
    _summary_body(r_c, k_c, v_c, n_c, lw_c, a_c, ka_ref, q_ref, y0_ref, p_ref, zz_ref, gd_ref, fillers=pieces[1:])
    for dst, src in zip(cur, new):
        dst[...] = src[...]


def _rwkv_project_and_summarise(u_rw, mu, g_up, dec_w0, dec_up, iclr_a0, iclr_up, k_k, k_a, n_ctx, seq_s):
    n = u_rw.shape[0]
    assert ROW_TILE == SUMMARY_CHUNKS * CHUNK and CHUNK == HEAD_DIM and C_RWKV % GROUP_W == 0
    n_tiles = n // ROW_TILE
    halo = ROW_TILE // 8
    last_blk = n // 8 - 1
    c = C_RWKV
    full = lambda shape: pl.BlockSpec(shape, lambda s: (0,) * len(shape))
    this = lambda s: jnp.minimum(s, n_tiles - 1)
    prev = lambda s: jnp.maximum(s - 1, 0)
    proj = lambda w: pl.BlockSpec((ROW_TILE, w), lambda s: (this(s), 0))
    summ = lambda w: pl.BlockSpec((ROW_TILE, w), lambda s: (prev(s), 0))
    vm = lambda w: pltpu.VMEM((ROW_TILE, w), F32)
    return pl.pallas_call(
        functools.partial(_rwkv_kernel, n_ctx=n_ctx, seq_s=seq_s, n_tiles=n_tiles),
        grid=(n_tiles + 1,),
        in_specs=[
            proj(N_SHIFT),
            pl.BlockSpec((8, N_SHIFT), lambda s: (jnp.maximum(this(s) * halo - 1, 0), 0)),
            pl.BlockSpec((8, N_SHIFT), lambda s: (jnp.minimum((this(s) + 1) * halo, last_blk), 0)),
            full((1, N_SHIFT)), full((G_RANK, c)), full((1, 2 * c)), full((2 * W_RANK, 2 * c)),
            full((1, 2 * c)), full((2 * A_RANK, 2 * c)), full((1, c)), full((c, c)), full((1, c)),
        ],
        out_specs=[proj(c), proj(c), proj(c), proj(c), summ(2 * c), summ(2 * c), summ(2 * c), summ(2 * c),
                   pl.BlockSpec((SUMMARY_CHUNKS, 1, 2 * c), lambda s: (prev(s), 0, 0))],
        out_shape=[jax.ShapeDtypeStruct((n, c), F32)] * 4
        + [jax.ShapeDtypeStruct((n, 2 * c), BF16), jax.ShapeDtypeStruct((n, 2 * c), F32),
           jax.ShapeDtypeStruct((n, 2 * c), BF16), jax.ShapeDtypeStruct((n, 2 * c), F32),
           jax.ShapeDtypeStruct((n // CHUNK, 1, 2 * c), F32)],
        scratch_shapes=[pltpu.VMEM((ROW_TILE + 16, N_SHIFT), F32)]
        + [vm(c), vm(c), vm(c), vm(c), vm(2 * c), vm(2 * c)] * 2,
        compiler_params=_cparams(("arbitrary",)),
        name="rwkv_project_summarise",
    )(u_rw, u_rw, u_rw, mu.reshape(1, N_SHIFT), g_up.astype(BF16), dec_w0.reshape(1, 2 * c),
      _block_diag2(dec_up).astype(BF16), iclr_a0.reshape(1, 2 * c), _block_diag2(iclr_up).astype(BF16),
      k_k.reshape(1, c), _head_ones(), k_a.reshape(1, c))


def _carry_kernel(tab_ref, qf_ref, y0f_ref, pf_ref, zf_ref, gf_ref, qb_ref, y0b_ref, pb_ref, zb_ref, gb_ref,
                  s0_ref, yf_ref, yb_ref, sfin_ref, st_ref):
    step = pl.program_id(0)

    @pl.when(tab_ref[3, step] == 1)
    def _():
        st_ref[...] = s0_ref[0]

    bd_mask = (lax.broadcasted_iota(jnp.int32, (GROUP * CHUNK, GROUP_W), 0) // CHUNK
               == lax.broadcasted_iota(jnp.int32, (GROUP * CHUNK, GROUP_W), 1) // HEAD_DIM)
    for i in range(CARRY_CHUNKS):
        for d, (q_ref, y0_ref, p_ref, z_ref, g_ref, y_ref) in enumerate(
                ((qf_ref, y0f_ref, pf_ref, zf_ref, gf_ref, yf_ref), (qb_ref, y0b_ref, pb_ref, zb_ref, gb_ref, yb_ref))):
            j = i if d == 0 else CARRY_CHUNKS - 1 - i
            rows = slice(j * CHUNK, (j + 1) * CHUNK)
            for q in range(C_RWKV // GROUP_W):
                sl = slice(q * GROUP_W, (q + 1) * GROUP_W)
                s = st_ref[d, :, sl]
                y_ref[rows, sl] = y0_ref[rows, sl] + _dot_nt(q_ref[rows, sl], _block_diag(s, bd_mask))
                st_ref[d, :, sl] = (s * g_ref[j, :, sl] + z_ref[rows, sl]
                                    + _dot(s.astype(BF16), _block_diag(p_ref[rows, sl], bd_mask)))

    @pl.when(tab_ref[4, step] == 1)
    def _():
        sfin_ref[0] = st_ref[...]


def _scan_table(n_ctx_seq, t_ctx, n_s_seq, t_s):
    rows = []
    base = 0
    seq = 0
    for n_seq, t_len in ((n_ctx_seq, t_ctx), (n_s_seq, t_s)):
        assert t_len % (CARRY_CHUNKS * CHUNK) == 0
        n_c = t_len // (CARRY_CHUNKS * CHUNK)
        for _ in range(n_seq):
            for c in range(n_c):
                rows.append((base + c, base + n_c - 1 - c, seq, int(c == 0), int(c == n_c - 1)))
            base += n_c
            seq += 1
    return np.asarray(rows, dtype=np.int32).T.copy()


def _carry_states(q, y0, p, z, g, s0, table):
    n = q.shape[0]
    n_seq = s0.shape[0]
    c = C_RWKV

    def tok(which, d):
        return pl.BlockSpec((CARRY_CHUNKS * CHUNK, c), lambda s, tab: (tab[which, s], d))

    def decay(which, d):
        return pl.BlockSpec((CARRY_CHUNKS, 1, c), lambda s, tab: (tab[which, s], 0, d))

    state_spec = pl.BlockSpec((1, 2, HEAD_DIM, c), lambda s, tab: (tab[2, s], 0, 0, 0))
    grid_spec = pltpu.PrefetchScalarGridSpec(
        num_scalar_prefetch=1,
        grid=(table.shape[1],),
        in_specs=[tok(0, 0), tok(0, 0), tok(0, 0), tok(0, 0), decay(0, 0),
                  tok(1, 1), tok(1, 1), tok(1, 1), tok(1, 1), decay(1, 1), state_spec],
        out_specs=[tok(0, 0), tok(1, 0), state_spec],
        scratch_shapes=[pltpu.VMEM((2, HEAD_DIM, c), F32)],
    )
    return pl.pallas_call(
        _carry_kernel,
        grid_spec=grid_spec,
        out_shape=[jax.ShapeDtypeStruct((n, c), F32), jax.ShapeDtypeStruct((n, c), F32),
                   jax.ShapeDtypeStruct((n_seq, 2, HEAD_DIM, c), F32)],
        compiler_params=_cparams(("arbitrary",)),
        name="rwkv_carry",
    )(jnp.asarray(table), q, y0, p, z, g, q, y0, p, z, g, s0)


def _merge_kernel(xc_ref, xl_ref, c_ref, fc_ref, fl_ref, yf_ref, yb_ref, r_ref, k_ref, v_ref, g_ref, n1_ref, wgt_ref, mod_ref,
                  wc_ref, wf_ref, wr_ref, wo_ref, gng_ref, gnb_ref, rk_ref, bd_ref, n2_ref, *rest, n_ctx, seq_s, route):
    if route:
        wrt_ref, brt_ref, x1_ref, h2_ref, comb_ref = rest
    else:
        x1_ref, h2_ref = rest
    row = _mod_row(pl.program_id(0), WIDE_TILE, n_ctx, seq_s)
    m = mod_ref[pl.ds(row, 1), :]
    sh1 = m[:, 0:D_MODEL]
    sc1 = m[:, D_MODEL:2 * D_MODEL]
    g1 = m[:, 2 * D_MODEL:3 * D_MODEL]
    sh2 = m[:, 3 * D_MODEL:4 * D_MODEL]
    sc2 = m[:, 4 * D_MODEL:5 * D_MODEL]
    bd = bd_ref[...]
    inv = 1.0 / HEAD_DIM
    y = yf_ref[...] + yb_ref[...]
    y_hi, y_lo = _split(y)
    d = y - (_dot(y_hi, bd) + _dot(y_lo, bd)) * inv
    var = _seg_sum(d * d, bd) * inv
    yn = d * lax.rsqrt(var + GN_EPS) * gng_ref[...] + gnb_ref[...]
    v = v_ref[...]
    yn = yn + _seg_sum(r_ref[...] * k_ref[...] * rk_ref[...], bd) * v
    yr = _dot((yn * g_ref[...]).astype(BF16), wr_ref[...])
    yc = _dot(c_ref[...], wc_ref[...])
    yf = _dot(_pick_stream(pl.program_id(0), WIDE_TILE, n_ctx, fc_ref, fl_ref), wf_ref[...])
    x = _pick_stream(pl.program_id(0), WIDE_TILE, n_ctx, xc_ref, xl_ref)
    hn = x * lax.rsqrt(jnp.mean(x * x, axis=-1, keepdims=True) + RMS_EPS) * n1_ref[...]
    hn = (hn * (1.0 + sc1) + sh1).astype(BF16)
    dm = D_MODEL
    merged = (_sigmoid(_dot(hn, wgt_ref[:, 0:dm])) * yc + _sigmoid(_dot(hn, wgt_ref[:, dm:2 * dm])) * yf
              + _sigmoid(_dot(hn, wgt_ref[:, 2 * dm:3 * dm])) * yr)
    x1 = x + g1 * _dot(merged.astype(BF16), wo_ref[...])
    x1_ref[...] = x1
    h2 = x1 * lax.rsqrt(jnp.mean(x1 * x1, axis=-1, keepdims=True) + RMS_EPS) * n2_ref[...]
    h2 = h2 * (1.0 + sc2) + sh2
    h2_ref[...] = h2.astype(BF16)
    if not route:
        return
    logits = _dot_x3(h2, wrt_ref[...]) + brt_ref[...]
    lane = lax.broadcasted_iota(jnp.int32, logits.shape, 1).astype(F32)
    neg = jnp.float32(-jnp.inf)
    lg = jnp.where(lane < N_EXPERTS, logits, neg)
    m1 = jnp.max(lg, axis=-1, keepdims=True)
    i1 = jnp.min(jnp.where(lg == m1, lane, float(LANES)), axis=-1, keepdims=True)
    lg2 = jnp.where(lane == i1, neg, lg)
    m2 = jnp.max(lg2, axis=-1, keepdims=True)
    i2 = jnp.min(jnp.where(lg2 == m2, lane, float(LANES)), axis=-1, keepdims=True)
    e = jnp.exp(m2 - m1)
    p1 = 1.0 / (1.0 + e)
    comb_ref[...] = jnp.where(lane == i1, p1, 0.0) + jnp.where(lane == i2, e * p1, 0.0)


def _merge(x, n, conv_h, fno_h, yf, yb, r, k, v, g, gain1, w_gates, mod, w_conv_out, w_fno_out, w_rwkv_out, w_o,
           gn_g, gn_b, r_k, gain2, n_ctx, seq_s, router=None):
    xc, xl, x_lat0 = _stream_args(x, WIDE_TILE, n_ctx)
    fc, fl, f_lat0 = _stream_args(fno_h, WIDE_TILE, n_ctx)
    c = C_RWKV
    tile = lambda w: pl.BlockSpec((WIDE_TILE, w), lambda i: (i, 0))
    full = lambda shape: pl.BlockSpec(shape, lambda i: (0,) * len(shape), pipeline_mode=pl.Buffered(1))
    route = router is not None
    route_args, route_specs, route_out, route_shape = [], [], [], []
    if route:
        route_args = [jnp.zeros((D_MODEL, LANES), F32).at[:, :N_EXPERTS].set(router[0]),
                      jnp.zeros((1, LANES), F32).at[0, :N_EXPERTS].set(router[1])]
        route_specs = [full((D_MODEL, LANES)), full((1, LANES))]
        route_out = [tile(LANES)]
        route_shape = [jax.ShapeDtypeStruct((n, LANES), F32)]
    return pl.pallas_call(
        functools.partial(_merge_kernel, n_ctx=n_ctx, seq_s=seq_s, route=route),
        grid=(n // WIDE_TILE,),
        in_specs=_stream_specs(WIDE_TILE, D_MODEL, n_ctx, x_lat0) + [tile(C_CONV)]
        + _stream_specs(WIDE_TILE, C_FNO, n_ctx, f_lat0) + [
            tile(c), tile(c), tile(c), tile(c), tile(c), tile(c),
            full((1, D_MODEL)), full((D_MODEL, 3 * D_MODEL)), full((8, 6 * D_MODEL)),
            full((C_CONV, D_MODEL)), full((C_FNO, D_MODEL)), full((c, D_MODEL)), full((D_MODEL, D_MODEL)),
            full((1, c)), full((1, c)), full((1, c)), full((c, c)), full((1, D_MODEL)),
        ] + route_specs,
        out_specs=[tile(D_MODEL), tile(D_MODEL)] + route_out,
        out_shape=[jax.ShapeDtypeStruct((n, D_MODEL), F32), jax.ShapeDtypeStruct((n, D_MODEL), BF16)] + route_shape,
        compiler_params=_cparams(("parallel",)),
        name="merge",
    )(xc, xl, conv_h, fc, fl, yf, yb, r, k, v, g, gain1.reshape(1, D_MODEL), w_gates, mod,
      w_conv_out.astype(BF16), w_fno_out.astype(BF16), w_rwkv_out.astype(BF16), w_o.astype(BF16),
      gn_g.reshape(1, c), gn_b.reshape(1, c), r_k.reshape(1, c), _head_ones(), gain2.reshape(1, D_MODEL), *route_args)


def _ffn_kernel(x_ref, h_ref, mod_ref, wg_ref, wu_ref, wd_ref, o_ref, *, n_ctx, seq_s):
    row = _mod_row(pl.program_id(0), FFN_TILE, n_ctx, seq_s)
    g2 = mod_ref[pl.ds(row, 1), 5 * D_MODEL:6 * D_MODEL]
    h = h_ref[...]
    half = D_FF // 2
    acc = jnp.zeros((FFN_TILE, D_MODEL), F32)
    for p in range(2):
        sl = slice(p * half, (p + 1) * half)
        t = _silu(_dot(h, wg_ref[:, sl])) * _dot(h, wu_ref[:, sl])
        acc = acc + _dot(t.astype(BF16), wd_ref[sl, :])
    o_ref[...] = x_ref[...] + g2 * acc


def _ffn_dense(x1, h2, mod, w_gate, w_up, w_down, n_ctx, seq_s):
    n = x1.shape[0]
    tile = pl.BlockSpec((FFN_TILE, D_MODEL), lambda i: (i, 0))
    full = lambda shape: pl.BlockSpec(shape, lambda i: (0,) * len(shape))
    return pl.pallas_call(
        functools.partial(_ffn_kernel, n_ctx=n_ctx, seq_s=seq_s),
        grid=(n // FFN_TILE,),
        in_specs=[tile, tile, full((8, 6 * D_MODEL)), full((D_MODEL, D_FF)), full((D_MODEL, D_FF)),
                  full((D_FF, D_MODEL))],
        out_specs=tile,
        out_shape=jax.ShapeDtypeStruct((n, D_MODEL), F32),
        compiler_params=_cparams(("parallel",)),
        name="ffn_dense",
    )(x1, h2, mod, w_gate.astype(BF16), w_up.astype(BF16), w_down.astype(BF16))


def _moe_kernel(x_ref, h_ref, comb_ref, mod_ref, wg_ref, wu_ref, wd_ref, fin_ref, *rest, n_ctx, seq_s, final):
    outs, (acc_ref, rank_c_ref, rank_r_ref, comb_t_ref) = rest[:-4], rest[-4:]
    e = pl.program_id(1)
    parts = [slice(p * MOE_PART, (p + 1) * MOE_PART) for p in range(MOE_TILE // MOE_PART)]

    @pl.when(e == 0)
    def _():
        acc_ref[...] = jnp.zeros_like(acc_ref)
        before = (lax.broadcasted_iota(jnp.int32, (MOE_PART, MOE_PART), 1)
                  < lax.broadcasted_iota(jnp.int32, (MOE_PART, MOE_PART), 0)).astype(BF16)
        for part in parts:
            comb = comb_ref[part, :]
            rank_c_ref[part, :] = _dot(before, (comb > 0.0).astype(BF16))
            comb_t = comb.T
            comb_t_ref[:, part] = comb_t
            rank_r_ref[:, part] = _dot_nt((comb_t > 0.0).astype(BF16), before)

    comb = comb_ref[...]
    lane = lax.broadcasted_iota(jnp.int32, comb.shape, 1)
    w_col = jnp.sum(jnp.where(lane == e, comb, 0.0), axis=-1, keepdims=True)
    rank_col = jnp.sum(jnp.where(lane == e, rank_c_ref[...], 0.0), axis=-1, keepdims=True)
    w_row = comb_t_ref[pl.ds(e, 1), :]
    rank_row = rank_r_ref[pl.ds(e, 1), :]
    count = functools.reduce(jnp.maximum, [jnp.sum((w_row[:, part] > 0.0).astype(jnp.int32)) for part in parts])

    def body(j, carry):
        base = (j * MOE_ROWS).astype(F32)
        slot_r = lax.broadcasted_iota(jnp.int32, (MOE_ROWS, MOE_PART), 0).astype(F32) + base
        slot_c = lax.broadcasted_iota(jnp.int32, (MOE_PART, MOE_ROWS), 1).astype(F32) + base
        rows = []
        for part in parts:
            hit = jnp.logical_and(rank_row[:, part] == slot_r, w_row[:, part] > 0.0)
            rows.append(_dot(jnp.where(hit, 1.0, 0.0).astype(BF16), h_ref[part, :]).astype(BF16))
        hg = jnp.concatenate(rows, axis=0)
        t = _silu(_dot(hg, wg_ref[0])) * _dot(hg, wu_ref[0])
        y = _dot(t.astype(BF16), wd_ref[0]).astype(BF16)
        for p, part in enumerate(parts):
            hit = jnp.logical_and(rank_col[part, :] == slot_c, w_col[part, :] > 0.0)
            acc_ref[part, :] += w_col[part, :] * _dot(jnp.where(hit, 1.0, 0.0).astype(BF16),
                                                      y[p * MOE_ROWS:(p + 1) * MOE_ROWS])
        return carry

    lax.fori_loop(0, (count + MOE_ROWS - 1) // MOE_ROWS, body, 0)

    @pl.when(e == N_EXPERTS - 1)
    def _():
        row = _mod_row(pl.program_id(0), MOE_TILE, n_ctx, seq_s)
        g2 = mod_ref[pl.ds(row, 1), 5 * D_MODEL:6 * D_MODEL]
        y = x_ref[...] + g2 * acc_ref[...]
        if not final:
            outs[0][...] = y
        else:
            y = y * lax.rsqrt(jnp.mean(y * y, axis=-1, keepdims=True) + RMS_EPS) * fin_ref[...]
            is_ctx = pl.program_id(0) < n_ctx // MOE_TILE

            @pl.when(is_ctx)
            def _():
                outs[0][...] = y

            @pl.when(jnp.logical_not(is_ctx))
            def _():
                outs[1][...] = y


def _ffn_moe(x1, h2, comb, mod, w_gate, w_up, w_down, n_ctx, seq_s, final_gain=None):
    n = x1.shape[0]
    final = final_gain is not None
    once = pl.Buffered(1)
    tile = lambda w: pl.BlockSpec((MOE_TILE, w), lambda i, e: (i, 0), pipeline_mode=once)
    if final:
        n_ctx_tiles = n_ctx // MOE_TILE
        out_specs = [pl.BlockSpec((MOE_TILE, D_MODEL), lambda i, e: (jnp.minimum(i, n_ctx_tiles - 1), 0),
                                  pipeline_mode=once),
                     pl.BlockSpec((MOE_TILE, D_MODEL), lambda i, e: (jnp.maximum(i - n_ctx_tiles, 0), 0),
                                  pipeline_mode=once)]
        out_shape = [jax.ShapeDtypeStruct((n_ctx, D_MODEL), F32), jax.ShapeDtypeStruct((n - n_ctx, D_MODEL), F32)]
        gain = final_gain.reshape(1, D_MODEL)
    else:
        out_specs = [tile(D_MODEL)]
        out_shape = [jax.ShapeDtypeStruct((n, D_MODEL), F32)]
        gain = jnp.ones((1, D_MODEL), F32)
    out = pl.pallas_call(
        functools.partial(_moe_kernel, n_ctx=n_ctx, seq_s=seq_s, final=final),
        grid=(n // MOE_TILE, N_EXPERTS),
        in_specs=[tile(D_MODEL), tile(D_MODEL), tile(LANES),
                  pl.BlockSpec((8, 6 * D_MODEL), lambda i, e: (0, 0)),
                  pl.BlockSpec((1, D_MODEL, D_FF_E), lambda i, e: (e, 0, 0)),
                  pl.BlockSpec((1, D_MODEL, D_FF_E), lambda i, e: (e, 0, 0)),
                  pl.BlockSpec((1, D_FF_E, D_MODEL), lambda i, e: (e, 0, 0)),
                  pl.BlockSpec((1, D_MODEL), lambda i, e: (0, 0))],
        out_specs=out_specs,
        out_shape=out_shape,
        scratch_shapes=[pltpu.VMEM((MOE_TILE, D_MODEL), F32), pltpu.VMEM((MOE_TILE, LANES), F32),
                        pltpu.VMEM((LANES, MOE_TILE), F32), pltpu.VMEM((LANES, MOE_TILE), F32)],
        compiler_params=_cparams(("arbitrary", "arbitrary") if final else ("parallel", "arbitrary")),
        name="ffn_moe",
    )(x1, h2, comb, mod, w_gate.astype(BF16), w_up.astype(BF16), w_down.astype(BF16), gain)
    return tuple(out) if final else out[0]


def _final_kernel(x_ref, g_ref, o_ref):
    x = x_ref[...]
    o_ref[...] = x * lax.rsqrt(jnp.mean(x * x, axis=-1, keepdims=True) + RMS_EPS) * g_ref[...]


def _final_norm(x, gain, row0, n_rows):
    blk0 = row0 // FFN_TILE
    return pl.pallas_call(
        _final_kernel,
        grid=(n_rows // FFN_TILE,),
        in_specs=[pl.BlockSpec((FFN_TILE, D_MODEL), lambda i: (blk0 + i, 0)),
                  pl.BlockSpec((1, D_MODEL), lambda i: (0, 0))],
        out_specs=pl.BlockSpec((FFN_TILE, D_MODEL), lambda i: (i, 0)),
        out_shape=jax.ShapeDtypeStruct((n_rows, D_MODEL), F32),
        compiler_params=_cparams(("parallel",)),
        name="final_norm",
    )(x, gain.reshape(1, D_MODEL))


def kernel(x_prompt, x_sample, state_rwkv, c, c_ctx, norm1, norm2, w_ada, b_ada, w_in, dw_w, dw_b, conv_ln_g, conv_ln_b, w_conv_out, w_fno_out, shift_mu, g_up, dec_w0, dec_up, iclr_a0, iclr_up, k_k, k_a, r_k, gn_g, gn_b, w_rwkv_out, w_o, ffn_w_gate, ffn_w_up, ffn_w_down, w_router, b_router, moe_w_gate, moe_w_up, moe_w_down, final_norm):
    b_p, t_p, _ = x_prompt.shape
    b_s, t_s, _ = x_sample.shape
    depth = w_in.shape[0]
    n_ctx = b_p * t_p
    n_lat = b_s * t_s
    assert t_p == ROW_TILE and t_s % MOE_TILE == 0 and n_ctx % MOE_TILE == 0 and b_s <= CTX_ROW
    assert MOE_TILE % FFN_TILE == 0 and n_ctx % WIDE_TILE == 0 and t_s % WIDE_TILE == 0
    assert t_s % GRID_W == 0 and CHUNK == GRID_W

    n = n_ctx + n_lat
    x = (x_prompt.reshape(n_ctx, D_MODEL), x_sample.reshape(n_lat, D_MODEL))
    cond = jnp.zeros((8, D_MODEL), F32).at[:b_s].set(c).at[CTX_ROW].set(c_ctx)
    mods = _ada(cond, w_ada, b_ada)
    table = _scan_table(b_p, t_p, b_s, t_s)
    zero_state = jnp.zeros((b_p, 2, HEAD_DIM, C_RWKV), F32)
    pack_state = lambda s: s.transpose(0, 1, 3, 2, 4).reshape(s.shape[0], 2, HEAD_DIM, C_RWKV)
    unpack_state = lambda s: s.reshape(s.shape[0], 2, HEAD_DIM, N_HEADS, HEAD_DIM).transpose(0, 1, 3, 2, 4)

    ctx_states = []
    for l in range(depth):
        mod = mods[l]
        u_conv, u_fno, u_rw = _inproj(x, n, mod, norm1[l], w_in[l, :, :OFF_GATE].astype(BF16), n_ctx, t_s)
        conv_h = _conv_branch(u_conv, dw_w[l], dw_b[l], conv_ln_g[l], conv_ln_b[l], n_ctx)
        fno_h = (_fno_branch(u_fno, 0, b_p, t_p), _fno_branch(u_fno, n_ctx, b_s, t_s))
        r, k, v, g, *summaries = _rwkv_project_and_summarise(u_rw, shift_mu[l], g_up[l], dec_w0[l], dec_up[l],
                                                             iclr_a0[l], iclr_up[l], k_k[l], k_a[l], n_ctx, t_s)
        s0 = jnp.concatenate([zero_state, pack_state(state_rwkv[:, l])], axis=0)
        yf, yb, s_fin = _carry_states(*summaries, s0, table)
        ctx_states.append(unpack_state(s_fin[:b_p]))
        i = l // 2
        x1, h2, *comb = _merge(x, n, conv_h, fno_h, yf, yb, r, k, v, g, norm1[l], w_in[l, :, OFF_GATE:].astype(BF16), mod, w_conv_out[l], w_fno_out[l],
                              w_rwkv_out[l], w_o[l], gn_g[l], gn_b[l], r_k[l], norm2[l], n_ctx, t_s,
                               router=(w_router[i], b_router[i]) if l % 2 == 1 else None)
        if l % 2 == 0:
            x = _ffn_dense(x1, h2, mod, ffn_w_gate[i], ffn_w_up[i], ffn_w_down[i], n_ctx, t_s)
        else:
            x = _ffn_moe(x1, h2, comb[0], mod, moe_w_gate[i], moe_w_up[i], moe_w_down[i], n_ctx, t_s,
                         final_gain=final_norm if l == depth - 1 else None)

    if isinstance(x, tuple):
        y_prompt, y_sample = x
    else:
        y_prompt, y_sample = _final_norm(x, final_norm, 0, n_ctx), _final_norm(x, final_norm, n_ctx, n_lat)
    y_prompt = y_prompt.reshape(b_p, t_p, D_MODEL)
    y_sample = y_sample.reshape(b_s, t_s, D_MODEL)
    new_state = jnp.stack(ctx_states, axis=1).astype(x_prompt.dtype)
    return (y_prompt, y_sample, new_state)
```

```python
import functools

import numpy as np
import jax
import jax.numpy as jnp
from jax import lax
from jax.experimental import pallas as pl
from jax.experimental.pallas import tpu as pltpu

F32 = jnp.float32
BF16 = jnp.bfloat16

D_MODEL = 1024
GRID_W = 64
C_CONV = 256
CONV_W = 31
C_FNO = 256
FNO_GW = 64
N_HEADS = 8
HEAD_DIM = 64
C_RWKV = N_HEADS * HEAD_DIM
G_RANK = 128
W_RANK = 64
A_RANK = 64
D_FF = 2816
N_EXPERTS = 8
D_FF_E = 1408
RMS_EPS = 1e-6
LN_EPS = 1e-5
GN_EPS = 64e-5

OFF_FNO = 2 * C_CONV
OFF_RWKV = OFF_FNO + C_FNO
N_SHIFT = 3 * C_RWKV + G_RANK + 2 * (W_RANK + A_RANK)
OFF_GATE = OFF_RWKV + N_SHIFT
D_IN = OFF_GATE + 3 * D_MODEL

ROW_TILE = 256
FFN_TILE = 1024
WIDE_TILE = 512
CHUNK = 64
GROUP = 4
GROUP_W = GROUP * HEAD_DIM
SUMMARY_CHUNKS = 4
CARRY_CHUNKS = 4
MOE_TILE = 1024
MOE_PART = 512
MOE_ROWS = 160
CTX_ROW = 4
LANES = 128
VMEM_LIMIT = 56 * 1024 * 1024


def _cparams(sem):
    return pltpu.CompilerParams(dimension_semantics=sem, vmem_limit_bytes=VMEM_LIMIT)


def _sigmoid(x):
    return 1.0 / (1.0 + jnp.exp(-x))


def _silu(x):
    return x * _sigmoid(x)


def _dot(a, b):
    return jnp.dot(a, b, preferred_element_type=F32)


def _split(x):
    hi = x.astype(BF16)
    lo = (x - hi.astype(F32)).astype(BF16)
    return hi, lo


def _dot_x3(a, b):
    ah, al = _split(a)
    bh, bl = _split(b)
    return _dot(ah, bh) + (_dot(ah, bl) + _dot(al, bh))


def _seg_sum(x, bd):
    return _dot(x.astype(BF16), bd)


def _mod_row(i, tile, n_ctx_rows, seq_s):
    n_ctx_tiles = n_ctx_rows // tile
    return jnp.where(i < n_ctx_tiles, CTX_ROW, (i - n_ctx_tiles) // (seq_s // tile))


def _stream_specs(tile, width, n_ctx, lat_block0):
    n_ctx_tiles = n_ctx // tile
    return [pl.BlockSpec((tile, width), lambda i: (jnp.minimum(i, n_ctx_tiles - 1), 0)),
            pl.BlockSpec((tile, width), lambda i: (jnp.maximum(i - n_ctx_tiles, 0) + lat_block0, 0))]


def _stream_args(x, tile, n_ctx):
    if isinstance(x, tuple):
        return x[0], x[1], 0
    return x, x, n_ctx // tile


def _pick_stream(i, tile, n_ctx, ctx_ref, lat_ref):
    return jnp.where(i < n_ctx // tile, ctx_ref[...], lat_ref[...])


def _ada_kernel(c_ref, w_ref, b_ref, o_ref):
    s = _silu(c_ref[...])
    o_ref[0] = _dot(s.astype(BF16), w_ref[0].astype(BF16)) + b_ref[0]


def _ada(cond, w_ada, b_ada):
    n_l = w_ada.shape[0]
    tn = 1536
    return pl.pallas_call(
        _ada_kernel,
        grid=(n_l, 6 * D_MODEL // tn),
        in_specs=[
            pl.BlockSpec((8, D_MODEL), lambda l, j: (0, 0)),
            pl.BlockSpec((1, D_MODEL, tn), lambda l, j: (l, 0, j)),
            pl.BlockSpec((1, 1, tn), lambda l, j: (l, 0, j)),
        ],
        out_specs=pl.BlockSpec((1, 8, tn), lambda l, j: (l, 0, j)),
        out_shape=jax.ShapeDtypeStruct((n_l, 8, 6 * D_MODEL), F32),
        compiler_params=_cparams(("parallel", "parallel")),
        name="ada",
    )(cond, w_ada, b_ada.reshape(n_l, 1, 6 * D_MODEL))


def _inproj_kernel(xc_ref, xl_ref, mod_ref, g_ref, w_ref, oc_ref, of_ref, or_ref, *, n_ctx, seq_s):
    row = _mod_row(pl.program_id(0), WIDE_TILE, n_ctx, seq_s)
    m = mod_ref[pl.ds(row, 1), :]
    sh = m[:, 0:D_MODEL]
    sc = m[:, D_MODEL:2 * D_MODEL]
    x = _pick_stream(pl.program_id(0), WIDE_TILE, n_ctx, xc_ref, xl_ref)
    y = x * lax.rsqrt(jnp.mean(x * x, axis=-1, keepdims=True) + RMS_EPS) * g_ref[...]
    h = (y * (1.0 + sc) + sh).astype(BF16)
    oc_ref[...] = _dot(h, w_ref[:, 0:OFF_FNO])
    of_ref[...] = _dot(h, w_ref[:, OFF_FNO:OFF_RWKV])
    or_ref[...] = _dot(h, w_ref[:, OFF_RWKV:OFF_GATE])


def _inproj(x, n, mod, gain, w_in, n_ctx, seq_s):
    xc, xl, lat0 = _stream_args(x, WIDE_TILE, n_ctx)
    widths = (OFF_FNO, C_FNO, N_SHIFT)
    return pl.pallas_call(
        functools.partial(_inproj_kernel, n_ctx=n_ctx, seq_s=seq_s),
        grid=(n // WIDE_TILE,),
        in_specs=_stream_specs(WIDE_TILE, D_MODEL, n_ctx, lat0) + [
            pl.BlockSpec((8, 6 * D_MODEL), lambda i: (0, 0)),
            pl.BlockSpec((1, D_MODEL), lambda i: (0, 0)),
            pl.BlockSpec((D_MODEL, OFF_GATE), lambda i: (0, 0), pipeline_mode=pl.Buffered(1)),
        ],
        out_specs=[pl.BlockSpec((WIDE_TILE, w), lambda i: (i, 0)) for w in widths],
        out_shape=[jax.ShapeDtypeStruct((n, w), F32) for w in widths],
        compiler_params=_cparams(("parallel",)),
        name="inproj",
    )(xc, xl, mod, gain.reshape(1, D_MODEL), w_in)


_CONV_HALO = 16
_CONV_ROWS = 64


def _conv_kernel(u_ref, w_ref, b_ref, g_ref, be_ref, o_ref, pad_ref, sh_ref, *, n_ctx):
    is_ctx = pl.program_id(0) < n_ctx // ROW_TILE
    zeros = jnp.zeros((_CONV_HALO, C_CONV), F32)
    n_parts = ROW_TILE // _CONV_ROWS

    def glu(lo, hi):
        return u_ref[lo:hi, 0:C_CONV] * _sigmoid(u_ref[lo:hi, C_CONV:2 * C_CONV])

    def finish(starts, n_rows):
        for s in range(8):
            sh_ref[s, 0:n_rows - 8, :] = pad_ref[s:s + n_rows - 8, :]
        for p in range(n_parts):
            acc = jnp.zeros((_CONV_ROWS, C_CONV), F32)
            for j in range(CONV_W):
                o = starts[p] + j
                acc = acc + w_ref[j:j + 1, :] * sh_ref[o % 8, o - o % 8:o - o % 8 + _CONV_ROWS, :]
            acc = acc + b_ref[...]
            mu = jnp.mean(acc, axis=-1, keepdims=True)
            d = acc - mu
            var = jnp.mean(d * d, axis=-1, keepdims=True)
            y = d * lax.rsqrt(var + LN_EPS) * g_ref[...] + be_ref[...]
            o_ref[p * _CONV_ROWS:(p + 1) * _CONV_ROWS, :] = _silu(y).astype(o_ref.dtype)

    shift = _CONV_HALO - CONV_W // 2

    @pl.when(is_ctx)
    def _():
        pad_ref[0:_CONV_HALO, :] = zeros
        pad_ref[_CONV_HALO:_CONV_HALO + ROW_TILE, :] = glu(0, ROW_TILE)
        pad_ref[_CONV_HALO + ROW_TILE:2 * _CONV_HALO + ROW_TILE, :] = zeros
        finish([shift + p * _CONV_ROWS for p in range(n_parts)], ROW_TILE + 2 * _CONV_HALO)

    @pl.when(jnp.logical_not(is_ctx))
    def _():
        stride = GRID_W + 2 * _CONV_HALO
        for p in range(n_parts):
            pad_ref[p * stride:p * stride + _CONV_HALO, :] = zeros
            pad_ref[p * stride + _CONV_HALO:p * stride + _CONV_HALO + GRID_W, :] = glu(p * GRID_W, (p + 1) * GRID_W)
            pad_ref[p * stride + _CONV_HALO + GRID_W:(p + 1) * stride, :] = zeros
        finish([p * stride + shift for p in range(n_parts)], n_parts * stride)


def _conv_branch(u_conv, dw_w, dw_b, ln_g, ln_b, n_ctx):
    n = u_conv.shape[0]
    assert _CONV_ROWS == GRID_W and ROW_TILE % GRID_W == 0
    vec = pl.BlockSpec((1, C_CONV), lambda i: (0, 0))
    return pl.pallas_call(
        functools.partial(_conv_kernel, n_ctx=n_ctx),
        grid=(n // ROW_TILE,),
        in_specs=[
            pl.BlockSpec((ROW_TILE, 2 * C_CONV), lambda i: (i, 0)),
            pl.BlockSpec((CONV_W, C_CONV), lambda i: (0, 0)),
            vec, vec, vec,
        ],
        out_specs=pl.BlockSpec((ROW_TILE, C_CONV), lambda i: (i, 0)),
        out_shape=jax.ShapeDtypeStruct((n, C_CONV), BF16),
        scratch_shapes=[pltpu.VMEM((ROW_TILE // GRID_W * (GRID_W + 2 * _CONV_HALO), C_CONV), F32),
                        pltpu.VMEM((8, ROW_TILE // GRID_W * (GRID_W + 2 * _CONV_HALO), C_CONV), F32)],
        compiler_params=_cparams(("parallel",)),
        name="conv_branch",
    )(u_conv, dw_w, dw_b.reshape(1, C_CONV), ln_g.reshape(1, C_CONV), ln_b.reshape(1, C_CONV))


def _dft_tables(t_len):
    def cs(n):
        k = np.arange(n, dtype=np.int64)
        ang = 2.0 * np.pi * ((k[:, None] * k[None, :]) % n).astype(np.float64) / n
        return np.cos(ang), np.sin(ang)
    cg, sg = cs(FNO_GW)
    eye = np.eye(C_FNO // FNO_GW)
    w1 = np.concatenate([np.kron(eye, cg), np.kron(eye, sg)], axis=1)
    ct, st = cs(t_len)
    w2 = np.concatenate([ct, -st], axis=1)
    return jnp.asarray(w1, dtype=F32).astype(BF16), jnp.asarray(w2, dtype=F32).astype(BF16)


def _fno_kernel(u_ref, w1_ref, w2_ref, o_ref, hs_ref, *, t_len, scale):
    @pl.when(pl.program_id(1) == 0)
    def _():
        hc = _dot(u_ref[...].astype(BF16), w1_ref[...])
        hs_ref[0:t_len, :] = hc[:, 0:C_FNO].astype(BF16)
        hs_ref[t_len:2 * t_len, :] = hc[:, C_FNO:2 * C_FNO].astype(BF16)

    o_ref[...] = (_dot(w2_ref[...], hs_ref[...]) * scale).astype(o_ref.dtype)


def _fno_branch(u_fno, row0, n_seq, t_len):
    w1, w2 = _dft_tables(t_len)
    tk = t_len
    blk0 = row0 // t_len
    return pl.pallas_call(
        functools.partial(_fno_kernel, t_len=t_len, scale=float(1.0 / np.sqrt(t_len * FNO_GW))),
        grid=(n_seq, t_len // tk),
        in_specs=[
            pl.BlockSpec((t_len, C_FNO), lambda b, j: (blk0 + b, 0)),
            pl.BlockSpec((C_FNO, 2 * C_FNO), lambda b, j: (0, 0)),
            pl.BlockSpec((tk, 2 * t_len), lambda b, j: (j, 0), pipeline_mode=pl.Buffered(1)),
        ],
        out_specs=pl.BlockSpec((tk, C_FNO), lambda b, j: (b * (t_len // tk) + j, 0)),
        out_shape=jax.ShapeDtypeStruct((n_seq * t_len, C_FNO), BF16),
        scratch_shapes=[pltpu.VMEM((2 * t_len, C_FNO), BF16)],
        compiler_params=_cparams(("parallel", "arbitrary")),
        name="fno_branch",
    )(u_fno, w1, w2)


def _prep_pieces(i, z_ref, zp_ref, zn_ref, mu_ref, gup_ref, w0_ref, dup_ref, a0_ref, aup_ref, kk_ref, bd_ref,
                 r_refs, k_refs, v_refs, kkn_ref, g_ref, lw_ref, a_ref, pad_ref, n_ctx, seq_s):
    n_ctx_tiles = n_ctx // ROW_TILE
    per_seq = seq_s // ROW_TILE
    j = (i - n_ctx_tiles) % per_seq
    first = jnp.logical_or(i < n_ctx_tiles, j == 0)
    last = jnp.logical_or(i < n_ctx_tiles, j == per_seq - 1)
    c = C_RWKV

    def shifted(lo, hi):
        z = pad_ref[8:8 + ROW_TILE, lo:hi]
        zp = pad_ref[7:7 + ROW_TILE, lo:hi]
        zn = pad_ref[9:9 + ROW_TILE, lo:hi]
        return z + mu_ref[:, lo:hi] * (0.5 * (zp + zn) - z)

    def fill():
        pad_ref[8:8 + ROW_TILE, :] = z_ref[...]
        pad_ref[0:8, :] = jnp.where(first, 0.0, zp_ref[...])
        pad_ref[8 + ROW_TILE:16 + ROW_TILE, :] = jnp.where(last, 0.0, zn_ref[...])

    def receptance():
        r = shifted(0, c)
        for ref in r_refs:
            ref[...] = r

    def key():
        k = shifted(c, 2 * c)
        for ref in k_refs:
            ref[...] = k
        kx = k * kk_ref[...]
        nrm = jnp.sqrt(_seg_sum(kx * kx, bd_ref[...]))
        kkn_ref[...] = kx / jnp.maximum(nrm, 1e-12)

    def value():
        v = shifted(2 * c, 3 * c)
        for ref in v_refs:
            ref[...] = v

    def gate():
        o = 3 * c
        g_ref[...] = _dot(_sigmoid(shifted(o, o + G_RANK)).astype(BF16), gup_ref[...])

    def decay():
        o = 3 * c + G_RANK
        xw = _dot(jnp.tanh(shifted(o, o + 2 * W_RANK)).astype(BF16), dup_ref[...]) + w0_ref[...]
        soft = jnp.maximum(-xw, 0.0) + jnp.log(1.0 + jnp.exp(-jnp.abs(xw)))
        lw_ref[...] = -jnp.exp(-soft - 0.5)

    def iclr():
        o = 3 * c + G_RANK + 2 * W_RANK
        xa = _dot(shifted(o, o + 2 * A_RANK).astype(BF16), aup_ref[...]) + a0_ref[...]
        a_ref[...] = _sigmoid(xa)

    return [fill, receptance, key, value, gate, decay, iclr]


def _block_diag2(w):
    z = jnp.zeros_like(w[0])
    return jnp.concatenate([jnp.concatenate([w[0], z], axis=1), jnp.concatenate([z, w[1]], axis=1)], axis=0)


def _head_ones():
    return jnp.asarray(np.kron(np.eye(N_HEADS), np.ones((HEAD_DIM, HEAD_DIM))), dtype=BF16)


def _block_diag(x, mask):
    xb = x.astype(BF16)
    return jnp.where(mask, jnp.concatenate([xb] * GROUP, axis=0), jnp.zeros((), BF16))


def _dot_nt(a, b):
    return lax.dot_general(a, b, (((1,), (1,)), ((), ())), preferred_element_type=F32)


def _diag_blocks(prod, lane_head):
    out = jnp.where(lane_head == 0, prod[0:HEAD_DIM], 0.0)
    for h in range(1, GROUP):
        out = out + jnp.where(lane_head == h, prod[h * HEAD_DIM:(h + 1) * HEAD_DIM], 0.0)
    return out


def _summary_body(r_ref, k_ref, v_ref, n_ref, lw_ref, a_ref, ka_ref, q_ref, y0_ref, p_ref, z_ref, g_ref, fillers=()):
    fillers = list(fillers)
    fill_one = lambda: fillers.pop(0)() if fillers else None
    row = lax.broadcasted_iota(jnp.int32, (CHUNK, GROUP_W), 0)
    lane = lax.broadcasted_iota(jnp.int32, (CHUNK, GROUP_W), 1)
    col = lane % CHUNK
    lane_head = lane // HEAD_DIM
    bd_mask = (lax.broadcasted_iota(jnp.int32, (GROUP * CHUNK, GROUP_W), 0) // CHUNK
               == lax.broadcasted_iota(jnp.int32, (GROUP * CHUNK, GROUP_W), 1) // HEAD_DIM)
    bd = lambda x: _block_diag(x, bd_mask)
    ka = ka_ref[...]

    units = []
    for j in range(SUMMARY_CHUNKS):
        rows = slice(j * CHUNK, (j + 1) * CHUNK)
        r = r_ref[rows, :]
        k = k_ref[rows, :]
        v = v_ref[rows, :]
        kkn = n_ref[rows, :]
        for d in range(2):
            lanes = slice(d * C_RWKV, (d + 1) * C_RWKV)
            earlier = (col < row) if d == 0 else (col > row)
            upto = jnp.logical_or(earlier, row == col)
            tri = upto[:, 0:CHUNK].astype(BF16)
            lw = lw_ref[rows, lanes]
            h1 = lw.astype(BF16)
            r1 = lw - h1.astype(F32)
            h2 = r1.astype(BF16)
            h3 = (r1 - h2.astype(F32)).astype(BF16)
            cum = _dot(tri, h1) + (_dot(tri, h2) + _dot(tri, h3))
            e_in = jnp.exp(cum)
            e_ex = jnp.exp(cum - lw)
            e_ng = jnp.exp(-cum)
            a = a_ref[rows, lanes]
            kd = k * (1.0 + (a - 1.0) * ka)
            at = -kkn * e_ex
            rt = r * e_in
            bt = kkn * a * e_ng
            kt = kd * e_ng
            end = CHUNK - 1 if d == 0 else 0
            g_end = e_in[end:end + 1, :]
            g_ref[j, :, lanes] = g_end
            bh = bt * g_end
            kh = kt * g_end
            for q in range(C_RWKV // GROUP_W):
                sl = slice(q * GROUP_W, (q + 1) * GROUP_W)
                units.append(dict(rows=rows, out=slice(d * C_RWKV + q * GROUP_W, d * C_RWKV + (q + 1) * GROUP_W),
                                  earlier=earlier, upto=upto, at=at[:, sl], rt=rt[:, sl],
                                  ar=jnp.concatenate([at[:, sl], rt[:, sl]], axis=0).astype(BF16),
                                  bt=bt[:, sl], kt=kt[:, sl], v=v[:, sl], bh=bh[:, sl].astype(BF16),
                                  bk=jnp.concatenate([bh[:, sl], kh[:, sl]], axis=0).astype(BF16)))

    for u in units:
        sb = _dot_nt(u["ar"], bd(u["bt"]))
        sk = _dot_nt(u["ar"], bd(u["kt"]))
        u["lab"] = jnp.where(u["earlier"], sb[0:CHUNK], 0.0)
        u["mrb"] = jnp.where(u["upto"], sb[CHUNK:2 * CHUNK], 0.0).astype(BF16)
        u["lm"] = jnp.concatenate([jnp.where(u["earlier"], sk[0:CHUNK], 0.0),
                                   jnp.where(u["upto"], sk[CHUNK:2 * CHUNK], 0.0)], axis=0).astype(BF16)

    fill_one()
    eye = (row == col).astype(F32)
    pair = jnp.logical_and(row // 2 == col // 2, row != col)
    for u in units:
        u["t"] = eye + jnp.where(pair, u["lab"], 0.0)
    n = 2
    while n < CHUNK:
        m = jnp.logical_and(row // (2 * n) == col // (2 * n), row // n != col // n)
        for u in units:
            u["w"] = _dot(jnp.where(m, u["lab"], 0.0).astype(BF16), bd(u["t"]))
        for u in units:
            u["t"] = u["t"] + _dot(u["t"].astype(BF16), bd(u["w"]))
        fill_one()
        n *= 2

    for u in units:
        u["t"] = u["t"].astype(BF16)
        u["wm"] = _dot(u["t"], bd(u["at"]))
        u["lv"] = _dot(u["lm"], bd(u["v"]))
    for u in units:
        u["u0"] = _dot(u["t"], bd(u["lv"][0:CHUNK]))
    for u in units:
        q_ref[u["rows"], u["out"]] = (u["rt"] + _dot(u["mrb"], bd(u["wm"]))).astype(q_ref.dtype)
        y0_ref[u["rows"], u["out"]] = u["lv"][CHUNK:2 * CHUNK] + _dot(u["mrb"], bd(u["u0"]))
    for u in units:
        p = _dot(u["wm"].T.astype(BF16), u["bh"])
        p_ref[u["rows"], u["out"]] = _diag_blocks(p, lane_head).astype(p_ref.dtype)
        z = _dot(jnp.concatenate([u["u0"], u["v"]], axis=0).T.astype(BF16), u["bk"])
        z_ref[u["rows"], u["out"]] = _diag_blocks(z, lane_head)
    while fillers:
        fill_one()


def _rwkv_kernel(z_ref, zp_ref, zn_ref, mu_ref, gup_ref, w0_ref, dup_ref, a0_ref, aup_ref, kk_ref, bd_ref, ka_ref,
                 r_ref, k_ref, v_ref, g_ref, q_ref, y0_ref, p_ref, zz_ref, gd_ref,
                 pad_ref, r_n, k_n, v_n, n_n, lw_n, a_n, r_c, k_c, v_c, n_c, lw_c, a_c, *, n_ctx, seq_s, n_tiles):
    s = pl.program_id(0)
    new = (r_n, k_n, v_n, n_n, lw_n, a_n)
    cur = (r_c, k_c, v_c, n_c, lw_c, a_c)

    @pl.when(s == 0)
    def _():
        for ref in cur:
            ref[...] = jnp.zeros_like(ref)

    pieces = _prep_pieces(jnp.minimum(s, n_tiles - 1), z_ref, zp_ref, zn_ref, mu_ref, gup_ref, w0_ref, dup_ref,
                          a0_ref, aup_ref, kk_ref, bd_ref, (r_n, r_ref), (k_n, k_ref), (v_n, v_ref), n_n, g_ref,
                          lw_n, a_n, pad_ref, n_ctx, seq_s)
    pieces[0]()
    _summary_body(r_c, k_c, v_c, n_c, lw_c, a_c, ka_ref, q_ref, y0_ref, p_ref, zz_ref, gd_ref, fillers=pieces[1:])
    for dst, src in zip(cur, new):
        dst[...] = src[...]


def _rwkv_project_and_summarise(u_rw, mu, g_up, dec_w0, dec_up, iclr_a0, iclr_up, k_k, k_a, n_ctx, seq_s):
    n = u_rw.shape[0]
    assert ROW_TILE == SUMMARY_CHUNKS * CHUNK and CHUNK == HEAD_DIM and C_RWKV % GROUP_W == 0
    n_tiles = n // ROW_TILE
    halo = ROW_TILE // 8
    last_blk = n // 8 - 1
    c = C_RWKV
    full = lambda shape: pl.BlockSpec(shape, lambda s: (0,) * len(shape))
    this = lambda s: jnp.minimum(s, n_tiles - 1)
    prev = lambda s: jnp.maximum(s - 1, 0)
    proj = lambda w: pl.BlockSpec((ROW_TILE, w), lambda s: (this(s), 0))
    summ = lambda w: pl.BlockSpec((ROW_TILE, w), lambda s: (prev(s), 0))
    vm = lambda w: pltpu.VMEM((ROW_TILE, w), F32)
    return pl.pallas_call(
        functools.partial(_rwkv_kernel, n_ctx=n_ctx, seq_s=seq_s, n_tiles=n_tiles),
        grid=(n_tiles + 1,),
        in_specs=[
            proj(N_SHIFT),
            pl.BlockSpec((8, N_SHIFT), lambda s: (jnp.maximum(this(s) * halo - 1, 0), 0)),
            pl.BlockSpec((8, N_SHIFT), lambda s: (jnp.minimum((this(s) + 1) * halo, last_blk), 0)),
            full((1, N_SHIFT)), full((G_RANK, c)), full((1, 2 * c)), full((2 * W_RANK, 2 * c)),
            full((1, 2 * c)), full((2 * A_RANK, 2 * c)), full((1, c)), full((c, c)), full((1, c)),
        ],
        out_specs=[proj(c), proj(c), proj(c), proj(c), summ(2 * c), summ(2 * c), summ(2 * c), summ(2 * c),
                   pl.BlockSpec((SUMMARY_CHUNKS, 1, 2 * c), lambda s: (prev(s), 0, 0))],
        out_shape=[jax.ShapeDtypeStruct((n, c), F32)] * 4
        + [jax.ShapeDtypeStruct((n, 2 * c), BF16), jax.ShapeDtypeStruct((n, 2 * c), F32),
           jax.ShapeDtypeStruct((n, 2 * c), BF16), jax.ShapeDtypeStruct((n, 2 * c), F32),
           jax.ShapeDtypeStruct((n // CHUNK, 1, 2 * c), F32)],
        scratch_shapes=[pltpu.VMEM((ROW_TILE + 16, N_SHIFT), F32)]
        + [vm(c), vm(c), vm(c), vm(c), vm(2 * c), vm(2 * c)] * 2,
        compiler_params=_cparams(("arbitrary",)),
        name="rwkv_project_summarise",
    )(u_rw, u_rw, u_rw, mu.reshape(1, N_SHIFT), g_up.astype(BF16), dec_w0.reshape(1, 2 * c),
      _block_diag2(dec_up).astype(BF16), iclr_a0.reshape(1, 2 * c), _block_diag2(iclr_up).astype(BF16),
      k_k.reshape(1, c), _head_ones(), k_a.reshape(1, c))


def _carry_kernel(tab_ref, qf_ref, y0f_ref, pf_ref, zf_ref, gf_ref, qb_ref, y0b_ref, pb_ref, zb_ref, gb_ref,
                  s0_ref, yf_ref, yb_ref, sfin_ref, st_ref):
    step = pl.program_id(0)

    @pl.when(tab_ref[3, step] == 1)
    def _():
        st_ref[...] = s0_ref[0]

    bd_mask = (lax.broadcasted_iota(jnp.int32, (GROUP * CHUNK, GROUP_W), 0) // CHUNK
               == lax.broadcasted_iota(jnp.int32, (GROUP * CHUNK, GROUP_W), 1) // HEAD_DIM)
    for i in range(CARRY_CHUNKS):
        for d, (q_ref, y0_ref, p_ref, z_ref, g_ref, y_ref) in enumerate(
                ((qf_ref, y0f_ref, pf_ref, zf_ref, gf_ref, yf_ref), (qb_ref, y0b_ref, pb_ref, zb_ref, gb_ref, yb_ref))):
            j = i if d == 0 else CARRY_CHUNKS - 1 - i
            rows = slice(j * CHUNK, (j + 1) * CHUNK)
            for q in range(C_RWKV // GROUP_W):
                sl = slice(q * GROUP_W, (q + 1) * GROUP_W)
                s = st_ref[d, :, sl]
                y_ref[rows, sl] = y0_ref[rows, sl] + _dot_nt(q_ref[rows, sl], _block_diag(s, bd_mask))
                st_ref[d, :, sl] = (s * g_ref[j, :, sl] + z_ref[rows, sl]
                                    + _dot(s.astype(BF16), _block_diag(p_ref[rows, sl], bd_mask)))

    @pl.when(tab_ref[4, step] == 1)
    def _():
        sfin_ref[0] = st_ref[...]


def _scan_table(n_ctx_seq, t_ctx, n_s_seq, t_s):
    rows = []
    base = 0
    seq = 0
    for n_seq, t_len in ((n_ctx_seq, t_ctx), (n_s_seq, t_s)):
        assert t_len % (CARRY_CHUNKS * CHUNK) == 0
        n_c = t_len // (CARRY_CHUNKS * CHUNK)
        for _ in range(n_seq):
            for c in range(n_c):
                rows.append((base + c, base + n_c - 1 - c, seq, int(c == 0), int(c == n_c - 1)))
            base += n_c
            seq += 1
    return np.asarray(rows, dtype=np.int32).T.copy()


def _carry_states(q, y0, p, z, g, s0, table):
    n = q.shape[0]
    n_seq = s0.shape[0]
    c = C_RWKV

    def tok(which, d):
        return pl.BlockSpec((CARRY_CHUNKS * CHUNK, c), lambda s, tab: (tab[which, s], d))

    def decay(which, d):
        return pl.BlockSpec((CARRY_CHUNKS, 1, c), lambda s, tab: (tab[which, s], 0, d))

    state_spec = pl.BlockSpec((1, 2, HEAD_DIM, c), lambda s, tab: (tab[2, s], 0, 0, 0))
    grid_spec = pltpu.PrefetchScalarGridSpec(
        num_scalar_prefetch=1,
        grid=(table.shape[1],),
        in_specs=[tok(0, 0), tok(0, 0), tok(0, 0), tok(0, 0), decay(0, 0),
                  tok(1, 1), tok(1, 1), tok(1, 1), tok(1, 1), decay(1, 1), state_spec],
        out_specs=[tok(0, 0), tok(1, 0), state_spec],
        scratch_shapes=[pltpu.VMEM((2, HEAD_DIM, c), F32)],
    )
    return pl.pallas_call(
        _carry_kernel,
        grid_spec=grid_spec,
        out_shape=[jax.ShapeDtypeStruct((n, c), F32), jax.ShapeDtypeStruct((n, c), F32),
                   jax.ShapeDtypeStruct((n_seq, 2, HEAD_DIM, c), F32)],
        compiler_params=_cparams(("arbitrary",)),
        name="rwkv_carry",
    )(jnp.asarray(table), q, y0, p, z, g, q, y0, p, z, g, s0)


def _merge_kernel(xc_ref, xl_ref, c_ref, fc_ref, fl_ref, yf_ref, yb_ref, r_ref, k_ref, v_ref, g_ref, n1_ref, wgt_ref, mod_ref,
                  wc_ref, wf_ref, wr_ref, wo_ref, gng_ref, gnb_ref, rk_ref, bd_ref, n2_ref, *rest, n_ctx, seq_s, route):
    if route:
        wrt_ref, brt_ref, x1_ref, h2_ref, comb_ref = rest
    else:
        x1_ref, h2_ref = rest
    row = _mod_row(pl.program_id(0), WIDE_TILE, n_ctx, seq_s)
    m = mod_ref[pl.ds(row, 1), :]
    sh1 = m[:, 0:D_MODEL]
    sc1 = m[:, D_MODEL:2 * D_MODEL]
    g1 = m[:, 2 * D_MODEL:3 * D_MODEL]
    sh2 = m[:, 3 * D_MODEL:4 * D_MODEL]
    sc2 = m[:, 4 * D_MODEL:5 * D_MODEL]
    bd = bd_ref[...]
    inv = 1.0 / HEAD_DIM
    y = yf_ref[...] + yb_ref[...]
    y_hi, y_lo = _split(y)
    d = y - (_dot(y_hi, bd) + _dot(y_lo, bd)) * inv
    var = _seg_sum(d * d, bd) * inv
    yn = d * lax.rsqrt(var + GN_EPS) * gng_ref[...] + gnb_ref[...]
    v = v_ref[...]
    yn = yn + _seg_sum(r_ref[...] * k_ref[...] * rk_ref[...], bd) * v
    yr = _dot((yn * g_ref[...]).astype(BF16), wr_ref[...])
    yc = _dot(c_ref[...], wc_ref[...])
    yf = _dot(_pick_stream(pl.program_id(0), WIDE_TILE, n_ctx, fc_ref, fl_ref), wf_ref[...])
    x = _pick_stream(pl.program_id(0), WIDE_TILE, n_ctx, xc_ref, xl_ref)
    hn = x * lax.rsqrt(jnp.mean(x * x, axis=-1, keepdims=True) + RMS_EPS) * n1_ref[...]
    hn = (hn * (1.0 + sc1) + sh1).astype(BF16)
    dm = D_MODEL
    merged = (_sigmoid(_dot(hn, wgt_ref[:, 0:dm])) * yc + _sigmoid(_dot(hn, wgt_ref[:, dm:2 * dm])) * yf
              + _sigmoid(_dot(hn, wgt_ref[:, 2 * dm:3 * dm])) * yr)
    x1 = x + g1 * _dot(merged.astype(BF16), wo_ref[...])
    x1_ref[...] = x1
    h2 = x1 * lax.rsqrt(jnp.mean(x1 * x1, axis=-1, keepdims=True) + RMS_EPS) * n2_ref[...]
    h2 = h2 * (1.0 + sc2) + sh2
    h2_ref[...] = h2.astype(BF16)
    if not route:
        return
    logits = _dot_x3(h2, wrt_ref[...]) + brt_ref[...]
    lane = lax.broadcasted_iota(jnp.int32, logits.shape, 1).astype(F32)
    neg = jnp.float32(-jnp.inf)
    lg = jnp.where(lane < N_EXPERTS, logits, neg)
    m1 = jnp.max(lg, axis=-1, keepdims=True)
    i1 = jnp.min(jnp.where(lg == m1, lane, float(LANES)), axis=-1, keepdims=True)
    lg2 = jnp.where(lane == i1, neg, lg)
    m2 = jnp.max(lg2, axis=-1, keepdims=True)
    i2 = jnp.min(jnp.where(lg2 == m2, lane, float(LANES)), axis=-1, keepdims=True)
    e = jnp.exp(m2 - m1)
    p1 = 1.0 / (1.0 + e)
    comb_ref[...] = jnp.where(lane == i1, p1, 0.0) + jnp.where(lane == i2, e * p1, 0.0)


def _merge(x, n, conv_h, fno_h, yf, yb, r, k, v, g, gain1, w_gates, mod, w_conv_out, w_fno_out, w_rwkv_out, w_o,
           gn_g, gn_b, r_k, gain2, n_ctx, seq_s, router=None):
    xc, xl, x_lat0 = _stream_args(x, WIDE_TILE, n_ctx)
    fc, fl, f_lat0 = _stream_args(fno_h, WIDE_TILE, n_ctx)
    c = C_RWKV
    tile = lambda w: pl.BlockSpec((WIDE_TILE, w), lambda i: (i, 0))
    full = lambda shape: pl.BlockSpec(shape, lambda i: (0,) * len(shape), pipeline_mode=pl.Buffered(1))
    route = router is not None
    route_args, route_specs, route_out, route_shape = [], [], [], []
    if route:
        route_args = [jnp.zeros((D_MODEL, LANES), F32).at[:, :N_EXPERTS].set(router[0]),
                      jnp.zeros((1, LANES), F32).at[0, :N_EXPERTS].set(router[1])]
        route_specs = [full((D_MODEL, LANES)), full((1, LANES))]
        route_out = [tile(LANES)]
        route_shape = [jax.ShapeDtypeStruct((n, LANES), F32)]
    return pl.pallas_call(
        functools.partial(_merge_kernel, n_ctx=n_ctx, seq_s=seq_s, route=route),
        grid=(n // WIDE_TILE,),
        in_specs=_stream_specs(WIDE_TILE, D_MODEL, n_ctx, x_lat0) + [tile(C_CONV)]
        + _stream_specs(WIDE_TILE, C_FNO, n_ctx, f_lat0) + [
            tile(c), tile(c), tile(c), tile(c), tile(c), tile(c),
            full((1, D_MODEL)), full((D_MODEL, 3 * D_MODEL)), full((8, 6 * D_MODEL)),
            full((C_CONV, D_MODEL)), full((C_FNO, D_MODEL)), full((c, D_MODEL)), full((D_MODEL, D_MODEL)),
            full((1, c)), full((1, c)), full((1, c)), full((c, c)), full((1, D_MODEL)),
        ] + route_specs,
        out_specs=[tile(D_MODEL), tile(D_MODEL)] + route_out,
        out_shape=[jax.ShapeDtypeStruct((n, D_MODEL), F32), jax.ShapeDtypeStruct((n, D_MODEL), BF16)] + route_shape,
        compiler_params=_cparams(("parallel",)),
        name="merge",
    )(xc, xl, conv_h, fc, fl, yf, yb, r, k, v, g, gain1.reshape(1, D_MODEL), w_gates, mod,
      w_conv_out.astype(BF16), w_fno_out.astype(BF16), w_rwkv_out.astype(BF16), w_o.astype(BF16),
      gn_g.reshape(1, c), gn_b.reshape(1, c), r_k.reshape(1, c), _head_ones(), gain2.reshape(1, D_MODEL), *route_args)


def _ffn_kernel(x_ref, h_ref, mod_ref, wg_ref, wu_ref, wd_ref, o_ref, *, n_ctx, seq_s):
    row = _mod_row(pl.program_id(0), FFN_TILE, n_ctx, seq_s)
    g2 = mod_ref[pl.ds(row, 1), 5 * D_MODEL:6 * D_MODEL]
    h = h_ref[...]
    half = D_FF // 2
    acc = jnp.zeros((FFN_TILE, D_MODEL), F32)
    for p in range(2):
        sl = slice(p * half, (p + 1) * half)
        t = _silu(_dot(h, wg_ref[:, sl])) * _dot(h, wu_ref[:, sl])
        acc = acc + _dot(t.astype(BF16), wd_ref[sl, :])
    o_ref[...] = x_ref[...] + g2 * acc


def _ffn_dense(x1, h2, mod, w_gate, w_up, w_down, n_ctx, seq_s):
    n = x1.shape[0]
    tile = pl.BlockSpec((FFN_TILE, D_MODEL), lambda i: (i, 0))
    full = lambda shape: pl.BlockSpec(shape, lambda i: (0,) * len(shape), pipeline_mode=pl.Buffered(1))
    return pl.pallas_call(
        functools.partial(_ffn_kernel, n_ctx=n_ctx, seq_s=seq_s),
        grid=(n // FFN_TILE,),
        in_specs=[tile, tile, full((8, 6 * D_MODEL)), full((D_MODEL, D_FF)), full((D_MODEL, D_FF)),
                  full((D_FF, D_MODEL))],
        out_specs=tile,
        out_shape=jax.ShapeDtypeStruct((n, D_MODEL), F32),
        compiler_params=_cparams(("parallel",)),
        name="ffn_dense",
    )(x1, h2, mod, w_gate.astype(BF16), w_up.astype(BF16), w_down.astype(BF16))


def _moe_kernel(x_ref, h_ref, comb_ref, mod_ref, wg_ref, wu_ref, wd_ref, fin_ref, *rest, n_ctx, seq_s, final):
    outs, (acc_ref, rank_c_ref, rank_r_ref, comb_t_ref) = rest[:-4], rest[-4:]
    e = pl.program_id(1)
    parts = [slice(p * MOE_PART, (p + 1) * MOE_PART) for p in range(MOE_TILE // MOE_PART)]

    @pl.when(e == 0)
    def _():
        acc_ref[...] = jnp.zeros_like(acc_ref)
        before = (lax.broadcasted_iota(jnp.int32, (MOE_PART, MOE_PART), 1)
                  < lax.broadcasted_iota(jnp.int32, (MOE_PART, MOE_PART), 0)).astype(BF16)
        for part in parts:
            comb = comb_ref[part, :]
            rank_c_ref[part, :] = _dot(before, (comb > 0.0).astype(BF16))
            comb_t = comb.T
            comb_t_ref[:, part] = comb_t
            rank_r_ref[:, part] = _dot_nt((comb_t > 0.0).astype(BF16), before)

    comb = comb_ref[...]
    lane = lax.broadcasted_iota(jnp.int32, comb.shape, 1)
    w_col = jnp.sum(jnp.where(lane == e, comb, 0.0), axis=-1, keepdims=True)
    rank_col = jnp.sum(jnp.where(lane == e, rank_c_ref[...], 0.0), axis=-1, keepdims=True)
    w_row = comb_t_ref[pl.ds(e, 1), :]
    rank_row = rank_r_ref[pl.ds(e, 1), :]
    count = functools.reduce(jnp.maximum, [jnp.sum((w_row[:, part] > 0.0).astype(jnp.int32)) for part in parts])

    def body(j, carry):
        base = (j * MOE_ROWS).astype(F32)
        slot_r = lax.broadcasted_iota(jnp.int32, (MOE_ROWS, MOE_PART), 0).astype(F32) + base
        slot_c = lax.broadcasted_iota(jnp.int32, (MOE_PART, MOE_ROWS), 1).astype(F32) + base
        rows = []
        for part in parts:
            hit = jnp.logical_and(rank_row[:, part] == slot_r, w_row[:, part] > 0.0)
            rows.append(_dot(jnp.where(hit, 1.0, 0.0).astype(BF16), h_ref[part, :]).astype(BF16))
        hg = jnp.concatenate(rows, axis=0)
        t = _silu(_dot(hg, wg_ref[0])) * _dot(hg, wu_ref[0])
        y = _dot(t.astype(BF16), wd_ref[0]).astype(BF16)
        for p, part in enumerate(parts):
            hit = jnp.logical_and(rank_col[part, :] == slot_c, w_col[part, :] > 0.0)
            acc_ref[part, :] += w_col[part, :] * _dot(jnp.where(hit, 1.0, 0.0).astype(BF16),
                                                      y[p * MOE_ROWS:(p + 1) * MOE_ROWS])
        return carry

    lax.fori_loop(0, (count + MOE_ROWS - 1) // MOE_ROWS, body, 0)

    @pl.when(e == N_EXPERTS - 1)
    def _():
        row = _mod_row(pl.program_id(0), MOE_TILE, n_ctx, seq_s)
        g2 = mod_ref[pl.ds(row, 1), 5 * D_MODEL:6 * D_MODEL]
        y = x_ref[...] + g2 * acc_ref[...]
        if not final:
            outs[0][...] = y
        else:
            y = y * lax.rsqrt(jnp.mean(y * y, axis=-1, keepdims=True) + RMS_EPS) * fin_ref[...]
            is_ctx = pl.program_id(0) < n_ctx // MOE_TILE

            @pl.when(is_ctx)
            def _():
                outs[0][...] = y

            @pl.when(jnp.logical_not(is_ctx))
            def _():
                outs[1][...] = y


def _ffn_moe(x1, h2, comb, mod, w_gate, w_up, w_down, n_ctx, seq_s, final_gain=None):
    n = x1.shape[0]
    final = final_gain is not None
    once = pl.Buffered(1)
    tile = lambda w: pl.BlockSpec((MOE_TILE, w), lambda i, e: (i, 0), pipeline_mode=once)
    if final:
        n_ctx_tiles = n_ctx // MOE_TILE
        out_specs = [pl.BlockSpec((MOE_TILE, D_MODEL), lambda i, e: (jnp.minimum(i, n_ctx_tiles - 1), 0),
                                  pipeline_mode=once),
                     pl.BlockSpec((MOE_TILE, D_MODEL), lambda i, e: (jnp.maximum(i - n_ctx_tiles, 0), 0),
                                  pipeline_mode=once)]
        out_shape = [jax.ShapeDtypeStruct((n_ctx, D_MODEL), F32), jax.ShapeDtypeStruct((n - n_ctx, D_MODEL), F32)]
        gain = final_gain.reshape(1, D_MODEL)
    else:
        out_specs = [tile(D_MODEL)]
        out_shape = [jax.ShapeDtypeStruct((n, D_MODEL), F32)]
        gain = jnp.ones((1, D_MODEL), F32)
    out = pl.pallas_call(
        functools.partial(_moe_kernel, n_ctx=n_ctx, seq_s=seq_s, final=final),
        grid=(n // MOE_TILE, N_EXPERTS),
        in_specs=[tile(D_MODEL), tile(D_MODEL), tile(LANES),
                  pl.BlockSpec((8, 6 * D_MODEL), lambda i, e: (0, 0)),
                  pl.BlockSpec((1, D_MODEL, D_FF_E), lambda i, e: (e, 0, 0)),
                  pl.BlockSpec((1, D_MODEL, D_FF_E), lambda i, e: (e, 0, 0)),
                  pl.BlockSpec((1, D_FF_E, D_MODEL), lambda i, e: (e, 0, 0)),
                  pl.BlockSpec((1, D_MODEL), lambda i, e: (0, 0))],
        out_specs=out_specs,
        out_shape=out_shape,
        scratch_shapes=[pltpu.VMEM((MOE_TILE, D_MODEL), F32), pltpu.VMEM((MOE_TILE, LANES), F32),
                        pltpu.VMEM((LANES, MOE_TILE), F32), pltpu.VMEM((LANES, MOE_TILE), F32)],
        compiler_params=_cparams(("arbitrary", "arbitrary") if final else ("parallel", "arbitrary")),
        name="ffn_moe",
    )(x1, h2, comb, mod, w_gate.astype(BF16), w_up.astype(BF16), w_down.astype(BF16), gain)
    return tuple(out) if final else out[0]


def _final_kernel(x_ref, g_ref, o_ref):
    x = x_ref[...]
    o_ref[...] = x * lax.rsqrt(jnp.mean(x * x, axis=-1, keepdims=True) + RMS_EPS) * g_ref[...]


def _final_norm(x, gain, row0, n_rows):
    blk0 = row0 // FFN_TILE
    return pl.pallas_call(
        _final_kernel,
        grid=(n_rows // FFN_TILE,),
        in_specs=[pl.BlockSpec((FFN_TILE, D_MODEL), lambda i: (blk0 + i, 0)),
                  pl.BlockSpec((1, D_MODEL), lambda i: (0, 0))],
        out_specs=pl.BlockSpec((FFN_TILE, D_MODEL), lambda i: (i, 0)),
        out_shape=jax.ShapeDtypeStruct((n_rows, D_MODEL), F32),
        compiler_params=_cparams(("parallel",)),
        name="final_norm",
    )(x, gain.reshape(1, D_MODEL))


def kernel(x_prompt, x_sample, state_rwkv, c, c_ctx, norm1, norm2, w_ada, b_ada, w_in, dw_w, dw_b, conv_ln_g, conv_ln_b, w_conv_out, w_fno_out, shift_mu, g_up, dec_w0, dec_up, iclr_a0, iclr_up, k_k, k_a, r_k, gn_g, gn_b, w_rwkv_out, w_o, ffn_w_gate, ffn_w_up, ffn_w_down, w_router, b_router, moe_w_gate, moe_w_up, moe_w_down, final_norm):
    b_p, t_p, _ = x_prompt.shape
    b_s, t_s, _ = x_sample.shape
    depth = w_in.shape[0]
    n_ctx = b_p * t_p
    n_lat = b_s * t_s
    assert t_p == ROW_TILE and t_s % MOE_TILE == 0 and n_ctx % MOE_TILE == 0 and b_s <= CTX_ROW
    assert MOE_TILE % FFN_TILE == 0 and n_ctx % WIDE_TILE == 0 and t_s % WIDE_TILE == 0
    assert t_s % GRID_W == 0 and CHUNK == GRID_W

    n = n_ctx + n_lat
    x = (x_prompt.reshape(n_ctx, D_MODEL), x_sample.reshape(n_lat, D_MODEL))
    cond = jnp.zeros((8, D_MODEL), F32).at[:b_s].set(c).at[CTX_ROW].set(c_ctx)
    mods = _ada(cond, w_ada, b_ada)
    table = _scan_table(b_p, t_p, b_s, t_s)
    zero_state = jnp.zeros((b_p, 2, HEAD_DIM, C_RWKV), F32)
    pack_state = lambda s: s.transpose(0, 1, 3, 2, 4).reshape(s.shape[0], 2, HEAD_DIM, C_RWKV)
    unpack_state = lambda s: s.reshape(s.shape[0], 2, HEAD_DIM, N_HEADS, HEAD_DIM).transpose(0, 1, 3, 2, 4)

    ctx_states = []
    for l in range(depth):
        mod = mods[l]
        u_conv, u_fno, u_rw = _inproj(x, n, mod, norm1[l], w_in[l, :, :OFF_GATE].astype(BF16), n_ctx, t_s)
        conv_h = _conv_branch(u_conv, dw_w[l], dw_b[l], conv_ln_g[l], conv_ln_b[l], n_ctx)
        fno_h = (_fno_branch(u_fno, 0, b_p, t_p), _fno_branch(u_fno, n_ctx, b_s, t_s))
        r, k, v, g, *summaries = _rwkv_project_and_summarise(u_rw, shift_mu[l], g_up[l], dec_w0[l], dec_up[l],
                                                             iclr_a0[l], iclr_up[l], k_k[l], k_a[l], n_ctx, t_s)
        s0 = jnp.concatenate([zero_state, pack_state(state_rwkv[:, l])], axis=0)
        yf, yb, s_fin = _carry_states(*summaries, s0, table)
        ctx_states.append(unpack_state(s_fin[:b_p]))
        i = l // 2
        x1, h2, *comb = _merge(x, n, conv_h, fno_h, yf, yb, r, k, v, g, norm1[l], w_in[l, :, OFF_GATE:].astype(BF16), mod, w_conv_out[l], w_fno_out[l],
                              w_rwkv_out[l], w_o[l], gn_g[l], gn_b[l], r_k[l], norm2[l], n_ctx, t_s,
                               router=(w_router[i], b_router[i]) if l % 2 == 1 else None)
        if l % 2 == 0:
            x = _ffn_dense(x1, h2, mod, ffn_w_gate[i], ffn_w_up[i], ffn_w_down[i], n_ctx, t_s)
        else:
            x = _ffn_moe(x1, h2, comb[0], mod, moe_w_gate[i], moe_w_up[i], moe_w_down[i], n_ctx, t_s,
                         final_gain=final_norm if l == depth - 1 else None)

    if isinstance(x, tuple):
        y_prompt, y_sample = x
    else:
        y_prompt, y_sample = _final_norm(x, final_norm, 0, n_ctx), _final_norm(x, final_norm, n_ctx, n_lat)
    y_prompt = y_prompt.reshape(b_p, t_p, D_MODEL)
    y_sample = y_sample.reshape(b_s, t_s, D_MODEL)
    new_state = jnp.stack(ctx_states, axis=1).astype(x_prompt.dtype)
    return (y_prompt, y_sample, new_state)
```

```python
import functools

import numpy as np
import jax
import jax.numpy as jnp
from jax import lax
from jax.experimental import pallas as pl
from jax.experimental.pallas import tpu as pltpu

F32 = jnp.float32
BF16 = jnp.bfloat16

D_MODEL = 1024
GRID_W = 64
C_CONV = 256
CONV_W = 31
C_FNO = 256
FNO_GW = 64
N_HEADS = 8
HEAD_DIM = 64
C_RWKV = N_HEADS * HEAD_DIM
G_RANK = 128
W_RANK = 64
A_RANK = 64
D_FF = 2816
N_EXPERTS = 8
D_FF_E = 1408
RMS_EPS = 1e-6
LN_EPS = 1e-5
GN_EPS = 64e-5

OFF_FNO = 2 * C_CONV
OFF_RWKV = OFF_FNO + C_FNO
N_SHIFT = 3 * C_RWKV + G_RANK + 2 * (W_RANK + A_RANK)
OFF_GATE = OFF_RWKV + N_SHIFT
D_IN = OFF_GATE + 3 * D_MODEL

ROW_TILE = 256
FFN_TILE = 1024
WIDE_TILE = 512
INPROJ_TILE = 1024
CHUNK = 64
GROUP = 4
GROUP_W = GROUP * HEAD_DIM
SUMMARY_CHUNKS = 4
CARRY_CHUNKS = 4
MOE_TILE = 1024
MOE_PART = 512
MOE_ROWS = 160
CTX_ROW = 4
LANES = 128
VMEM_LIMIT = 56 * 1024 * 1024


def _cparams(sem):
    return pltpu.CompilerParams(dimension_semantics=sem, vmem_limit_bytes=VMEM_LIMIT)


def _sigmoid(x):
    return 1.0 / (1.0 + jnp.exp(-x))


def _silu(x):
    return x * _sigmoid(x)


def _dot(a, b):
    return jnp.dot(a, b, preferred_element_type=F32)


def _split(x):
    hi = x.astype(BF16)
    lo = (x - hi.astype(F32)).astype(BF16)
    return hi, lo


def _dot_x3(a, b):
    ah, al = _split(a)
    bh, bl = _split(b)
    return _dot(ah, bh) + (_dot(ah, bl) + _dot(al, bh))


def _seg_sum(x, bd):
    return _dot(x.astype(BF16), bd)


def _mod_row(i, tile, n_ctx_rows, seq_s):
    n_ctx_tiles = n_ctx_rows // tile
    return jnp.where(i < n_ctx_tiles, CTX_ROW, (i - n_ctx_tiles) // (seq_s // tile))


def _stream_specs(tile, width, n_ctx, lat_block0):
    n_ctx_tiles = n_ctx // tile
    return [pl.BlockSpec((tile, width), lambda i: (jnp.minimum(i, n_ctx_tiles - 1), 0)),
            pl.BlockSpec((tile, width), lambda i: (jnp.maximum(i - n_ctx_tiles, 0) + lat_block0, 0))]


def _stream_args(x, tile, n_ctx):
    if isinstance(x, tuple):
        return x[0], x[1], 0
    return x, x, n_ctx // tile


def _pick_stream(i, tile, n_ctx, ctx_ref, lat_ref):
    return jnp.where(i < n_ctx // tile, ctx_ref[...], lat_ref[...])


def _ada_kernel(c_ref, w_ref, b_ref, o_ref):
    s = _silu(c_ref[...])
    o_ref[0] = _dot(s.astype(BF16), w_ref[0].astype(BF16)) + b_ref[0]


def _ada(cond, w_ada, b_ada):
    n_l = w_ada.shape[0]
    tn = 1536
    return pl.pallas_call(
        _ada_kernel,
        grid=(n_l, 6 * D_MODEL // tn),
        in_specs=[
            pl.BlockSpec((8, D_MODEL), lambda l, j: (0, 0)),
            pl.BlockSpec((1, D_MODEL, tn), lambda l, j: (l, 0, j)),
            pl.BlockSpec((1, 1, tn), lambda l, j: (l, 0, j)),
        ],
        out_specs=pl.BlockSpec((1, 8, tn), lambda l, j: (l, 0, j)),
        out_shape=jax.ShapeDtypeStruct((n_l, 8, 6 * D_MODEL), F32),
        compiler_params=_cparams(("parallel", "parallel")),
        name="ada",
    )(cond, w_ada, b_ada.reshape(n_l, 1, 6 * D_MODEL))


def _inproj_kernel(xc_ref, xl_ref, mod_ref, g_ref, w_ref, oc_ref, of_ref, or_ref, *, n_ctx, seq_s):
    row = _mod_row(pl.program_id(0), INPROJ_TILE, n_ctx, seq_s)
    m = mod_ref[pl.ds(row, 1), :]
    sh = m[:, 0:D_MODEL]
    sc = m[:, D_MODEL:2 * D_MODEL]
    x = _pick_stream(pl.program_id(0), INPROJ_TILE, n_ctx, xc_ref, xl_ref)
    y = x * lax.rsqrt(jnp.mean(x * x, axis=-1, keepdims=True) + RMS_EPS) * g_ref[...]
    h = (y * (1.0 + sc) + sh).astype(BF16)
    oc_ref[...] = _dot(h, w_ref[:, 0:OFF_FNO])
    of_ref[...] = _dot(h, w_ref[:, OFF_FNO:OFF_RWKV])
    or_ref[...] = _dot(h, w_ref[:, OFF_RWKV:OFF_GATE])


def _inproj(x, n, mod, gain, w_in, n_ctx, seq_s):
    xc, xl, lat0 = _stream_args(x, INPROJ_TILE, n_ctx)
    widths = (OFF_FNO, C_FNO, N_SHIFT)
    return pl.pallas_call(
        functools.partial(_inproj_kernel, n_ctx=n_ctx, seq_s=seq_s),
        grid=(n // INPROJ_TILE,),
        in_specs=_stream_specs(INPROJ_TILE, D_MODEL, n_ctx, lat0) + [
            pl.BlockSpec((8, 6 * D_MODEL), lambda i: (0, 0)),
            pl.BlockSpec((1, D_MODEL), lambda i: (0, 0)),
            pl.BlockSpec((D_MODEL, OFF_GATE), lambda i: (0, 0), pipeline_mode=pl.Buffered(1)),
        ],
        out_specs=[pl.BlockSpec((INPROJ_TILE, w), lambda i: (i, 0)) for w in widths],
        out_shape=[jax.ShapeDtypeStruct((n, w), F32) for w in widths],
        compiler_params=_cparams(("parallel",)),
        name="inproj",
    )(xc, xl, mod, gain.reshape(1, D_MODEL), w_in)


_CONV_HALO = 16
_CONV_ROWS = 64


def _conv_kernel(u_ref, w_ref, b_ref, g_ref, be_ref, o_ref, pad_ref, sh_ref, *, n_ctx):
    is_ctx = pl.program_id(0) < n_ctx // ROW_TILE
    zeros = jnp.zeros((_CONV_HALO, C_CONV), F32)
    n_parts = ROW_TILE // _CONV_ROWS

    def glu(lo, hi):
        return u_ref[lo:hi, 0:C_CONV] * _sigmoid(u_ref[lo:hi, C_CONV:2 * C_CONV])

    def finish(starts, n_rows):
        for s in range(8):
            sh_ref[s, 0:n_rows - 8, :] = pad_ref[s:s + n_rows - 8, :]
        for p in range(n_parts):
            acc = jnp.zeros((_CONV_ROWS, C_CONV), F32)
            for j in range(CONV_W):
                o = starts[p] + j
                acc = acc + w_ref[j:j + 1, :] * sh_ref[o % 8, o - o % 8:o - o % 8 + _CONV_ROWS, :]
            acc = acc + b_ref[...]
            mu = jnp.mean(acc, axis=-1, keepdims=True)
            d = acc - mu
            var = jnp.mean(d * d, axis=-1, keepdims=True)
            y = d * lax.rsqrt(var + LN_EPS) * g_ref[...] + be_ref[...]
            o_ref[p * _CONV_ROWS:(p + 1) * _CONV_ROWS, :] = _silu(y).astype(o_ref.dtype)

    shift = _CONV_HALO - CONV_W // 2

    @pl.when(is_ctx)
    def _():
        pad_ref[0:_CONV_HALO, :] = zeros
        pad_ref[_CONV_HALO:_CONV_HALO + ROW_TILE, :] = glu(0, ROW_TILE)
        pad_ref[_CONV_HALO + ROW_TILE:2 * _CONV_HALO + ROW_TILE, :] = zeros
        finish([shift + p * _CONV_ROWS for p in range(n_parts)], ROW_TILE + 2 * _CONV_HALO)

    @pl.when(jnp.logical_not(is_ctx))
    def _():
        stride = GRID_W + 2 * _CONV_HALO
        for p in range(n_parts):
            pad_ref[p * stride:p * stride + _CONV_HALO, :] = zeros
            pad_ref[p * stride + _CONV_HALO:p * stride + _CONV_HALO + GRID_W, :] = glu(p * GRID_W, (p + 1) * GRID_W)
            pad_ref[p * stride + _CONV_HALO + GRID_W:(p + 1) * stride, :] = zeros
        finish([p * stride + shift for p in range(n_parts)], n_parts * stride)


def _conv_branch(u_conv, dw_w, dw_b, ln_g, ln_b, n_ctx):
    n = u_conv.shape[0]
    assert _CONV_ROWS == GRID_W and ROW_TILE % GRID_W == 0
    vec = pl.BlockSpec((1, C_CONV), lambda i: (0, 0))
    return pl.pallas_call(
        functools.partial(_conv_kernel, n_ctx=n_ctx),
        grid=(n // ROW_TILE,),
        in_specs=[
            pl.BlockSpec((ROW_TILE, 2 * C_CONV), lambda i: (i, 0)),
            pl.BlockSpec((CONV_W, C_CONV), lambda i: (0, 0)),
            vec, vec, vec,
        ],
        out_specs=pl.BlockSpec((ROW_TILE, C_CONV), lambda i: (i, 0)),
        out_shape=jax.ShapeDtypeStruct((n, C_CONV), BF16),
        scratch_shapes=[pltpu.VMEM((ROW_TILE // GRID_W * (GRID_W + 2 * _CONV_HALO), C_CONV), F32),
                        pltpu.VMEM((8, ROW_TILE // GRID_W * (GRID_W + 2 * _CONV_HALO), C_CONV), F32)],
        compiler_params=_cparams(("parallel",)),
        name="conv_branch",
    )(u_conv, dw_w, dw_b.reshape(1, C_CONV), ln_g.reshape(1, C_CONV), ln_b.reshape(1, C_CONV))


def _dft_tables(t_len):
    def cs(n):
        k = np.arange(n, dtype=np.int64)
        ang = 2.0 * np.pi * ((k[:, None] * k[None, :]) % n).astype(np.float64) / n
        return np.cos(ang), np.sin(ang)
    cg, sg = cs(FNO_GW)
    eye = np.eye(C_FNO // FNO_GW)
    w1 = np.concatenate([np.kron(eye, cg), np.kron(eye, sg)], axis=1)
    ct, st = cs(t_len)
    w2 = np.concatenate([ct, -st], axis=1)
    return jnp.asarray(w1, dtype=F32).astype(BF16), jnp.asarray(w2, dtype=F32).astype(BF16)


def _fno_kernel(u_ref, w1_ref, w2_ref, o_ref, hs_ref, *, t_len, scale):
    @pl.when(pl.program_id(1) == 0)
    def _():
        hc = _dot(u_ref[...].astype(BF16), w1_ref[...])
        hs_ref[0:t_len, :] = hc[:, 0:C_FNO].astype(BF16)
        hs_ref[t_len:2 * t_len, :] = hc[:, C_FNO:2 * C_FNO].astype(BF16)

    o_ref[...] = (_dot(w2_ref[...], hs_ref[...]) * scale).astype(o_ref.dtype)


def _fno_branch(u_fno, row0, n_seq, t_len):
    w1, w2 = _dft_tables(t_len)
    tk = t_len
    blk0 = row0 // t_len
    return pl.pallas_call(
        functools.partial(_fno_kernel, t_len=t_len, scale=float(1.0 / np.sqrt(t_len * FNO_GW))),
        grid=(n_seq, t_len // tk),
        in_specs=[
            pl.BlockSpec((t_len, C_FNO), lambda b, j: (blk0 + b, 0)),
            pl.BlockSpec((C_FNO, 2 * C_FNO), lambda b, j: (0, 0)),
            pl.BlockSpec((tk, 2 * t_len), lambda b, j: (j, 0), pipeline_mode=pl.Buffered(1)),
        ],
        out_specs=pl.BlockSpec((tk, C_FNO), lambda b, j: (b * (t_len // tk) + j, 0)),
        out_shape=jax.ShapeDtypeStruct((n_seq * t_len, C_FNO), BF16),
        scratch_shapes=[pltpu.VMEM((2 * t_len, C_FNO), BF16)],
        compiler_params=_cparams(("parallel", "arbitrary")),
        name="fno_branch",
    )(u_fno, w1, w2)


def _prep_pieces(i, z_ref, zp_ref, zn_ref, mu_ref, gup_ref, w0_ref, dup_ref, a0_ref, aup_ref, kk_ref, bd_ref,
                 r_refs, k_refs, v_refs, kkn_ref, g_ref, lw_ref, a_ref, pad_ref, n_ctx, seq_s):
    n_ctx_tiles = n_ctx // ROW_TILE
    per_seq = seq_s // ROW_TILE
    j = (i - n_ctx_tiles) % per_seq
    first = jnp.logical_or(i < n_ctx_tiles, j == 0)
    last = jnp.logical_or(i < n_ctx_tiles, j == per_seq - 1)
    c = C_RWKV

    def shifted(lo, hi):
        z = pad_ref[8:8 + ROW_TILE, lo:hi]
        zp = pad_ref[7:7 + ROW_TILE, lo:hi]
        zn = pad_ref[9:9 + ROW_TILE, lo:hi]
        return z + mu_ref[:, lo:hi] * (0.5 * (zp + zn) - z)

    def fill():
        pad_ref[8:8 + ROW_TILE, :] = z_ref[...]
        pad_ref[0:8, :] = jnp.where(first, 0.0, zp_ref[...])
        pad_ref[8 + ROW_TILE:16 + ROW_TILE, :] = jnp.where(last, 0.0, zn_ref[...])

    def receptance():
        r = shifted(0, c)
        for ref in r_refs:
            ref[...] = r

    def key():
        k = shifted(c, 2 * c)
        for ref in k_refs:
            ref[...] = k
        kx = k * kk_ref[...]
        nrm = jnp.sqrt(_seg_sum(kx * kx, bd_ref[...]))
        kkn_ref[...] = kx / jnp.maximum(nrm, 1e-12)

    def value():
        v = shifted(2 * c, 3 * c)
        for ref in v_refs:
            ref[...] = v

    def gate():
        o = 3 * c
        g_ref[...] = _dot(_sigmoid(shifted(o, o + G_RANK)).astype(BF16), gup_ref[...])

    def decay():
        o = 3 * c + G_RANK
        xw = _dot(jnp.tanh(shifted(o, o + 2 * W_RANK)).astype(BF16), dup_ref[...]) + w0_ref[...]
        soft = jnp.maximum(-xw, 0.0) + jnp.log(1.0 + jnp.exp(-jnp.abs(xw)))
        lw_ref[...] = -jnp.exp(-soft - 0.5)

    def iclr():
        o = 3 * c + G_RANK + 2 * W_RANK
        xa = _dot(shifted(o, o + 2 * A_RANK).astype(BF16), aup_ref[...]) + a0_ref[...]
        a_ref[...] = _sigmoid(xa)

    return [fill, receptance, key, value, gate, decay, iclr]


def _block_diag2(w):
    z = jnp.zeros_like(w[0])
    return jnp.concatenate([jnp.concatenate([w[0], z], axis=1), jnp.concatenate([z, w[1]], axis=1)], axis=0)


def _head_ones():
    return jnp.asarray(np.kron(np.eye(N_HEADS), np.ones((HEAD_DIM, HEAD_DIM))), dtype=BF16)


def _block_diag(x, mask):
    xb = x.astype(BF16)
    return jnp.where(mask, jnp.concatenate([xb] * GROUP, axis=0), jnp.zeros((), BF16))


def _dot_nt(a, b):
    return lax.dot_general(a, b, (((1,), (1,)), ((), ())), preferred_element_type=F32)


def _diag_blocks(prod, lane_head):
    out = jnp.where(lane_head == 0, prod[0:HEAD_DIM], 0.0)
    for h in range(1, GROUP):
        out = out + jnp.where(lane_head == h, prod[h * HEAD_DIM:(h + 1) * HEAD_DIM], 0.0)
    return out


def _summary_body(r_ref, k_ref, v_ref, n_ref, lw_ref, a_ref, ka_ref, q_ref, y0_ref, p_ref, z_ref, g_ref, fillers=()):
    fillers = list(fillers)
    fill_one = lambda: fillers.pop(0)() if fillers else None
    row = lax.broadcasted_iota(jnp.int32, (CHUNK, GROUP_W), 0)
    lane = lax.broadcasted_iota(jnp.int32, (CHUNK, GROUP_W), 1)
    col = lane % CHUNK
    lane_head = lane // HEAD_DIM
    bd_mask = (lax.broadcasted_iota(jnp.int32, (GROUP * CHUNK, GROUP_W), 0) // CHUNK
               == lax.broadcasted_iota(jnp.int32, (GROUP * CHUNK, GROUP_W), 1) // HEAD_DIM)
    bd = lambda x: _block_diag(x, bd_mask)
    ka = ka_ref[...]

    units = []
    for j in range(SUMMARY_CHUNKS):
        rows = slice(j * CHUNK, (j + 1) * CHUNK)
        r = r_ref[rows, :]
        k = k_ref[rows, :]
        v = v_ref[rows, :]
        kkn = n_ref[rows, :]
        for d in range(2):
            lanes = slice(d * C_RWKV, (d + 1) * C_RWKV)
            earlier = (col < row) if d == 0 else (col > row)
            upto = jnp.logical_or(earlier, row == col)
            tri = upto[:, 0:CHUNK].astype(BF16)
            lw = lw_ref[rows, lanes]
            h1 = lw.astype(BF16)
            r1 = lw - h1.astype(F32)
            h2 = r1.astype(BF16)
            h3 = (r1 - h2.astype(F32)).astype(BF16)
            cum = _dot(tri, h1) + (_dot(tri, h2) + _dot(tri, h3))
            e_in = jnp.exp(cum)
            e_ex = jnp.exp(cum - lw)
            e_ng = jnp.exp(-cum)
            a = a_ref[rows, lanes]
            kd = k * (1.0 + (a - 1.0) * ka)
            at = -kkn * e_ex
            rt = r * e_in
            bt = kkn * a * e_ng
            kt = kd * e_ng
            end = CHUNK - 1 if d == 0 else 0
            g_end = e_in[end:end + 1, :]
            g_ref[j, :, lanes] = g_end
            bh = bt * g_end
            kh = kt * g_end
            for q in range(C_RWKV // GROUP_W):
                sl = slice(q * GROUP_W, (q + 1) * GROUP_W)
                units.append(dict(rows=rows, out=slice(d * C_RWKV + q * GROUP_W, d * C_RWKV + (q + 1) * GROUP_W),
                                  earlier=earlier, upto=upto, at=at[:, sl], rt=rt[:, sl],
                                  ar=jnp.concatenate([at[:, sl], rt[:, sl]], axis=0).astype(BF16),
                                  bt=bt[:, sl], kt=kt[:, sl], v=v[:, sl], bh=bh[:, sl].astype(BF16),
                                  bk=jnp.concatenate([bh[:, sl], kh[:, sl]], axis=0).astype(BF16)))

    for u in units:
        sb = _dot_nt(u["ar"], bd(u["bt"]))
        sk = _dot_nt(u["ar"], bd(u["kt"]))
        u["lab"] = jnp.where(u["earlier"], sb[0:CHUNK], 0.0)
        u["mrb"] = jnp.where(u["upto"], sb[CHUNK:2 * CHUNK], 0.0).astype(BF16)
        u["lm"] = jnp.concatenate([jnp.where(u["earlier"], sk[0:CHUNK], 0.0),
                                   jnp.where(u["upto"], sk[CHUNK:2 * CHUNK], 0.0)], axis=0).astype(BF16)

    fill_one()
    eye = (row == col).astype(F32)
    pair = jnp.logical_and(row // 2 == col // 2, row != col)
    for u in units:
        u["t"] = eye + jnp.where(pair, u["lab"], 0.0)
    n = 2
    while n < CHUNK:
        m = jnp.logical_and(row // (2 * n) == col // (2 * n), row // n != col // n)
        for u in units:
            u["w"] = _dot(jnp.where(m, u["lab"], 0.0).astype(BF16), bd(u["t"]))
        for u in units:
            u["t"] = u["t"] + _dot(u["t"].astype(BF16), bd(u["w"]))
        fill_one()
        n *= 2

    for u in units:
        u["t"] = u["t"].astype(BF16)
        u["wm"] = _dot(u["t"], bd(u["at"]))
        u["lv"] = _dot(u["lm"], bd(u["v"]))
    for u in units:
        u["u0"] = _dot(u["t"], bd(u["lv"][0:CHUNK]))
    for u in units:
        q_ref[u["rows"], u["out"]] = (u["rt"] + _dot(u["mrb"], bd(u["wm"]))).astype(q_ref.dtype)
        y0_ref[u["rows"], u["out"]] = u["lv"][CHUNK:2 * CHUNK] + _dot(u["mrb"], bd(u["u0"]))
    for u in units:
        p = _dot(u["wm"].T.astype(BF16), u["bh"])
        p_ref[u["rows"], u["out"]] = _diag_blocks(p, lane_head).astype(p_ref.dtype)
        z = _dot(jnp.concatenate([u["u0"], u["v"]], axis=0).T.astype(BF16), u["bk"])
        z_ref[u["rows"], u["out"]] = _diag_blocks(z, lane_head)
    while fillers:
        fill_one()


def _rwkv_kernel(z_ref, zp_ref, zn_ref, mu_ref, gup_ref, w0_ref, dup_ref, a0_ref, aup_ref, kk_ref, bd_ref, ka_ref,
                 r_ref, k_ref, v_ref, g_ref, q_ref, y0_ref, p_ref, zz_ref, gd_ref,
                 pad_ref, r_n, k_n, v_n, n_n, lw_n, a_n, r_c, k_c, v_c, n_c, lw_c, a_c, *, n_ctx, seq_s, n_tiles):
    s = pl.program_id(0)
    new = (r_n, k_n, v_n, n_n, lw_n, a_n)
    cur = (r_c, k_c, v_c, n_c, lw_c, a_c)

    @pl.when(s == 0)
    def _():
        for ref in cur:
            ref[...] = jnp.zeros_like(ref)

    pieces = _prep_pieces(jnp.minimum(s, n_tiles - 1), z_ref, zp_ref, zn_ref, mu_ref, gup_ref, w0_ref, dup_ref,
                          a0_ref, aup_ref, kk_ref, bd_ref, (r_n, r_ref), (k_n, k_ref), (v_n, v_ref), n_n, g_ref,
                          lw_n, a_n, pad_ref, n_ctx, seq_s)
    pieces[0]()
    _summary_body(r_c, k_c, v_c, n_c, lw_c, a_c, ka_ref, q_ref, y0_ref, p_ref, zz_ref, gd_ref, fillers=pieces[1:])
    for dst, src in zip(cur, new):
        dst[...] = src[...]


def _rwkv_project_and_summarise(u_rw, mu, g_up, dec_w0, dec_up, iclr_a0, iclr_up, k_k, k_a, n_ctx, seq_s):
    n = u_rw.shape[0]
    assert ROW_TILE == SUMMARY_CHUNKS * CHUNK and CHUNK == HEAD_DIM and C_RWKV % GROUP_W == 0
    n_tiles = n // ROW_TILE
    halo = ROW_TILE // 8
    last_blk = n // 8 - 1
    c = C_RWKV
    full = lambda shape: pl.BlockSpec(shape, lambda s: (0,) * len(shape))
    this = lambda s: jnp.minimum(s, n_tiles - 1)
    prev = lambda s: jnp.maximum(s - 1, 0)
    proj = lambda w: pl.BlockSpec((ROW_TILE, w), lambda s: (this(s), 0))
    summ = lambda w: pl.BlockSpec((ROW_TILE, w), lambda s: (prev(s), 0))
    vm = lambda w: pltpu.VMEM((ROW_TILE, w), F32)
    return pl.pallas_call(
        functools.partial(_rwkv_kernel, n_ctx=n_ctx, seq_s=seq_s, n_tiles=n_tiles),
        grid=(n_tiles + 1,),
        in_specs=[
            proj(N_SHIFT),
            pl.BlockSpec((8, N_SHIFT), lambda s: (jnp.maximum(this(s) * halo - 1, 0), 0)),
            pl.BlockSpec((8, N_SHIFT), lambda s: (jnp.minimum((this(s) + 1) * halo, last_blk), 0)),
            full((1, N_SHIFT)), full((G_RANK, c)), full((1, 2 * c)), full((2 * W_RANK, 2 * c)),
            full((1, 2 * c)), full((2 * A_RANK, 2 * c)), full((1, c)), full((c, c)), full((1, c)),
        ],
        out_specs=[proj(c), proj(c), proj(c), proj(c), summ(2 * c), summ(2 * c), summ(2 * c), summ(2 * c),
                   pl.BlockSpec((SUMMARY_CHUNKS, 1, 2 * c), lambda s: (prev(s), 0, 0))],
        out_shape=[jax.ShapeDtypeStruct((n, c), F32)] * 4
        + [jax.ShapeDtypeStruct((n, 2 * c), BF16), jax.ShapeDtypeStruct((n, 2 * c), F32),
           jax.ShapeDtypeStruct((n, 2 * c), BF16), jax.ShapeDtypeStruct((n, 2 * c), F32),
           jax.ShapeDtypeStruct((n // CHUNK, 1, 2 * c), F32)],
        scratch_shapes=[pltpu.VMEM((ROW_TILE + 16, N_SHIFT), F32)]
        + [vm(c), vm(c), vm(c), vm(c), vm(2 * c), vm(2 * c)] * 2,
        compiler_params=_cparams(("arbitrary",)),
        name="rwkv_project_summarise",
    )(u_rw, u_rw, u_rw, mu.reshape(1, N_SHIFT), g_up.astype(BF16), dec_w0.reshape(1, 2 * c),
      _block_diag2(dec_up).astype(BF16), iclr_a0.reshape(1, 2 * c), _block_diag2(iclr_up).astype(BF16),
      k_k.reshape(1, c), _head_ones(), k_a.reshape(1, c))


def _carry_kernel(tab_ref, qf_ref, y0f_ref, pf_ref, zf_ref, gf_ref, qb_ref, y0b_ref, pb_ref, zb_ref, gb_ref,
                  s0_ref, yf_ref, yb_ref, sfin_ref, st_ref):
    step = pl.program_id(0)

    @pl.when(tab_ref[3, step] == 1)
    def _():
        for d in range(2):
            for h in range(N_HEADS):
                st_ref[d, :, h * HEAD_DIM:(h + 1) * HEAD_DIM] = s0_ref[0, d, h]

    bd_mask = (lax.broadcasted_iota(jnp.int32, (GROUP * CHUNK, GROUP_W), 0) // CHUNK
               == lax.broadcasted_iota(jnp.int32, (GROUP * CHUNK, GROUP_W), 1) // HEAD_DIM)
    for i in range(CARRY_CHUNKS):
        for d, (q_ref, y0_ref, p_ref, z_ref, g_ref, y_ref) in enumerate(
                ((qf_ref, y0f_ref, pf_ref, zf_ref, gf_ref, yf_ref), (qb_ref, y0b_ref, pb_ref, zb_ref, gb_ref, yb_ref))):
            j = i if d == 0 else CARRY_CHUNKS - 1 - i
            rows = slice(j * CHUNK, (j + 1) * CHUNK)
            for q in range(C_RWKV // GROUP_W):
                sl = slice(q * GROUP_W, (q + 1) * GROUP_W)
                s = st_ref[d, :, sl]
                y_ref[rows, sl] = y0_ref[rows, sl] + _dot_nt(q_ref[rows, sl], _block_diag(s, bd_mask))
                st_ref[d, :, sl] = (s * g_ref[j, :, sl] + z_ref[rows, sl]
                                    + _dot(s.astype(BF16), _block_diag(p_ref[rows, sl], bd_mask)))

    @pl.when(tab_ref[4, step] == 1)
    def _():
        for d in range(2):
            for h in range(N_HEADS):
                sfin_ref[0, d, h] = st_ref[d, :, h * HEAD_DIM:(h + 1) * HEAD_DIM]


def _scan_table(n_ctx_seq, t_ctx, n_s_seq, t_s):
    rows = []
    base = 0
    seq = 0
    for n_seq, t_len in ((n_ctx_seq, t_ctx), (n_s_seq, t_s)):
        assert t_len % (CARRY_CHUNKS * CHUNK) == 0
        n_c = t_len // (CARRY_CHUNKS * CHUNK)
        for _ in range(n_seq):
            for c in range(n_c):
                rows.append((base + c, base + n_c - 1 - c, seq, int(c == 0), int(c == n_c - 1)))
            base += n_c
            seq += 1
    return np.asarray(rows, dtype=np.int32).T.copy()


def _carry_states(q, y0, p, z, g, s0, table):
    n = q.shape[0]
    n_seq = s0.shape[0]
    c = C_RWKV

    def tok(which, d):
        return pl.BlockSpec((CARRY_CHUNKS * CHUNK, c), lambda s, tab: (tab[which, s], d))

    def decay(which, d):
        return pl.BlockSpec((CARRY_CHUNKS, 1, c), lambda s, tab: (tab[which, s], 0, d))

    state_spec = pl.BlockSpec((1, 2, N_HEADS, HEAD_DIM, HEAD_DIM), lambda s, tab: (tab[2, s], 0, 0, 0, 0))
    grid_spec = pltpu.PrefetchScalarGridSpec(
        num_scalar_prefetch=1,
        grid=(table.shape[1],),
        in_specs=[tok(0, 0), tok(0, 0), tok(0, 0), tok(0, 0), decay(0, 0),
                  tok(1, 1), tok(1, 1), tok(1, 1), tok(1, 1), decay(1, 1), state_spec],
        out_specs=[tok(0, 0), tok(1, 0), state_spec],
        scratch_shapes=[pltpu.VMEM((2, HEAD_DIM, c), F32)],
    )
    return pl.pallas_call(
        _carry_kernel,
        grid_spec=grid_spec,
        out_shape=[jax.ShapeDtypeStruct((n, c), F32), jax.ShapeDtypeStruct((n, c), F32),
                   jax.ShapeDtypeStruct((n_seq, 2, N_HEADS, HEAD_DIM, HEAD_DIM), F32)],
        compiler_params=_cparams(("arbitrary",)),
        name="rwkv_carry",
    )(jnp.asarray(table), q, y0, p, z, g, q, y0, p, z, g, s0)


def _merge_kernel(xc_ref, xl_ref, c_ref, fc_ref, fl_ref, yf_ref, yb_ref, r_ref, k_ref, v_ref, g_ref, n1_ref, wgt_ref, mod_ref,
                  wc_ref, wf_ref, wr_ref, wo_ref, gng_ref, gnb_ref, rk_ref, bd_ref, n2_ref, *rest, n_ctx, seq_s, route):
    if route:
        wrt_ref, brt_ref, x1_ref, h2_ref, comb_ref = rest
    else:
        x1_ref, h2_ref = rest
    row = _mod_row(pl.program_id(0), WIDE_TILE, n_ctx, seq_s)
    m = mod_ref[pl.ds(row, 1), :]
    sh1 = m[:, 0:D_MODEL]
    sc1 = m[:, D_MODEL:2 * D_MODEL]
    g1 = m[:, 2 * D_MODEL:3 * D_MODEL]
    sh2 = m[:, 3 * D_MODEL:4 * D_MODEL]
    sc2 = m[:, 4 * D_MODEL:5 * D_MODEL]
    bd = bd_ref[...]
    inv = 1.0 / HEAD_DIM
    y = yf_ref[...] + yb_ref[...]
    y_hi, y_lo = _split(y)
    d = y - (_dot(y_hi, bd) + _dot(y_lo, bd)) * inv
    var = _seg_sum(d * d, bd) * inv
    yn = d * lax.rsqrt(var + GN_EPS) * gng_ref[...] + gnb_ref[...]
    v = v_ref[...]
    yn = yn + _seg_sum(r_ref[...] * k_ref[...] * rk_ref[...], bd) * v
    yr = _dot((yn * g_ref[...]).astype(BF16), wr_ref[...])
    yc = _dot(c_ref[...], wc_ref[...])
    yf = _dot(_pick_stream(pl.program_id(0), WIDE_TILE, n_ctx, fc_ref, fl_ref), wf_ref[...])
    x = _pick_stream(pl.program_id(0), WIDE_TILE, n_ctx, xc_ref, xl_ref)
    hn = x * lax.rsqrt(jnp.mean(x * x, axis=-1, keepdims=True) + RMS_EPS) * n1_ref[...]
    hn = (hn * (1.0 + sc1) + sh1).astype(BF16)
    dm = D_MODEL
    merged = (_sigmoid(_dot(hn, wgt_ref[:, 0:dm])) * yc + _sigmoid(_dot(hn, wgt_ref[:, dm:2 * dm])) * yf
              + _sigmoid(_dot(hn, wgt_ref[:, 2 * dm:3 * dm])) * yr)
    x1 = x + g1 * _dot(merged.astype(BF16), wo_ref[...])
    x1_ref[...] = x1
    h2 = x1 * lax.rsqrt(jnp.mean(x1 * x1, axis=-1, keepdims=True) + RMS_EPS) * n2_ref[...]
    h2 = h2 * (1.0 + sc2) + sh2
    h2_ref[...] = h2.astype(BF16)
    if not route:
        return
    logits = _dot_x3(h2, wrt_ref[...]) + brt_ref[...]
    lane = lax.broadcasted_iota(jnp.int32, logits.shape, 1).astype(F32)
    neg = jnp.float32(-jnp.inf)
    lg = jnp.where(lane < N_EXPERTS, logits, neg)
    m1 = jnp.max(lg, axis=-1, keepdims=True)
    i1 = jnp.min(jnp.where(lg == m1, lane, float(LANES)), axis=-1, keepdims=True)
    lg2 = jnp.where(lane == i1, neg, lg)
    m2 = jnp.max(lg2, axis=-1, keepdims=True)
    i2 = jnp.min(jnp.where(lg2 == m2, lane, float(LANES)), axis=-1, keepdims=True)
    e = jnp.exp(m2 - m1)
    p1 = 1.0 / (1.0 + e)
    comb_ref[...] = jnp.where(lane == i1, p1, 0.0) + jnp.where(lane == i2, e * p1, 0.0)


def _merge(x, n, conv_h, fno_h, yf, yb, r, k, v, g, gain1, w_gates, mod, w_conv_out, w_fno_out, w_rwkv_out, w_o,
           gn_g, gn_b, r_k, gain2, n_ctx, seq_s, router=None):
    xc, xl, x_lat0 = _stream_args(x, WIDE_TILE, n_ctx)
    fc, fl, f_lat0 = _stream_args(fno_h, WIDE_TILE, n_ctx)
    c = C_RWKV
    tile = lambda w: pl.BlockSpec((WIDE_TILE, w), lambda i: (i, 0))
    full = lambda shape: pl.BlockSpec(shape, lambda i: (0,) * len(shape), pipeline_mode=pl.Buffered(1))
    route = router is not None
    route_args, route_specs, route_out, route_shape = [], [], [], []
    if route:
        route_args = [jnp.zeros((D_MODEL, LANES), F32).at[:, :N_EXPERTS].set(router[0]),
                      jnp.zeros((1, LANES), F32).at[0, :N_EXPERTS].set(router[1])]
        route_specs = [full((D_MODEL, LANES)), full((1, LANES))]
        route_out = [tile(LANES)]
        route_shape = [jax.ShapeDtypeStruct((n, LANES), F32)]
    return pl.pallas_call(
        functools.partial(_merge_kernel, n_ctx=n_ctx, seq_s=seq_s, route=route),
        grid=(n // WIDE_TILE,),
        in_specs=_stream_specs(WIDE_TILE, D_MODEL, n_ctx, x_lat0) + [tile(C_CONV)]
        + _stream_specs(WIDE_TILE, C_FNO, n_ctx, f_lat0) + [
            tile(c), tile(c), tile(c), tile(c), tile(c), tile(c),
            full((1, D_MODEL)), full((D_MODEL, 3 * D_MODEL)), full((8, 6 * D_MODEL)),
            full((C_CONV, D_MODEL)), full((C_FNO, D_MODEL)), full((c, D_MODEL)), full((D_MODEL, D_MODEL)),
            full((1, c)), full((1, c)), full((1, c)), full((c, c)), full((1, D_MODEL)),
        ] + route_specs,
        out_specs=[tile(D_MODEL), tile(D_MODEL)] + route_out,
        out_shape=[jax.ShapeDtypeStruct((n, D_MODEL), F32), jax.ShapeDtypeStruct((n, D_MODEL), BF16)] + route_shape,
        compiler_params=_cparams(("parallel",)),
        name="merge",
    )(xc, xl, conv_h, fc, fl, yf, yb, r, k, v, g, gain1.reshape(1, D_MODEL), w_gates, mod,
      w_conv_out.astype(BF16), w_fno_out.astype(BF16), w_rwkv_out.astype(BF16), w_o.astype(BF16),
      gn_g.reshape(1, c), gn_b.reshape(1, c), r_k.reshape(1, c), _head_ones(), gain2.reshape(1, D_MODEL), *route_args)


def _ffn_kernel(x_ref, h_ref, mod_ref, wg_ref, wu_ref, wd_ref, o_ref, *, n_ctx, seq_s):
    row = _mod_row(pl.program_id(0), FFN_TILE, n_ctx, seq_s)
    g2 = mod_ref[pl.ds(row, 1), 5 * D_MODEL:6 * D_MODEL]
    h = h_ref[...]
    half = D_FF // 2
    acc = jnp.zeros((FFN_TILE, D_MODEL), F32)
    for p in range(2):
        sl = slice(p * half, (p + 1) * half)
        t = _silu(_dot(h, wg_ref[:, sl])) * _dot(h, wu_ref[:, sl])
        acc = acc + _dot(t.astype(BF16), wd_ref[sl, :])
    o_ref[...] = x_ref[...] + g2 * acc


def _ffn_dense(x1, h2, mod, w_gate, w_up, w_down, n_ctx, seq_s):
    n = x1.shape[0]
    tile = pl.BlockSpec((FFN_TILE, D_MODEL), lambda i: (i, 0))
    full = lambda shape: pl.BlockSpec(shape, lambda i: (0,) * len(shape), pipeline_mode=pl.Buffered(1))
    return pl.pallas_call(
        functools.partial(_ffn_kernel, n_ctx=n_ctx, seq_s=seq_s),
        grid=(n // FFN_TILE,),
        in_specs=[tile, tile, full((8, 6 * D_MODEL)), full((D_MODEL, D_FF)), full((D_MODEL, D_FF)),
                  full((D_FF, D_MODEL))],
        out_specs=tile,
        out_shape=jax.ShapeDtypeStruct((n, D_MODEL), F32),
        compiler_params=_cparams(("parallel",)),
        name="ffn_dense",
    )(x1, h2, mod, w_gate.astype(BF16), w_up.astype(BF16), w_down.astype(BF16))


def _moe_kernel(x_ref, h_ref, comb_ref, mod_ref, wg_ref, wu_ref, wd_ref, fin_ref, *rest, n_ctx, seq_s, final):
    outs, (acc_ref, rank_c_ref, rank_r_ref, comb_t_ref) = rest[:-4], rest[-4:]
    e = pl.program_id(1)
    parts = [slice(p * MOE_PART, (p + 1) * MOE_PART) for p in range(MOE_TILE // MOE_PART)]

    @pl.when(e == 0)
    def _():
        acc_ref[...] = jnp.zeros_like(acc_ref)
        before = (lax.broadcasted_iota(jnp.int32, (MOE_PART, MOE_PART), 1)
                  < lax.broadcasted_iota(jnp.int32, (MOE_PART, MOE_PART), 0)).astype(BF16)
        for part in parts:
            comb = comb_ref[part, :]
            rank_c_ref[part, :] = _dot(before, (comb > 0.0).astype(BF16))
            comb_t = comb.T
            comb_t_ref[:, part] = comb_t
            rank_r_ref[:, part] = _dot_nt((comb_t > 0.0).astype(BF16), before)

    comb = comb_ref[...]
    lane = lax.broadcasted_iota(jnp.int32, comb.shape, 1)
    w_col = jnp.sum(jnp.where(lane == e, comb, 0.0), axis=-1, keepdims=True)
    rank_col = jnp.sum(jnp.where(lane == e, rank_c_ref[...], 0.0), axis=-1, keepdims=True)
    w_row = comb_t_ref[pl.ds(e, 1), :]
    rank_row = rank_r_ref[pl.ds(e, 1), :]
    count = functools.reduce(jnp.maximum, [jnp.sum((w_row[:, part] > 0.0).astype(jnp.int32)) for part in parts])

    def body(j, carry):
        base = (j * MOE_ROWS).astype(F32)
        slot_r = lax.broadcasted_iota(jnp.int32, (MOE_ROWS, MOE_PART), 0).astype(F32) + base
        slot_c = lax.broadcasted_iota(jnp.int32, (MOE_PART, MOE_ROWS), 1).astype(F32) + base
        rows = []
        for part in parts:
            hit = jnp.logical_and(rank_row[:, part] == slot_r, w_row[:, part] > 0.0)
            rows.append(_dot(jnp.where(hit, 1.0, 0.0).astype(BF16), h_ref[part, :]).astype(BF16))
        hg = jnp.concatenate(rows, axis=0)
        t = _silu(_dot(hg, wg_ref[0])) * _dot(hg, wu_ref[0])
        y = _dot(t.astype(BF16), wd_ref[0]).astype(BF16)
        for p, part in enumerate(parts):
            hit = jnp.logical_and(rank_col[part, :] == slot_c, w_col[part, :] > 0.0)
            acc_ref[part, :] += w_col[part, :] * _dot(jnp.where(hit, 1.0, 0.0).astype(BF16),
                                                      y[p * MOE_ROWS:(p + 1) * MOE_ROWS])
        return carry

    lax.fori_loop(0, (count + MOE_ROWS - 1) // MOE_ROWS, body, 0)

    @pl.when(e == N_EXPERTS - 1)
    def _():
        row = _mod_row(pl.program_id(0), MOE_TILE, n_ctx, seq_s)
        g2 = mod_ref[pl.ds(row, 1), 5 * D_MODEL:6 * D_MODEL]
        y = x_ref[...] + g2 * acc_ref[...]
        if not final:
            outs[0][...] = y
        else:
            y = y * lax.rsqrt(jnp.mean(y * y, axis=-1, keepdims=True) + RMS_EPS) * fin_ref[...]
            is_ctx = pl.program_id(0) < n_ctx // MOE_TILE

            @pl.when(is_ctx)
            def _():
                outs[0][...] = y

            @pl.when(jnp.logical_not(is_ctx))
            def _():
                outs[1][...] = y


def _ffn_moe(x1, h2, comb, mod, w_gate, w_up, w_down, n_ctx, seq_s, final_gain=None):
    n = x1.shape[0]
    final = final_gain is not None
    once = pl.Buffered(1)
    tile = lambda w: pl.BlockSpec((MOE_TILE, w), lambda i, e: (i, 0), pipeline_mode=once)
    if final:
        n_ctx_tiles = n_ctx // MOE_TILE
        out_specs = [pl.BlockSpec((MOE_TILE, D_MODEL), lambda i, e: (jnp.minimum(i, n_ctx_tiles - 1), 0),
                                  pipeline_mode=once),
                     pl.BlockSpec((MOE_TILE, D_MODEL), lambda i, e: (jnp.maximum(i - n_ctx_tiles, 0), 0),
                                  pipeline_mode=once)]
        out_shape = [jax.ShapeDtypeStruct((n_ctx, D_MODEL), F32), jax.ShapeDtypeStruct((n - n_ctx, D_MODEL), F32)]
        gain = final_gain.reshape(1, D_MODEL)
    else:
        out_specs = [tile(D_MODEL)]
        out_shape = [jax.ShapeDtypeStruct((n, D_MODEL), F32)]
        gain = jnp.ones((1, D_MODEL), F32)
    out = pl.pallas_call(
        functools.partial(_moe_kernel, n_ctx=n_ctx, seq_s=seq_s, final=final),
        grid=(n // MOE_TILE, N_EXPERTS),
        in_specs=[tile(D_MODEL), tile(D_MODEL), tile(LANES),
                  pl.BlockSpec((8, 6 * D_MODEL), lambda i, e: (0, 0)),
                  pl.BlockSpec((1, D_MODEL, D_FF_E), lambda i, e: (e, 0, 0)),
                  pl.BlockSpec((1, D_MODEL, D_FF_E), lambda i, e: (e, 0, 0)),
                  pl.BlockSpec((1, D_FF_E, D_MODEL), lambda i, e: (e, 0, 0)),
                  pl.BlockSpec((1, D_MODEL), lambda i, e: (0, 0))],
        out_specs=out_specs,
        out_shape=out_shape,
        scratch_shapes=[pltpu.VMEM((MOE_TILE, D_MODEL), F32), pltpu.VMEM((MOE_TILE, LANES), F32),
                        pltpu.VMEM((LANES, MOE_TILE), F32), pltpu.VMEM((LANES, MOE_TILE), F32)],
        compiler_params=_cparams(("arbitrary", "arbitrary") if final else ("parallel", "arbitrary")),
        name="ffn_moe",
    )(x1, h2, comb, mod, w_gate.astype(BF16), w_up.astype(BF16), w_down.astype(BF16), gain)
    return tuple(out) if final else out[0]


def _final_kernel(x_ref, g_ref, o_ref):
    x = x_ref[...]
    o_ref[...] = x * lax.rsqrt(jnp.mean(x * x, axis=-1, keepdims=True) + RMS_EPS) * g_ref[...]


def _final_norm(x, gain, row0, n_rows):
    blk0 = row0 // FFN_TILE
    return pl.pallas_call(
        _final_kernel,
        grid=(n_rows // FFN_TILE,),
        in_specs=[pl.BlockSpec((FFN_TILE, D_MODEL), lambda i: (blk0 + i, 0)),
                  pl.BlockSpec((1, D_MODEL), lambda i: (0, 0))],
        out_specs=pl.BlockSpec((FFN_TILE, D_MODEL), lambda i: (i, 0)),
        out_shape=jax.ShapeDtypeStruct((n_rows, D_MODEL), F32),
        compiler_params=_cparams(("parallel",)),
        name="final_norm",
    )(x, gain.reshape(1, D_MODEL))


def kernel(x_prompt, x_sample, state_rwkv, c, c_ctx, norm1, norm2, w_ada, b_ada, w_in, dw_w, dw_b, conv_ln_g, conv_ln_b, w_conv_out, w_fno_out, shift_mu, g_up, dec_w0, dec_up, iclr_a0, iclr_up, k_k, k_a, r_k, gn_g, gn_b, w_rwkv_out, w_o, ffn_w_gate, ffn_w_up, ffn_w_down, w_router, b_router, moe_w_gate, moe_w_up, moe_w_down, final_norm):
    b_p, t_p, _ = x_prompt.shape
    b_s, t_s, _ = x_sample.shape
    depth = w_in.shape[0]
    n_ctx = b_p * t_p
    n_lat = b_s * t_s
    assert t_p == ROW_TILE and t_s % MOE_TILE == 0 and n_ctx % MOE_TILE == 0 and b_s <= CTX_ROW
    assert MOE_TILE % FFN_TILE == 0 and n_ctx % WIDE_TILE == 0 and t_s % WIDE_TILE == 0 and INPROJ_TILE % WIDE_TILE == 0
    assert t_s % GRID_W == 0 and CHUNK == GRID_W

    n = n_ctx + n_lat
    x = (x_prompt.reshape(n_ctx, D_MODEL), x_sample.reshape(n_lat, D_MODEL))
    cond = jnp.zeros((8, D_MODEL), F32).at[:b_s].set(c).at[CTX_ROW].set(c_ctx)
    mods = _ada(cond, w_ada, b_ada)
    table = _scan_table(b_p, t_p, b_s, t_s)
    zero_state = jnp.zeros((b_p, 2, N_HEADS, HEAD_DIM, HEAD_DIM), F32)

    ctx_states = []
    for l in range(depth):
        mod = mods[l]
        u_conv, u_fno, u_rw = _inproj(x, n, mod, norm1[l], w_in[l, :, :OFF_GATE].astype(BF16), n_ctx, t_s)
        conv_h = _conv_branch(u_conv, dw_w[l], dw_b[l], conv_ln_g[l], conv_ln_b[l], n_ctx)
        fno_h = (_fno_branch(u_fno, 0, b_p, t_p), _fno_branch(u_fno, n_ctx, b_s, t_s))
        r, k, v, g, *summaries = _rwkv_project_and_summarise(u_rw, shift_mu[l], g_up[l], dec_w0[l], dec_up[l],
                                                             iclr_a0[l], iclr_up[l], k_k[l], k_a[l], n_ctx, t_s)
        s0 = jnp.concatenate([zero_state, state_rwkv[:, l]], axis=0)
        yf, yb, s_fin = _carry_states(*summaries, s0, table)
        ctx_states.append(s_fin[:b_p])
        i = l // 2
        x1, h2, *comb = _merge(x, n, conv_h, fno_h, yf, yb, r, k, v, g, norm1[l], w_in[l, :, OFF_GATE:].astype(BF16), mod, w_conv_out[l], w_fno_out[l],
                              w_rwkv_out[l], w_o[l], gn_g[l], gn_b[l], r_k[l], norm2[l], n_ctx, t_s,
                               router=(w_router[i], b_router[i]) if l % 2 == 1 else None)
        if l % 2 == 0:
            x = _ffn_dense(x1, h2, mod, ffn_w_gate[i], ffn_w_up[i], ffn_w_down[i], n_ctx, t_s)
        else:
            x = _ffn_moe(x1, h2, comb[0], mod, moe_w_gate[i], moe_w_up[i], moe_w_down[i], n_ctx, t_s,
                         final_gain=final_norm if l == depth - 1 else None)

    if isinstance(x, tuple):
        y_prompt, y_sample = x
    else:
        y_prompt, y_sample = _final_norm(x, final_norm, 0, n_ctx), _final_norm(x, final_norm, n_ctx, n_lat)
    y_prompt = y_prompt.reshape(b_p, t_p, D_MODEL)
    y_sample = y_sample.reshape(b_s, t_s, D_MODEL)
    new_state = jnp.stack(ctx_states, axis=1).astype(x_prompt.dtype)
    return (y_prompt, y_sample, new_state)
```

```python
import functools

import numpy as np
import jax
import jax.numpy as jnp
from jax import lax
from jax.experimental import pallas as pl
from jax.experimental.pallas import tpu as pltpu

F32 = jnp.float32
BF16 = jnp.bfloat16

D_MODEL = 1024
GRID_W = 64
C_CONV = 256
CONV_W = 31
C_FNO = 256
FNO_GW = 64
N_HEADS = 8
HEAD_DIM = 64
C_RWKV = N_HEADS * HEAD_DIM
G_RANK = 128
W_RANK = 64
A_RANK = 64
D_FF = 2816
N_EXPERTS = 8
D_FF_E = 1408
RMS_EPS = 1e-6
LN_EPS = 1e-5
GN_EPS = 64e-5

OFF_FNO = 2 * C_CONV
OFF_RWKV = OFF_FNO + C_FNO
N_SHIFT = 3 * C_RWKV + G_RANK + 2 * (W_RANK + A_RANK)
OFF_GATE = OFF_RWKV + N_SHIFT
D_IN = OFF_GATE + 3 * D_MODEL

ROW_TILE = 256
FFN_TILE = 1024
WIDE_TILE = 512
CHUNK = 64
GROUP = 4
GROUP_W = GROUP * HEAD_DIM
SUMMARY_CHUNKS = 4
CARRY_CHUNKS = 4
MOE_TILE = 1024
MOE_PART = 256
MOE_ROWS = 80
CTX_ROW = 4
LANES = 128
VMEM_LIMIT = 56 * 1024 * 1024


def _cparams(sem):
    return pltpu.CompilerParams(dimension_semantics=sem, vmem_limit_bytes=VMEM_LIMIT)


def _sigmoid(x):
    return 1.0 / (1.0 + jnp.exp(-x))


def _silu(x):
    return x * _sigmoid(x)


def _dot(a, b):
    return jnp.dot(a, b, preferred_element_type=F32)


def _split(x):
    hi = x.astype(BF16)
    lo = (x - hi.astype(F32)).astype(BF16)
    return hi, lo


def _dot_x3(a, b):
    ah, al = _split(a)
    bh, bl = _split(b)
    return _dot(ah, bh) + (_dot(ah, bl) + _dot(al, bh))


def _seg_sum(x, bd):
    return _dot(x.astype(BF16), bd)


def _mod_row(i, tile, n_ctx_rows, seq_s):
    n_ctx_tiles = n_ctx_rows // tile
    return jnp.where(i < n_ctx_tiles, CTX_ROW, (i - n_ctx_tiles) // (seq_s // tile))


def _stream_specs(tile, width, n_ctx, lat_block0):
    n_ctx_tiles = n_ctx // tile
    return [pl.BlockSpec((tile, width), lambda i: (jnp.minimum(i, n_ctx_tiles - 1), 0)),
            pl.BlockSpec((tile, width), lambda i: (jnp.maximum(i - n_ctx_tiles, 0) + lat_block0, 0))]


def _stream_args(x, tile, n_ctx):
    if isinstance(x, tuple):
        return x[0], x[1], 0
    return x, x, n_ctx // tile


def _pick_stream(i, tile, n_ctx, ctx_ref, lat_ref):
    return jnp.where(i < n_ctx // tile, ctx_ref[...], lat_ref[...])


def _ada_kernel(c_ref, w_ref, b_ref, o_ref):
    s = _silu(c_ref[...])
    o_ref[0] = _dot(s.astype(BF16), w_ref[0].astype(BF16)) + b_ref[0]


def _ada(cond, w_ada, b_ada):
    n_l = w_ada.shape[0]
    tn = 1536
    return pl.pallas_call(
        _ada_kernel,
        grid=(n_l, 6 * D_MODEL // tn),
        in_specs=[
            pl.BlockSpec((8, D_MODEL), lambda l, j: (0, 0)),
            pl.BlockSpec((1, D_MODEL, tn), lambda l, j: (l, 0, j)),
            pl.BlockSpec((1, 1, tn), lambda l, j: (l, 0, j)),
        ],
        out_specs=pl.BlockSpec((1, 8, tn), lambda l, j: (l, 0, j)),
        out_shape=jax.ShapeDtypeStruct((n_l, 8, 6 * D_MODEL), F32),
        compiler_params=_cparams(("parallel", "parallel")),
        name="ada",
    )(cond, w_ada, b_ada.reshape(n_l, 1, 6 * D_MODEL))


def _inproj_kernel(xc_ref, xl_ref, mod_ref, g_ref, w_ref, oc_ref, of_ref, or_ref, *, n_ctx, seq_s):
    row = _mod_row(pl.program_id(0), WIDE_TILE, n_ctx, seq_s)
    m = mod_ref[pl.ds(row, 1), :]
    sh = m[:, 0:D_MODEL]
    sc = m[:, D_MODEL:2 * D_MODEL]
    x = _pick_stream(pl.program_id(0), WIDE_TILE, n_ctx, xc_ref, xl_ref)
    y = x * lax.rsqrt(jnp.mean(x * x, axis=-1, keepdims=True) + RMS_EPS) * g_ref[...]
    h = (y * (1.0 + sc) + sh).astype(BF16)
    oc_ref[...] = _dot(h, w_ref[:, 0:OFF_FNO])
    of_ref[...] = _dot(h, w_ref[:, OFF_FNO:OFF_RWKV])
    or_ref[...] = _dot(h, w_ref[:, OFF_RWKV:OFF_GATE])


def _inproj(x, n, mod, gain, w_in, n_ctx, seq_s):
    xc, xl, lat0 = _stream_args(x, WIDE_TILE, n_ctx)
    widths = (OFF_FNO, C_FNO, N_SHIFT)
    return pl.pallas_call(
        functools.partial(_inproj_kernel, n_ctx=n_ctx, seq_s=seq_s),
        grid=(n // WIDE_TILE,),
        in_specs=_stream_specs(WIDE_TILE, D_MODEL, n_ctx, lat0) + [
            pl.BlockSpec((8, 6 * D_MODEL), lambda i: (0, 0)),
            pl.BlockSpec((1, D_MODEL), lambda i: (0, 0)),
            pl.BlockSpec((D_MODEL, OFF_GATE), lambda i: (0, 0), pipeline_mode=pl.Buffered(1)),
        ],
        out_specs=[pl.BlockSpec((WIDE_TILE, w), lambda i: (i, 0)) for w in widths],
        out_shape=[jax.ShapeDtypeStruct((n, w), F32) for w in widths],
        compiler_params=_cparams(("parallel",)),
        name="inproj",
    )(xc, xl, mod, gain.reshape(1, D_MODEL), w_in)


_CONV_HALO = 16
_CONV_ROWS = 64


def _conv_kernel(u_ref, w_ref, b_ref, g_ref, be_ref, o_ref, pad_ref, sh_ref, *, n_ctx):
    is_ctx = pl.program_id(0) < n_ctx // ROW_TILE
    zeros = jnp.zeros((_CONV_HALO, C_CONV), F32)
    n_parts = ROW_TILE // _CONV_ROWS

    def glu(lo, hi):
        return u_ref[lo:hi, 0:C_CONV] * _sigmoid(u_ref[lo:hi, C_CONV:2 * C_CONV])

    def finish(starts, n_rows):
        for s in range(8):
            sh_ref[s, 0:n_rows - 8, :] = pad_ref[s:s + n_rows - 8, :]
        for p in range(n_parts):
            acc = jnp.zeros((_CONV_ROWS, C_CONV), F32)
            for j in range(CONV_W):
                o = starts[p] + j
                acc = acc + w_ref[j:j + 1, :] * sh_ref[o % 8, o - o % 8:o - o % 8 + _CONV_ROWS, :]
            acc = acc + b_ref[...]
            mu = jnp.mean(acc, axis=-1, keepdims=True)
            d = acc - mu
            var = jnp.mean(d * d, axis=-1, keepdims=True)
            y = d * lax.rsqrt(var + LN_EPS) * g_ref[...] + be_ref[...]
            o_ref[p * _CONV_ROWS:(p + 1) * _CONV_ROWS, :] = _silu(y).astype(o_ref.dtype)

    shift = _CONV_HALO - CONV_W // 2

    @pl.when(is_ctx)
    def _():
        pad_ref[0:_CONV_HALO, :] = zeros
        pad_ref[_CONV_HALO:_CONV_HALO + ROW_TILE, :] = glu(0, ROW_TILE)
        pad_ref[_CONV_HALO + ROW_TILE:2 * _CONV_HALO + ROW_TILE, :] = zeros
        finish([shift + p * _CONV_ROWS for p in range(n_parts)], ROW_TILE + 2 * _CONV_HALO)

    @pl.when(jnp.logical_not(is_ctx))
    def _():
        stride = GRID_W + 2 * _CONV_HALO
        for p in range(n_parts):
            pad_ref[p * stride:p * stride + _CONV_HALO, :] = zeros
            pad_ref[p * stride + _CONV_HALO:p * stride + _CONV_HALO + GRID_W, :] = glu(p * GRID_W, (p + 1) * GRID_W)
            pad_ref[p * stride + _CONV_HALO + GRID_W:(p + 1) * stride, :] = zeros
        finish([p * stride + shift for p in range(n_parts)], n_parts * stride)


def _conv_branch(u_conv, dw_w, dw_b, ln_g, ln_b, n_ctx):
    n = u_conv.shape[0]
    assert _CONV_ROWS == GRID_W and ROW_TILE % GRID_W == 0
    vec = pl.BlockSpec((1, C_CONV), lambda i: (0, 0))
    return pl.pallas_call(
        functools.partial(_conv_kernel, n_ctx=n_ctx),
        grid=(n // ROW_TILE,),
        in_specs=[
            pl.BlockSpec((ROW_TILE, 2 * C_CONV), lambda i: (i, 0)),
            pl.BlockSpec((CONV_W, C_CONV), lambda i: (0, 0)),
            vec, vec, vec,
        ],
        out_specs=pl.BlockSpec((ROW_TILE, C_CONV), lambda i: (i, 0)),
        out_shape=jax.ShapeDtypeStruct((n, C_CONV), BF16),
        scratch_shapes=[pltpu.VMEM((ROW_TILE // GRID_W * (GRID_W + 2 * _CONV_HALO), C_CONV), F32),
                        pltpu.VMEM((8, ROW_TILE // GRID_W * (GRID_W + 2 * _CONV_HALO), C_CONV), F32)],
        compiler_params=_cparams(("parallel",)),
        name="conv_branch",
    )(u_conv, dw_w, dw_b.reshape(1, C_CONV), ln_g.reshape(1, C_CONV), ln_b.reshape(1, C_CONV))


def _dft_tables(t_len):
    def cs(n):
        k = np.arange(n, dtype=np.int64)
        ang = 2.0 * np.pi * ((k[:, None] * k[None, :]) % n).astype(np.float64) / n
        return np.cos(ang), np.sin(ang)
    cg, sg = cs(FNO_GW)
    eye = np.eye(C_FNO // FNO_GW)
    w1 = np.concatenate([np.kron(eye, cg), np.kron(eye, sg)], axis=1)
    ct, st = cs(t_len)
    w2 = np.concatenate([ct, -st], axis=1)
    return jnp.asarray(w1, dtype=F32).astype(BF16), jnp.asarray(w2, dtype=F32).astype(BF16)


def _fno_kernel(u_ref, w1_ref, w2_ref, o_ref, hs_ref, *, t_len, scale):
    @pl.when(pl.program_id(1) == 0)
    def _():
        hc = _dot(u_ref[...].astype(BF16), w1_ref[...])
        hs_ref[0:t_len, :] = hc[:, 0:C_FNO].astype(BF16)
        hs_ref[t_len:2 * t_len, :] = hc[:, C_FNO:2 * C_FNO].astype(BF16)

    o_ref[...] = (_dot(w2_ref[...], hs_ref[...]) * scale).astype(o_ref.dtype)


def _fno_branch(u_fno, row0, n_seq, t_len):
    w1, w2 = _dft_tables(t_len)
    tk = t_len
    blk0 = row0 // t_len
    return pl.pallas_call(
        functools.partial(_fno_kernel, t_len=t_len, scale=float(1.0 / np.sqrt(t_len * FNO_GW))),
        grid=(n_seq, t_len // tk),
        in_specs=[
            pl.BlockSpec((t_len, C_FNO), lambda b, j: (blk0 + b, 0)),
            pl.BlockSpec((C_FNO, 2 * C_FNO), lambda b, j: (0, 0)),
            pl.BlockSpec((tk, 2 * t_len), lambda b, j: (j, 0), pipeline_mode=pl.Buffered(1)),
        ],
        out_specs=pl.BlockSpec((tk, C_FNO), lambda b, j: (b * (t_len // tk) + j, 0)),
        out_shape=jax.ShapeDtypeStruct((n_seq * t_len, C_FNO), BF16),
        scratch_shapes=[pltpu.VMEM((2 * t_len, C_FNO), BF16)],
        compiler_params=_cparams(("parallel", "arbitrary")),
        name="fno_branch",
    )(u_fno, w1, w2)


def _prep_pieces(i, z_ref, zp_ref, zn_ref, mu_ref, gup_ref, w0_ref, dup_ref, a0_ref, aup_ref, kk_ref, bd_ref,
                 r_refs, k_refs, v_refs, kkn_ref, g_ref, lw_ref, a_ref, pad_ref, n_ctx, seq_s):
    n_ctx_tiles = n_ctx // ROW_TILE
    per_seq = seq_s // ROW_TILE
    j = (i - n_ctx_tiles) % per_seq
    first = jnp.logical_or(i < n_ctx_tiles, j == 0)
    last = jnp.logical_or(i < n_ctx_tiles, j == per_seq - 1)
    c = C_RWKV

    def shifted(lo, hi):
        z = pad_ref[8:8 + ROW_TILE, lo:hi]
        zp = pad_ref[7:7 + ROW_TILE, lo:hi]
        zn = pad_ref[9:9 + ROW_TILE, lo:hi]
        return z + mu_ref[:, lo:hi] * (0.5 * (zp + zn) - z)

    def fill():
        pad_ref[8:8 + ROW_TILE, :] = z_ref[...]
        pad_ref[0:8, :] = jnp.where(first, 0.0, zp_ref[...])
        pad_ref[8 + ROW_TILE:16 + ROW_TILE, :] = jnp.where(last, 0.0, zn_ref[...])

    def receptance():
        r = shifted(0, c)
        for ref in r_refs:
            ref[...] = r

    def key():
        k = shifted(c, 2 * c)
        for ref in k_refs:
            ref[...] = k
        kx = k * kk_ref[...]
        nrm = jnp.sqrt(_seg_sum(kx * kx, bd_ref[...]))
        kkn_ref[...] = kx / jnp.maximum(nrm, 1e-12)

    def value():
        v = shifted(2 * c, 3 * c)
        for ref in v_refs:
            ref[...] = v

    def gate():
        o = 3 * c
        g_ref[...] = _dot(_sigmoid(shifted(o, o + G_RANK)).astype(BF16), gup_ref[...])

    def decay():
        o = 3 * c + G_RANK
        xw = _dot(jnp.tanh(shifted(o, o + 2 * W_RANK)).astype(BF16), dup_ref[...]) + w0_ref[...]
        soft = jnp.maximum(-xw, 0.0) + jnp.log(1.0 + jnp.exp(-jnp.abs(xw)))
        lw_ref[...] = -jnp.exp(-soft - 0.5)

    def iclr():
        o = 3 * c + G_RANK + 2 * W_RANK
        xa = _dot(shifted(o, o + 2 * A_RANK).astype(BF16), aup_ref[...]) + a0_ref[...]
        a_ref[...] = _sigmoid(xa)

    return [fill, receptance, key, value, gate, decay, iclr]


def _block_diag2(w):
    z = jnp.zeros_like(w[0])
    return jnp.concatenate([jnp.concatenate([w[0], z], axis=1), jnp.concatenate([z, w[1]], axis=1)], axis=0)


def _head_ones():
    return jnp.asarray(np.kron(np.eye(N_HEADS), np.ones((HEAD_DIM, HEAD_DIM))), dtype=BF16)


def _block_diag(x, mask):
    xb = x.astype(BF16)
    return jnp.where(mask, jnp.concatenate([xb] * GROUP, axis=0), jnp.zeros((), BF16))


def _dot_nt(a, b):
    return lax.dot_general(a, b, (((1,), (1,)), ((), ())), preferred_element_type=F32)


def _diag_blocks(prod, lane_head):
    out = jnp.where(lane_head == 0, prod[0:HEAD_DIM], 0.0)
    for h in range(1, GROUP):
        out = out + jnp.where(lane_head == h, prod[h * HEAD_DIM:(h + 1) * HEAD_DIM], 0.0)
    return out


def _summary_body(r_ref, k_ref, v_ref, n_ref, lw_ref, a_ref, ka_ref, q_ref, y0_ref, p_ref, z_ref, g_ref, fillers=()):
    fillers = list(fillers)
    fill_one = lambda: fillers.pop(0)() if fillers else None
    row = lax.broadcasted_iota(jnp.int32, (CHUNK, GROUP_W), 0)
    lane = lax.broadcasted_iota(jnp.int32, (CHUNK, GROUP_W), 1)
    col = lane % CHUNK
    lane_head = lane // HEAD_DIM
    bd_mask = (lax.broadcasted_iota(jnp.int32, (GROUP * CHUNK, GROUP_W), 0) // CHUNK
               == lax.broadcasted_iota(jnp.int32, (GROUP * CHUNK, GROUP_W), 1) // HEAD_DIM)
    bd = lambda x: _block_diag(x, bd_mask)
    ka = ka_ref[...]

    units = []
    for j in range(SUMMARY_CHUNKS):
        rows = slice(j * CHUNK, (j + 1) * CHUNK)
        r = r_ref[rows, :]
        k = k_ref[rows, :]
        v = v_ref[rows, :]
        kkn = n_ref[rows, :]
        for d in range(2):
            lanes = slice(d * C_RWKV, (d + 1) * C_RWKV)
            earlier = (col < row) if d == 0 else (col > row)
            upto = jnp.logical_or(earlier, row == col)
            tri = upto[:, 0:CHUNK].astype(BF16)
            lw = lw_ref[rows, lanes]
            h1 = lw.astype(BF16)
            r1 = lw - h1.astype(F32)
            h2 = r1.astype(BF16)
            h3 = (r1 - h2.astype(F32)).astype(BF16)
            cum = _dot(tri, h1) + (_dot(tri, h2) + _dot(tri, h3))
            e_in = jnp.exp(cum)
            e_ex = jnp.exp(cum - lw)
            e_ng = jnp.exp(-cum)
            a = a_ref[rows, lanes]
            kd = k * (1.0 + (a - 1.0) * ka)
            at = -kkn * e_ex
            rt = r * e_in
            bt = kkn * a * e_ng
            kt = kd * e_ng
            end = CHUNK - 1 if d == 0 else 0
            g_end = e_in[end:end + 1, :]
            g_ref[j, :, lanes] = g_end
            bh = bt * g_end
            kh = kt * g_end
            for q in range(C_RWKV // GROUP_W):
                sl = slice(q * GROUP_W, (q + 1) * GROUP_W)
                units.append(dict(rows=rows, out=slice(d * C_RWKV + q * GROUP_W, d * C_RWKV + (q + 1) * GROUP_W),
                                  earlier=earlier, upto=upto, at=at[:, sl], rt=rt[:, sl],
                                  ar=jnp.concatenate([at[:, sl], rt[:, sl]], axis=0).astype(BF16),
                                  bt=bt[:, sl], kt=kt[:, sl], v=v[:, sl], bh=bh[:, sl].astype(BF16),
                                  bk=jnp.concatenate([bh[:, sl], kh[:, sl]], axis=0).astype(BF16)))

    for u in units:
        sb = _dot_nt(u["ar"], bd(u["bt"]))
        sk = _dot_nt(u["ar"], bd(u["kt"]))
        u["lab"] = jnp.where(u["earlier"], sb[0:CHUNK], 0.0)
        u["mrb"] = jnp.where(u["upto"], sb[CHUNK:2 * CHUNK], 0.0).astype(BF16)
        u["lm"] = jnp.concatenate([jnp.where(u["earlier"], sk[0:CHUNK], 0.0),
                                   jnp.where(u["upto"], sk[CHUNK:2 * CHUNK], 0.0)], axis=0).astype(BF16)

    fill_one()
    eye = (row == col).astype(F32)
    pair = jnp.logical_and(row // 2 == col // 2, row != col)
    for u in units:
        u["t"] = eye + jnp.where(pair, u["lab"], 0.0)
    n = 2
    while n < CHUNK:
        m = jnp.logical_and(row // (2 * n) == col // (2 * n), row // n != col // n)
        for u in units:
            u["w"] = _dot(jnp.where(m, u["lab"], 0.0).astype(BF16), bd(u["t"]))
        for u in units:
            u["t"] = u["t"] + _dot(u["t"].astype(BF16), bd(u["w"]))
        fill_one()
        n *= 2

    for u in units:
        u["t"] = u["t"].astype(BF16)
        u["wm"] = _dot(u["t"], bd(u["at"]))
        u["lv"] = _dot(u["lm"], bd(u["v"]))
    for u in units:
        u["u0"] = _dot(u["t"], bd(u["lv"][0:CHUNK]))
    for u in units:
        q_ref[u["rows"], u["out"]] = (u["rt"] + _dot(u["mrb"], bd(u["wm"]))).astype(q_ref.dtype)
        y0_ref[u["rows"], u["out"]] = u["lv"][CHUNK:2 * CHUNK] + _dot(u["mrb"], bd(u["u0"]))
    for u in units:
        p = _dot(u["wm"].T.astype(BF16), u["bh"])
        p_ref[u["rows"], u["out"]] = _diag_blocks(p, lane_head).astype(p_ref.dtype)
        z = _dot(jnp.concatenate([u["u0"], u["v"]], axis=0).T.astype(BF16), u["bk"])
        z_ref[u["rows"], u["out"]] = _diag_blocks(z, lane_head)
    while fillers:
        fill_one()


def _rwkv_kernel(z_ref, zp_ref, zn_ref, mu_ref, gup_ref, w0_ref, dup_ref, a0_ref, aup_ref, kk_ref, bd_ref, ka_ref,
                 r_ref, k_ref, v_ref, g_ref, q_ref, y0_ref, p_ref, zz_ref, gd_ref,
                 pad_ref, r_n, k_n, v_n, n_n, lw_n, a_n, r_c, k_c, v_c, n_c, lw_c, a_c, *, n_ctx, seq_s, n_tiles):
    s = pl.program_id(0)
    new = (r_n, k_n, v_n, n_n, lw_n, a_n)
    cur = (r_c, k_c, v_c, n_c, lw_c, a_c)

    @pl.when(s == 0)
    def _():
        for ref in cur:
            ref[...] = jnp.zeros_like(ref)

    pieces = _prep_pieces(jnp.minimum(s, n_tiles - 1), z_ref, zp_ref, zn_ref, mu_ref, gup_ref, w0_ref, dup_ref,
                          a0_ref, aup_ref, kk_ref, bd_ref, (r_n, r_ref), (k_n, k_ref), (v_n, v_ref), n_n, g_ref,
                          lw_n, a_n, pad_ref, n_ctx, seq_s)
    pieces[0]()
    _summary_body(r_c, k_c, v_c, n_c, lw_c, a_c, ka_ref, q_ref, y0_ref, p_ref, zz_ref, gd_ref, fillers=pieces[1:])
    for dst, src in zip(cur, new):
        dst[...] = src[...]


def _rwkv_project_and_summarise(u_rw, mu, g_up, dec_w0, dec_up, iclr_a0, iclr_up, k_k, k_a, n_ctx, seq_s):
    n = u_rw.shape[0]
    assert ROW_TILE == SUMMARY_CHUNKS * CHUNK and CHUNK == HEAD_DIM and C_RWKV % GROUP_W == 0
    n_tiles = n // ROW_TILE
    halo = ROW_TILE // 8
    last_blk = n // 8 - 1
    c = C_RWKV
    full = lambda shape: pl.BlockSpec(shape, lambda s: (0,) * len(shape))
    this = lambda s: jnp.minimum(s, n_tiles - 1)
    prev = lambda s: jnp.maximum(s - 1, 0)
    proj = lambda w: pl.BlockSpec((ROW_TILE, w), lambda s: (this(s), 0))
    summ = lambda w: pl.BlockSpec((ROW_TILE, w), lambda s: (prev(s), 0))
    vm = lambda w: pltpu.VMEM((ROW_TILE, w), F32)
    return pl.pallas_call(
        functools.partial(_rwkv_kernel, n_ctx=n_ctx, seq_s=seq_s, n_tiles=n_tiles),
        grid=(n_tiles + 1,),
        in_specs=[
            proj(N_SHIFT),
            pl.BlockSpec((8, N_SHIFT), lambda s: (jnp.maximum(this(s) * halo - 1, 0), 0)),
            pl.BlockSpec((8, N_SHIFT), lambda s: (jnp.minimum((this(s) + 1) * halo, last_blk), 0)),
            full((1, N_SHIFT)), full((G_RANK, c)), full((1, 2 * c)), full((2 * W_RANK, 2 * c)),
            full((1, 2 * c)), full((2 * A_RANK, 2 * c)), full((1, c)), full((c, c)), full((1, c)),
        ],
        out_specs=[proj(c), proj(c), proj(c), proj(c), summ(2 * c), summ(2 * c), summ(2 * c), summ(2 * c),
                   pl.BlockSpec((SUMMARY_CHUNKS, 1, 2 * c), lambda s: (prev(s), 0, 0))],
        out_shape=[jax.ShapeDtypeStruct((n, c), F32)] * 4
        + [jax.ShapeDtypeStruct((n, 2 * c), BF16), jax.ShapeDtypeStruct((n, 2 * c), F32),
           jax.ShapeDtypeStruct((n, 2 * c), BF16), jax.ShapeDtypeStruct((n, 2 * c), F32),
           jax.ShapeDtypeStruct((n // CHUNK, 1, 2 * c), F32)],
        scratch_shapes=[pltpu.VMEM((ROW_TILE + 16, N_SHIFT), F32)]
        + [vm(c), vm(c), vm(c), vm(c), vm(2 * c), vm(2 * c)] * 2,
        compiler_params=_cparams(("arbitrary",)),
        name="rwkv_project_summarise",
    )(u_rw, u_rw, u_rw, mu.reshape(1, N_SHIFT), g_up.astype(BF16), dec_w0.reshape(1, 2 * c),
      _block_diag2(dec_up).astype(BF16), iclr_a0.reshape(1, 2 * c), _block_diag2(iclr_up).astype(BF16),
      k_k.reshape(1, c), _head_ones(), k_a.reshape(1, c))


def _carry_kernel(tab_ref, qf_ref, y0f_ref, pf_ref, zf_ref, gf_ref, qb_ref, y0b_ref, pb_ref, zb_ref, gb_ref,
                  s0_ref, yf_ref, yb_ref, sfin_ref, st_ref):
    step = pl.program_id(0)

    @pl.when(tab_ref[3, step] == 1)
    def _():
        st_ref[...] = s0_ref[0]

    bd_mask = (lax.broadcasted_iota(jnp.int32, (GROUP * CHUNK, GROUP_W), 0) // CHUNK
               == lax.broadcasted_iota(jnp.int32, (GROUP * CHUNK, GROUP_W), 1) // HEAD_DIM)
    for i in range(CARRY_CHUNKS):
        for d, (q_ref, y0_ref, p_ref, z_ref, g_ref, y_ref) in enumerate(
                ((qf_ref, y0f_ref, pf_ref, zf_ref, gf_ref, yf_ref), (qb_ref, y0b_ref, pb_ref, zb_ref, gb_ref, yb_ref))):
            j = i if d == 0 else CARRY_CHUNKS - 1 - i
            rows = slice(j * CHUNK, (j + 1) * CHUNK)
            for q in range(C_RWKV // GROUP_W):
                sl = slice(q * GROUP_W, (q + 1) * GROUP_W)
                s = st_ref[d, :, sl]
                y_ref[rows, sl] = y0_ref[rows, sl] + _dot_nt(q_ref[rows, sl], _block_diag(s, bd_mask))
                st_ref[d, :, sl] = (s * g_ref[j, :, sl] + z_ref[rows, sl]
                                    + _dot(s.astype(BF16), _block_diag(p_ref[rows, sl], bd_mask)))

    @pl.when(tab_ref[4, step] == 1)
    def _():
        sfin_ref[0] = st_ref[...]


def _scan_table(n_ctx_seq, t_ctx, n_s_seq, t_s):
    rows = []
    base = 0
    seq = 0
    for n_seq, t_len in ((n_ctx_seq, t_ctx), (n_s_seq, t_s)):
        assert t_len % (CARRY_CHUNKS * CHUNK) == 0
        n_c = t_len // (CARRY_CHUNKS * CHUNK)
        for _ in range(n_seq):
            for c in range(n_c):
                rows.append((base + c, base + n_c - 1 - c, seq, int(c == 0), int(c == n_c - 1)))
            base += n_c
            seq += 1
    return np.asarray(rows, dtype=np.int32).T.copy()


def _carry_states(q, y0, p, z, g, s0, table):
    n = q.shape[0]
    n_seq = s0.shape[0]
    c = C_RWKV

    def tok(which, d):
        return pl.BlockSpec((CARRY_CHUNKS * CHUNK, c), lambda s, tab: (tab[which, s], d))

    def decay(which, d):
        return pl.BlockSpec((CARRY_CHUNKS, 1, c), lambda s, tab: (tab[which, s], 0, d))

    state_spec = pl.BlockSpec((1, 2, HEAD_DIM, c), lambda s, tab: (tab[2, s], 0, 0, 0))
    grid_spec = pltpu.PrefetchScalarGridSpec(
        num_scalar_prefetch=1,
        grid=(table.shape[1],),
        in_specs=[tok(0, 0), tok(0, 0), tok(0, 0), tok(0, 0), decay(0, 0),
                  tok(1, 1), tok(1, 1), tok(1, 1), tok(1, 1), decay(1, 1), state_spec],
        out_specs=[tok(0, 0), tok(1, 0), state_spec],
        scratch_shapes=[pltpu.VMEM((2, HEAD_DIM, c), F32)],
    )
    return pl.pallas_call(
        _carry_kernel,
        grid_spec=grid_spec,
        out_shape=[jax.ShapeDtypeStruct((n, c), F32), jax.ShapeDtypeStruct((n, c), F32),
                   jax.ShapeDtypeStruct((n_seq, 2, HEAD_DIM, c), F32)],
        compiler_params=_cparams(("arbitrary",)),
        name="rwkv_carry",
    )(jnp.asarray(table), q, y0, p, z, g, q, y0, p, z, g, s0)


def _merge_kernel(xc_ref, xl_ref, c_ref, fc_ref, fl_ref, yf_ref, yb_ref, r_ref, k_ref, v_ref, g_ref, n1_ref, wgt_ref, mod_ref,
                  wc_ref, wf_ref, wr_ref, wo_ref, gng_ref, gnb_ref, rk_ref, bd_ref, n2_ref, *rest, n_ctx, seq_s, route):
    if route:
        wrt_ref, brt_ref, x1_ref, h2_ref, comb_ref = rest
    else:
        x1_ref, h2_ref = rest
    row = _mod_row(pl.program_id(0), WIDE_TILE, n_ctx, seq_s)
    m = mod_ref[pl.ds(row, 1), :]
    sh1 = m[:, 0:D_MODEL]
    sc1 = m[:, D_MODEL:2 * D_MODEL]
    g1 = m[:, 2 * D_MODEL:3 * D_MODEL]
    sh2 = m[:, 3 * D_MODEL:4 * D_MODEL]
    sc2 = m[:, 4 * D_MODEL:5 * D_MODEL]
    bd = bd_ref[...]
    inv = 1.0 / HEAD_DIM
    y = yf_ref[...] + yb_ref[...]
    y_hi, y_lo = _split(y)
    d = y - (_dot(y_hi, bd) + _dot(y_lo, bd)) * inv
    var = _seg_sum(d * d, bd) * inv
    yn = d * lax.rsqrt(var + GN_EPS) * gng_ref[...] + gnb_ref[...]
    v = v_ref[...]
    yn = yn + _seg_sum(r_ref[...] * k_ref[...] * rk_ref[...], bd) * v
    yr = _dot((yn * g_ref[...]).astype(BF16), wr_ref[...])
    yc = _dot(c_ref[...], wc_ref[...])
    yf = _dot(_pick_stream(pl.program_id(0), WIDE_TILE, n_ctx, fc_ref, fl_ref), wf_ref[...])
    x = _pick_stream(pl.program_id(0), WIDE_TILE, n_ctx, xc_ref, xl_ref)
    hn = x * lax.rsqrt(jnp.mean(x * x, axis=-1, keepdims=True) + RMS_EPS) * n1_ref[...]
    hn = (hn * (1.0 + sc1) + sh1).astype(BF16)
    dm = D_MODEL
    merged = (_sigmoid(_dot(hn, wgt_ref[:, 0:dm])) * yc + _sigmoid(_dot(hn, wgt_ref[:, dm:2 * dm])) * yf
              + _sigmoid(_dot(hn, wgt_ref[:, 2 * dm:3 * dm])) * yr)
    x1 = x + g1 * _dot(merged.astype(BF16), wo_ref[...])
    x1_ref[...] = x1
    h2 = x1 * lax.rsqrt(jnp.mean(x1 * x1, axis=-1, keepdims=True) + RMS_EPS) * n2_ref[...]
    h2 = h2 * (1.0 + sc2) + sh2
    h2_ref[...] = h2.astype(BF16)
    if not route:
        return
    logits = _dot_x3(h2, wrt_ref[...]) + brt_ref[...]
    lane = lax.broadcasted_iota(jnp.int32, logits.shape, 1).astype(F32)
    neg = jnp.float32(-jnp.inf)
    lg = jnp.where(lane < N_EXPERTS, logits, neg)
    m1 = jnp.max(lg, axis=-1, keepdims=True)
    i1 = jnp.min(jnp.where(lg == m1, lane, float(LANES)), axis=-1, keepdims=True)
    lg2 = jnp.where(lane == i1, neg, lg)
    m2 = jnp.max(lg2, axis=-1, keepdims=True)
    i2 = jnp.min(jnp.where(lg2 == m2, lane, float(LANES)), axis=-1, keepdims=True)
    e = jnp.exp(m2 - m1)
    p1 = 1.0 / (1.0 + e)
    comb_ref[...] = jnp.where(lane == i1, p1, 0.0) + jnp.where(lane == i2, e * p1, 0.0)


def _merge(x, n, conv_h, fno_h, yf, yb, r, k, v, g, gain1, w_gates, mod, w_conv_out, w_fno_out, w_rwkv_out, w_o,
           gn_g, gn_b, r_k, gain2, n_ctx, seq_s, router=None):
    xc, xl, x_lat0 = _stream_args(x, WIDE_TILE, n_ctx)
    fc, fl, f_lat0 = _stream_args(fno_h, WIDE_TILE, n_ctx)
    c = C_RWKV
    tile = lambda w: pl.BlockSpec((WIDE_TILE, w), lambda i: (i, 0))
    full = lambda shape: pl.BlockSpec(shape, lambda i: (0,) * len(shape), pipeline_mode=pl.Buffered(1))
    route = router is not None
    route_args, route_specs, route_out, route_shape = [], [], [], []
    if route:
        route_args = [jnp.zeros((D_MODEL, LANES), F32).at[:, :N_EXPERTS].set(router[0]),
                      jnp.zeros((1, LANES), F32).at[0, :N_EXPERTS].set(router[1])]
        route_specs = [full((D_MODEL, LANES)), full((1, LANES))]
        route_out = [tile(LANES)]
        route_shape = [jax.ShapeDtypeStruct((n, LANES), F32)]
    return pl.pallas_call(
        functools.partial(_merge_kernel, n_ctx=n_ctx, seq_s=seq_s, route=route),
        grid=(n // WIDE_TILE,),
        in_specs=_stream_specs(WIDE_TILE, D_MODEL, n_ctx, x_lat0) + [tile(C_CONV)]
        + _stream_specs(WIDE_TILE, C_FNO, n_ctx, f_lat0) + [
            tile(c), tile(c), tile(c), tile(c), tile(c), tile(c),
            full((1, D_MODEL)), full((D_MODEL, 3 * D_MODEL)), full((8, 6 * D_MODEL)),
            full((C_CONV, D_MODEL)), full((C_FNO, D_MODEL)), full((c, D_MODEL)), full((D_MODEL, D_MODEL)),
            full((1, c)), full((1, c)), full((1, c)), full((c, c)), full((1, D_MODEL)),
        ] + route_specs,
        out_specs=[tile(D_MODEL), tile(D_MODEL)] + route_out,
        out_shape=[jax.ShapeDtypeStruct((n, D_MODEL), F32), jax.ShapeDtypeStruct((n, D_MODEL), BF16)] + route_shape,
        compiler_params=_cparams(("parallel",)),
        name="merge",
    )(xc, xl, conv_h, fc, fl, yf, yb, r, k, v, g, gain1.reshape(1, D_MODEL), w_gates, mod,
      w_conv_out.astype(BF16), w_fno_out.astype(BF16), w_rwkv_out.astype(BF16), w_o.astype(BF16),
      gn_g.reshape(1, c), gn_b.reshape(1, c), r_k.reshape(1, c), _head_ones(), gain2.reshape(1, D_MODEL), *route_args)


def _ffn_kernel(x_ref, h_ref, mod_ref, wg_ref, wu_ref, wd_ref, o_ref, *, n_ctx, seq_s):
    row = _mod_row(pl.program_id(0), FFN_TILE, n_ctx, seq_s)
    g2 = mod_ref[pl.ds(row, 1), 5 * D_MODEL:6 * D_MODEL]
    h = h_ref[...]
    half = D_FF // 2
    acc = jnp.zeros((FFN_TILE, D_MODEL), F32)
    for p in range(2):
        sl = slice(p * half, (p + 1) * half)
        t = _silu(_dot(h, wg_ref[:, sl])) * _dot(h, wu_ref[:, sl])
        acc = acc + _dot(t.astype(BF16), wd_ref[sl, :])
    o_ref[...] = x_ref[...] + g2 * acc


def _ffn_dense(x1, h2, mod, w_gate, w_up, w_down, n_ctx, seq_s):
    n = x1.shape[0]
    tile = pl.BlockSpec((FFN_TILE, D_MODEL), lambda i: (i, 0))
    full = lambda shape: pl.BlockSpec(shape, lambda i: (0,) * len(shape), pipeline_mode=pl.Buffered(1))
    return pl.pallas_call(
        functools.partial(_ffn_kernel, n_ctx=n_ctx, seq_s=seq_s),
        grid=(n // FFN_TILE,),
        in_specs=[tile, tile, full((8, 6 * D_MODEL)), full((D_MODEL, D_FF)), full((D_MODEL, D_FF)),
                  full((D_FF, D_MODEL))],
        out_specs=tile,
        out_shape=jax.ShapeDtypeStruct((n, D_MODEL), F32),
        compiler_params=_cparams(("parallel",)),
        name="ffn_dense",
    )(x1, h2, mod, w_gate.astype(BF16), w_up.astype(BF16), w_down.astype(BF16))


def _moe_kernel(x_ref, h_ref, comb_ref, mod_ref, wg_ref, wu_ref, wd_ref, fin_ref, *rest, n_ctx, seq_s, final):
    outs, (acc_ref, rank_c_ref, rank_r_ref, comb_t_ref) = rest[:-4], rest[-4:]
    e = pl.program_id(1)
    parts = [slice(p * MOE_PART, (p + 1) * MOE_PART) for p in range(MOE_TILE // MOE_PART)]

    @pl.when(e == 0)
    def _():
        acc_ref[...] = jnp.zeros_like(acc_ref)
        before = (lax.broadcasted_iota(jnp.int32, (MOE_PART, MOE_PART), 1)
                  < lax.broadcasted_iota(jnp.int32, (MOE_PART, MOE_PART), 0)).astype(BF16)
        for part in parts:
            comb = comb_ref[part, :]
            rank_c_ref[part, :] = _dot(before, (comb > 0.0).astype(BF16))
            comb_t = comb.T
            comb_t_ref[:, part] = comb_t
            rank_r_ref[:, part] = _dot_nt((comb_t > 0.0).astype(BF16), before)

    comb = comb_ref[...]
    lane = lax.broadcasted_iota(jnp.int32, comb.shape, 1)
    w_col = jnp.sum(jnp.where(lane == e, comb, 0.0), axis=-1, keepdims=True)
    rank_col = jnp.sum(jnp.where(lane == e, rank_c_ref[...], 0.0), axis=-1, keepdims=True)
    w_row = comb_t_ref[pl.ds(e, 1), :]
    rank_row = rank_r_ref[pl.ds(e, 1), :]
    count = functools.reduce(jnp.maximum, [jnp.sum((w_row[:, part] > 0.0).astype(jnp.int32)) for part in parts])

    def body(j, carry):
        base = (j * MOE_ROWS).astype(F32)
        slot_r = lax.broadcasted_iota(jnp.int32, (MOE_ROWS, MOE_PART), 0).astype(F32) + base
        slot_c = lax.broadcasted_iota(jnp.int32, (MOE_PART, MOE_ROWS), 1).astype(F32) + base
        rows = []
        for part in parts:
            hit = jnp.logical_and(rank_row[:, part] == slot_r, w_row[:, part] > 0.0)
            rows.append(_dot(jnp.where(hit, 1.0, 0.0).astype(BF16), h_ref[part, :]).astype(BF16))
        hg = jnp.concatenate(rows, axis=0)
        t = _silu(_dot(hg, wg_ref[0])) * _dot(hg, wu_ref[0])
        y = _dot(t.astype(BF16), wd_ref[0]).astype(BF16)
        for p, part in enumerate(parts):
            hit = jnp.logical_and(rank_col[part, :] == slot_c, w_col[part, :] > 0.0)
            acc_ref[part, :] += w_col[part, :] * _dot(jnp.where(hit, 1.0, 0.0).astype(BF16),
                                                      y[p * MOE_ROWS:(p + 1) * MOE_ROWS])
        return carry

    lax.fori_loop(0, (count + MOE_ROWS - 1) // MOE_ROWS, body, 0)

    @pl.when(e == N_EXPERTS - 1)
    def _():
        row = _mod_row(pl.program_id(0), MOE_TILE, n_ctx, seq_s)
        g2 = mod_ref[pl.ds(row, 1), 5 * D_MODEL:6 * D_MODEL]
        y = x_ref[...] + g2 * acc_ref[...]
        if not final:
            outs[0][...] = y
        else:
            y = y * lax.rsqrt(jnp.mean(y * y, axis=-1, keepdims=True) + RMS_EPS) * fin_ref[...]
            is_ctx = pl.program_id(0) < n_ctx // MOE_TILE

            @pl.when(is_ctx)
            def _():
                outs[0][...] = y

            @pl.when(jnp.logical_not(is_ctx))
            def _():
                outs[1][...] = y


def _ffn_moe(x1, h2, comb, mod, w_gate, w_up, w_down, n_ctx, seq_s, final_gain=None):
    n = x1.shape[0]
    final = final_gain is not None
    once = pl.Buffered(1)
    tile = lambda w: pl.BlockSpec((MOE_TILE, w), lambda i, e: (i, 0), pipeline_mode=once)
    if final:
        n_ctx_tiles = n_ctx // MOE_TILE
        out_specs = [pl.BlockSpec((MOE_TILE, D_MODEL), lambda i, e: (jnp.minimum(i, n_ctx_tiles - 1), 0),
                                  pipeline_mode=once),
                     pl.BlockSpec((MOE_TILE, D_MODEL), lambda i, e: (jnp.maximum(i - n_ctx_tiles, 0), 0),
                                  pipeline_mode=once)]
        out_shape = [jax.ShapeDtypeStruct((n_ctx, D_MODEL), F32), jax.ShapeDtypeStruct((n - n_ctx, D_MODEL), F32)]
        gain = final_gain.reshape(1, D_MODEL)
    else:
        out_specs = [tile(D_MODEL)]
        out_shape = [jax.ShapeDtypeStruct((n, D_MODEL), F32)]
        gain = jnp.ones((1, D_MODEL), F32)
    out = pl.pallas_call(
        functools.partial(_moe_kernel, n_ctx=n_ctx, seq_s=seq_s, final=final),
        grid=(n // MOE_TILE, N_EXPERTS),
        in_specs=[tile(D_MODEL), tile(D_MODEL), tile(LANES),
                  pl.BlockSpec((8, 6 * D_MODEL), lambda i, e: (0, 0)),
                  pl.BlockSpec((1, D_MODEL, D_FF_E), lambda i, e: (e, 0, 0)),
                  pl.BlockSpec((1, D_MODEL, D_FF_E), lambda i, e: (e, 0, 0)),
                  pl.BlockSpec((1, D_FF_E, D_MODEL), lambda i, e: (e, 0, 0)),
                  pl.BlockSpec((1, D_MODEL), lambda i, e: (0, 0))],
        out_specs=out_specs,
        out_shape=out_shape,
        scratch_shapes=[pltpu.VMEM((MOE_TILE, D_MODEL), F32), pltpu.VMEM((MOE_TILE, LANES), F32),
                        pltpu.VMEM((LANES, MOE_TILE), F32), pltpu.VMEM((LANES, MOE_TILE), F32)],
        compiler_params=_cparams(("arbitrary", "arbitrary") if final else ("parallel", "arbitrary")),
        name="ffn_moe",
    )(x1, h2, comb, mod, w_gate.astype(BF16), w_up.astype(BF16), w_down.astype(BF16), gain)
    return tuple(out) if final else out[0]


def _final_kernel(x_ref, g_ref, o_ref):
    x = x_ref[...]
    o_ref[...] = x * lax.rsqrt(jnp.mean(x * x, axis=-1, keepdims=True) + RMS_EPS) * g_ref[...]


def _final_norm(x, gain, row0, n_rows):
    blk0 = row0 // FFN_TILE
    return pl.pallas_call(
        _final_kernel,
        grid=(n_rows // FFN_TILE,),
        in_specs=[pl.BlockSpec((FFN_TILE, D_MODEL), lambda i: (blk0 + i, 0)),
                  pl.BlockSpec((1, D_MODEL), lambda i: (0, 0))],
        out_specs=pl.BlockSpec((FFN_TILE, D_MODEL), lambda i: (i, 0)),
        out_shape=jax.ShapeDtypeStruct((n_rows, D_MODEL), F32),
        compiler_params=_cparams(("parallel",)),
        name="final_norm",
    )(x, gain.reshape(1, D_MODEL))


def kernel(x_prompt, x_sample, state_rwkv, c, c_ctx, norm1, norm2, w_ada, b_ada, w_in, dw_w, dw_b, conv_ln_g, conv_ln_b, w_conv_out, w_fno_out, shift_mu, g_up, dec_w0, dec_up, iclr_a0, iclr_up, k_k, k_a, r_k, gn_g, gn_b, w_rwkv_out, w_o, ffn_w_gate, ffn_w_up, ffn_w_down, w_router, b_router, moe_w_gate, moe_w_up, moe_w_down, final_norm):
    b_p, t_p, _ = x_prompt.shape
    b_s, t_s, _ = x_sample.shape
    depth = w_in.shape[0]
    n_ctx = b_p * t_p
    n_lat = b_s * t_s
    assert t_p == ROW_TILE and t_s % MOE_TILE == 0 and n_ctx % MOE_TILE == 0 and b_s <= CTX_ROW
    assert MOE_TILE % FFN_TILE == 0 and n_ctx % WIDE_TILE == 0 and t_s % WIDE_TILE == 0
    assert t_s % GRID_W == 0 and CHUNK == GRID_W

    n = n_ctx + n_lat
    x = (x_prompt.reshape(n_ctx, D_MODEL), x_sample.reshape(n_lat, D_MODEL))
    cond = jnp.zeros((8, D_MODEL), F32).at[:b_s].set(c).at[CTX_ROW].set(c_ctx)
    mods = _ada(cond, w_ada, b_ada)
    table = _scan_table(b_p, t_p, b_s, t_s)
    zero_state = jnp.zeros((b_p, 2, HEAD_DIM, C_RWKV), F32)
    pack_state = lambda s: s.transpose(0, 1, 3, 2, 4).reshape(s.shape[0], 2, HEAD_DIM, C_RWKV)
    unpack_state = lambda s: s.reshape(s.shape[0], 2, HEAD_DIM, N_HEADS, HEAD_DIM).transpose(0, 1, 3, 2, 4)

    ctx_states = []
    for l in range(depth):
        mod = mods[l]
        u_conv, u_fno, u_rw = _inproj(x, n, mod, norm1[l], w_in[l, :, :OFF_GATE].astype(BF16), n_ctx, t_s)
        conv_h = _conv_branch(u_conv, dw_w[l], dw_b[l], conv_ln_g[l], conv_ln_b[l], n_ctx)
        fno_h = (_fno_branch(u_fno, 0, b_p, t_p), _fno_branch(u_fno, n_ctx, b_s, t_s))
        r, k, v, g, *summaries = _rwkv_project_and_summarise(u_rw, shift_mu[l], g_up[l], dec_w0[l], dec_up[l],
                                                             iclr_a0[l], iclr_up[l], k_k[l], k_a[l], n_ctx, t_s)
        s0 = jnp.concatenate([zero_state, pack_state(state_rwkv[:, l])], axis=0)
        yf, yb, s_fin = _carry_states(*summaries, s0, table)
        ctx_states.append(unpack_state(s_fin[:b_p]))
        i = l // 2
        x1, h2, *comb = _merge(x, n, conv_h, fno_h, yf, yb, r, k, v, g, norm1[l], w_in[l, :, OFF_GATE:].astype(BF16), mod, w_conv_out[l], w_fno_out[l],
                              w_rwkv_out[l], w_o[l], gn_g[l], gn_b[l], r_k[l], norm2[l], n_ctx, t_s,
                               router=(w_router[i], b_router[i]) if l % 2 == 1 else None)
        if l % 2 == 0:
            x = _ffn_dense(x1, h2, mod, ffn_w_gate[i], ffn_w_up[i], ffn_w_down[i], n_ctx, t_s)
        else:
            x = _ffn_moe(x1, h2, comb[0], mod, moe_w_gate[i], moe_w_up[i], moe_w_down[i], n_ctx, t_s,
                         final_gain=final_norm if l == depth - 1 else None)

    if isinstance(x, tuple):
        y_prompt, y_sample = x
    else:
        y_prompt, y_sample = _final_norm(x, final_norm, 0, n_ctx), _final_norm(x, final_norm, n_ctx, n_lat)
    y_prompt = y_prompt.reshape(b_p, t_p, D_MODEL)
    y_sample = y_sample.reshape(b_s, t_s, D_MODEL)
    new_state = jnp.stack(ctx_states, axis=1).astype(x_prompt.dtype)
    return (y_prompt, y_sample, new_state)
```

```python
import functools

import numpy as np
import jax
import jax.numpy as jnp
from jax import lax
from jax.experimental import pallas as pl
from jax.experimental.pallas import tpu as pltpu

F32 = jnp.float32
BF16 = jnp.bfloat16

D_MODEL = 1024
GRID_W = 64
C_CONV = 256
CONV_W = 31
C_FNO = 256
FNO_GW = 64
N_HEADS = 8
HEAD_DIM = 64
C_RWKV = N_HEADS * HEAD_DIM
G_RANK = 128
W_RANK = 64
A_RANK = 64
D_FF = 2816
N_EXPERTS = 8
D_FF_E = 1408
RMS_EPS = 1e-6
LN_EPS = 1e-5
GN_EPS = 64e-5

OFF_FNO = 2 * C_CONV
OFF_RWKV = OFF_FNO + C_FNO
N_SHIFT = 3 * C_RWKV + G_RANK + 2 * (W_RANK + A_RANK)
OFF_GATE = OFF_RWKV + N_SHIFT
D_IN = OFF_GATE + 3 * D_MODEL

ROW_TILE = 256
FFN_TILE = 1024
WIDE_TILE = 512
CHUNK = 64
GROUP = 4
GROUP_W = GROUP * HEAD_DIM
SUMMARY_CHUNKS = 4
CARRY_CHUNKS = 4
MOE_TILE = 1024
MOE_PART = 512
MOE_ROWS = 160
CTX_ROW = 4
LANES = 128
VMEM_LIMIT = 56 * 1024 * 1024


def _cparams(sem):
    return pltpu.CompilerParams(dimension_semantics=sem, vmem_limit_bytes=VMEM_LIMIT)


def _sigmoid(x):
    return 1.0 / (1.0 + jnp.exp(-x))


def _silu(x):
    return x * _sigmoid(x)


def _dot(a, b):
    return jnp.dot(a, b, preferred_element_type=F32)


def _split(x):
    hi = x.astype(BF16)
    lo = (x - hi.astype(F32)).astype(BF16)
    return hi, lo


def _dot_x3(a, b):
    ah, al = _split(a)
    bh, bl = _split(b)
    return _dot(ah, bh) + (_dot(ah, bl) + _dot(al, bh))


def _seg_sum(x, bd):
    return _dot(x.astype(BF16), bd)


def _mod_row(i, tile, n_ctx_rows, seq_s):
    n_ctx_tiles = n_ctx_rows // tile
    return jnp.where(i < n_ctx_tiles, CTX_ROW, (i - n_ctx_tiles) // (seq_s // tile))


def _stream_specs(tile, width, n_ctx, lat_block0):
    n_ctx_tiles = n_ctx // tile
    return [pl.BlockSpec((tile, width), lambda i: (jnp.minimum(i, n_ctx_tiles - 1), 0)),
            pl.BlockSpec((tile, width), lambda i: (jnp.maximum(i - n_ctx_tiles, 0) + lat_block0, 0))]


def _stream_args(x, tile, n_ctx):
    if isinstance(x, tuple):
        return x[0], x[1], 0
    return x, x, n_ctx // tile


def _pick_stream(i, tile, n_ctx, ctx_ref, lat_ref):
    return jnp.where(i < n_ctx // tile, ctx_ref[...], lat_ref[...])


def _ada_kernel(c_ref, w_ref, b_ref, o_ref):
    s = _silu(c_ref[...])
    o_ref[0] = _dot(s.astype(BF16), w_ref[0].astype(BF16)) + b_ref[0]


def _ada(cond, w_ada, b_ada):
    n_l = w_ada.shape[0]
    tn = 1536
    return pl.pallas_call(
        _ada_kernel,
        grid=(n_l, 6 * D_MODEL // tn),
        in_specs=[
            pl.BlockSpec((8, D_MODEL), lambda l, j: (0, 0)),
            pl.BlockSpec((1, D_MODEL, tn), lambda l, j: (l, 0, j)),
            pl.BlockSpec((1, 1, tn), lambda l, j: (l, 0, j)),
        ],
        out_specs=pl.BlockSpec((1, 8, tn), lambda l, j: (l, 0, j)),
        out_shape=jax.ShapeDtypeStruct((n_l, 8, 6 * D_MODEL), F32),
        compiler_params=_cparams(("parallel", "parallel")),
        name="ada",
    )(cond, w_ada, b_ada.reshape(n_l, 1, 6 * D_MODEL))


def _inproj_kernel(xc_ref, xl_ref, mod_ref, g_ref, w_ref, oc_ref, of_ref, or_ref, *, n_ctx, seq_s):
    row = _mod_row(pl.program_id(0), WIDE_TILE, n_ctx, seq_s)
    m = mod_ref[pl.ds(row, 1), :]
    sh = m[:, 0:D_MODEL]
    sc = m[:, D_MODEL:2 * D_MODEL]
    x = _pick_stream(pl.program_id(0), WIDE_TILE, n_ctx, xc_ref, xl_ref)
    y = x * lax.rsqrt(jnp.mean(x * x, axis=-1, keepdims=True) + RMS_EPS) * g_ref[...]
    h = (y * (1.0 + sc) + sh).astype(BF16)
    oc_ref[...] = _dot(h, w_ref[:, 0:OFF_FNO])
    of_ref[...] = _dot(h, w_ref[:, OFF_FNO:OFF_RWKV])
    or_ref[...] = _dot(h, w_ref[:, OFF_RWKV:OFF_GATE])


def _inproj(x, n, mod, gain, w_in, n_ctx, seq_s):
    xc, xl, lat0 = _stream_args(x, WIDE_TILE, n_ctx)
    widths = (OFF_FNO, C_FNO, N_SHIFT)
    return pl.pallas_call(
        functools.partial(_inproj_kernel, n_ctx=n_ctx, seq_s=seq_s),
        grid=(n // WIDE_TILE,),
        in_specs=_stream_specs(WIDE_TILE, D_MODEL, n_ctx, lat0) + [
            pl.BlockSpec((8, 6 * D_MODEL), lambda i: (0, 0)),
            pl.BlockSpec((1, D_MODEL), lambda i: (0, 0)),
            pl.BlockSpec((D_MODEL, OFF_GATE), lambda i: (0, 0), pipeline_mode=pl.Buffered(1)),
        ],
        out_specs=[pl.BlockSpec((WIDE_TILE, w), lambda i: (i, 0)) for w in widths],
        out_shape=[jax.ShapeDtypeStruct((n, w), F32) for w in widths],
        compiler_params=_cparams(("parallel",)),
        name="inproj",
    )(xc, xl, mod, gain.reshape(1, D_MODEL), w_in)


_CONV_HALO = 16
_CONV_ROWS = 64


def _conv_kernel(u_ref, w_ref, b_ref, g_ref, be_ref, o_ref, pad_ref, sh_ref, *, n_ctx):
    is_ctx = pl.program_id(0) < n_ctx // ROW_TILE
    zeros = jnp.zeros((_CONV_HALO, C_CONV), F32)
    n_parts = ROW_TILE // _CONV_ROWS

    def glu(lo, hi):
        return u_ref[lo:hi, 0:C_CONV] * _sigmoid(u_ref[lo:hi, C_CONV:2 * C_CONV])

    def finish(starts, n_rows):
        for s in range(8):
            sh_ref[s, 0:n_rows - 8, :] = pad_ref[s:s + n_rows - 8, :]
        for p in range(n_parts):
            acc = jnp.zeros((_CONV_ROWS, C_CONV), F32)
            for j in range(CONV_W):
                o = starts[p] + j
                acc = acc + w_ref[j:j + 1, :] * sh_ref[o % 8, o - o % 8:o - o % 8 + _CONV_ROWS, :]
            acc = acc + b_ref[...]
            mu = jnp.mean(acc, axis=-1, keepdims=True)
            d = acc - mu
            var = jnp.mean(d * d, axis=-1, keepdims=True)
            y = d * lax.rsqrt(var + LN_EPS) * g_ref[...] + be_ref[...]
            o_ref[p * _CONV_ROWS:(p + 1) * _CONV_ROWS, :] = _silu(y).astype(o_ref.dtype)

    shift = _CONV_HALO - CONV_W // 2

    @pl.when(is_ctx)
    def _():
        pad_ref[0:_CONV_HALO, :] = zeros
        pad_ref[_CONV_HALO:_CONV_HALO + ROW_TILE, :] = glu(0, ROW_TILE)
        pad_ref[_CONV_HALO + ROW_TILE:2 * _CONV_HALO + ROW_TILE, :] = zeros
        finish([shift + p * _CONV_ROWS for p in range(n_parts)], ROW_TILE + 2 * _CONV_HALO)

    @pl.when(jnp.logical_not(is_ctx))
    def _():
        stride = GRID_W + 2 * _CONV_HALO
        for p in range(n_parts):
            pad_ref[p * stride:p * stride + _CONV_HALO, :] = zeros
            pad_ref[p * stride + _CONV_HALO:p * stride + _CONV_HALO + GRID_W, :] = glu(p * GRID_W, (p + 1) * GRID_W)
            pad_ref[p * stride + _CONV_HALO + GRID_W:(p + 1) * stride, :] = zeros
        finish([p * stride + shift for p in range(n_parts)], n_parts * stride)


def _conv_branch(u_conv, dw_w, dw_b, ln_g, ln_b, n_ctx):
    n = u_conv.shape[0]
    assert _CONV_ROWS == GRID_W and ROW_TILE % GRID_W == 0
    vec = pl.BlockSpec((1, C_CONV), lambda i: (0, 0))
    return pl.pallas_call(
        functools.partial(_conv_kernel, n_ctx=n_ctx),
        grid=(n // ROW_TILE,),
        in_specs=[
            pl.BlockSpec((ROW_TILE, 2 * C_CONV), lambda i: (i, 0)),
            pl.BlockSpec((CONV_W, C_CONV), lambda i: (0, 0)),
            vec, vec, vec,
        ],
        out_specs=pl.BlockSpec((ROW_TILE, C_CONV), lambda i: (i, 0)),
        out_shape=jax.ShapeDtypeStruct((n, C_CONV), BF16),
        scratch_shapes=[pltpu.VMEM((ROW_TILE // GRID_W * (GRID_W + 2 * _CONV_HALO), C_CONV), F32),
                        pltpu.VMEM((8, ROW_TILE // GRID_W * (GRID_W + 2 * _CONV_HALO), C_CONV), F32)],
        compiler_params=_cparams(("parallel",)),
        name="conv_branch",
    )(u_conv, dw_w, dw_b.reshape(1, C_CONV), ln_g.reshape(1, C_CONV), ln_b.reshape(1, C_CONV))


def _dft_tables(t_len):
    def cs(n):
        k = np.arange(n, dtype=np.int64)
        ang = 2.0 * np.pi * ((k[:, None] * k[None, :]) % n).astype(np.float64) / n
        return np.cos(ang), np.sin(ang)
    cg, sg = cs(FNO_GW)
    eye = np.eye(C_FNO // FNO_GW)
    w1 = np.concatenate([np.kron(eye, cg), np.kron(eye, sg)], axis=1)
    ct, st = cs(t_len)
    w2 = np.concatenate([ct, -st], axis=1)
    return jnp.asarray(w1, dtype=F32).astype(BF16), jnp.asarray(w2, dtype=F32).astype(BF16)


def _fno_kernel(u_ref, w1_ref, w2_ref, o_ref, hs_ref, *, t_len, scale):
    @pl.when(pl.program_id(1) == 0)
    def _():
        hc = _dot(u_ref[...].astype(BF16), w1_ref[...])
        hs_ref[0:t_len, :] = hc[:, 0:C_FNO].astype(BF16)
        hs_ref[t_len:2 * t_len, :] = hc[:, C_FNO:2 * C_FNO].astype(BF16)

    o_ref[...] = (_dot(w2_ref[...], hs_ref[...]) * scale).astype(o_ref.dtype)


def _fno_branch(u_fno, row0, n_seq, t_len):
    w1, w2 = _dft_tables(t_len)
    tk = t_len
    blk0 = row0 // t_len
    return pl.pallas_call(
        functools.partial(_fno_kernel, t_len=t_len, scale=float(1.0 / np.sqrt(t_len * FNO_GW))),
        grid=(n_seq, t_len // tk),
        in_specs=[
            pl.BlockSpec((t_len, C_FNO), lambda b, j: (blk0 + b, 0)),
            pl.BlockSpec((C_FNO, 2 * C_FNO), lambda b, j: (0, 0)),
            pl.BlockSpec((tk, 2 * t_len), lambda b, j: (j, 0), pipeline_mode=pl.Buffered(1)),
        ],
        out_specs=pl.BlockSpec((tk, C_FNO), lambda b, j: (b * (t_len // tk) + j, 0)),
        out_shape=jax.ShapeDtypeStruct((n_seq * t_len, C_FNO), BF16),
        scratch_shapes=[pltpu.VMEM((2 * t_len, C_FNO), BF16)],
        compiler_params=_cparams(("parallel", "arbitrary")),
        name="fno_branch",
    )(u_fno, w1, w2)


def _prep_pieces(i, z_ref, zp_ref, zn_ref, mu_ref, gup_ref, w0_ref, dup_ref, a0_ref, aup_ref, kk_ref, bd_ref,
                 r_refs, k_refs, v_refs, kkn_ref, g_ref, lw_ref, a_ref, pad_ref, n_ctx, seq_s):
    n_ctx_tiles = n_ctx // ROW_TILE
    per_seq = seq_s // ROW_TILE
    j = (i - n_ctx_tiles) % per_seq
    first = jnp.logical_or(i < n_ctx_tiles, j == 0)
    last = jnp.logical_or(i < n_ctx_tiles, j == per_seq - 1)
    c = C_RWKV

    def shifted(lo, hi):
        z = pad_ref[8:8 + ROW_TILE, lo:hi]
        zp = pad_ref[7:7 + ROW_TILE, lo:hi]
        zn = pad_ref[9:9 + ROW_TILE, lo:hi]
        return z + mu_ref[:, lo:hi] * (0.5 * (zp + zn) - z)

    def fill():
        pad_ref[8:8 + ROW_TILE, :] = z_ref[...]
        pad_ref[0:8, :] = jnp.where(first, 0.0, zp_ref[...])
        pad_ref[8 + ROW_TILE:16 + ROW_TILE, :] = jnp.where(last, 0.0, zn_ref[...])

    def receptance():
        r = shifted(0, c)
        for ref in r_refs:
            ref[...] = r

    def key():
        k = shifted(c, 2 * c)
        for ref in k_refs:
            ref[...] = k
        kx = k * kk_ref[...]
        nrm = jnp.sqrt(_seg_sum(kx * kx, bd_ref[...]))
        kkn_ref[...] = kx / jnp.maximum(nrm, 1e-12)

    def value():
        v = shifted(2 * c, 3 * c)
        for ref in v_refs:
            ref[...] = v

    def gate():
        o = 3 * c
        g_ref[...] = _dot(_sigmoid(shifted(o, o + G_RANK)).astype(BF16), gup_ref[...])

    def decay():
        o = 3 * c + G_RANK
        xw = _dot(jnp.tanh(shifted(o, o + 2 * W_RANK)).astype(BF16), dup_ref[...]) + w0_ref[...]
        soft = jnp.maximum(-xw, 0.0) + jnp.log(1.0 + jnp.exp(-jnp.abs(xw)))
        lw_ref[...] = -jnp.exp(-soft - 0.5)

    def iclr():
        o = 3 * c + G_RANK + 2 * W_RANK
        xa = _dot(shifted(o, o + 2 * A_RANK).astype(BF16), aup_ref[...]) + a0_ref[...]
        a_ref[...] = _sigmoid(xa)

    return [fill, receptance, key, value, gate, decay, iclr]


def _block_diag2(w):
    z = jnp.zeros_like(w[0])
    return jnp.concatenate([jnp.concatenate([w[0], z], axis=1), jnp.concatenate([z, w[1]], axis=1)], axis=0)


def _head_ones():
    return jnp.asarray(np.kron(np.eye(N_HEADS), np.ones((HEAD_DIM, HEAD_DIM))), dtype=BF16)


def _block_diag(x, mask):
    xb = x.astype(BF16)
    return jnp.where(mask, jnp.concatenate([xb] * GROUP, axis=0), jnp.zeros((), BF16))


def _dot_nt(a, b):
    return lax.dot_general(a, b, (((1,), (1,)), ((), ())), preferred_element_type=F32)


def _diag_blocks(prod, lane_head):
    out = jnp.where(lane_head == 0, prod[0:HEAD_DIM], 0.0)
    for h in range(1, GROUP):
        out = out + jnp.where(lane_head == h, prod[h * HEAD_DIM:(h + 1) * HEAD_DIM], 0.0)
    return out


def _summary_body(r_ref, k_ref, v_ref, n_ref, lw_ref, a_ref, ka_ref, q_ref, y0_ref, p_ref, z_ref, g_ref, fillers=(),
                  chunks=range(SUMMARY_CHUNKS)):
    fillers = list(fillers)
    fill_one = lambda: fillers.pop(0)() if fillers else None
    row = lax.broadcasted_iota(jnp.int32, (CHUNK, GROUP_W), 0)
    lane = lax.broadcasted_iota(jnp.int32, (CHUNK, GROUP_W), 1)
    col = lane % CHUNK
    lane_head = lane // HEAD_DIM
    bd_mask = (lax.broadcasted_iota(jnp.int32, (GROUP * CHUNK, GROUP_W), 0) // CHUNK
               == lax.broadcasted_iota(jnp.int32, (GROUP * CHUNK, GROUP_W), 1) // HEAD_DIM)
    bd = lambda x: _block_diag(x, bd_mask)
    ka = ka_ref[...]

    units = []
    for j in chunks:
        rows = slice(j * CHUNK, (j + 1) * CHUNK)
        r = r_ref[rows, :]
        k = k_ref[rows, :]
        v = v_ref[rows, :]
        kkn = n_ref[rows, :]
        for d in range(2):
            lanes = slice(d * C_RWKV, (d + 1) * C_RWKV)
            earlier = (col < row) if d == 0 else (col > row)
            upto = jnp.logical_or(earlier, row == col)
            tri = upto[:, 0:CHUNK].astype(BF16)
            lw = lw_ref[rows, lanes]
            h1 = lw.astype(BF16)
            r1 = lw - h1.astype(F32)
            h2 = r1.astype(BF16)
            h3 = (r1 - h2.astype(F32)).astype(BF16)
            cum = _dot(tri, h1) + (_dot(tri, h2) + _dot(tri, h3))
            e_in = jnp.exp(cum)
            e_ex = jnp.exp(cum - lw)
            e_ng = jnp.exp(-cum)
            a = a_ref[rows, lanes]
            kd = k * (1.0 + (a - 1.0) * ka)
            at = -kkn * e_ex
            rt = r * e_in
            bt = kkn * a * e_ng
            kt = kd * e_ng
            end = CHUNK - 1 if d == 0 else 0
            g_end = e_in[end:end + 1, :]
            g_ref[j, :, lanes] = g_end
            bh = bt * g_end
            kh = kt * g_end
            for q in range(C_RWKV // GROUP_W):
                sl = slice(q * GROUP_W, (q + 1) * GROUP_W)
                units.append(dict(rows=rows, out=slice(d * C_RWKV + q * GROUP_W, d * C_RWKV + (q + 1) * GROUP_W),
                                  earlier=earlier, upto=upto, at=at[:, sl], rt=rt[:, sl],
                                  ar=jnp.concatenate([at[:, sl], rt[:, sl]], axis=0).astype(BF16),
                                  bt=bt[:, sl], kt=kt[:, sl], v=v[:, sl], bh=bh[:, sl].astype(BF16),
                                  bk=jnp.concatenate([bh[:, sl], kh[:, sl]], axis=0).astype(BF16)))

    for u in units:
        sb = _dot_nt(u["ar"], bd(u["bt"]))
        sk = _dot_nt(u["ar"], bd(u["kt"]))
        u["lab"] = jnp.where(u["earlier"], sb[0:CHUNK], 0.0)
        u["mrb"] = jnp.where(u["upto"], sb[CHUNK:2 * CHUNK], 0.0).astype(BF16)
        u["lm"] = jnp.concatenate([jnp.where(u["earlier"], sk[0:CHUNK], 0.0),
                                   jnp.where(u["upto"], sk[CHUNK:2 * CHUNK], 0.0)], axis=0).astype(BF16)

    fill_one()
    eye = (row == col).astype(F32)
    pair = jnp.logical_and(row // 2 == col // 2, row != col)
    for u in units:
        u["t"] = eye + jnp.where(pair, u["lab"], 0.0)
    n = 2
    while n < CHUNK:
        m = jnp.logical_and(row // (2 * n) == col // (2 * n), row // n != col // n)
        for u in units:
            u["w"] = _dot(jnp.where(m, u["lab"], 0.0).astype(BF16), bd(u["t"]))
        for u in units:
            u["t"] = u["t"] + _dot(u["t"].astype(BF16), bd(u["w"]))
        fill_one()
        n *= 2

    for u in units:
        u["t"] = u["t"].astype(BF16)
        u["wm"] = _dot(u["t"], bd(u["at"]))
        u["lv"] = _dot(u["lm"], bd(u["v"]))
    for u in units:
        u["u0"] = _dot(u["t"], bd(u["lv"][0:CHUNK]))
    for u in units:
        q_ref[u["rows"], u["out"]] = (u["rt"] + _dot(u["mrb"], bd(u["wm"]))).astype(q_ref.dtype)
        y0_ref[u["rows"], u["out"]] = u["lv"][CHUNK:2 * CHUNK] + _dot(u["mrb"], bd(u["u0"]))
    for u in units:
        p = _dot(u["wm"].T.astype(BF16), u["bh"])
        p_ref[u["rows"], u["out"]] = _diag_blocks(p, lane_head).astype(p_ref.dtype)
        z = _dot(jnp.concatenate([u["u0"], u["v"]], axis=0).T.astype(BF16), u["bk"])
        z_ref[u["rows"], u["out"]] = _diag_blocks(z, lane_head)
    while fillers:
        fill_one()


def _rwkv_kernel(z_ref, zp_ref, zn_ref, mu_ref, gup_ref, w0_ref, dup_ref, a0_ref, aup_ref, kk_ref, bd_ref, ka_ref,
                 r_ref, k_ref, v_ref, g_ref, q_ref, y0_ref, p_ref, zz_ref, gd_ref,
                 pad_ref, r_n, k_n, v_n, n_n, lw_n, a_n, r_c, k_c, v_c, n_c, lw_c, a_c, *, n_ctx, seq_s, n_tiles):
    s = pl.program_id(0)
    new = (r_n, k_n, v_n, n_n, lw_n, a_n)
    cur = (r_c, k_c, v_c, n_c, lw_c, a_c)

    @pl.when(s == 0)
    def _():
        for ref in cur:
            ref[...] = jnp.zeros_like(ref)

    pieces = _prep_pieces(jnp.minimum(s, n_tiles - 1), z_ref, zp_ref, zn_ref, mu_ref, gup_ref, w0_ref, dup_ref,
                          a0_ref, aup_ref, kk_ref, bd_ref, (r_n, r_ref), (k_n, k_ref), (v_n, v_ref), n_n, g_ref,
                          lw_n, a_n, pad_ref, n_ctx, seq_s)
    pieces[0]()
    _summary_body(r_c, k_c, v_c, n_c, lw_c, a_c, ka_ref, q_ref, y0_ref, p_ref, zz_ref, gd_ref, fillers=pieces[1:4],
                  chunks=range(0, 2))
    _summary_body(r_c, k_c, v_c, n_c, lw_c, a_c, ka_ref, q_ref, y0_ref, p_ref, zz_ref, gd_ref, fillers=pieces[4:],
                  chunks=range(2, 4))
    for dst, src in zip(cur, new):
        dst[...] = src[...]


def _rwkv_project_and_summarise(u_rw, mu, g_up, dec_w0, dec_up, iclr_a0, iclr_up, k_k, k_a, n_ctx, seq_s):
    n = u_rw.shape[0]
    assert ROW_TILE == SUMMARY_CHUNKS * CHUNK and CHUNK == HEAD_DIM and C_RWKV % GROUP_W == 0
    n_tiles = n // ROW_TILE
    halo = ROW_TILE // 8
    last_blk = n // 8 - 1
    c = C_RWKV
    full = lambda shape: pl.BlockSpec(shape, lambda s: (0,) * len(shape))
    this = lambda s: jnp.minimum(s, n_tiles - 1)
    prev = lambda s: jnp.maximum(s - 1, 0)
    proj = lambda w: pl.BlockSpec((ROW_TILE, w), lambda s: (this(s), 0))
    summ = lambda w: pl.BlockSpec((ROW_TILE, w), lambda s: (prev(s), 0))
    vm = lambda w: pltpu.VMEM((ROW_TILE, w), F32)
    return pl.pallas_call(
        functools.partial(_rwkv_kernel, n_ctx=n_ctx, seq_s=seq_s, n_tiles=n_tiles),
        grid=(n_tiles + 1,),
        in_specs=[
            proj(N_SHIFT),
            pl.BlockSpec((8, N_SHIFT), lambda s: (jnp.maximum(this(s) * halo - 1, 0), 0)),
            pl.BlockSpec((8, N_SHIFT), lambda s: (jnp.minimum((this(s) + 1) * halo, last_blk), 0)),
            full((1, N_SHIFT)), full((G_RANK, c)), full((1, 2 * c)), full((2 * W_RANK, 2 * c)),
            full((1, 2 * c)), full((2 * A_RANK, 2 * c)), full((1, c)), full((c, c)), full((1, c)),
        ],
        out_specs=[proj(c), proj(c), proj(c), proj(c), summ(2 * c), summ(2 * c), summ(2 * c), summ(2 * c),
                   pl.BlockSpec((SUMMARY_CHUNKS, 1, 2 * c), lambda s: (prev(s), 0, 0))],
        out_shape=[jax.ShapeDtypeStruct((n, c), F32)] * 4
        + [jax.ShapeDtypeStruct((n, 2 * c), BF16), jax.ShapeDtypeStruct((n, 2 * c), F32),
           jax.ShapeDtypeStruct((n, 2 * c), BF16), jax.ShapeDtypeStruct((n, 2 * c), F32),
           jax.ShapeDtypeStruct((n // CHUNK, 1, 2 * c), F32)],
        scratch_shapes=[pltpu.VMEM((ROW_TILE + 16, N_SHIFT), F32)]
        + [vm(c), vm(c), vm(c), vm(c), vm(2 * c), vm(2 * c)] * 2,
        compiler_params=_cparams(("arbitrary",)),
        name="rwkv_project_summarise",
    )(u_rw, u_rw, u_rw, mu.reshape(1, N_SHIFT), g_up.astype(BF16), dec_w0.reshape(1, 2 * c),
      _block_diag2(dec_up).astype(BF16), iclr_a0.reshape(1, 2 * c), _block_diag2(iclr_up).astype(BF16),
      k_k.reshape(1, c), _head_ones(), k_a.reshape(1, c))


def _carry_kernel(tab_ref, qf_ref, y0f_ref, pf_ref, zf_ref, gf_ref, qb_ref, y0b_ref, pb_ref, zb_ref, gb_ref,
                  s0_ref, yf_ref, yb_ref, sfin_ref, st_ref):
    step = pl.program_id(0)

    @pl.when(tab_ref[3, step] == 1)
    def _():
        st_ref[...] = s0_ref[0]

    bd_mask = (lax.broadcasted_iota(jnp.int32, (GROUP * CHUNK, GROUP_W), 0) // CHUNK
               == lax.broadcasted_iota(jnp.int32, (GROUP * CHUNK, GROUP_W), 1) // HEAD_DIM)
    for i in range(CARRY_CHUNKS):
        for d, (q_ref, y0_ref, p_ref, z_ref, g_ref, y_ref) in enumerate(
                ((qf_ref, y0f_ref, pf_ref, zf_ref, gf_ref, yf_ref), (qb_ref, y0b_ref, pb_ref, zb_ref, gb_ref, yb_ref))):
            j = i if d == 0 else CARRY_CHUNKS - 1 - i
            rows = slice(j * CHUNK, (j + 1) * CHUNK)
            for q in range(C_RWKV // GROUP_W):
                sl = slice(q * GROUP_W, (q + 1) * GROUP_W)
                s = st_ref[d, :, sl]
                y_ref[rows, sl] = y0_ref[rows, sl] + _dot_nt(q_ref[rows, sl], _block_diag(s, bd_mask))
                st_ref[d, :, sl] = (s * g_ref[j, :, sl] + z_ref[rows, sl]
                                    + _dot(s.astype(BF16), _block_diag(p_ref[rows, sl], bd_mask)))

    @pl.when(tab_ref[4, step] == 1)
    def _():
        sfin_ref[0] = st_ref[...]


def _scan_table(n_ctx_seq, t_ctx, n_s_seq, t_s):
    rows = []
    base = 0
    seq = 0
    for n_seq, t_len in ((n_ctx_seq, t_ctx), (n_s_seq, t_s)):
        assert t_len % (CARRY_CHUNKS * CHUNK) == 0
        n_c = t_len // (CARRY_CHUNKS * CHUNK)
        for _ in range(n_seq):
            for c in range(n_c):
                rows.append((base + c, base + n_c - 1 - c, seq, int(c == 0), int(c == n_c - 1)))
            base += n_c
            seq += 1
    return np.asarray(rows, dtype=np.int32).T.copy()


def _carry_states(q, y0, p, z, g, s0, table):
    n = q.shape[0]
    n_seq = s0.shape[0]
    c = C_RWKV

    def tok(which, d):
        return pl.BlockSpec((CARRY_CHUNKS * CHUNK, c), lambda s, tab: (tab[which, s], d))

    def decay(which, d):
        return pl.BlockSpec((CARRY_CHUNKS, 1, c), lambda s, tab: (tab[which, s], 0, d))

    state_spec = pl.BlockSpec((1, 2, HEAD_DIM, c), lambda s, tab: (tab[2, s], 0, 0, 0))
    grid_spec = pltpu.PrefetchScalarGridSpec(
        num_scalar_prefetch=1,
        grid=(table.shape[1],),
        in_specs=[tok(0, 0), tok(0, 0), tok(0, 0), tok(0, 0), decay(0, 0),
                  tok(1, 1), tok(1, 1), tok(1, 1), tok(1, 1), decay(1, 1), state_spec],
        out_specs=[tok(0, 0), tok(1, 0), state_spec],
        scratch_shapes=[pltpu.VMEM((2, HEAD_DIM, c), F32)],
    )
    return pl.pallas_call(
        _carry_kernel,
        grid_spec=grid_spec,
        out_shape=[jax.ShapeDtypeStruct((n, c), F32), jax.ShapeDtypeStruct((n, c), F32),
                   jax.ShapeDtypeStruct((n_seq, 2, HEAD_DIM, c), F32)],
        compiler_params=_cparams(("arbitrary",)),
        name="rwkv_carry",
    )(jnp.asarray(table), q, y0, p, z, g, q, y0, p, z, g, s0)


def _merge_kernel(xc_ref, xl_ref, c_ref, fc_ref, fl_ref, yf_ref, yb_ref, r_ref, k_ref, v_ref, g_ref, n1_ref, wgt_ref, mod_ref,
                  wc_ref, wf_ref, wr_ref, wo_ref, gng_ref, gnb_ref, rk_ref, bd_ref, n2_ref, *rest, n_ctx, seq_s, route):
    if route:
        wrt_ref, brt_ref, x1_ref, h2_ref, comb_ref = rest
    else:
        x1_ref, h2_ref = rest
    row = _mod_row(pl.program_id(0), WIDE_TILE, n_ctx, seq_s)
    m = mod_ref[pl.ds(row, 1), :]
    sh1 = m[:, 0:D_MODEL]
    sc1 = m[:, D_MODEL:2 * D_MODEL]
    g1 = m[:, 2 * D_MODEL:3 * D_MODEL]
    sh2 = m[:, 3 * D_MODEL:4 * D_MODEL]
    sc2 = m[:, 4 * D_MODEL:5 * D_MODEL]
    bd = bd_ref[...]
    inv = 1.0 / HEAD_DIM
    y = yf_ref[...] + yb_ref[...]
    y_hi, y_lo = _split(y)
    d = y - (_dot(y_hi, bd) + _dot(y_lo, bd)) * inv
    var = _seg_sum(d * d, bd) * inv
    yn = d * lax.rsqrt(var + GN_EPS) * gng_ref[...] + gnb_ref[...]
    v = v_ref[...]
    yn = yn + _seg_sum(r_ref[...] * k_ref[...] * rk_ref[...], bd) * v
    yr = _dot((yn * g_ref[...]).astype(BF16), wr_ref[...])
    yc = _dot(c_ref[...], wc_ref[...])
    yf = _dot(_pick_stream(pl.program_id(0), WIDE_TILE, n_ctx, fc_ref, fl_ref), wf_ref[...])
    x = _pick_stream(pl.program_id(0), WIDE_TILE, n_ctx, xc_ref, xl_ref)
    hn = x * lax.rsqrt(jnp.mean(x * x, axis=-1, keepdims=True) + RMS_EPS) * n1_ref[...]
    hn = (hn * (1.0 + sc1) + sh1).astype(BF16)
    dm = D_MODEL
    merged = (_sigmoid(_dot(hn, wgt_ref[:, 0:dm])) * yc + _sigmoid(_dot(hn, wgt_ref[:, dm:2 * dm])) * yf
              + _sigmoid(_dot(hn, wgt_ref[:, 2 * dm:3 * dm])) * yr)
    x1 = x + g1 * _dot(merged.astype(BF16), wo_ref[...])
    x1_ref[...] = x1
    h2 = x1 * lax.rsqrt(jnp.mean(x1 * x1, axis=-1, keepdims=True) + RMS_EPS) * n2_ref[...]
    h2 = h2 * (1.0 + sc2) + sh2
    h2_ref[...] = h2.astype(BF16)
    if not route:
        return
    logits = _dot_x3(h2, wrt_ref[...]) + brt_ref[...]
    lane = lax.broadcasted_iota(jnp.int32, logits.shape, 1).astype(F32)
    neg = jnp.float32(-jnp.inf)
    lg = jnp.where(lane < N_EXPERTS, logits, neg)
    m1 = jnp.max(lg, axis=-1, keepdims=True)
    i1 = jnp.min(jnp.where(lg == m1, lane, float(LANES)), axis=-1, keepdims=True)
    lg2 = jnp.where(lane == i1, neg, lg)
    m2 = jnp.max(lg2, axis=-1, keepdims=True)
    i2 = jnp.min(jnp.where(lg2 == m2, lane, float(LANES)), axis=-1, keepdims=True)
    e = jnp.exp(m2 - m1)
    p1 = 1.0 / (1.0 + e)
    comb_ref[...] = jnp.where(lane == i1, p1, 0.0) + jnp.where(lane == i2, e * p1, 0.0)


def _merge(x, n, conv_h, fno_h, yf, yb, r, k, v, g, gain1, w_gates, mod, w_conv_out, w_fno_out, w_rwkv_out, w_o,
           gn_g, gn_b, r_k, gain2, n_ctx, seq_s, router=None):
    xc, xl, x_lat0 = _stream_args(x, WIDE_TILE, n_ctx)
    fc, fl, f_lat0 = _stream_args(fno_h, WIDE_TILE, n_ctx)
    c = C_RWKV
    tile = lambda w: pl.BlockSpec((WIDE_TILE, w), lambda i: (i, 0))
    full = lambda shape: pl.BlockSpec(shape, lambda i: (0,) * len(shape), pipeline_mode=pl.Buffered(1))
    route = router is not None
    route_args, route_specs, route_out, route_shape = [], [], [], []
    if route:
        route_args = [jnp.zeros((D_MODEL, LANES), F32).at[:, :N_EXPERTS].set(router[0]),
                      jnp.zeros((1, LANES), F32).at[0, :N_EXPERTS].set(router[1])]
        route_specs = [full((D_MODEL, LANES)), full((1, LANES))]
        route_out = [tile(LANES)]
        route_shape = [jax.ShapeDtypeStruct((n, LANES), F32)]
    return pl.pallas_call(
        functools.partial(_merge_kernel, n_ctx=n_ctx, seq_s=seq_s, route=route),
        grid=(n // WIDE_TILE,),
        in_specs=_stream_specs(WIDE_TILE, D_MODEL, n_ctx, x_lat0) + [tile(C_CONV)]
        + _stream_specs(WIDE_TILE, C_FNO, n_ctx, f_lat0) + [
            tile(c), tile(c), tile(c), tile(c), tile(c), tile(c),
            full((1, D_MODEL)), full((D_MODEL, 3 * D_MODEL)), full((8, 6 * D_MODEL)),
            full((C_CONV, D_MODEL)), full((C_FNO, D_MODEL)), full((c, D_MODEL)), full((D_MODEL, D_MODEL)),
            full((1, c)), full((1, c)), full((1, c)), full((c, c)), full((1, D_MODEL)),
        ] + route_specs,
        out_specs=[tile(D_MODEL), tile(D_MODEL)] + route_out,
        out_shape=[jax.ShapeDtypeStruct((n, D_MODEL), F32), jax.ShapeDtypeStruct((n, D_MODEL), BF16)] + route_shape,
        compiler_params=_cparams(("parallel",)),
        name="merge",
    )(xc, xl, conv_h, fc, fl, yf, yb, r, k, v, g, gain1.reshape(1, D_MODEL), w_gates, mod,
      w_conv_out.astype(BF16), w_fno_out.astype(BF16), w_rwkv_out.astype(BF16), w_o.astype(BF16),
      gn_g.reshape(1, c), gn_b.reshape(1, c), r_k.reshape(1, c), _head_ones(), gain2.reshape(1, D_MODEL), *route_args)


def _ffn_kernel(x_ref, h_ref, mod_ref, wg_ref, wu_ref, wd_ref, o_ref, *, n_ctx, seq_s):
    row = _mod_row(pl.program_id(0), FFN_TILE, n_ctx, seq_s)
    g2 = mod_ref[pl.ds(row, 1), 5 * D_MODEL:6 * D_MODEL]
    h = h_ref[...]
    half = D_FF // 2
    acc = jnp.zeros((FFN_TILE, D_MODEL), F32)
    for p in range(2):
        sl = slice(p * half, (p + 1) * half)
        t = _silu(_dot(h, wg_ref[:, sl])) * _dot(h, wu_ref[:, sl])
        acc = acc + _dot(t.astype(BF16), wd_ref[sl, :])
    o_ref[...] = x_ref[...] + g2 * acc


def _ffn_dense(x1, h2, mod, w_gate, w_up, w_down, n_ctx, seq_s):
    n = x1.shape[0]
    tile = pl.BlockSpec((FFN_TILE, D_MODEL), lambda i: (i, 0))
    full = lambda shape: pl.BlockSpec(shape, lambda i: (0,) * len(shape), pipeline_mode=pl.Buffered(1))
    return pl.pallas_call(
        functools.partial(_ffn_kernel, n_ctx=n_ctx, seq_s=seq_s),
        grid=(n // FFN_TILE,),
        in_specs=[tile, tile, full((8, 6 * D_MODEL)), full((D_MODEL, D_FF)), full((D_MODEL, D_FF)),
                  full((D_FF, D_MODEL))],
        out_specs=tile,
        out_shape=jax.ShapeDtypeStruct((n, D_MODEL), F32),
        compiler_params=_cparams(("parallel",)),
        name="ffn_dense",
    )(x1, h2, mod, w_gate.astype(BF16), w_up.astype(BF16), w_down.astype(BF16))


def _moe_kernel(x_ref, h_ref, comb_ref, mod_ref, wg_ref, wu_ref, wd_ref, fin_ref, *rest, n_ctx, seq_s, final):
    outs, (acc_ref, rank_c_ref, rank_r_ref, comb_t_ref) = rest[:-4], rest[-4:]
    e = pl.program_id(1)
    parts = [slice(p * MOE_PART, (p + 1) * MOE_PART) for p in range(MOE_TILE // MOE_PART)]

    @pl.when(e == 0)
    def _():
        acc_ref[...] = jnp.zeros_like(acc_ref)
        before = (lax.broadcasted_iota(jnp.int32, (MOE_PART, MOE_PART), 1)
                  < lax.broadcasted_iota(jnp.int32, (MOE_PART, MOE_PART), 0)).astype(BF16)
        for part in parts:
            comb = comb_ref[part, :]
            rank_c_ref[part, :] = _dot(before, (comb > 0.0).astype(BF16))
            comb_t = comb.T
            comb_t_ref[:, part] = comb_t
            rank_r_ref[:, part] = _dot_nt((comb_t > 0.0).astype(BF16), before)

    comb = comb_ref[...]
    lane = lax.broadcasted_iota(jnp.int32, comb.shape, 1)
    w_col = jnp.sum(jnp.where(lane == e, comb, 0.0), axis=-1, keepdims=True)
    rank_col = jnp.sum(jnp.where(lane == e, rank_c_ref[...], 0.0), axis=-1, keepdims=True)
    w_row = comb_t_ref[pl.ds(e, 1), :]
    rank_row = rank_r_ref[pl.ds(e, 1), :]
    count = functools.reduce(jnp.maximum, [jnp.sum((w_row[:, part] > 0.0).astype(jnp.int32)) for part in parts])

    def body(j, carry):
        base = (j * MOE_ROWS).astype(F32)
        slot_r = lax.broadcasted_iota(jnp.int32, (MOE_ROWS, MOE_PART), 0).astype(F32) + base
        slot_c = lax.broadcasted_iota(jnp.int32, (MOE_PART, MOE_ROWS), 1).astype(F32) + base
        rows = []
        for part in parts:
            hit = jnp.logical_and(rank_row[:, part] == slot_r, w_row[:, part] > 0.0)
            rows.append(_dot(jnp.where(hit, 1.0, 0.0).astype(BF16), h_ref[part, :]).astype(BF16))
        hg = jnp.concatenate(rows, axis=0)
        t = _silu(_dot(hg, wg_ref[0])) * _dot(hg, wu_ref[0])
        y = _dot(t.astype(BF16), wd_ref[0]).astype(BF16)
        for p, part in enumerate(parts):
            hit = jnp.logical_and(rank_col[part, :] == slot_c, w_col[part, :] > 0.0)
            acc_ref[part, :] += w_col[part, :] * _dot(jnp.where(hit, 1.0, 0.0).astype(BF16),
                                                      y[p * MOE_ROWS:(p + 1) * MOE_ROWS])
        return carry

    lax.fori_loop(0, (count + MOE_ROWS - 1) // MOE_ROWS, body, 0)

    @pl.when(e == N_EXPERTS - 1)
    def _():
        row = _mod_row(pl.program_id(0), MOE_TILE, n_ctx, seq_s)
        g2 = mod_ref[pl.ds(row, 1), 5 * D_MODEL:6 * D_MODEL]
        y = x_ref[...] + g2 * acc_ref[...]
        if not final:
            outs[0][...] = y
        else:
            y = y * lax.rsqrt(jnp.mean(y * y, axis=-1, keepdims=True) + RMS_EPS) * fin_ref[...]
            is_ctx = pl.program_id(0) < n_ctx // MOE_TILE

            @pl.when(is_ctx)
            def _():
                outs[0][...] = y

            @pl.when(jnp.logical_not(is_ctx))
            def _():
                outs[1][...] = y


def _ffn_moe(x1, h2, comb, mod, w_gate, w_up, w_down, n_ctx, seq_s, final_gain=None):
    n = x1.shape[0]
    final = final_gain is not None
    once = pl.Buffered(1)
    tile = lambda w: pl.BlockSpec((MOE_TILE, w), lambda i, e: (i, 0), pipeline_mode=once)
    if final:
        n_ctx_tiles = n_ctx // MOE_TILE
        out_specs = [pl.BlockSpec((MOE_TILE, D_MODEL), lambda i, e: (jnp.minimum(i, n_ctx_tiles - 1), 0),
                                  pipeline_mode=once),
                     pl.BlockSpec((MOE_TILE, D_MODEL), lambda i, e: (jnp.maximum(i - n_ctx_tiles, 0), 0),
                                  pipeline_mode=once)]
        out_shape = [jax.ShapeDtypeStruct((n_ctx, D_MODEL), F32), jax.ShapeDtypeStruct((n - n_ctx, D_MODEL), F32)]
        gain = final_gain.reshape(1, D_MODEL)
    else:
        out_specs = [tile(D_MODEL)]
        out_shape = [jax.ShapeDtypeStruct((n, D_MODEL), F32)]
        gain = jnp.ones((1, D_MODEL), F32)
    out = pl.pallas_call(
        functools.partial(_moe_kernel, n_ctx=n_ctx, seq_s=seq_s, final=final),
        grid=(n // MOE_TILE, N_EXPERTS),
        in_specs=[tile(D_MODEL), tile(D_MODEL), tile(LANES),
                  pl.BlockSpec((8, 6 * D_MODEL), lambda i, e: (0, 0)),
                  pl.BlockSpec((1, D_MODEL, D_FF_E), lambda i, e: (e, 0, 0)),
                  pl.BlockSpec((1, D_MODEL, D_FF_E), lambda i, e: (e, 0, 0)),
                  pl.BlockSpec((1, D_FF_E, D_MODEL), lambda i, e: (e, 0, 0)),
                  pl.BlockSpec((1, D_MODEL), lambda i, e: (0, 0))],
        out_specs=out_specs,
        out_shape=out_shape,
        scratch_shapes=[pltpu.VMEM((MOE_TILE, D_MODEL), F32), pltpu.VMEM((MOE_TILE, LANES), F32),
                        pltpu.VMEM((LANES, MOE_TILE), F32), pltpu.VMEM((LANES, MOE_TILE), F32)],
        compiler_params=_cparams(("arbitrary", "arbitrary") if final else ("parallel", "arbitrary")),
        name="ffn_moe",
    )(x1, h2, comb, mod, w_gate.astype(BF16), w_up.astype(BF16), w_down.astype(BF16), gain)
    return tuple(out) if final else out[0]


def _final_kernel(x_ref, g_ref, o_ref):
    x = x_ref[...]
    o_ref[...] = x * lax.rsqrt(jnp.mean(x * x, axis=-1, keepdims=True) + RMS_EPS) * g_ref[...]


def _final_norm(x, gain, row0, n_rows):
    blk0 = row0 // FFN_TILE
    return pl.pallas_call(
        _final_kernel,
        grid=(n_rows // FFN_TILE,),
        in_specs=[pl.BlockSpec((FFN_TILE, D_MODEL), lambda i: (blk0 + i, 0)),
                  pl.BlockSpec((1, D_MODEL), lambda i: (0, 0))],
        out_specs=pl.BlockSpec((FFN_TILE, D_MODEL), lambda i: (i, 0)),
        out_shape=jax.ShapeDtypeStruct((n_rows, D_MODEL), F32),
        compiler_params=_cparams(("parallel",)),
        name="final_norm",
    )(x, gain.reshape(1, D_MODEL))


def kernel(x_prompt, x_sample, state_rwkv, c, c_ctx, norm1, norm2, w_ada, b_ada, w_in, dw_w, dw_b, conv_ln_g, conv_ln_b, w_conv_out, w_fno_out, shift_mu, g_up, dec_w0, dec_up, iclr_a0, iclr_up, k_k, k_a, r_k, gn_g, gn_b, w_rwkv_out, w_o, ffn_w_gate, ffn_w_up, ffn_w_down, w_router, b_router, moe_w_gate, moe_w_up, moe_w_down, final_norm):
    b_p, t_p, _ = x_prompt.shape
    b_s, t_s, _ = x_sample.shape
    depth = w_in.shape[0]
    n_ctx = b_p * t_p
    n_lat = b_s * t_s
    assert t_p == ROW_TILE and t_s % MOE_TILE == 0 and n_ctx % MOE_TILE == 0 and b_s <= CTX_ROW
    assert MOE_TILE % FFN_TILE == 0 and n_ctx % WIDE_TILE == 0 and t_s % WIDE_TILE == 0
    assert t_s % GRID_W == 0 and CHUNK == GRID_W

    n = n_ctx + n_lat
    x = (x_prompt.reshape(n_ctx, D_MODEL), x_sample.reshape(n_lat, D_MODEL))
    cond = jnp.zeros((8, D_MODEL), F32).at[:b_s].set(c).at[CTX_ROW].set(c_ctx)
    mods = _ada(cond, w_ada, b_ada)
    table = _scan_table(b_p, t_p, b_s, t_s)
    zero_state = jnp.zeros((b_p, 2, HEAD_DIM, C_RWKV), F32)
    pack_state = lambda s: s.transpose(0, 1, 3, 2, 4).reshape(s.shape[0], 2, HEAD_DIM, C_RWKV)
    unpack_state = lambda s: s.reshape(s.shape[0], 2, HEAD_DIM, N_HEADS, HEAD_DIM).transpose(0, 1, 3, 2, 4)

    ctx_states = []
    for l in range(depth):
        mod = mods[l]
        u_conv, u_fno, u_rw = _inproj(x, n, mod, norm1[l], w_in[l, :, :OFF_GATE].astype(BF16), n_ctx, t_s)
        conv_h = _conv_branch(u_conv, dw_w[l], dw_b[l], conv_ln_g[l], conv_ln_b[l], n_ctx)
        fno_h = (_fno_branch(u_fno, 0, b_p, t_p), _fno_branch(u_fno, n_ctx, b_s, t_s))
        r, k, v, g, *summaries = _rwkv_project_and_summarise(u_rw, shift_mu[l], g_up[l], dec_w0[l], dec_up[l],
                                                             iclr_a0[l], iclr_up[l], k_k[l], k_a[l], n_ctx, t_s)
        s0 = jnp.concatenate([zero_state, pack_state(state_rwkv[:, l])], axis=0)
        yf, yb, s_fin = _carry_states(*summaries, s0, table)
        ctx_states.append(unpack_state(s_fin[:b_p]))
        i = l // 2
        x1, h2, *comb = _merge(x, n, conv_h, fno_h, yf, yb, r, k, v, g, norm1[l], w_in[l, :, OFF_GATE:].astype(BF16), mod, w_conv_out[l], w_fno_out[l],
                              w_rwkv_out[l], w_o[l], gn_g[l], gn_b[l], r_k[l], norm2[l], n_ctx, t_s,
                               router=(w_router[i], b_router[i]) if l % 2 == 1 else None)
        if l % 2 == 0:
            x = _ffn_dense(x1, h2, mod, ffn_w_gate[i], ffn_w_up[i], ffn_w_down[i], n_ctx, t_s)
        else:
            x = _ffn_moe(x1, h2, comb[0], mod, moe_w_gate[i], moe_w_up[i], moe_w_down[i], n_ctx, t_s,
                         final_gain=final_norm if l == depth - 1 else None)

    if isinstance(x, tuple):
        y_prompt, y_sample = x
    else:
        y_prompt, y_sample = _final_norm(x, final_norm, 0, n_ctx), _final_norm(x, final_norm, n_ctx, n_lat)
    y_prompt = y_prompt.reshape(b_p, t_p, D_MODEL)
    y_sample = y_sample.reshape(b_s, t_s, D_MODEL)
    new_state = jnp.stack(ctx_states, axis=1).astype(x_prompt.dtype)
    return (y_prompt, y_sample, new_state)
```
